```python
import math
import jax, jax.numpy as jnp
from jax import lax
import numpy as np

D_MODEL = 1024
BATCH = 2
SEQ = 8192
DEPTH = 2
DEC_BATCH = 128
DEC_SEQ = 4
PAST_LEN = 2048
PAGE_SIZE = 128

N_EVEN = (DEPTH + 1) // 2
N_ODD = DEPTH // 2
PLE_DIM = 256
D_FF = 4 * D_MODEL
EPS = 1e-6
ROPE_THETA = 10000.0

POOL_WINDOWS = (2, 4, 8, 16)
POOL_GROUP = D_MODEL // 8
POOL_WIDTH = POOL_GROUP * len(POOL_WINDOWS)
POOL_STATE = max(POOL_WINDOWS) - 1
DIL_CFG = ((128, 1), (512, 4), (2048, 16))
B_HEADS = 8
B_HEAD_DIM = 64
B_WIDTH = B_HEADS * B_HEAD_DIM
ATT_BLOCK = 128
CHUNK = 128
C_GROUPS = 4
C_WIDTH = D_MODEL // 2
C_GROUP_DIM = C_WIDTH // C_GROUPS
CONV_W = 3
D_WIDTH = D_MODEL // 2

EV_IN = POOL_WIDTH + 3 * len(DIL_CFG) * B_WIDTH
EV_OUT = POOL_WIDTH + B_WIDTH
OD_IN = 2 * C_WIDTH + 3 * D_WIDTH
OD_OUT = C_WIDTH + D_WIDTH

kernel_name = "hybrid_pool_dilattn_gmlp_shortconv_step"


def rmsnorm(x, g):
    xf = x.astype(jnp.float32)
    y = xf * lax.rsqrt(jnp.mean(xf * xf, axis=-1, keepdims=True) + EPS)
    return (y * g.astype(jnp.float32)).astype(x.dtype)


def rope(x, pos):
    half = x.shape[-1] // 2
    inv = jnp.power(jnp.float32(ROPE_THETA), -jnp.arange(half, dtype=jnp.float32) / half)
    ang = pos.astype(jnp.float32)[:, None] * inv[None, :]
    cos = jnp.cos(ang)[None, :, None, :]
    sin = jnp.sin(ang)[None, :, None, :]
    xf = x.astype(jnp.float32)
    x1, x2 = xf[..., :half], xf[..., half:]
    return jnp.concatenate([x1 * cos - x2 * sin, x2 * cos + x1 * sin], axis=-1).astype(x.dtype)


def causal_pool_mix(ext, n_ctx, pos, pool_w, pool_scale):
    N, Le, _ = ext.shape
    L = Le - n_ctx
    xf = ext.astype(jnp.float32)
    csum = jnp.concatenate([jnp.zeros((N, 1, POOL_WIDTH), jnp.float32), jnp.cumsum(xf, axis=1)], axis=1)
    end = n_ctx + 1 + jnp.arange(L)
    outs = []
    for gi, w in enumerate(POOL_WINDOWS):
        sl = slice(gi * POOL_GROUP, (gi + 1) * POOL_GROUP)
        cg = csum[..., sl]
        wsum = cg[:, end] - cg[:, jnp.maximum(end - w, 0)]
        cnt = jnp.minimum(w, pos + 1).astype(jnp.float32)
        outs.append(wsum / cnt[None, :, None] - xf[:, n_ctx:, sl])
    pooled = jnp.stack(outs, axis=2)
    mixed = jnp.einsum('nlgc,gcd->nlgd', pooled, pool_w.astype(jnp.float32))
    return (mixed.reshape(N, L, POOL_WIDTH) * pool_scale.astype(jnp.float32)).astype(ext.dtype)


def dilated_attn_prompt(q, k, v, window, dil):
    N, S, H, Dh = q.shape
    span = window // dil
    unit = dil * ATT_BLOCK
    Sp = -(-S // unit) * unit
    M = Sp // dil
    nb = M // ATT_BLOCK

    def to_blocks(t):
        t = jnp.pad(t, ((0, 0), (0, Sp - S), (0, 0), (0, 0)))
        t = t.reshape(N, M, dil, H, Dh).transpose(0, 2, 1, 3, 4)
        return t.reshape(N, dil, nb, ATT_BLOCK, H, Dh)

    def with_prev(t):
        prev = jnp.pad(t, ((0, 0), (0, 0), (1, 0), (0, 0), (0, 0), (0, 0)))[:, :, :-1]
        return jnp.concatenate([prev, t], axis=3)

    qb = to_blocks(q)
    kk = with_prev(to_blocks(k))
    vv = with_prev(to_blocks(v))
    s = jnp.einsum('nrbqhd,nrbkhd->nrbhqk', qb, kk, preferred_element_type=jnp.float32) * (Dh ** -0.5)
    qi = jnp.arange(ATT_BLOCK)[:, None]
    ki = jnp.arange(2 * ATT_BLOCK)[None, :]
    rel = qi + ATT_BLOCK - ki
    blk = jnp.arange(nb)[:, None, None]
    valid = (rel >= 0) & (rel <= span) & (blk * ATT_BLOCK + ki - ATT_BLOCK >= 0)
    s = jnp.where(valid[:, None], s, -jnp.inf)
    m = jnp.max(s, axis=-1, keepdims=True)
    e = jnp.exp(s - m)
    l = jnp.sum(e, axis=-1, keepdims=True)
    o = jnp.einsum('nrbhqk,nrbkhd->nrbqhd', e / l, vv.astype(jnp.float32))
    lse = (m + jnp.log(l))[..., 0]
    o = o.reshape(N, dil, M, H, Dh).transpose(0, 2, 1, 3, 4).reshape(N, Sp, H, Dh)[:, :S]
    lse = lse.transpose(0, 1, 2, 4, 3).reshape(N, dil, M, H).transpose(0, 2, 1, 3).reshape(N, Sp, H)[:, :S]
    return o, lse


def dilated_attn_sample(q, k_new, v_new, kv_buf, window, dil):
    N, T, H, Dh = q.shape
    Lb = kv_buf.shape[1]
    span = window // dil
    kc = jnp.concatenate([kv_buf[:, :, 0].astype(k_new.dtype), k_new], axis=1)
    vc = jnp.concatenate([kv_buf[:, :, 1].astype(v_new.dtype), v_new], axis=1)
    idx = Lb + jnp.arange(T)[:, None] - dil * jnp.arange(span + 1)[None, :]
    valid = idx >= 0
    idx_c = jnp.maximum(idx, 0)
    kg = kc[:, idx_c]
    vg = vc[:, idx_c]
    s = jnp.einsum('nthd,ntjhd->nthj', q, kg, preferred_element_type=jnp.float32) * (Dh ** -0.5)
    s = jnp.where(valid[:, None, :], s, -jnp.inf)
    m = jnp.max(s, axis=-1, keepdims=True)
    e = jnp.exp(s - m)
    l = jnp.sum(e, axis=-1, keepdims=True)
    o = jnp.einsum('nthj,ntjhd->nthd', e / l, vg.astype(jnp.float32))
    return o, (m + jnp.log(l))[..., 0]


def even_mixer(h, pos, pool_ctx, kv_bufs, w_in, pool_w, pool_scale, w_out):
    N, L, _ = h.shape
    z = jnp.einsum('nld,de->nle', h, w_in)
    a = z[..., :POOL_WIDTH]
    qkv = z[..., POOL_WIDTH:].reshape(N, L, len(DIL_CFG), 3, B_HEADS, B_HEAD_DIM)
    if pool_ctx is None:
        ext, n_ctx = a, 0
    else:
        ext, n_ctx = jnp.concatenate([pool_ctx.astype(a.dtype), a], axis=1), pool_ctx.shape[1]
    ya = causal_pool_mix(ext, n_ctx, pos, pool_w, pool_scale)
    outs, lses, new_kv = [], [], []
    for g, (win, dil) in enumerate(DIL_CFG):
        q = rope(qkv[:, :, g, 0], pos)
        k = rope(qkv[:, :, g, 1], pos)
        v = qkv[:, :, g, 2]
        if kv_bufs is None:
            o, lse = dilated_attn_prompt(q, k, v, win, dil)
            keep = min(win, L)
            new_kv.append(jnp.stack([k[:, L - keep:], v[:, L - keep:]], axis=2))
        else:
            o, lse = dilated_attn_sample(q, k, v, kv_bufs[g], win, dil)
            new_kv.append(jnp.stack([k, v], axis=2))
        outs.append(o)
        lses.append(lse)
    wts = jax.nn.softmax(jnp.stack(lses, axis=0), axis=0)
    yb = jnp.einsum('gnlh,gnlhd->nlhd', wts, jnp.stack(outs, axis=0)).reshape(N, L, B_WIDTH).astype(h.dtype)
    y = jnp.einsum('nle,ed->nld', jnp.concatenate([ya, yb], axis=-1), w_out)
    return y, new_kv, ext[:, -POOL_STATE:]


def odd_mixer(h, pos, conv_ctx, is_prompt, w_in, ln_g, ln_b, ws, bs, conv_w, w_out):
    N, L, _ = h.shape
    z = jnp.einsum('nld,de->nle', h, w_in)
    u = jax.nn.gelu(z[..., :C_WIDTH])
    v = jax.nn.gelu(z[..., C_WIDTH:2 * C_WIDTH])
    vg = v.reshape(N, L, C_GROUPS, C_GROUP_DIM).astype(jnp.float32)
    mu = jnp.mean(vg, axis=-1, keepdims=True)
    var = jnp.mean(jnp.square(vg - mu), axis=-1, keepdims=True)
    vn = ((vg - mu) * lax.rsqrt(var + EPS)).reshape(N, L, C_WIDTH) * ln_g.astype(jnp.float32) + ln_b.astype(jnp.float32)
    vn = vn.astype(h.dtype)
    vn_g = vn.reshape(N, L, C_GROUPS, C_GROUP_DIM)
    w_mask = ws * jnp.tril(jnp.ones((CHUNK, CHUNK), ws.dtype))
    if is_prompt:
        vc = vn_g.reshape(N, L // CHUNK, CHUNK, C_GROUPS, C_GROUP_DIM)
        sp = jnp.einsum('gts,nksgc->nktgc', w_mask, vc) + bs.T[None, None, :, :, None]
        sp = sp.reshape(N, L, C_WIDTH)
    else:
        local = pos % CHUNK
        same = (pos[:, None] // CHUNK) == (pos[None, :] // CHUNK)
        mix = w_mask[:, local[:, None], local[None, :]] * same.astype(ws.dtype)
        sp = jnp.einsum('gts,nsgc->ntgc', mix, vn_g) + bs[:, local].T[None, :, :, None]
        sp = sp.reshape(N, L, C_WIDTH)
    yc = u * sp.astype(u.dtype)
    o0 = 2 * C_WIDTH
    g_out = z[..., o0:o0 + D_WIDTH]
    g_in = z[..., o0 + D_WIDTH:o0 + 2 * D_WIDTH]
    xin = z[..., o0 + 2 * D_WIDTH:]
    hd = g_in * xin
    ctx = jnp.zeros((N, CONV_W - 1, D_WIDTH), hd.dtype) if conv_ctx is None else conv_ctx.astype(hd.dtype)
    ext = jnp.concatenate([ctx, hd], axis=1)
    conv = sum(conv_w[j] * ext[:, j:j + L] for j in range(CONV_W))
    yd = g_out * conv
    y = jnp.einsum('nle,ed->nld', jnp.concatenate([yc, yd], axis=-1), w_out)
    return y, ext[:, -(CONV_W - 1):], vn


def setup_inputs(seed: int = 0) -> dict:
    key = jax.random.key(seed)
    ks = jax.random.split(key, 32)

    def nrm(i, shape, scale=1.0):
        return jax.random.normal(ks[i], shape, jnp.float32) * scale

    d = D_MODEL
    buf = [min(w, PAST_LEN) for w, _ in DIL_CFG]
    return {
        "x_prompt": nrm(0, (BATCH, SEQ, d)),
        "x_sample": nrm(1, (DEC_BATCH, DEC_SEQ, d)),
        "cache_kv_w128": nrm(2, (N_EVEN, DEC_BATCH, buf[0], 2, B_HEADS, B_HEAD_DIM)),
        "cache_kv_w512": nrm(3, (N_EVEN, DEC_BATCH, buf[1], 2, B_HEADS, B_HEAD_DIM)),
        "cache_kv_w2048": nrm(4, (N_EVEN, DEC_BATCH, buf[2], 2, B_HEADS, B_HEAD_DIM)),
        "state_pool": nrm(5, (N_EVEN, DEC_BATCH, POOL_STATE, POOL_WIDTH)),
        "state_conv": nrm(6, (N_ODD, DEC_BATCH, CONV_W - 1, D_WIDTH)),
        "p_prompt": nrm(7, (DEPTH, BATCH, SEQ, PLE_DIM)),
        "p_sample": nrm(8, (DEPTH, DEC_BATCH, DEC_SEQ, PLE_DIM)),
        "ev_w_in": nrm(9, (N_EVEN, d, EV_IN), d ** -0.5),
        "ev_pool_w": nrm(10, (N_EVEN, len(POOL_WINDOWS), POOL_GROUP, POOL_GROUP), POOL_GROUP ** -0.5),
        "ev_pool_scale": 1.0 + nrm(11, (N_EVEN, POOL_WIDTH), 0.01),
        "ev_w_out": nrm(12, (N_EVEN, EV_OUT, d), EV_OUT ** -0.5),
        "od_w_in": nrm(13, (N_ODD, d, OD_IN), d ** -0.5),
        "od_ln_g": 1.0 + nrm(14, (N_ODD, C_WIDTH), 0.01),
        "od_ln_b": nrm(15, (N_ODD, C_WIDTH), 0.01),
        "od_ws": nrm(16, (N_ODD, C_GROUPS, CHUNK, CHUNK), CHUNK ** -0.5),
        "od_bs": 1.0 + nrm(17, (N_ODD, C_GROUPS, CHUNK), 0.01),
        "od_conv_w": nrm(18, (N_ODD, CONV_W, D_WIDTH), CONV_W ** -0.5),
        "od_w_out": nrm(19, (N_ODD, OD_OUT, d), OD_OUT ** -0.5),
        "norm_mix": 1.0 + nrm(20, (DEPTH, d), 0.01),
        "norm_ffn": 1.0 + nrm(21, (DEPTH, d), 0.01),
        "norm_ple": 1.0 + nrm(22, (DEPTH, d), 0.01),
        "ffn_w1": nrm(23, (DEPTH, d, D_FF), d ** -0.5),
        "ffn_w2": nrm(24, (DEPTH, D_FF, d), D_FF ** -0.5),
        "ple_w_proj": nrm(25, (DEPTH, PLE_DIM, d), PLE_DIM ** -0.5),
        "ple_w_gate": nrm(26, (DEPTH, d, d), d ** -0.5),
        "norm_final": 1.0 + nrm(27, (d,), 0.01),
    }


def reference(x_prompt, x_sample, cache_kv_w128, cache_kv_w512, cache_kv_w2048, state_pool, state_conv,
              p_prompt, p_sample, ev_w_in, ev_pool_w, ev_pool_scale, ev_w_out, od_w_in, od_ln_g, od_ln_b,
              od_ws, od_bs, od_conv_w, od_w_out, norm_mix, norm_ffn, norm_ple, ffn_w1, ffn_w2,
              ple_w_proj, ple_w_gate, norm_final):
    def run(x, p, pos, cache_kv, st_pool, st_conv):
        is_prompt = cache_kv is None
        r = x
        new_kv = [[] for _ in DIL_CFG]
        new_pool, new_conv, new_cv = [], [], []
        for i in range(DEPTH):
            h = rmsnorm(r, norm_mix[i])
            if i % 2 == 0:
                e = i // 2
                bufs = None if is_prompt else tuple(c[e] for c in cache_kv)
                pctx = None if is_prompt else st_pool[e]
                y, kvs, pst = even_mixer(h, pos, pctx, bufs, ev_w_in[e], ev_pool_w[e], ev_pool_scale[e], ev_w_out[e])
                for g in range(len(DIL_CFG)):
                    new_kv[g].append(kvs[g])
                new_pool.append(pst)
            else:
                o = i // 2
                cctx = None if is_prompt else st_conv[o]
                y, cst, cv = odd_mixer(h, pos, cctx, is_prompt, od_w_in[o], od_ln_g[o], od_ln_b[o],
                                       od_ws[o], od_bs[o], od_conv_w[o], od_w_out[o])
                new_conv.append(cst)
                new_cv.append(cv)
            r = r + y
            h = rmsnorm(r, norm_ffn[i])
            r = r + jnp.einsum('nlf,fd->nld', jnp.square(jax.nn.relu(jnp.einsum('nld,df->nlf', h, ffn_w1[i]))), ffn_w2[i])
            h = rmsnorm(r, norm_ple[i])
            gate = jax.nn.sigmoid(jnp.einsum('nld,de->nle', h, ple_w_gate[i]))
            r = r + gate * jnp.einsum('nlp,pd->nld', p[i], ple_w_proj[i])
        out = rmsnorm(r, norm_final)
        kv_out = [jnp.stack(lst, axis=0) for lst in new_kv]
        return out, kv_out, jnp.stack(new_pool, axis=0), jnp.stack(new_conv, axis=0), jnp.stack(new_cv, axis=0)

    pos_p = jnp.arange(SEQ)
    pos_s = PAST_LEN + jnp.arange(DEC_SEQ)
    y_prompt, kv_p, pool_p, conv_p, _ = run(x_prompt, p_prompt, pos_p, None, None, None)
    y_sample, kv_s, pool_s, conv_s, cv_s = run(x_sample, p_sample, pos_s,
                                               (cache_kv_w128, cache_kv_w512, cache_kv_w2048),
                                               state_pool, state_conv)
    return (y_prompt, y_sample, kv_p[0], kv_p[1], kv_p[2], kv_s[0], kv_s[1], kv_s[2],
            pool_p, pool_s, conv_p, conv_s, cv_s)
```

```python
import functools
import math

import jax
import jax.numpy as jnp
from jax import lax
from jax.experimental import pallas as pl
from jax.experimental.pallas import tpu as pltpu

F32 = jnp.float32
BF16 = jnp.bfloat16

D_MODEL = 1024
D_FF = 4 * D_MODEL
PLE_DIM = 256
EPS = 1e-6
ROPE_THETA = 10000.0
PAST_LEN = 2048

SLAB = 512
POOL_WINDOWS = (2, 4, 8, 16)
POOL_GROUP = 128
POOL_STATE = 15
POOL_HALO = 16
DIL_CFG = ((128, 1), (512, 4), (2048, 16))
HEADS = 8
HEAD_DIM = 64
ATT_BLOCK = 128
ATT_ROWS = 512
CHUNK = 128
C_GROUPS = 4
CONV_W = 3
CONV_HALO = 8
LANES = 128
NEG = -1e30

VMEM_LIMIT = 52 * 1024 * 1024


def _params(*sem):
    return pltpu.CompilerParams(dimension_semantics=sem, vmem_limit_bytes=VMEM_LIMIT)


def _rms(x, g):
    ms = jnp.mean(x * x, axis=-1, keepdims=True)
    return x * lax.rsqrt(ms + EPS) * g


def _gelu(x):
    c = math.sqrt(2.0 / math.pi)
    return 0.5 * x * (1.0 + jnp.tanh(c * (x + 0.044715 * (x * x * x))))


def _proj_even_kernel(x_ref, g_ref, w_ref, cos_ref, sin_ref, o_ref, hn_ref):
    j = pl.program_id(1)

    @pl.when(j == 0)
    def _():
        hn_ref[...] = _rms(x_ref[...], g_ref[...]).astype(BF16)

    z = jnp.dot(hn_ref[...], w_ref[...], preferred_element_type=F32)
    is_rope = jnp.logical_and(j >= 1, (j + 2) % 3 != 2)

    @pl.when(is_rope)
    def _():
        cos = cos_ref[...]
        sin = sin_ref[...]
        lane = lax.broadcasted_iota(jnp.int32, cos.shape, 1)
        first_half = jnp.bitwise_and(lane, HEAD_DIM - 1) < (HEAD_DIM // 2)
        for c in range(SLAB // LANES):
            zc = z[:, c * LANES:(c + 1) * LANES]
            partner = jnp.where(first_half,
                                pltpu.roll(zc, LANES - HEAD_DIM // 2, 1),
                                pltpu.roll(zc, HEAD_DIM // 2, 1))
            o_ref[:, c * LANES:(c + 1) * LANES] = zc * cos + partner * sin

    @pl.when(jnp.logical_not(is_rope))
    def _():
        o_ref[...] = z


def _proj_even(x, g, w, cos, sin, tm):
    T = x.shape[0]
    ns = w.shape[1] // SLAB
    ntab = cos.shape[0] // tm
    return pl.pallas_call(
        _proj_even_kernel,
        grid=(T // tm, ns),
        in_specs=[
            pl.BlockSpec((tm, D_MODEL), lambda i, j: (i, 0)),
            pl.BlockSpec((1, D_MODEL), lambda i, j: (0, 0)),
            pl.BlockSpec((D_MODEL, SLAB), lambda i, j: (0, j)),
            pl.BlockSpec((tm, LANES), lambda i, j: (i % ntab, 0)),
            pl.BlockSpec((tm, LANES), lambda i, j: (i % ntab, 0)),
        ],
        out_specs=pl.BlockSpec((None, tm, SLAB), lambda i, j: (j, i, 0)),
        out_shape=jax.ShapeDtypeStruct((ns, T, SLAB), F32),
        scratch_shapes=[pltpu.VMEM((tm, D_MODEL), BF16)],
        compiler_params=_params("arbitrary", "arbitrary"),
        name="proj_even",
    )(x, g, w, cos, sin)


def _proj_odd_kernel(x_ref, g_ref, w_ref, lng_ref, lnb_ref, o_ref, hn_ref):
    j = pl.program_id(1)

    @pl.when(j == 0)
    def _():
        hn_ref[...] = _rms(x_ref[...], g_ref[...]).astype(BF16)

    z = jnp.dot(hn_ref[...], w_ref[...], preferred_element_type=F32)

    @pl.when(j == 0)
    def _():
        o_ref[...] = _gelu(z)

    @pl.when(j == 1)
    def _():
        for c in range(C_GROUPS):
            sl = slice(c * LANES, (c + 1) * LANES)
            v = _gelu(z[:, sl])
            mu = jnp.mean(v, axis=-1, keepdims=True)
            dv = v - mu
            var = jnp.mean(dv * dv, axis=-1, keepdims=True)
            o_ref[:, sl] = dv * lax.rsqrt(var + EPS) * lng_ref[:, sl] + lnb_ref[:, sl]

    @pl.when(j >= 2)
    def _():
        o_ref[...] = z


def _proj_odd(x, g, w, ln_g, ln_b, tm):
    T = x.shape[0]
    ns = w.shape[1] // SLAB
    return pl.pallas_call(
        _proj_odd_kernel,
        grid=(T // tm, ns),
        in_specs=[
            pl.BlockSpec((tm, D_MODEL), lambda i, j: (i, 0)),
            pl.BlockSpec((1, D_MODEL), lambda i, j: (0, 0)),
            pl.BlockSpec((D_MODEL, SLAB), lambda i, j: (0, j)),
            pl.BlockSpec((1, SLAB), lambda i, j: (0, 0)),
            pl.BlockSpec((1, SLAB), lambda i, j: (0, 0)),
        ],
        out_specs=pl.BlockSpec((None, tm, SLAB), lambda i, j: (j, i, 0)),
        out_shape=jax.ShapeDtypeStruct((ns, T, SLAB), F32),
        scratch_shapes=[pltpu.VMEM((tm, D_MODEL), BF16)],
        compiler_params=_params("arbitrary", "arbitrary"),
        name="proj_odd",
    )(x, g, w, ln_g, ln_b)


def _attn_prompt_kernel(q_ref, kp_ref, kc_ref, vp_ref, vc_ref, o_ref, lse_ref,
                        qs_ref, ks_ref, vs_ref):
    step = pl.program_id(2)
    scale = HEAD_DIM ** -0.5
    qs_ref[...] = (q_ref[...] * scale).astype(BF16)
    ks_ref[0:ATT_BLOCK, :] = kp_ref[...].astype(BF16)
    ks_ref[ATT_BLOCK:, :] = kc_ref[...].astype(BF16)
    vs_ref[0:ATT_BLOCK, :] = vp_ref[...].astype(BF16)
    vs_ref[ATT_BLOCK:, :] = vc_ref[...].astype(BF16)

    qi = lax.broadcasted_iota(jnp.int32, (ATT_BLOCK, 2 * ATT_BLOCK), 0)
    ki = lax.broadcasted_iota(jnp.int32, (ATT_BLOCK, 2 * ATT_BLOCK), 1)
    rel = qi + ATT_BLOCK - ki
    band = jnp.logical_and(rel >= 0, rel <= ATT_BLOCK)
    lane = lax.broadcasted_iota(jnp.int32, (ATT_BLOCK, LANES), 1)
    low_head = lane < HEAD_DIM

    def body(qb, carry):
        row0 = pl.multiple_of(qb * ATT_BLOCK, ATT_BLOCK)
        first = jnp.logical_and(step == 0, qb == 0)
        valid = jnp.logical_and(band, jnp.logical_or(jnp.logical_not(first), ki >= ATT_BLOCK))
        lse_tile = jnp.zeros((ATT_BLOCK, LANES), F32)
        for hp in range(HEADS // 2):
            cols = slice(hp * LANES, (hp + 1) * LANES)
            q2 = qs_ref[pl.ds(row0, ATT_BLOCK), cols]
            k2 = ks_ref[pl.ds(row0, 2 * ATT_BLOCK), cols]
            v2 = vs_ref[pl.ds(row0, 2 * ATT_BLOCK), cols]
            outs = []
            for hh in range(2):
                sel = low_head if hh == 0 else jnp.logical_not(low_head)
                qm = jnp.where(sel, q2, jnp.zeros_like(q2))
                s = lax.dot_general(qm, k2, (((1,), (1,)), ((), ())),
                                    preferred_element_type=F32)
                s = jnp.where(valid, s, NEG)
                m = jnp.max(s, axis=-1, keepdims=True)
                e = jnp.exp(s - m)
                l = jnp.sum(e, axis=-1, keepdims=True)
                pv = jnp.dot(e.astype(BF16), v2, preferred_element_type=F32)
                outs.append(pv * (1.0 / l))
                lse_tile = jnp.where(lane == 2 * hp + hh, m + jnp.log(l), lse_tile)
            o_ref[pl.ds(row0, ATT_BLOCK), cols] = jnp.where(low_head, outs[0], outs[1])
        lse_ref[pl.ds(row0, ATT_BLOCK), :] = lse_tile
        return carry

    lax.fori_loop(0, ATT_ROWS // ATT_BLOCK, body, 0)


def _attn_prompt(z, g, dil, n_seq, seq):
    T = n_seq * seq
    m_rows = seq // dil
    steps = m_rows // ATT_ROWS
    sub = ATT_ROWS // ATT_BLOCK
    zv = z.reshape(z.shape[0], T // dil, dil * SLAB)
    sq, sk, sv = 1 + 3 * g, 2 + 3 * g, 3 + 3 * g

    def cur(slab):
        return pl.BlockSpec((None, ATT_ROWS, SLAB), lambda n, r, b: (slab, n * steps + b, r))

    def prev(slab):
        return pl.BlockSpec(
            (None, ATT_BLOCK, SLAB),
            lambda n, r, b: (slab, jnp.maximum((n * steps + b) * sub - 1, 0), r))

    o, lse = pl.pallas_call(
        _attn_prompt_kernel,
        grid=(n_seq, dil, steps),
        in_specs=[cur(sq), prev(sk), cur(sk), prev(sv), cur(sv)],
        out_specs=[
            pl.BlockSpec((ATT_ROWS, SLAB), lambda n, r, b: (n * steps + b, r)),
            pl.BlockSpec((ATT_ROWS, LANES), lambda n, r, b: (n * steps + b, r)),
        ],
        out_shape=[
            jax.ShapeDtypeStruct((T // dil, dil * SLAB), F32),
            jax.ShapeDtypeStruct((T // dil, dil * LANES), F32),
        ],
        scratch_shapes=[
            pltpu.VMEM((ATT_ROWS, SLAB), BF16),
            pltpu.VMEM((ATT_ROWS + ATT_BLOCK, SLAB), BF16),
            pltpu.VMEM((ATT_ROWS + ATT_BLOCK, SLAB), BF16),
        ],
        compiler_params=_params("arbitrary", "arbitrary", "arbitrary"),
        name=f"attn_prompt_d{dil}",
    )(zv, zv, zv, zv, zv)
    return o.reshape(T, SLAB), lse.reshape(T, LANES)


def _pool_mix(window_terms, a_cols, cnt, pw_ref, scale_ref, gi):
    acc = a_cols
    for term in window_terms:
        acc = acc + term
    pooled = acc / cnt - a_cols
    cols = slice(gi * POOL_GROUP, (gi + 1) * POOL_GROUP)
    mixed = jnp.dot(pooled.astype(BF16), pw_ref[gi], preferred_element_type=F32)
    return mixed * scale_ref[:, cols]


def _even_out_prompt_kernel(a_ref, halo_ref, o0_ref, o1_ref, o2_ref, l0_ref, l1_ref, l2_ref,
                            pw_ref, ps_ref, wo_ref, x_ref, out_ref, ext_ref, ya_ref, yb_ref,
                            *, tiles_per_seq):
    tm = a_ref.shape[0]
    it = pl.program_id(0) % tiles_per_seq
    halo = halo_ref[...]
    ext_ref[0:POOL_HALO, :] = jnp.where(it == 0, jnp.zeros_like(halo), halo)
    ext_ref[POOL_HALO:, :] = a_ref[...]
    pos = it * tm + lax.broadcasted_iota(jnp.int32, (tm, 1), 0)
    for gi, w in enumerate(POOL_WINDOWS):
        cols = slice(gi * POOL_GROUP, (gi + 1) * POOL_GROUP)
        terms = [ext_ref[POOL_HALO - k:POOL_HALO - k + tm, cols] for k in range(1, w)]
        cnt = jnp.minimum(w, pos + 1).astype(F32)
        ya_ref[:, cols] = _pool_mix(terms, a_ref[:, cols], cnt, pw_ref, ps_ref, gi).astype(BF16)

    l0, l1, l2 = l0_ref[...], l1_ref[...], l2_ref[...]
    mx = jnp.maximum(jnp.maximum(l0, l1), l2)
    e0, e1, e2 = jnp.exp(l0 - mx), jnp.exp(l1 - mx), jnp.exp(l2 - mx)
    inv = 1.0 / (e0 + e1 + e2)
    w0, w1, w2 = e0 * inv, e1 * inv, e2 * inv
    for h in range(HEADS):
        cols = slice(h * HEAD_DIM, (h + 1) * HEAD_DIM)
        yb = (w0[:, h:h + 1] * o0_ref[:, cols] + w1[:, h:h + 1] * o1_ref[:, cols]
              + w2[:, h:h + 1] * o2_ref[:, cols])
        yb_ref[:, cols] = yb.astype(BF16)

    y = jnp.dot(ya_ref[...], wo_ref[0:SLAB, :], preferred_element_type=F32)
    y = y + jnp.dot(yb_ref[...], wo_ref[SLAB:, :], preferred_element_type=F32)
    out_ref[...] = x_ref[...] + y


def _even_out_prompt(z, attn, pool_w, pool_scale, w_out, x, seq, tm):
    T = x.shape[0]
    tiles_per_seq = seq // tm
    hb = tm // POOL_HALO
    tile512 = pl.BlockSpec((tm, SLAB), lambda i: (i, 0))
    tile128 = pl.BlockSpec((tm, LANES), lambda i: (i, 0))
    (o0, s0), (o1, s1), (o2, s2) = attn
    return pl.pallas_call(
        functools.partial(_even_out_prompt_kernel, tiles_per_seq=tiles_per_seq),
        grid=(T // tm,),
        in_specs=[
            pl.BlockSpec((None, tm, SLAB), lambda i: (0, i, 0)),
            pl.BlockSpec((None, POOL_HALO, SLAB), lambda i: (0, jnp.maximum(i * hb - 1, 0), 0)),
            tile512, tile512, tile512, tile128, tile128, tile128,
            pl.BlockSpec((len(POOL_WINDOWS), POOL_GROUP, POOL_GROUP), lambda i: (0, 0, 0)),
            pl.BlockSpec((1, SLAB), lambda i: (0, 0)),
            pl.BlockSpec((2 * SLAB, D_MODEL), lambda i: (0, 0)),
            pl.BlockSpec((tm, D_MODEL), lambda i: (i, 0)),
        ],
        out_specs=pl.BlockSpec((tm, D_MODEL), lambda i: (i, 0)),
        out_shape=jax.ShapeDtypeStruct((T, D_MODEL), F32),
        scratch_shapes=[
            pltpu.VMEM((tm + POOL_HALO, SLAB), F32),
            pltpu.VMEM((tm, SLAB), BF16),
            pltpu.VMEM((tm, SLAB), BF16),
        ],
        compiler_params=_params("arbitrary"),
        name="even_out_prompt",
    )(z, z, o0, o1, o2, s0, s1, s2, pool_w, pool_scale, w_out, x)


def _odd_out_prompt_kernel(u_ref, vn_ref, go_ref, gi_ref, xi_ref, gih_ref, xih_ref,
                           ws_ref, bs_ref, cw_ref, wo_ref, x_ref, out_ref, hd_tail_ref,
                           ext_ref, yc_ref, yd_ref, *, tiles_per_seq):
    tm = u_ref.shape[0]
    it = pl.program_id(0) % tiles_per_seq

    ti = lax.broadcasted_iota(jnp.int32, (CHUNK, CHUNK), 0)
    si = lax.broadcasted_iota(jnp.int32, (CHUNK, CHUNK), 1)
    for g in range(C_GROUPS):
        cols = slice(g * LANES, (g + 1) * LANES)
        wm = jnp.where(si <= ti, ws_ref[g], 0.0).astype(BF16)
        for c in range(tm // CHUNK):
            rows = slice(c * CHUNK, (c + 1) * CHUNK)
            sp = jnp.dot(wm, vn_ref[rows, cols].astype(BF16), preferred_element_type=F32)
            sp = sp + bs_ref[:, cols]
            yc_ref[rows, cols] = (u_ref[rows, cols] * sp).astype(BF16)

    hd = gi_ref[...] * xi_ref[...]
    halo = gih_ref[...] * xih_ref[...]
    ext_ref[0:CONV_HALO, :] = jnp.where(it == 0, jnp.zeros_like(halo), halo)
    ext_ref[CONV_HALO:, :] = hd
    conv = cw_ref[CONV_W - 1:CONV_W, :] * hd
    for j in range(CONV_W - 1):
        off = CONV_HALO - (CONV_W - 1) + j
        conv = conv + cw_ref[j:j + 1, :] * ext_ref[off:off + tm, :]
    yd_ref[...] = (go_ref[...] * conv).astype(BF16)
    hd_tail_ref[...] = ext_ref[tm:tm + CONV_HALO, :]

    y = jnp.dot(yc_ref[...], wo_ref[0:SLAB, :], preferred_element_type=F32)
    y = y + jnp.dot(yd_ref[...], wo_ref[SLAB:, :], preferred_element_type=F32)
    out_ref[...] = x_ref[...] + y


def _odd_out_prompt(z, ws, bs_rows, conv_w, w_out, x, seq, tm):
    T = x.shape[0]
    tiles_per_seq = seq // tm
    hb = tm // CONV_HALO

    def slab(s):
        return pl.BlockSpec((None, tm, SLAB), lambda i: (s, i, 0))

    def halo(s):
        return pl.BlockSpec((None, CONV_HALO, SLAB), lambda i: (s, jnp.maximum(i * hb - 1, 0), 0))

    return pl.pallas_call(
        functools.partial(_odd_out_prompt_kernel, tiles_per_seq=tiles_per_seq),
        grid=(T // tm,),
        in_specs=[
            slab(0), slab(1), slab(2), slab(3), slab(4), halo(3), halo(4),
            pl.BlockSpec((C_GROUPS, CHUNK, CHUNK), lambda i: (0, 0, 0)),
            pl.BlockSpec((CHUNK, SLAB), lambda i: (0, 0)),
            pl.BlockSpec((CONV_W, SLAB), lambda i: (0, 0)),
            pl.BlockSpec((2 * SLAB, D_MODEL), lambda i: (0, 0)),
            pl.BlockSpec((tm, D_MODEL), lambda i: (i, 0)),
        ],
        out_specs=[
            pl.BlockSpec((tm, D_MODEL), lambda i: (i, 0)),
            pl.BlockSpec((None, CONV_HALO, SLAB), lambda i: (i, 0, 0)),
        ],
        out_shape=[
            jax.ShapeDtypeStruct((T, D_MODEL), F32),
            jax.ShapeDtypeStruct((T // tm, CONV_HALO, SLAB), F32),
        ],
        scratch_shapes=[
            pltpu.VMEM((tm + CONV_HALO, SLAB), F32),
            pltpu.VMEM((tm, SLAB), BF16),
            pltpu.VMEM((tm, SLAB), BF16),
        ],
        compiler_params=_params("arbitrary"),
        name="odd_out_prompt",
    )(z, z, z, z, z, z, z, ws, bs_rows, conv_w, w_out, x)


def _ffn_ple_kernel(x_ref, p_ref, gf_ref, w1_ref, w2_ref, gp_ref, wg_ref, wp_ref, gl_ref,
                    out_ref, hn_ref, acc_ref, *, final_norm):
    f = pl.program_id(1)

    @pl.when(f == 0)
    def _():
        hn_ref[...] = _rms(x_ref[...], gf_ref[...]).astype(BF16)
        acc_ref[...] = jnp.zeros_like(acc_ref)

    h1 = jnp.dot(hn_ref[...], w1_ref[...], preferred_element_type=F32)
    h1 = jnp.square(jnp.maximum(h1, 0.0)).astype(BF16)
    acc_ref[...] += jnp.dot(h1, w2_ref[...], preferred_element_type=F32)

    @pl.when(f == pl.num_programs(1) - 1)
    def _():
        r = x_ref[...] + acc_ref[...]
        hp = _rms(r, gp_ref[...]).astype(BF16)
        gate = jax.nn.sigmoid(jnp.dot(hp, wg_ref[...], preferred_element_type=F32))
        proj = jnp.dot(p_ref[...].astype(BF16), wp_ref[...], preferred_element_type=F32)
        r = r + gate * proj
        if final_norm:
            r = _rms(r, gl_ref[...])
        out_ref[...] = r


def _ffn_ple(x, p, g_ffn, w1, w2, g_ple, wg, wp, g_last, final_norm, tm, tf):
    T = x.shape[0]
    const2 = lambda i, f: (0, 0)
    return pl.pallas_call(
        functools.partial(_ffn_ple_kernel, final_norm=final_norm),
        grid=(T // tm, D_FF // tf),
        in_specs=[
            pl.BlockSpec((tm, D_MODEL), lambda i, f: (i, 0)),
            pl.BlockSpec((tm, PLE_DIM), lambda i, f: (i, 0)),
            pl.BlockSpec((1, D_MODEL), const2),
            pl.BlockSpec((D_MODEL, tf), lambda i, f: (0, f)),
            pl.BlockSpec((tf, D_MODEL), lambda i, f: (f, 0)),
            pl.BlockSpec((1, D_MODEL), const2),
            pl.BlockSpec((D_MODEL, D_MODEL), const2),
            pl.BlockSpec((PLE_DIM, D_MODEL), const2),
            pl.BlockSpec((1, D_MODEL), const2),
        ],
        out_specs=pl.BlockSpec((tm, D_MODEL), lambda i, f: (i, 0)),
        out_shape=jax.ShapeDtypeStruct((T, D_MODEL), F32),
        scratch_shapes=[pltpu.VMEM((tm, D_MODEL), BF16), pltpu.VMEM((tm, D_MODEL), F32)],
        compiler_params=_params("arbitrary", "arbitrary"),
        name="ffn_ple",
    )(x, p, g_ffn, w1, w2, g_ple, wg, wp, g_last)


def _attn_sample_kernel(q_ref, kv_ref, c0_ref, c1_ref, c2_ref, seg_ref, exp_ref, o_ref):
    n_tok = q_ref.shape[1]
    scale = HEAD_DIM ** -0.5
    seg = seg_ref[...]
    spread = exp_ref[...]
    row = lax.broadcasted_iota(jnp.int32, (ATT_BLOCK, LANES), 0)
    trow = lax.broadcasted_iota(jnp.int32, (n_tok, LANES), 0)

    def scores(k, q):
        return jnp.dot((k * q).astype(BF16), seg, preferred_element_type=F32)

    def weighted(e, v):
        ex = jnp.dot(e.astype(BF16), spread, preferred_element_type=F32)
        return ex * v, ex

    q0 = q_ref[0] * scale
    q1 = q_ref[1] * scale
    q2 = q_ref[2] * scale
    kn0, vn0 = kv_ref[0], kv_ref[1]
    sn0 = [jnp.where(trow >= tp, scores(kn0[tp:tp + 1, :], q0), NEG) for tp in range(n_tok)]
    sn1 = scores(kv_ref[2], q1)
    sn2 = scores(kv_ref[4], q2)
    m_new = jnp.maximum(sn1, sn2)
    for s in sn0:
        m_new = jnp.maximum(m_new, s)

    def cache_kv(t):
        lo = t * 2 * SLAB
        return ((c0_ref[:, 0:SLAB], c0_ref[:, SLAB:2 * SLAB]),
                (c1_ref[:, lo:lo + SLAB], c1_ref[:, lo + SLAB:lo + 2 * SLAB]),
                (c2_ref[:, lo:lo + SLAB], c2_ref[:, lo + SLAB:lo + 2 * SLAB]))

    m_all, s_cache = [], []
    for t in range(n_tok):
        kvs = cache_kv(t)
        st = [
            jnp.where(row >= t, scores(kvs[0][0], q0[t:t + 1, :]), NEG),
            scores(kvs[1][0], q1[t:t + 1, :]),
            scores(kvs[2][0], q2[t:t + 1, :]),
        ]
        m = m_new[t:t + 1, :]
        for s in st:
            m = jnp.maximum(m, jnp.max(s, axis=0, keepdims=True))
        m_all.append(m)
        s_cache.append(st)

    m_rows = jnp.zeros((n_tok, LANES), F32)
    for t in range(n_tok):
        m_rows = jnp.where(trow == t, m_all[t], m_rows)
    num_new = jnp.zeros((n_tok, SLAB), F32)
    den_new = jnp.zeros((n_tok, SLAB), F32)
    for tp in range(n_tok):
        a, b = weighted(jnp.exp(sn0[tp] - m_rows), vn0[tp:tp + 1, :])
        num_new, den_new = num_new + a, den_new + b
    for sn, vn in ((sn1, kv_ref[3]), (sn2, kv_ref[5])):
        a, b = weighted(jnp.exp(sn - m_rows), vn)
        num_new, den_new = num_new + a, den_new + b

    for t in range(n_tok):
        kvs = cache_kv(t)
        num = num_new[t:t + 1, :]
        den = den_new[t:t + 1, :]
        for g in range(3):
            a, b = weighted(jnp.exp(s_cache[t][g] - m_all[t]), kvs[g][1])
            num = num + jnp.sum(a, axis=0, keepdims=True)
            den = den + jnp.sum(b, axis=0, keepdims=True)
        o_ref[t:t + 1, :] = num / den


def _attn_sample(qn, kvn, c0, c1, c2):
    n_seq, _, n_tok, _ = qn.shape
    d1, d2 = DIL_CFG[1][1], DIL_CFG[2][1]
    c0v = c0.reshape(n_seq, ATT_BLOCK, 2 * SLAB)
    c1v = c1.reshape(n_seq, ATT_BLOCK, d1 * 2 * SLAB)
    c2v = c2.reshape(n_seq, ATT_BLOCK, d2 * 2 * SLAB)
    head_of_lane = jnp.arange(SLAB) // HEAD_DIM
    seg = (head_of_lane[:, None] == jnp.arange(LANES)[None, :]).astype(BF16)
    return pl.pallas_call(
        _attn_sample_kernel,
        grid=(n_seq,),
        in_specs=[
            pl.BlockSpec((None, 3, n_tok, SLAB), lambda n: (n, 0, 0, 0)),
            pl.BlockSpec((None, 6, n_tok, SLAB), lambda n: (n, 0, 0, 0)),
            pl.BlockSpec((None, ATT_BLOCK, 2 * SLAB), lambda n: (n, 0, 0)),
            pl.BlockSpec((None, ATT_BLOCK, n_tok * 2 * SLAB), lambda n: (n, 0, 0)),
            pl.BlockSpec((None, ATT_BLOCK, n_tok * 2 * SLAB), lambda n: (n, 0, 0)),
            pl.BlockSpec((SLAB, LANES), lambda n: (0, 0)),
            pl.BlockSpec((LANES, SLAB), lambda n: (0, 0)),
        ],
        out_specs=pl.BlockSpec((None, n_tok, SLAB), lambda n: (n, 0, 0)),
        out_shape=jax.ShapeDtypeStruct((n_seq, n_tok, SLAB), F32),
        compiler_params=_params("arbitrary"),
        name="attn_sample",
    )(qn, kvn, c0v, c1v, c2v, seg, seg.T)


def _even_out_sample_kernel(a_ref, ctx_ref, yb_ref, pw_ref, ps_ref, wo_ref, x_ref, out_ref,
                            *, n_seq, n_tok):
    def ext_row(e, cols):
        if e >= POOL_STATE:
            t = e - POOL_STATE
            return a_ref[t * n_seq:(t + 1) * n_seq, cols]
        return ctx_ref[:, e * SLAB + cols.start:e * SLAB + cols.stop]

    for t in range(n_tok):
        rows = slice(t * n_seq, (t + 1) * n_seq)
        ya = []
        for gi, w in enumerate(POOL_WINDOWS):
            cols = slice(gi * POOL_GROUP, (gi + 1) * POOL_GROUP)
            terms = [ext_row(POOL_STATE + t - k, cols) for k in range(1, w)]
            ya.append(_pool_mix(terms, a_ref[rows, cols], float(w), pw_ref, ps_ref, gi))
        y = jnp.zeros((n_seq, D_MODEL), F32)
        for gi in range(len(POOL_WINDOWS)):
            y = y + jnp.dot(ya[gi].astype(BF16), wo_ref[gi * POOL_GROUP:(gi + 1) * POOL_GROUP, :],
                            preferred_element_type=F32)
        yb = yb_ref[:, t * SLAB:(t + 1) * SLAB].astype(BF16)
        y = y + jnp.dot(yb, wo_ref[SLAB:, :], preferred_element_type=F32)
        out_ref[rows, :] = x_ref[rows, :] + y


def _even_out_sample(z, ctx, yb, pool_w, pool_scale, w_out, x, n_seq, n_tok):
    T = x.shape[0]
    full = lambda *shape: pl.BlockSpec(shape, lambda i: (0,) * len(shape))
    return pl.pallas_call(
        functools.partial(_even_out_sample_kernel, n_seq=n_seq, n_tok=n_tok),
        grid=(1,),
        in_specs=[
            pl.BlockSpec((None, T, SLAB), lambda i: (0, 0, 0)),
            full(n_seq, POOL_STATE * SLAB),
            full(n_seq, n_tok * SLAB),
            full(len(POOL_WINDOWS), POOL_GROUP, POOL_GROUP),
            full(1, SLAB),
            full(2 * SLAB, D_MODEL),
            full(T, D_MODEL),
        ],
        out_specs=full(T, D_MODEL),
        out_shape=jax.ShapeDtypeStruct((T, D_MODEL), F32),
        compiler_params=_params("arbitrary"),
        name="even_out_sample",
    )(z, ctx, yb, pool_w, pool_scale, w_out, x)


def _odd_out_sample_kernel(z_ref, ctx_ref, coef_ref, bias_ref, cw_ref, wo_ref, x_ref,
                           out_ref, hd_ref, *, n_seq, n_tok, mix_terms):
    hd_ref[...] = z_ref[3] * z_ref[4]

    def ext_row(e):
        if e >= CONV_W - 1:
            t = e - (CONV_W - 1)
            return hd_ref[t * n_seq:(t + 1) * n_seq, :]
        return ctx_ref[:, e * SLAB:(e + 1) * SLAB]

    for t in range(n_tok):
        rows = slice(t * n_seq, (t + 1) * n_seq)
        sp = jnp.zeros((n_seq, SLAB), F32) + bias_ref[t:t + 1, :]
        for s in mix_terms[t]:
            r = t * n_tok + s
            sp = sp + coef_ref[r:r + 1, :] * z_ref[1, s * n_seq:(s + 1) * n_seq, :]
        yc = z_ref[0, rows, :] * sp
        conv = jnp.zeros((n_seq, SLAB), F32)
        for j in range(CONV_W):
            conv = conv + cw_ref[j:j + 1, :] * ext_row(t + j)
        yd = z_ref[2, rows, :] * conv
        y = jnp.dot(yc.astype(BF16), wo_ref[0:SLAB, :], preferred_element_type=F32)
        y = y + jnp.dot(yd.astype(BF16), wo_ref[SLAB:, :], preferred_element_type=F32)
        out_ref[rows, :] = x_ref[rows, :] + y


def _odd_out_sample(z, ctx, coef, bias, conv_w, w_out, x, n_seq, n_tok, mix_terms):
    T = x.shape[0]
    full = lambda *shape: pl.BlockSpec(shape, lambda i: (0,) * len(shape))
    return pl.pallas_call(
        functools.partial(_odd_out_sample_kernel, n_seq=n_seq, n_tok=n_tok, mix_terms=mix_terms),
        grid=(1,),
        in_specs=[
            full(5, T, SLAB),
            full(n_seq, (CONV_W - 1) * SLAB),
            full(n_tok * n_tok, SLAB),
            full(n_tok, SLAB),
            full(CONV_W, SLAB),
            full(2 * SLAB, D_MODEL),
            full(T, D_MODEL),
        ],
        out_specs=[full(T, D_MODEL), full(T, SLAB)],
        out_shape=[jax.ShapeDtypeStruct((T, D_MODEL), F32), jax.ShapeDtypeStruct((T, SLAB), F32)],
        compiler_params=_params("arbitrary"),
        name="odd_out_sample",
    )(z, ctx, coef, bias, conv_w, w_out, x)


def _rope_tables(pos):
    half = HEAD_DIM // 2
    inv = jnp.power(jnp.float32(ROPE_THETA), -jnp.arange(half, dtype=F32) / half)
    ang = pos.astype(F32)[:, None] * inv[None, :]
    cos = jnp.cos(ang)
    sin = jnp.sin(ang)
    cos_t = jnp.tile(jnp.concatenate([cos, cos], axis=-1), (1, LANES // HEAD_DIM))
    sin_t = jnp.tile(jnp.concatenate([-sin, sin], axis=-1), (1, LANES // HEAD_DIM))
    return cos_t, sin_t


def kernel(x_prompt, x_sample, cache_kv_w128, cache_kv_w512, cache_kv_w2048, state_pool, state_conv,
           p_prompt, p_sample, ev_w_in, ev_pool_w, ev_pool_scale, ev_w_out, od_w_in, od_ln_g, od_ln_b,
           od_ws, od_bs, od_conv_w, od_w_out, norm_mix, norm_ffn, norm_ple, ffn_w1, ffn_w2,
           ple_w_proj, ple_w_gate, norm_final):
    n_p, seq, _ = x_prompt.shape
    n_s, n_tok, _ = x_sample.shape
    depth = norm_mix.shape[0]
    tp = n_p * seq
    ts = n_s * n_tok

    bf = lambda w: w.astype(BF16)
    row = lambda v: v.reshape(1, -1)
    ev_w_in_b, ev_pool_w_b, ev_w_out_b = bf(ev_w_in), bf(ev_pool_w), bf(ev_w_out)
    od_w_in_b, od_w_out_b = bf(od_w_in), bf(od_w_out)
    w1_b, w2_b, wg_b, wp_b = bf(ffn_w1), bf(ffn_w2), bf(ple_w_gate), bf(ple_w_proj)

    cos_p, sin_p = _rope_tables(jnp.arange(seq))
    pos_s = [PAST_LEN + t for t in range(n_tok)]
    cos_s, sin_s = _rope_tables(jnp.repeat(jnp.asarray(pos_s, jnp.int32), n_s))

    mix_terms = tuple(
        tuple(s for s in range(n_tok)
              if pos_s[s] // CHUNK == pos_s[t] // CHUNK and pos_s[s] % CHUNK <= pos_s[t] % CHUNK)
        for t in range(n_tok))
    local = [p % CHUNK for p in pos_s]

    rp = x_prompt.reshape(tp, D_MODEL)
    rs = x_sample.transpose(1, 0, 2).reshape(ts, D_MODEL)
    pp = p_prompt.reshape(depth, tp, PLE_DIM)
    ps = p_sample.transpose(0, 2, 1, 3).reshape(depth, ts, PLE_DIM)

    tm_p, tm_mix, tm_ffn, tf = 1024, 512, 512, 512
    kv_p = [[] for _ in DIL_CFG]
    kv_s = [[] for _ in DIL_CFG]
    pool_p, pool_s, conv_p, conv_s, cv_s = [], [], [], [], []

    for i in range(depth):
        g_mix = row(norm_mix[i])
        if i % 2 == 0:
            e = i // 2
            pscale = row(ev_pool_scale[e])
            z = _proj_even(rp, g_mix, ev_w_in_b[e], cos_p, sin_p, tm_p)
            attn = [_attn_prompt(z, g, dil, n_p, seq) for g, (_, dil) in enumerate(DIL_CFG)]
            rp = _even_out_prompt(z, attn, ev_pool_w_b[e], pscale, ev_w_out_b[e], rp, seq, tm_mix)
            z4 = z.reshape(z.shape[0], n_p, seq, SLAB)
            for g, (win, _) in enumerate(DIL_CFG):
                keep = min(win, seq)
                k = z4[2 + 3 * g, :, seq - keep:].reshape(n_p, keep, HEADS, HEAD_DIM)
                v = z4[3 + 3 * g, :, seq - keep:].reshape(n_p, keep, HEADS, HEAD_DIM)
                kv_p[g].append(jnp.stack([k, v], axis=2))
            pool_p.append(z4[0, :, seq - POOL_STATE:])
            zs = _proj_even(rs, g_mix, ev_w_in_b[e], cos_s, sin_s, ts)
            zn = zs.reshape(zs.shape[0], n_tok, n_s, SLAB).transpose(2, 0, 1, 3)
            qn = zn[:, 1::3]
            kvn = jnp.stack([zn[:, 2::3], zn[:, 3::3]], axis=2).reshape(n_s, 6, n_tok, SLAB)
            yb = _attn_sample(qn, kvn, cache_kv_w128[e], cache_kv_w512[e], cache_kv_w2048[e])
            rs = _even_out_sample(zs, state_pool[e].reshape(n_s, POOL_STATE * SLAB),
                                  yb.reshape(n_s, n_tok * SLAB), ev_pool_w_b[e], pscale,
                                  ev_w_out_b[e], rs, n_s, n_tok)
            for g in range(len(DIL_CFG)):
                k = zn[:, 2 + 3 * g].reshape(n_s, n_tok, HEADS, HEAD_DIM)
                v = zn[:, 3 + 3 * g].reshape(n_s, n_tok, HEADS, HEAD_DIM)
                kv_s[g].append(jnp.stack([k, v], axis=2))
            pool_s.append(jnp.concatenate([state_pool[e], zn[:, 0]], axis=1)[:, -POOL_STATE:])
        else:
            o = i // 2
            ln_g, ln_b = row(od_ln_g[o]), row(od_ln_b[o])
            bs_rows = jnp.repeat(od_bs[o].T, LANES, axis=1)
            z = _proj_odd(rp, g_mix, od_w_in_b[o], ln_g, ln_b, tm_p)
            rp, hd_tail = _odd_out_prompt(z, od_ws[o], bs_rows, od_conv_w[o], od_w_out_b[o], rp,
                                          seq, tm_mix)
            tails = hd_tail.reshape(n_p, seq // tm_mix, CONV_HALO, SLAB)
            conv_p.append(tails[:, -1, CONV_HALO - (CONV_W - 1):])
            zs = _proj_odd(rs, g_mix, od_w_in_b[o], ln_g, ln_b, ts)
            coef = jnp.stack([jnp.repeat(od_ws[o][:, local[t], local[s]], LANES)
                              for t in range(n_tok) for s in range(n_tok)])
            bias = jnp.stack([bs_rows[local[t]] for t in range(n_tok)])
            rs, hd = _odd_out_sample(zs, state_conv[o].reshape(n_s, (CONV_W - 1) * SLAB), coef, bias,
                                     od_conv_w[o], od_w_out_b[o], rs, n_s, n_tok, mix_terms)
            hd_n = hd.reshape(n_tok, n_s, SLAB).transpose(1, 0, 2)
            conv_s.append(jnp.concatenate([state_conv[o], hd_n], axis=1)[:, -(CONV_W - 1):])
            cv_s.append(zs[1].reshape(n_tok, n_s, SLAB).transpose(1, 0, 2))

        last = i == depth - 1
        args = (row(norm_ffn[i]), w1_b[i], w2_b[i], row(norm_ple[i]), wg_b[i], wp_b[i],
                row(norm_final), last)
        rp = _ffn_ple(rp, pp[i], *args, tm_ffn, tf)
        rs = _ffn_ple(rs, ps[i], *args, ts, tf)

    y_prompt = rp.reshape(n_p, seq, D_MODEL)
    y_sample = rs.reshape(n_tok, n_s, D_MODEL).transpose(1, 0, 2)
    st = lambda lst: jnp.stack(lst, axis=0)
    return (y_prompt, y_sample, st(kv_p[0]), st(kv_p[1]), st(kv_p[2]),
            st(kv_s[0]), st(kv_s[1]), st(kv_s[2]),
            st(pool_p), st(pool_s), st(conv_p), st(conv_s), st(cv_s))
```

```python
import functools
import math

import jax
import jax.numpy as jnp
from jax import lax
from jax.experimental import pallas as pl
from jax.experimental.pallas import tpu as pltpu

F32 = jnp.float32
BF16 = jnp.bfloat16

D_MODEL = 1024
D_FF = 4 * D_MODEL
PLE_DIM = 256
EPS = 1e-6
ROPE_THETA = 10000.0
PAST_LEN = 2048

SLAB = 512
POOL_WINDOWS = (2, 4, 8, 16)
POOL_GROUP = 128
POOL_STATE = 15
POOL_HALO = 16
DIL_CFG = ((128, 1), (512, 4), (2048, 16))
HEADS = 8
HEAD_DIM = 64
ATT_BLOCK = 128
ATT_ROWS = 512
CHUNK = 128
C_GROUPS = 4
CONV_W = 3
CONV_HALO = 8
LANES = 128
NEG = -1e30

VMEM_LIMIT = 52 * 1024 * 1024


def _params(*sem):
    return pltpu.CompilerParams(dimension_semantics=sem, vmem_limit_bytes=VMEM_LIMIT)


def _rms(x, g):
    ms = jnp.mean(x * x, axis=-1, keepdims=True)
    return x * lax.rsqrt(ms + EPS) * g


def _gelu(x):
    c = math.sqrt(2.0 / math.pi)
    return 0.5 * x * (1.0 + jnp.tanh(c * (x + 0.044715 * (x * x * x))))


def _proj_even_kernel(x_ref, g_ref, w_ref, cos_ref, sin_ref, za_ref, g0_ref, g1_ref, g2_ref,
                      hn_ref, zs_ref, *, dils):
    j = pl.program_id(1)
    tm = x_ref.shape[0]

    @pl.when(j == 0)
    def _():
        hn_ref[...] = _rms(x_ref[...], g_ref[...]).astype(BF16)

    z = jnp.dot(hn_ref[...], w_ref[...], preferred_element_type=F32)

    @pl.when(j == 0)
    def _():
        za_ref[...] = z

    def rotated(zc):
        lane = lax.broadcasted_iota(jnp.int32, zc.shape, 1)
        first_half = jnp.bitwise_and(lane, HEAD_DIM - 1) < (HEAD_DIM // 2)
        partner = jnp.where(first_half,
                            pltpu.roll(zc, LANES - HEAD_DIM // 2, 1),
                            pltpu.roll(zc, HEAD_DIM // 2, 1))
        return zc * cos_ref[...] + partner * sin_ref[...]

    for g, (out_ref, dil) in enumerate(zip((g0_ref, g1_ref, g2_ref), dils)):
        for c in range(3):
            @pl.when(j == 1 + 3 * g + c)
            def _(out_ref=out_ref, dil=dil, c=c):
                for lc in range(SLAB // LANES):
                    cols = slice(lc * LANES, (lc + 1) * LANES)
                    zc = z[:, cols]
                    if c < 2:
                        zc = rotated(zc)
                    if dil == 1:
                        out_ref[:, cols] = zc
                    else:
                        zs_ref[lc] = zc
                        for r in range(dil):
                            out_ref[r, :, cols] = zs_ref[lc, pl.ds(r, tm // dil, stride=dil), :]


def _proj_even(x, g, w, cos, sin, tm, n_seq, dils):
    T = x.shape[0]
    ns = w.shape[1] // SLAB
    ntab = cos.shape[0] // tm
    tps = T // n_seq // tm

    def group_spec(g, dil):
        sel = lambda j: jnp.clip(j - (1 + 3 * g), 0, 2)
        if dil == 1:
            return pl.BlockSpec((None, tm, SLAB), lambda i, j: (sel(j), i, 0))
        return pl.BlockSpec((None, None, dil, tm // dil, SLAB),
                            lambda i, j: (sel(j), i // tps, 0, i % tps, 0))

    def group_shape(dil):
        if dil == 1:
            return jax.ShapeDtypeStruct((3, T, SLAB), F32)
        return jax.ShapeDtypeStruct((3, n_seq, dil, T // n_seq // dil, SLAB), F32)

    return pl.pallas_call(
        functools.partial(_proj_even_kernel, dils=dils),
        grid=(T // tm, ns),
        in_specs=[
            pl.BlockSpec((tm, D_MODEL), lambda i, j: (i, 0)),
            pl.BlockSpec((1, D_MODEL), lambda i, j: (0, 0)),
            pl.BlockSpec((D_MODEL, SLAB), lambda i, j: (0, j)),
            pl.BlockSpec((tm, LANES), lambda i, j: (i % ntab, 0)),
            pl.BlockSpec((tm, LANES), lambda i, j: (i % ntab, 0)),
        ],
        out_specs=[pl.BlockSpec((tm, SLAB), lambda i, j: (i, 0))]
        + [group_spec(g, d) for g, d in enumerate(dils)],
        out_shape=[jax.ShapeDtypeStruct((T, SLAB), F32)] + [group_shape(d) for d in dils],
        scratch_shapes=[pltpu.VMEM((tm, D_MODEL), BF16),
                        pltpu.VMEM((SLAB // LANES, tm, LANES), F32)],
        compiler_params=_params("arbitrary", "arbitrary"),
        name="proj_even",
    )(x, g, w, cos, sin)


def _proj_odd_kernel(x_ref, g_ref, w_ref, lng_ref, lnb_ref, o_ref, hn_ref):
    j = pl.program_id(1)

    @pl.when(j == 0)
    def _():
        hn_ref[...] = _rms(x_ref[...], g_ref[...]).astype(BF16)

    z = jnp.dot(hn_ref[...], w_ref[...], preferred_element_type=F32)

    @pl.when(j == 0)
    def _():
        o_ref[...] = _gelu(z)

    @pl.when(j == 1)
    def _():
        for c in range(C_GROUPS):
            sl = slice(c * LANES, (c + 1) * LANES)
            v = _gelu(z[:, sl])
            mu = jnp.mean(v, axis=-1, keepdims=True)
            dv = v - mu
            var = jnp.mean(dv * dv, axis=-1, keepdims=True)
            o_ref[:, sl] = dv * lax.rsqrt(var + EPS) * lng_ref[:, sl] + lnb_ref[:, sl]

    @pl.when(j >= 2)
    def _():
        o_ref[...] = z


def _proj_odd(x, g, w, ln_g, ln_b, tm):
    T = x.shape[0]
    ns = w.shape[1] // SLAB
    return pl.pallas_call(
        _proj_odd_kernel,
        grid=(T // tm, ns),
        in_specs=[
            pl.BlockSpec((tm, D_MODEL), lambda i, j: (i, 0)),
            pl.BlockSpec((1, D_MODEL), lambda i, j: (0, 0)),
            pl.BlockSpec((D_MODEL, SLAB), lambda i, j: (0, j)),
            pl.BlockSpec((1, SLAB), lambda i, j: (0, 0)),
            pl.BlockSpec((1, SLAB), lambda i, j: (0, 0)),
        ],
        out_specs=pl.BlockSpec((None, tm, SLAB), lambda i, j: (j, i, 0)),
        out_shape=jax.ShapeDtypeStruct((ns, T, SLAB), F32),
        scratch_shapes=[pltpu.VMEM((tm, D_MODEL), BF16)],
        compiler_params=_params("arbitrary", "arbitrary"),
        name="proj_odd",
    )(x, g, w, ln_g, ln_b)


def _attn_prompt_kernel(q_ref, kp_ref, kc_ref, vp_ref, vc_ref, o_ref, lse_ref,
                        qs_ref, ks_ref, vs_ref):
    step = pl.program_id(2)
    scale = HEAD_DIM ** -0.5
    qs_ref[...] = (q_ref[...] * scale).astype(BF16)
    ks_ref[0:ATT_BLOCK, :] = kp_ref[...].astype(BF16)
    ks_ref[ATT_BLOCK:, :] = kc_ref[...].astype(BF16)
    vs_ref[0:ATT_BLOCK, :] = vp_ref[...].astype(BF16)
    vs_ref[ATT_BLOCK:, :] = vc_ref[...].astype(BF16)

    qi = lax.broadcasted_iota(jnp.int32, (ATT_BLOCK, 2 * ATT_BLOCK), 0)
    ki = lax.broadcasted_iota(jnp.int32, (ATT_BLOCK, 2 * ATT_BLOCK), 1)
    rel = qi + ATT_BLOCK - ki
    band = jnp.logical_and(rel >= 0, rel <= ATT_BLOCK)
    lane = lax.broadcasted_iota(jnp.int32, (ATT_BLOCK, LANES), 1)
    low_head = lane < HEAD_DIM

    def body(qb, carry):
        row0 = pl.multiple_of(qb * ATT_BLOCK, ATT_BLOCK)
        first = jnp.logical_and(step == 0, qb == 0)
        valid = jnp.logical_and(band, jnp.logical_or(jnp.logical_not(first), ki >= ATT_BLOCK))
        lse_tile = jnp.zeros((ATT_BLOCK, LANES), F32)
        for hp in range(HEADS // 2):
            cols = slice(hp * LANES, (hp + 1) * LANES)
            q2 = qs_ref[pl.ds(row0, ATT_BLOCK), cols]
            k2 = ks_ref[pl.ds(row0, 2 * ATT_BLOCK), cols]
            v2 = vs_ref[pl.ds(row0, 2 * ATT_BLOCK), cols]
            outs = []
            for hh in range(2):
                sel = low_head if hh == 0 else jnp.logical_not(low_head)
                qm = jnp.where(sel, q2, jnp.zeros_like(q2))
                s = lax.dot_general(qm, k2, (((1,), (1,)), ((), ())),
                                    preferred_element_type=F32)
                s = jnp.where(valid, s, NEG)
                m = jnp.max(s, axis=-1, keepdims=True)
                e = jnp.exp(s - m)
                l = jnp.sum(e, axis=-1, keepdims=True)
                pv = jnp.dot(e.astype(BF16), v2, preferred_element_type=F32)
                outs.append(pv * (1.0 / l))
                lse_tile = jnp.where(lane == 2 * hp + hh, m + jnp.log(l), lse_tile)
            o_ref[pl.ds(row0, ATT_BLOCK), cols] = jnp.where(low_head, outs[0], outs[1])
        lse_ref[pl.ds(row0, ATT_BLOCK), :] = lse_tile
        return carry

    lax.fori_loop(0, ATT_ROWS // ATT_BLOCK, body, 0)


def _attn_prompt(qkv, dil):
    _, n_seq, _, m_rows, _ = qkv.shape
    steps = m_rows // ATT_ROWS
    sub = ATT_ROWS // ATT_BLOCK

    def cur(slab):
        return pl.BlockSpec((None, None, None, ATT_ROWS, SLAB), lambda n, r, b: (slab, n, r, b, 0))

    def prev(slab):
        return pl.BlockSpec((None, None, None, ATT_BLOCK, SLAB),
                            lambda n, r, b: (slab, n, r, jnp.maximum(b * sub - 1, 0), 0))

    return pl.pallas_call(
        _attn_prompt_kernel,
        grid=(n_seq, dil, steps),
        in_specs=[cur(0), prev(1), cur(1), prev(2), cur(2)],
        out_specs=[
            pl.BlockSpec((None, None, ATT_ROWS, SLAB), lambda n, r, b: (n, r, b, 0)),
            pl.BlockSpec((None, None, ATT_ROWS, LANES), lambda n, r, b: (n, r, b, 0)),
        ],
        out_shape=[
            jax.ShapeDtypeStruct((n_seq, dil, m_rows, SLAB), F32),
            jax.ShapeDtypeStruct((n_seq, dil, m_rows, LANES), F32),
        ],
        scratch_shapes=[
            pltpu.VMEM((ATT_ROWS, SLAB), BF16),
            pltpu.VMEM((ATT_ROWS + ATT_BLOCK, SLAB), BF16),
            pltpu.VMEM((ATT_ROWS + ATT_BLOCK, SLAB), BF16),
        ],
        compiler_params=_params("arbitrary", "arbitrary", "arbitrary"),
        name=f"attn_prompt_d{dil}",
    )(qkv, qkv, qkv, qkv, qkv)


def _pool_mix(window_terms, a_cols, cnt, pw_ref, scale_ref, gi):
    acc = a_cols
    for term in window_terms:
        acc = acc + term
    pooled = acc / cnt - a_cols
    cols = slice(gi * POOL_GROUP, (gi + 1) * POOL_GROUP)
    mixed = jnp.dot(pooled.astype(BF16), pw_ref[gi], preferred_element_type=F32)
    return mixed * scale_ref[:, cols]


def _even_out_prompt_kernel(a_ref, halo_ref, o0_ref, o1_ref, o2_ref, l0_ref, l1_ref, l2_ref,
                            pw_ref, ps_ref, wo_ref, x_ref, out_ref, ext_ref, ya_ref, yb_ref,
                            oi_ref, li_ref, *, tiles_per_seq):
    tm = a_ref.shape[0]
    it = pl.program_id(0) % tiles_per_seq
    halo = halo_ref[...]
    ext_ref[0:POOL_HALO, :] = jnp.where(it == 0, jnp.zeros_like(halo), halo)
    ext_ref[POOL_HALO:, :] = a_ref[...]
    pos = it * tm + lax.broadcasted_iota(jnp.int32, (tm, 1), 0)
    for gi, w in enumerate(POOL_WINDOWS):
        cols = slice(gi * POOL_GROUP, (gi + 1) * POOL_GROUP)
        terms = [ext_ref[POOL_HALO - k:POOL_HALO - k + tm, cols] for k in range(1, w)]
        cnt = jnp.minimum(w, pos + 1).astype(F32)
        ya_ref[:, cols] = _pool_mix(terms, a_ref[:, cols], cnt, pw_ref, ps_ref, gi).astype(BF16)

    for g, (o_ref, l_ref) in enumerate(((o0_ref, l0_ref), (o1_ref, l1_ref), (o2_ref, l2_ref))):
        dil = o_ref.shape[0]
        for r in range(dil):
            rows = pl.ds(r, tm // dil, stride=dil)
            li_ref[g, rows, :] = l_ref[r]
            for lc in range(SLAB // LANES):
                oi_ref[g, lc, rows, :] = o_ref[r, :, lc * LANES:(lc + 1) * LANES]

    l0, l1, l2 = li_ref[0], li_ref[1], li_ref[2]
    mx = jnp.maximum(jnp.maximum(l0, l1), l2)
    e0, e1, e2 = jnp.exp(l0 - mx), jnp.exp(l1 - mx), jnp.exp(l2 - mx)
    inv = 1.0 / (e0 + e1 + e2)
    w0, w1, w2 = e0 * inv, e1 * inv, e2 * inv
    for h in range(HEADS):
        lc, lo = divmod(h * HEAD_DIM, LANES)
        sl = slice(lo, lo + HEAD_DIM)
        yb = (w0[:, h:h + 1] * oi_ref[0, lc, :, sl] + w1[:, h:h + 1] * oi_ref[1, lc, :, sl]
              + w2[:, h:h + 1] * oi_ref[2, lc, :, sl])
        yb_ref[:, h * HEAD_DIM:(h + 1) * HEAD_DIM] = yb.astype(BF16)

    y = jnp.dot(ya_ref[...], wo_ref[0:SLAB, :], preferred_element_type=F32)
    y = y + jnp.dot(yb_ref[...], wo_ref[SLAB:, :], preferred_element_type=F32)
    out_ref[...] = x_ref[...] + y


def _even_out_prompt(za, attn, pool_w, pool_scale, w_out, x, seq, tm):
    T = x.shape[0]
    tps = seq // tm
    hb = tm // POOL_HALO

    def residue_spec(dil, width):
        return pl.BlockSpec((None, dil, tm // dil, width), lambda i: (i // tps, 0, i % tps, 0))

    dils = [o.shape[1] for o, _ in attn]
    return pl.pallas_call(
        functools.partial(_even_out_prompt_kernel, tiles_per_seq=tps),
        grid=(T // tm,),
        in_specs=[
            pl.BlockSpec((tm, SLAB), lambda i: (i, 0)),
            pl.BlockSpec((POOL_HALO, SLAB), lambda i: (jnp.maximum(i * hb - 1, 0), 0)),
        ]
        + [residue_spec(d, SLAB) for d in dils] + [residue_spec(d, LANES) for d in dils]
        + [
            pl.BlockSpec((len(POOL_WINDOWS), POOL_GROUP, POOL_GROUP), lambda i: (0, 0, 0)),
            pl.BlockSpec((1, SLAB), lambda i: (0, 0)),
            pl.BlockSpec((2 * SLAB, D_MODEL), lambda i: (0, 0)),
            pl.BlockSpec((tm, D_MODEL), lambda i: (i, 0)),
        ],
        out_specs=pl.BlockSpec((tm, D_MODEL), lambda i: (i, 0)),
        out_shape=jax.ShapeDtypeStruct((T, D_MODEL), F32),
        scratch_shapes=[
            pltpu.VMEM((tm + POOL_HALO, SLAB), F32),
            pltpu.VMEM((tm, SLAB), BF16),
            pltpu.VMEM((tm, SLAB), BF16),
            pltpu.VMEM((len(dils), SLAB // LANES, tm, LANES), F32),
            pltpu.VMEM((len(dils), tm, LANES), F32),
        ],
        compiler_params=_params("arbitrary"),
        name="even_out_prompt",
    )(za, za, *[o for o, _ in attn], *[l for _, l in attn], pool_w, pool_scale, w_out, x)


def _odd_out_prompt_kernel(u_ref, vn_ref, go_ref, gi_ref, xi_ref, gih_ref, xih_ref,
                           ws_ref, bs_ref, cw_ref, wo_ref, x_ref, out_ref, hd_tail_ref,
                           ext_ref, yc_ref, yd_ref, *, tiles_per_seq):
    tm = u_ref.shape[0]
    it = pl.program_id(0) % tiles_per_seq

    ti = lax.broadcasted_iota(jnp.int32, (CHUNK, CHUNK), 0)
    si = lax.broadcasted_iota(jnp.int32, (CHUNK, CHUNK), 1)
    for g in range(C_GROUPS):
        cols = slice(g * LANES, (g + 1) * LANES)
        wm = jnp.where(si <= ti, ws_ref[g], 0.0).astype(BF16)
        for c in range(tm // CHUNK):
            rows = slice(c * CHUNK, (c + 1) * CHUNK)
            sp = jnp.dot(wm, vn_ref[rows, cols].astype(BF16), preferred_element_type=F32)
            sp = sp + bs_ref[:, cols]
            yc_ref[rows, cols] = (u_ref[rows, cols] * sp).astype(BF16)

    hd = gi_ref[...] * xi_ref[...]
    halo = gih_ref[...] * xih_ref[...]
    ext_ref[0:CONV_HALO, :] = jnp.where(it == 0, jnp.zeros_like(halo), halo)
    ext_ref[CONV_HALO:, :] = hd
    conv = cw_ref[CONV_W - 1:CONV_W, :] * hd
    for j in range(CONV_W - 1):
        off = CONV_HALO - (CONV_W - 1) + j
        conv = conv + cw_ref[j:j + 1, :] * ext_ref[off:off + tm, :]
    yd_ref[...] = (go_ref[...] * conv).astype(BF16)
    hd_tail_ref[...] = ext_ref[tm:tm + CONV_HALO, :]

    y = jnp.dot(yc_ref[...], wo_ref[0:SLAB, :], preferred_element_type=F32)
    y = y + jnp.dot(yd_ref[...], wo_ref[SLAB:, :], preferred_element_type=F32)
    out_ref[...] = x_ref[...] + y


def _odd_out_prompt(z, ws, bs_rows, conv_w, w_out, x, seq, tm):
    T = x.shape[0]
    tiles_per_seq = seq // tm
    hb = tm // CONV_HALO

    def slab(s):
        return pl.BlockSpec((None, tm, SLAB), lambda i: (s, i, 0))

    def halo(s):
        return pl.BlockSpec((None, CONV_HALO, SLAB), lambda i: (s, jnp.maximum(i * hb - 1, 0), 0))

    return pl.pallas_call(
        functools.partial(_odd_out_prompt_kernel, tiles_per_seq=tiles_per_seq),
        grid=(T // tm,),
        in_specs=[
            slab(0), slab(1), slab(2), slab(3), slab(4), halo(3), halo(4),
            pl.BlockSpec((C_GROUPS, CHUNK, CHUNK), lambda i: (0, 0, 0)),
            pl.BlockSpec((CHUNK, SLAB), lambda i: (0, 0)),
            pl.BlockSpec((CONV_W, SLAB), lambda i: (0, 0)),
            pl.BlockSpec((2 * SLAB, D_MODEL), lambda i: (0, 0)),
            pl.BlockSpec((tm, D_MODEL), lambda i: (i, 0)),
        ],
        out_specs=[
            pl.BlockSpec((tm, D_MODEL), lambda i: (i, 0)),
            pl.BlockSpec((None, CONV_HALO, SLAB), lambda i: (i, 0, 0)),
        ],
        out_shape=[
            jax.ShapeDtypeStruct((T, D_MODEL), F32),
            jax.ShapeDtypeStruct((T // tm, CONV_HALO, SLAB), F32),
        ],
        scratch_shapes=[
            pltpu.VMEM((tm + CONV_HALO, SLAB), F32),
            pltpu.VMEM((tm, SLAB), BF16),
            pltpu.VMEM((tm, SLAB), BF16),
        ],
        compiler_params=_params("arbitrary"),
        name="odd_out_prompt",
    )(z, z, z, z, z, z, z, ws, bs_rows, conv_w, w_out, x)


def _ffn_ple_kernel(x_ref, p_ref, gf_ref, w1_ref, w2_ref, gp_ref, wg_ref, wp_ref, gl_ref,
                    out_ref, hn_ref, acc_ref, *, final_norm):
    f = pl.program_id(1)

    @pl.when(f == 0)
    def _():
        hn_ref[...] = _rms(x_ref[...], gf_ref[...]).astype(BF16)
        acc_ref[...] = jnp.zeros_like(acc_ref)

    h1 = jnp.dot(hn_ref[...], w1_ref[...], preferred_element_type=F32)
    h1 = jnp.square(jnp.maximum(h1, 0.0)).astype(BF16)
    acc_ref[...] += jnp.dot(h1, w2_ref[...], preferred_element_type=F32)

    @pl.when(f == pl.num_programs(1) - 1)
    def _():
        r = x_ref[...] + acc_ref[...]
        hp = _rms(r, gp_ref[...]).astype(BF16)
        gate = jax.nn.sigmoid(jnp.dot(hp, wg_ref[...], preferred_element_type=F32))
        proj = jnp.dot(p_ref[...].astype(BF16), wp_ref[...], preferred_element_type=F32)
        r = r + gate * proj
        if final_norm:
            r = _rms(r, gl_ref[...])
        out_ref[...] = r


def _ffn_ple(x, p, g_ffn, w1, w2, g_ple, wg, wp, g_last, final_norm, tm, tf):
    T = x.shape[0]
    const2 = lambda i, f: (0, 0)
    return pl.pallas_call(
        functools.partial(_ffn_ple_kernel, final_norm=final_norm),
        grid=(T // tm, D_FF // tf),
        in_specs=[
            pl.BlockSpec((tm, D_MODEL), lambda i, f: (i, 0)),
            pl.BlockSpec((tm, PLE_DIM), lambda i, f: (i, 0)),
            pl.BlockSpec((1, D_MODEL), const2),
            pl.BlockSpec((D_MODEL, tf), lambda i, f: (0, f)),
            pl.BlockSpec((tf, D_MODEL), lambda i, f: (f, 0)),
            pl.BlockSpec((1, D_MODEL), const2),
            pl.BlockSpec((D_MODEL, D_MODEL), const2),
            pl.BlockSpec((PLE_DIM, D_MODEL), const2),
            pl.BlockSpec((1, D_MODEL), const2),
        ],
        out_specs=pl.BlockSpec((tm, D_MODEL), lambda i, f: (i, 0)),
        out_shape=jax.ShapeDtypeStruct((T, D_MODEL), F32),
        scratch_shapes=[pltpu.VMEM((tm, D_MODEL), BF16), pltpu.VMEM((tm, D_MODEL), F32)],
        compiler_params=_params("arbitrary", "arbitrary"),
        name="ffn_ple",
    )(x, p, g_ffn, w1, w2, g_ple, wg, wp, g_last)


def _attn_sample_kernel(q_ref, kv_ref, c0_ref, c1_ref, c2_ref, o_ref):
    n_tok = q_ref.shape[1]
    n_rows = n_tok * HEADS
    scale = HEAD_DIM ** -0.5
    sub = lax.broadcasted_iota(jnp.int32, (HEADS, SLAB), 0)
    lane_head = lax.broadcasted_iota(jnp.int32, (HEADS, SLAB), 1) // HEAD_DIM
    own = sub == lane_head
    row_tok = lax.broadcasted_iota(jnp.int32, (n_rows, 1), 0) // HEADS

    caches = (c0_ref, c1_ref, c2_ref)
    dils = tuple(d for _, d in DIL_CFG)
    m = jnp.full((n_rows, 1), NEG, F32)
    s_cache, s_new = [], []
    for g, (c_ref, dil) in enumerate(zip(caches, dils)):
        qg = q_ref[g] * scale
        qbd = jnp.concatenate(
            [jnp.where(own, jnp.broadcast_to(qg[t:t + 1, :], (HEADS, SLAB)), 0.0)
             for t in range(n_tok)], axis=0)
        s = jnp.dot(qbd.astype(BF16), c_ref[0].astype(BF16), preferred_element_type=F32)
        pos = lax.broadcasted_iota(jnp.int32, s.shape, 1)
        valid = (pos >= row_tok) if dil == 1 else (jnp.bitwise_and(pos, dil - 1) == row_tok)
        s = jnp.where(valid, s, NEG)
        m = jnp.maximum(m, jnp.max(s, axis=1, keepdims=True))
        s_cache.append(s)
        kn = kv_ref[2 * g]
        for tp in range(n_tok):
            sn = jnp.sum(qbd * kn[tp:tp + 1, :], axis=1, keepdims=True)
            ok = (row_tok >= tp) if dil == 1 else (row_tok == tp)
            sn = jnp.where(ok, sn, NEG)
            m = jnp.maximum(m, sn)
            s_new.append((g, tp, sn))

    acc = jnp.zeros((n_rows, SLAB), F32)
    den = jnp.zeros((n_rows, 1), F32)
    for g, c_ref in enumerate(caches):
        e = jnp.exp(s_cache[g] - m)
        den = den + jnp.sum(e, axis=1, keepdims=True)
        acc = acc + lax.dot_general(e.astype(BF16), c_ref[1].astype(BF16),
                                    (((1,), (1,)), ((), ())), preferred_element_type=F32)
    for g, tp, sn in s_new:
        e = jnp.exp(sn - m)
        den = den + e
        acc = acc + e * kv_ref[2 * g + 1][tp:tp + 1, :]
    res = acc * (1.0 / den)
    for t in range(n_tok):
        rows = res[t * HEADS:(t + 1) * HEADS, :]
        o_ref[t:t + 1, :] = jnp.sum(jnp.where(own, rows, 0.0), axis=0, keepdims=True)


def _attn_sample(qn, kvn, c0, c1, c2):
    n_seq, _, n_tok, _ = qn.shape

    def cache_spec(c):
        return pl.BlockSpec((None, 2, SLAB, c.shape[-1]), lambda n: (n, 0, 0, 0))

    return pl.pallas_call(
        _attn_sample_kernel,
        grid=(n_seq,),
        in_specs=[
            pl.BlockSpec((None, 3, n_tok, SLAB), lambda n: (n, 0, 0, 0)),
            pl.BlockSpec((None, 6, n_tok, SLAB), lambda n: (n, 0, 0, 0)),
            cache_spec(c0), cache_spec(c1), cache_spec(c2),
        ],
        out_specs=pl.BlockSpec((None, n_tok, SLAB), lambda n: (n, 0, 0)),
        out_shape=jax.ShapeDtypeStruct((n_seq, n_tok, SLAB), F32),
        compiler_params=_params("arbitrary"),
        name="attn_sample",
    )(qn, kvn, c0, c1, c2)


def _even_out_sample_kernel(a_ref, ctx_ref, yb_ref, pw_ref, ps_ref, wo_ref, x_ref, out_ref,
                            *, n_seq, n_tok):
    def ext_row(e, cols):
        if e >= POOL_STATE:
            t = e - POOL_STATE
            return a_ref[t * n_seq:(t + 1) * n_seq, cols]
        return ctx_ref[e, :, cols]

    for t in range(n_tok):
        rows = slice(t * n_seq, (t + 1) * n_seq)
        ya = []
        for gi, w in enumerate(POOL_WINDOWS):
            cols = slice(gi * POOL_GROUP, (gi + 1) * POOL_GROUP)
            terms = [ext_row(POOL_STATE + t - k, cols) for k in range(1, w)]
            ya.append(_pool_mix(terms, a_ref[rows, cols], float(w), pw_ref, ps_ref, gi))
        y = jnp.zeros((n_seq, D_MODEL), F32)
        for gi in range(len(POOL_WINDOWS)):
            y = y + jnp.dot(ya[gi].astype(BF16), wo_ref[gi * POOL_GROUP:(gi + 1) * POOL_GROUP, :],
                            preferred_element_type=F32)
        yb = yb_ref[:, t * SLAB:(t + 1) * SLAB].astype(BF16)
        y = y + jnp.dot(yb, wo_ref[SLAB:, :], preferred_element_type=F32)
        out_ref[rows, :] = x_ref[rows, :] + y


def _even_out_sample(z, ctx, yb, pool_w, pool_scale, w_out, x, n_seq, n_tok):
    T = x.shape[0]
    full = lambda *shape: pl.BlockSpec(shape, lambda i: (0,) * len(shape))
    return pl.pallas_call(
        functools.partial(_even_out_sample_kernel, n_seq=n_seq, n_tok=n_tok),
        grid=(1,),
        in_specs=[
            full(T, SLAB),
            full(POOL_STATE, n_seq, SLAB),
            full(n_seq, n_tok * SLAB),
            full(len(POOL_WINDOWS), POOL_GROUP, POOL_GROUP),
            full(1, SLAB),
            full(2 * SLAB, D_MODEL),
            full(T, D_MODEL),
        ],
        out_specs=full(T, D_MODEL),
        out_shape=jax.ShapeDtypeStruct((T, D_MODEL), F32),
        compiler_params=_params("arbitrary"),
        name="even_out_sample",
    )(z, ctx, yb, pool_w, pool_scale, w_out, x)


def _odd_out_sample_kernel(z_ref, ctx_ref, coef_ref, bias_ref, cw_ref, wo_ref, x_ref,
                           out_ref, hd_ref, *, n_seq, n_tok, mix_terms):
    hd_ref[...] = z_ref[3] * z_ref[4]

    def ext_row(e):
        if e >= CONV_W - 1:
            t = e - (CONV_W - 1)
            return hd_ref[t * n_seq:(t + 1) * n_seq, :]
        return ctx_ref[:, e * SLAB:(e + 1) * SLAB]

    for t in range(n_tok):
        rows = slice(t * n_seq, (t + 1) * n_seq)
        sp = jnp.zeros((n_seq, SLAB), F32) + bias_ref[t:t + 1, :]
        for s in mix_terms[t]:
            r = t * n_tok + s
            sp = sp + coef_ref[r:r + 1, :] * z_ref[1, s * n_seq:(s + 1) * n_seq, :]
        yc = z_ref[0, rows, :] * sp
        conv = jnp.zeros((n_seq, SLAB), F32)
        for j in range(CONV_W):
            conv = conv + cw_ref[j:j + 1, :] * ext_row(t + j)
        yd = z_ref[2, rows, :] * conv
        y = jnp.dot(yc.astype(BF16), wo_ref[0:SLAB, :], preferred_element_type=F32)
        y = y + jnp.dot(yd.astype(BF16), wo_ref[SLAB:, :], preferred_element_type=F32)
        out_ref[rows, :] = x_ref[rows, :] + y


def _odd_out_sample(z, ctx, coef, bias, conv_w, w_out, x, n_seq, n_tok, mix_terms):
    T = x.shape[0]
    full = lambda *shape: pl.BlockSpec(shape, lambda i: (0,) * len(shape))
    return pl.pallas_call(
        functools.partial(_odd_out_sample_kernel, n_seq=n_seq, n_tok=n_tok, mix_terms=mix_terms),
        grid=(1,),
        in_specs=[
            full(5, T, SLAB),
            full(n_seq, (CONV_W - 1) * SLAB),
            full(n_tok * n_tok, SLAB),
            full(n_tok, SLAB),
            full(CONV_W, SLAB),
            full(2 * SLAB, D_MODEL),
            full(T, D_MODEL),
        ],
        out_specs=[full(T, D_MODEL), full(T, SLAB)],
        out_shape=[jax.ShapeDtypeStruct((T, D_MODEL), F32), jax.ShapeDtypeStruct((T, SLAB), F32)],
        compiler_params=_params("arbitrary"),
        name="odd_out_sample",
    )(z, ctx, coef, bias, conv_w, w_out, x)


def _rope_tables(pos):
    half = HEAD_DIM // 2
    inv = jnp.power(jnp.float32(ROPE_THETA), -jnp.arange(half, dtype=F32) / half)
    ang = pos.astype(F32)[:, None] * inv[None, :]
    cos = jnp.cos(ang)
    sin = jnp.sin(ang)
    cos_t = jnp.tile(jnp.concatenate([cos, cos], axis=-1), (1, LANES // HEAD_DIM))
    sin_t = jnp.tile(jnp.concatenate([-sin, sin], axis=-1), (1, LANES // HEAD_DIM))
    return cos_t, sin_t


def kernel(x_prompt, x_sample, cache_kv_w128, cache_kv_w512, cache_kv_w2048, state_pool, state_conv,
           p_prompt, p_sample, ev_w_in, ev_pool_w, ev_pool_scale, ev_w_out, od_w_in, od_ln_g, od_ln_b,
           od_ws, od_bs, od_conv_w, od_w_out, norm_mix, norm_ffn, norm_ple, ffn_w1, ffn_w2,
           ple_w_proj, ple_w_gate, norm_final):
    n_p, seq, _ = x_prompt.shape
    n_s, n_tok, _ = x_sample.shape
    depth = norm_mix.shape[0]
    tp = n_p * seq
    ts = n_s * n_tok

    bf = lambda w: w.astype(BF16)
    row = lambda v: v.reshape(1, -1)
    ev_w_in_b, ev_pool_w_b, ev_w_out_b = bf(ev_w_in), bf(ev_pool_w), bf(ev_w_out)
    od_w_in_b, od_w_out_b = bf(od_w_in), bf(od_w_out)
    w1_b, w2_b, wg_b, wp_b = bf(ffn_w1), bf(ffn_w2), bf(ple_w_gate), bf(ple_w_proj)

    cos_p, sin_p = _rope_tables(jnp.arange(seq))
    pos_s = [PAST_LEN + t for t in range(n_tok)]
    cos_s, sin_s = _rope_tables(jnp.repeat(jnp.asarray(pos_s, jnp.int32), n_s))

    mix_terms = tuple(
        tuple(s for s in range(n_tok)
              if pos_s[s] // CHUNK == pos_s[t] // CHUNK and pos_s[s] % CHUNK <= pos_s[t] % CHUNK)
        for t in range(n_tok))
    local = [p % CHUNK for p in pos_s]

    rp = x_prompt.reshape(tp, D_MODEL)
    rs = x_sample.transpose(1, 0, 2).reshape(ts, D_MODEL)
    pp = p_prompt.reshape(depth, tp, PLE_DIM)
    ps = p_sample.transpose(0, 2, 1, 3).reshape(depth, ts, PLE_DIM)

    tm_p, tm_mix, tm_ffn, tf = 1024, 512, 512, 512
    kv_p = [[] for _ in DIL_CFG]
    kv_s = [[] for _ in DIL_CFG]
    pool_p, pool_s, conv_p, conv_s, cv_s = [], [], [], [], []

    for i in range(depth):
        g_mix = row(norm_mix[i])
        if i % 2 == 0:
            e = i // 2
            pscale = row(ev_pool_scale[e])
            dils = tuple(d for _, d in DIL_CFG)
            za, *groups = _proj_even(rp, g_mix, ev_w_in_b[e], cos_p, sin_p, tm_p, n_p, dils)
            attn = []
            for grp, dil in zip(groups, dils):
                if dil == 1:
                    grp = grp.reshape(3, n_p, 1, seq, SLAB)
                attn.append(_attn_prompt(grp, dil))
            rp = _even_out_prompt(za, attn, ev_pool_w_b[e], pscale, ev_w_out_b[e], rp, seq, tm_mix)
            for g, ((win, dil), grp) in enumerate(zip(DIL_CFG, groups)):
                keep = min(win, seq)
                if dil == 1:
                    tail = grp.reshape(3, n_p, seq, SLAB)[1:, :, seq - keep:]
                else:
                    tail = grp[1:, :, :, (seq - keep) // dil:].transpose(0, 1, 3, 2, 4)
                tail = tail.reshape(2, n_p, keep, HEADS, HEAD_DIM)
                kv_p[g].append(jnp.stack([tail[0], tail[1]], axis=2))
            pool_p.append(za.reshape(n_p, seq, SLAB)[:, seq - POOL_STATE:])
            zas, *sgroups = _proj_even(rs, g_mix, ev_w_in_b[e], cos_s, sin_s, ts, 1, (1, 1, 1))
            zn = jnp.stack(sgroups).reshape(3, 3, n_tok, n_s, SLAB).transpose(3, 0, 1, 2, 4)
            qn = zn[:, :, 0]
            kvn = zn[:, :, 1:].reshape(n_s, 6, n_tok, SLAB)
            native = lambda c: c.transpose(0, 2, 3, 4, 1).reshape(n_s, 2, SLAB, c.shape[1])
            yb = _attn_sample(qn, kvn, native(cache_kv_w128[e]), native(cache_kv_w512[e]),
                              native(cache_kv_w2048[e]))
            rs = _even_out_sample(zas, state_pool[e].transpose(1, 0, 2),
                                  yb.reshape(n_s, n_tok * SLAB), ev_pool_w_b[e], pscale,
                                  ev_w_out_b[e], rs, n_s, n_tok)
            for g in range(len(DIL_CFG)):
                k = zn[:, g, 1].reshape(n_s, n_tok, HEADS, HEAD_DIM)
                v = zn[:, g, 2].reshape(n_s, n_tok, HEADS, HEAD_DIM)
                kv_s[g].append(jnp.stack([k, v], axis=2))
            a_n = zas.reshape(n_tok, n_s, SLAB).transpose(1, 0, 2)
            pool_s.append(jnp.concatenate([state_pool[e], a_n], axis=1)[:, -POOL_STATE:])
        else:
            o = i // 2
            ln_g, ln_b = row(od_ln_g[o]), row(od_ln_b[o])
            bs_rows = jnp.repeat(od_bs[o].T, LANES, axis=1)
            z = _proj_odd(rp, g_mix, od_w_in_b[o], ln_g, ln_b, tm_p)
            rp, hd_tail = _odd_out_prompt(z, od_ws[o], bs_rows, od_conv_w[o], od_w_out_b[o], rp,
                                          seq, tm_mix)
            tails = hd_tail.reshape(n_p, seq // tm_mix, CONV_HALO, SLAB)
            conv_p.append(tails[:, -1, CONV_HALO - (CONV_W - 1):])
            zs = _proj_odd(rs, g_mix, od_w_in_b[o], ln_g, ln_b, ts)
            coef = jnp.stack([jnp.repeat(od_ws[o][:, local[t], local[s]], LANES)
                              for t in range(n_tok) for s in range(n_tok)])
            bias = jnp.stack([bs_rows[local[t]] for t in range(n_tok)])
            rs, hd = _odd_out_sample(zs, state_conv[o].reshape(n_s, (CONV_W - 1) * SLAB), coef, bias,
                                     od_conv_w[o], od_w_out_b[o], rs, n_s, n_tok, mix_terms)
            hd_n = hd.reshape(n_tok, n_s, SLAB).transpose(1, 0, 2)
            conv_s.append(jnp.concatenate([state_conv[o], hd_n], axis=1)[:, -(CONV_W - 1):])
            cv_s.append(zs[1].reshape(n_tok, n_s, SLAB).transpose(1, 0, 2))

        last = i == depth - 1
        args = (row(norm_ffn[i]), w1_b[i], w2_b[i], row(norm_ple[i]), wg_b[i], wp_b[i],
                row(norm_final), last)
        rp = _ffn_ple(rp, pp[i], *args, tm_ffn, tf)
        rs = _ffn_ple(rs, ps[i], *args, ts, tf)

    y_prompt = rp.reshape(n_p, seq, D_MODEL)
    y_sample = rs.reshape(n_tok, n_s, D_MODEL).transpose(1, 0, 2)
    st = lambda lst: jnp.stack(lst, axis=0)
    return (y_prompt, y_sample, st(kv_p[0]), st(kv_p[1]), st(kv_p[2]),
            st(kv_s[0]), st(kv_s[1]), st(kv_s[2]),
            st(pool_p), st(pool_s), st(conv_p), st(conv_s), st(cv_s))
```

```python
import functools
import math

import jax
import jax.numpy as jnp
from jax import lax
from jax.experimental import pallas as pl
from jax.experimental.pallas import tpu as pltpu

F32 = jnp.float32
BF16 = jnp.bfloat16

D_MODEL = 1024
D_FF = 4 * D_MODEL
PLE_DIM = 256
EPS = 1e-6
ROPE_THETA = 10000.0
PAST_LEN = 2048

SLAB = 512
POOL_WINDOWS = (2, 4, 8, 16)
POOL_GROUP = 128
POOL_STATE = 15
POOL_HALO = 16
DIL_CFG = ((128, 1), (512, 4), (2048, 16))
HEADS = 8
HEAD_DIM = 64
ATT_BLOCK = 128
ATT_ROWS = 512
CHUNK = 128
C_GROUPS = 4
CONV_W = 3
CONV_HALO = 8
LANES = 128
NEG = -1e30

VMEM_LIMIT = 52 * 1024 * 1024


def _params(*sem):
    return pltpu.CompilerParams(dimension_semantics=sem, vmem_limit_bytes=VMEM_LIMIT)


def _rms(x, g):
    ms = jnp.mean(x * x, axis=-1, keepdims=True)
    return x * lax.rsqrt(ms + EPS) * g


def _gelu(x):
    c = math.sqrt(2.0 / math.pi)
    return 0.5 * x * (1.0 + jnp.tanh(c * (x + 0.044715 * (x * x * x))))


def _proj_even_kernel(x_ref, g_ref, w_ref, cos_ref, sin_ref, za_ref, g0_ref, g1_ref, g2_ref,
                      zs_ref, *, dils, split):
    tm = x_ref.shape[0]
    nl = SLAB // LANES
    hn = _rms(x_ref[...], g_ref[...]).astype(BF16)
    za_ref[...] = jnp.dot(hn, w_ref[:, 0:SLAB], preferred_element_type=F32)

    def rotate(chunks):
        cos = cos_ref[...]
        sin = sin_ref[...]
        if split:
            h = nl // 2
            return ([chunks[i] * cos - chunks[i + h] * sin for i in range(h)]
                    + [chunks[i] * cos + chunks[i - h] * sin for i in range(h, nl)])
        lane = lax.broadcasted_iota(jnp.int32, cos.shape, 1)
        first_half = jnp.bitwise_and(lane, HEAD_DIM - 1) < (HEAD_DIM // 2)
        out = []
        for zc in chunks:
            partner = jnp.where(first_half,
                                pltpu.roll(zc, LANES - HEAD_DIM // 2, 1),
                                pltpu.roll(zc, HEAD_DIM // 2, 1))
            out.append(zc * cos + partner * sin)
        return out

    slot = 0
    for g, (out_ref, dil) in enumerate(zip((g0_ref, g1_ref, g2_ref), dils)):
        for c in range(3):
            col0 = (1 + 3 * g + c) * SLAB
            z = jnp.dot(hn, w_ref[:, col0:col0 + SLAB], preferred_element_type=F32)
            chunks = [z[:, i * LANES:(i + 1) * LANES] for i in range(nl)]
            if c < 2:
                chunks = rotate(chunks)
            for i, zc in enumerate(chunks):
                cols = slice(i * LANES, (i + 1) * LANES)
                if dil == 1:
                    out_ref[c, :, cols] = zc
                else:
                    zs_ref[slot] = zc
                    for r in range(dil):
                        out_ref[c, r, :, cols] = zs_ref[slot, pl.ds(r, tm // dil, stride=dil), :]
                    slot += 1


def _resident(shape):
    return pl.BlockSpec(shape, lambda *_: (0,) * len(shape), pipeline_mode=pl.Buffered(1))


def _proj_even(x, g, w, cos, sin, tm, n_seq, dils, split):
    T = x.shape[0]
    ntab = cos.shape[0] // tm
    tps = T // n_seq // tm

    def group_spec(dil):
        if dil == 1:
            return pl.BlockSpec((3, tm, SLAB), lambda i: (0, i, 0))
        return pl.BlockSpec((3, None, dil, tm // dil, SLAB), lambda i: (0, i // tps, 0, i % tps, 0))

    def group_shape(dil):
        if dil == 1:
            return jax.ShapeDtypeStruct((3, T, SLAB), F32)
        return jax.ShapeDtypeStruct((3, n_seq, dil, T // n_seq // dil, SLAB), F32)

    n_slots = max(1, 3 * (SLAB // LANES) * sum(d > 1 for d in dils))
    return pl.pallas_call(
        functools.partial(_proj_even_kernel, dils=dils, split=split),
        grid=(T // tm,),
        in_specs=[
            pl.BlockSpec((tm, D_MODEL), lambda i: (i, 0)),
            _resident((1, D_MODEL)),
            _resident(w.shape),
            pl.BlockSpec((tm, LANES), lambda i: (i % ntab, 0)),
            pl.BlockSpec((tm, LANES), lambda i: (i % ntab, 0)),
        ],
        out_specs=[pl.BlockSpec((tm, SLAB), lambda i: (i, 0))] + [group_spec(d) for d in dils],
        out_shape=[jax.ShapeDtypeStruct((T, SLAB), F32)] + [group_shape(d) for d in dils],
        scratch_shapes=[pltpu.VMEM((n_slots, tm, LANES), F32)],
        compiler_params=_params("arbitrary"),
        name="proj_even",
    )(x, g, w, cos, sin)


def _proj_odd_kernel(x_ref, g_ref, w_ref, lng_ref, lnb_ref, o_ref):
    hn = _rms(x_ref[...], g_ref[...]).astype(BF16)

    def slab(s):
        return jnp.dot(hn, w_ref[:, s * SLAB:(s + 1) * SLAB], preferred_element_type=F32)

    o_ref[0] = _gelu(slab(0))
    zv = slab(1)
    for c in range(C_GROUPS):
        sl = slice(c * LANES, (c + 1) * LANES)
        v = _gelu(zv[:, sl])
        mu = jnp.mean(v, axis=-1, keepdims=True)
        dv = v - mu
        var = jnp.mean(dv * dv, axis=-1, keepdims=True)
        o_ref[1, :, sl] = dv * lax.rsqrt(var + EPS) * lng_ref[:, sl] + lnb_ref[:, sl]
    o_ref[2] = slab(2)
    o_ref[3] = slab(3) * slab(4)


def _proj_odd(x, g, w, ln_g, ln_b, tm):
    T = x.shape[0]
    return pl.pallas_call(
        _proj_odd_kernel,
        grid=(T // tm,),
        in_specs=[
            pl.BlockSpec((tm, D_MODEL), lambda i: (i, 0)),
            _resident((1, D_MODEL)),
            _resident(w.shape),
            _resident((1, SLAB)),
            _resident((1, SLAB)),
        ],
        out_specs=pl.BlockSpec((4, tm, SLAB), lambda i: (0, i, 0)),
        out_shape=jax.ShapeDtypeStruct((4, T, SLAB), F32),
        compiler_params=_params("arbitrary"),
        name="proj_odd",
    )(x, g, w, ln_g, ln_b)


def _attn_prompt_kernel(q_ref, kp_ref, kc_ref, vp_ref, vc_ref, o_ref, lse_ref,
                        qs_ref, ks_ref, vs_ref):
    step = pl.program_id(2)
    scale = HEAD_DIM ** -0.5
    half_rot = HEAD_DIM // 2
    heads_per_blk = LANES // half_rot
    n_blk = HEADS // heads_per_blk
    for j in range(n_blk):
        for part in range(2):
            src = slice((part * n_blk + j) * LANES, (part * n_blk + j + 1) * LANES)
            dst = slice((2 * j + part) * LANES, (2 * j + part + 1) * LANES)
            qs_ref[:, dst] = (q_ref[:, src] * scale).astype(BF16)
            ks_ref[0:ATT_BLOCK, dst] = kp_ref[:, src].astype(BF16)
            ks_ref[ATT_BLOCK:, dst] = kc_ref[:, src].astype(BF16)
    vs_ref[0:ATT_BLOCK, :] = vp_ref[...].astype(BF16)
    vs_ref[ATT_BLOCK:, :] = vc_ref[...].astype(BF16)

    qi = lax.broadcasted_iota(jnp.int32, (ATT_BLOCK, 2 * ATT_BLOCK), 0)
    ki = lax.broadcasted_iota(jnp.int32, (ATT_BLOCK, 2 * ATT_BLOCK), 1)
    rel = qi + ATT_BLOCK - ki
    band = jnp.logical_and(rel >= 0, rel <= ATT_BLOCK)
    lane = lax.broadcasted_iota(jnp.int32, (ATT_BLOCK, LANES), 1)
    low_head = lane < HEAD_DIM
    qlane = lax.broadcasted_iota(jnp.int32, (ATT_BLOCK, 2 * LANES), 1)
    head_in_blk = jnp.bitwise_and(qlane, LANES - 1) // half_rot

    def body(qb, carry):
        row0 = pl.multiple_of(qb * ATT_BLOCK, ATT_BLOCK)
        first = jnp.logical_and(step == 0, qb == 0)
        valid = jnp.logical_and(band, jnp.logical_or(jnp.logical_not(first), ki >= ATT_BLOCK))
        lse_tile = jnp.zeros((ATT_BLOCK, LANES), F32)
        outs = []
        for h in range(HEADS):
            j, hq = divmod(h, heads_per_blk)
            qcols = slice(2 * j * LANES, (2 * j + 2) * LANES)
            q2 = qs_ref[pl.ds(row0, ATT_BLOCK), qcols]
            k2 = ks_ref[pl.ds(row0, 2 * ATT_BLOCK), qcols]
            vcols = slice((h // 2) * LANES, (h // 2 + 1) * LANES)
            v2 = vs_ref[pl.ds(row0, 2 * ATT_BLOCK), vcols]
            qm = jnp.where(head_in_blk == hq, q2, jnp.zeros_like(q2))
            s = lax.dot_general(qm, k2, (((1,), (1,)), ((), ())), preferred_element_type=F32)
            s = jnp.where(valid, s, NEG)
            m = jnp.max(s, axis=-1, keepdims=True)
            e = jnp.exp(s - m)
            l = jnp.sum(e, axis=-1, keepdims=True)
            pv = jnp.dot(e.astype(BF16), v2, preferred_element_type=F32)
            outs.append(pv * (1.0 / l))
            lse_tile = jnp.where(lane == h, m + jnp.log(l), lse_tile)
            if h % 2 == 1:
                o_ref[pl.ds(row0, ATT_BLOCK), vcols] = jnp.where(low_head, outs[h - 1], outs[h])
        lse_ref[pl.ds(row0, ATT_BLOCK), :] = lse_tile
        return carry

    lax.fori_loop(0, ATT_ROWS // ATT_BLOCK, body, 0)


def _attn_prompt(qkv, dil):
    _, n_seq, _, m_rows, _ = qkv.shape
    steps = m_rows // ATT_ROWS
    sub = ATT_ROWS // ATT_BLOCK

    def cur(slab):
        return pl.BlockSpec((None, None, None, ATT_ROWS, SLAB), lambda n, r, b: (slab, n, r, b, 0))

    def prev(slab):
        return pl.BlockSpec((None, None, None, ATT_BLOCK, SLAB),
                            lambda n, r, b: (slab, n, r, jnp.maximum(b * sub - 1, 0), 0))

    return pl.pallas_call(
        _attn_prompt_kernel,
        grid=(n_seq, dil, steps),
        in_specs=[cur(0), prev(1), cur(1), prev(2), cur(2)],
        out_specs=[
            pl.BlockSpec((None, None, ATT_ROWS, SLAB), lambda n, r, b: (n, r, b, 0)),
            pl.BlockSpec((None, None, ATT_ROWS, LANES), lambda n, r, b: (n, r, b, 0)),
        ],
        out_shape=[
            jax.ShapeDtypeStruct((n_seq, dil, m_rows, SLAB), F32),
            jax.ShapeDtypeStruct((n_seq, dil, m_rows, LANES), F32),
        ],
        scratch_shapes=[
            pltpu.VMEM((ATT_ROWS, SLAB), BF16),
            pltpu.VMEM((ATT_ROWS + ATT_BLOCK, SLAB), BF16),
            pltpu.VMEM((ATT_ROWS + ATT_BLOCK, SLAB), BF16),
        ],
        compiler_params=_params("arbitrary", "arbitrary", "arbitrary"),
        name=f"attn_prompt_d{dil}",
    )(qkv, qkv, qkv, qkv, qkv)


def _pool_mix(window_terms, a_cols, cnt, pw_ref, scale_ref, gi):
    acc = a_cols
    for term in window_terms:
        acc = acc + term
    pooled = acc / cnt - a_cols
    cols = slice(gi * POOL_GROUP, (gi + 1) * POOL_GROUP)
    mixed = jnp.dot(pooled.astype(BF16), pw_ref[gi], preferred_element_type=F32)
    return mixed * scale_ref[:, cols]


def _even_out_prompt_kernel(a_ref, halo_ref, o0_ref, o1_ref, o2_ref, l0_ref, l1_ref, l2_ref,
                            pw_ref, ps_ref, wo_ref, x_ref, out_ref, ext_ref, ya_ref, yb_ref,
                            oi_ref, li_ref, *, tiles_per_seq):
    tm = a_ref.shape[0]
    it = pl.program_id(0) % tiles_per_seq
    halo = halo_ref[...]
    ext_ref[0:POOL_HALO, :] = jnp.where(it == 0, jnp.zeros_like(halo), halo)
    ext_ref[POOL_HALO:, :] = a_ref[...]
    pos = it * tm + lax.broadcasted_iota(jnp.int32, (tm, 1), 0)
    for gi, w in enumerate(POOL_WINDOWS):
        cols = slice(gi * POOL_GROUP, (gi + 1) * POOL_GROUP)
        terms = [ext_ref[POOL_HALO - k:POOL_HALO - k + tm, cols] for k in range(1, w)]
        cnt = jnp.minimum(w, pos + 1).astype(F32)
        ya_ref[:, cols] = _pool_mix(terms, a_ref[:, cols], cnt, pw_ref, ps_ref, gi).astype(BF16)

    for g, (o_ref, l_ref) in enumerate(((o0_ref, l0_ref), (o1_ref, l1_ref), (o2_ref, l2_ref))):
        dil = o_ref.shape[0]
        for r in range(dil):
            rows = pl.ds(r, tm // dil, stride=dil)
            li_ref[g, rows, :] = l_ref[r]
            for lc in range(SLAB // LANES):
                oi_ref[g, lc, rows, :] = o_ref[r, :, lc * LANES:(lc + 1) * LANES]

    l0, l1, l2 = li_ref[0], li_ref[1], li_ref[2]
    mx = jnp.maximum(jnp.maximum(l0, l1), l2)
    e0, e1, e2 = jnp.exp(l0 - mx), jnp.exp(l1 - mx), jnp.exp(l2 - mx)
    inv = 1.0 / (e0 + e1 + e2)
    w0, w1, w2 = e0 * inv, e1 * inv, e2 * inv
    for h in range(HEADS):
        lc, lo = divmod(h * HEAD_DIM, LANES)
        sl = slice(lo, lo + HEAD_DIM)
        yb = (w0[:, h:h + 1] * oi_ref[0, lc, :, sl] + w1[:, h:h + 1] * oi_ref[1, lc, :, sl]
              + w2[:, h:h + 1] * oi_ref[2, lc, :, sl])
        yb_ref[:, h * HEAD_DIM:(h + 1) * HEAD_DIM] = yb.astype(BF16)

    y = jnp.dot(ya_ref[...], wo_ref[0:SLAB, :], preferred_element_type=F32)
    y = y + jnp.dot(yb_ref[...], wo_ref[SLAB:, :], preferred_element_type=F32)
    out_ref[...] = x_ref[...] + y


def _even_out_prompt(za, attn, pool_w, pool_scale, w_out, x, seq, tm):
    T = x.shape[0]
    tps = seq // tm
    hb = tm // POOL_HALO

    def residue_spec(dil, width):
        return pl.BlockSpec((None, dil, tm // dil, width), lambda i: (i // tps, 0, i % tps, 0))

    dils = [o.shape[1] for o, _ in attn]
    return pl.pallas_call(
        functools.partial(_even_out_prompt_kernel, tiles_per_seq=tps),
        grid=(T // tm,),
        in_specs=[
            pl.BlockSpec((tm, SLAB), lambda i: (i, 0)),
            pl.BlockSpec((POOL_HALO, SLAB), lambda i: (jnp.maximum(i * hb - 1, 0), 0)),
        ]
        + [residue_spec(d, SLAB) for d in dils] + [residue_spec(d, LANES) for d in dils]
        + [
            pl.BlockSpec((len(POOL_WINDOWS), POOL_GROUP, POOL_GROUP), lambda i: (0, 0, 0)),
            pl.BlockSpec((1, SLAB), lambda i: (0, 0)),
            pl.BlockSpec((2 * SLAB, D_MODEL), lambda i: (0, 0)),
            pl.BlockSpec((tm, D_MODEL), lambda i: (i, 0)),
        ],
        out_specs=pl.BlockSpec((tm, D_MODEL), lambda i: (i, 0)),
        out_shape=jax.ShapeDtypeStruct((T, D_MODEL), F32),
        scratch_shapes=[
            pltpu.VMEM((tm + POOL_HALO, SLAB), F32),
            pltpu.VMEM((tm, SLAB), BF16),
            pltpu.VMEM((tm, SLAB), BF16),
            pltpu.VMEM((len(dils), SLAB // LANES, tm, LANES), F32),
            pltpu.VMEM((len(dils), tm, LANES), F32),
        ],
        compiler_params=_params("arbitrary"),
        name="even_out_prompt",
    )(za, za, *[o for o, _ in attn], *[l for _, l in attn], pool_w, pool_scale, w_out, x)


def _odd_out_prompt_kernel(u_ref, vn_ref, go_ref, hd_ref, hdh_ref,
                           ws_ref, bs_ref, cw_ref, wo_ref, x_ref, out_ref,
                           ext_ref, yc_ref, yd_ref, *, tiles_per_seq):
    tm = u_ref.shape[0]
    it = pl.program_id(0) % tiles_per_seq

    ti = lax.broadcasted_iota(jnp.int32, (CHUNK, CHUNK), 0)
    si = lax.broadcasted_iota(jnp.int32, (CHUNK, CHUNK), 1)
    for g in range(C_GROUPS):
        cols = slice(g * LANES, (g + 1) * LANES)
        wm = jnp.where(si <= ti, ws_ref[g], 0.0).astype(BF16)
        for c in range(tm // CHUNK):
            rows = slice(c * CHUNK, (c + 1) * CHUNK)
            sp = jnp.dot(wm, vn_ref[rows, cols].astype(BF16), preferred_element_type=F32)
            sp = sp + bs_ref[:, cols]
            yc_ref[rows, cols] = (u_ref[rows, cols] * sp).astype(BF16)

    hd = hd_ref[...]
    halo = hdh_ref[...]
    ext_ref[0:CONV_HALO, :] = jnp.where(it == 0, jnp.zeros_like(halo), halo)
    ext_ref[CONV_HALO:, :] = hd
    conv = cw_ref[CONV_W - 1:CONV_W, :] * hd
    for j in range(CONV_W - 1):
        off = CONV_HALO - (CONV_W - 1) + j
        conv = conv + cw_ref[j:j + 1, :] * ext_ref[off:off + tm, :]
    yd_ref[...] = (go_ref[...] * conv).astype(BF16)

    y = jnp.dot(yc_ref[...], wo_ref[0:SLAB, :], preferred_element_type=F32)
    y = y + jnp.dot(yd_ref[...], wo_ref[SLAB:, :], preferred_element_type=F32)
    out_ref[...] = x_ref[...] + y


def _odd_out_prompt(z, ws, bs_rows, conv_w, w_out, x, seq, tm):
    T = x.shape[0]
    tiles_per_seq = seq // tm
    hb = tm // CONV_HALO

    def slab(s):
        return pl.BlockSpec((None, tm, SLAB), lambda i: (s, i, 0))

    return pl.pallas_call(
        functools.partial(_odd_out_prompt_kernel, tiles_per_seq=tiles_per_seq),
        grid=(T // tm,),
        in_specs=[
            slab(0), slab(1), slab(2), slab(3),
            pl.BlockSpec((None, CONV_HALO, SLAB), lambda i: (3, jnp.maximum(i * hb - 1, 0), 0)),
            _resident((C_GROUPS, CHUNK, CHUNK)),
            _resident((CHUNK, SLAB)),
            _resident((CONV_W, SLAB)),
            _resident((2 * SLAB, D_MODEL)),
            pl.BlockSpec((tm, D_MODEL), lambda i: (i, 0)),
        ],
        out_specs=pl.BlockSpec((tm, D_MODEL), lambda i: (i, 0)),
        out_shape=jax.ShapeDtypeStruct((T, D_MODEL), F32),
        scratch_shapes=[
            pltpu.VMEM((tm + CONV_HALO, SLAB), F32),
            pltpu.VMEM((tm, SLAB), BF16),
            pltpu.VMEM((tm, SLAB), BF16),
        ],
        compiler_params=_params("arbitrary"),
        name="odd_out_prompt",
    )(z, z, z, z, z, ws, bs_rows, conv_w, w_out, x)


def _ffn_ple_kernel(x_ref, p_ref, gf_ref, w1_ref, w2_ref, gp_ref, wg_ref, wp_ref, gl_ref,
                    out_ref, *, final_norm, tf):
    x = x_ref[...]
    hn = _rms(x, gf_ref[...]).astype(BF16)
    acc = None
    for c in range(D_FF // tf):
        h1 = jnp.dot(hn, w1_ref[:, c * tf:(c + 1) * tf], preferred_element_type=F32)
        h1 = jnp.square(jnp.maximum(h1, 0.0)).astype(BF16)
        part = jnp.dot(h1, w2_ref[c * tf:(c + 1) * tf, :], preferred_element_type=F32)
        acc = part if acc is None else acc + part
    r = x + acc
    hp = _rms(r, gp_ref[...]).astype(BF16)
    gate = jax.nn.sigmoid(jnp.dot(hp, wg_ref[...], preferred_element_type=F32))
    proj = jnp.dot(p_ref[...].astype(BF16), wp_ref[...], preferred_element_type=F32)
    r = r + gate * proj
    if final_norm:
        r = _rms(r, gl_ref[...])
    out_ref[...] = r


def _ffn_ple(x, p, g_ffn, w1, w2, g_ple, wg, wp, g_last, final_norm, tm, tf):
    T = x.shape[0]
    return pl.pallas_call(
        functools.partial(_ffn_ple_kernel, final_norm=final_norm, tf=tf),
        grid=(T // tm,),
        in_specs=[
            pl.BlockSpec((tm, D_MODEL), lambda i: (i, 0)),
            pl.BlockSpec((tm, PLE_DIM), lambda i: (i, 0)),
            _resident((1, D_MODEL)),
            _resident((D_MODEL, D_FF)),
            _resident((D_FF, D_MODEL)),
            _resident((1, D_MODEL)),
            _resident((D_MODEL, D_MODEL)),
            _resident((PLE_DIM, D_MODEL)),
            _resident((1, D_MODEL)),
        ],
        out_specs=pl.BlockSpec((tm, D_MODEL), lambda i: (i, 0)),
        out_shape=jax.ShapeDtypeStruct((T, D_MODEL), F32),
        compiler_params=_params("arbitrary"),
        name="ffn_ple",
    )(x, p, g_ffn, w1, w2, g_ple, wg, wp, g_last)


def _attn_sample_kernel(q_ref, kv_ref, c0_ref, c1_ref, c2_ref, o_ref):
    n_tok = q_ref.shape[1]
    n_rows = n_tok * HEADS
    scale = HEAD_DIM ** -0.5
    sub = lax.broadcasted_iota(jnp.int32, (HEADS, SLAB), 0)
    lane_head = lax.broadcasted_iota(jnp.int32, (HEADS, SLAB), 1) // HEAD_DIM
    own = sub == lane_head
    row_tok = lax.broadcasted_iota(jnp.int32, (n_rows, 1), 0) // HEADS

    caches = (c0_ref, c1_ref, c2_ref)
    dils = tuple(d for _, d in DIL_CFG)
    m = jnp.full((n_rows, 1), NEG, F32)
    s_cache, s_new = [], []
    for g, (c_ref, dil) in enumerate(zip(caches, dils)):
        qg = q_ref[g] * scale
        qbd = jnp.concatenate(
            [jnp.where(own, jnp.broadcast_to(qg[t:t + 1, :], (HEADS, SLAB)), 0.0)
             for t in range(n_tok)], axis=0)
        s = jnp.dot(qbd.astype(BF16), c_ref[0].astype(BF16), preferred_element_type=F32)
        pos = lax.broadcasted_iota(jnp.int32, s.shape, 1)
        valid = (pos >= row_tok) if dil == 1 else (jnp.bitwise_and(pos, dil - 1) == row_tok)
        s = jnp.where(valid, s, NEG)
        m = jnp.maximum(m, jnp.max(s, axis=1, keepdims=True))
        s_cache.append(s)
        kn = kv_ref[2 * g]
        for tp in range(n_tok):
            sn = jnp.sum(qbd * kn[tp:tp + 1, :], axis=1, keepdims=True)
            ok = (row_tok >= tp) if dil == 1 else (row_tok == tp)
            sn = jnp.where(ok, sn, NEG)
            m = jnp.maximum(m, sn)
            s_new.append((g, tp, sn))

    acc = jnp.zeros((n_rows, SLAB), F32)
    den = jnp.zeros((n_rows, 1), F32)
    for g, c_ref in enumerate(caches):
        e = jnp.exp(s_cache[g] - m)
        den = den + jnp.sum(e, axis=1, keepdims=True)
        acc = acc + lax.dot_general(e.astype(BF16), c_ref[1].astype(BF16),
                                    (((1,), (1,)), ((), ())), preferred_element_type=F32)
    for g, tp, sn in s_new:
        e = jnp.exp(sn - m)
        den = den + e
        acc = acc + e * kv_ref[2 * g + 1][tp:tp + 1, :]
    res = acc * (1.0 / den)
    for t in range(n_tok):
        rows = res[t * HEADS:(t + 1) * HEADS, :]
        o_ref[t:t + 1, :] = jnp.sum(jnp.where(own, rows, 0.0), axis=0, keepdims=True)


def _attn_sample(qn, kvn, c0, c1, c2):
    n_seq, _, n_tok, _ = qn.shape

    def cache_spec(c):
        return pl.BlockSpec((None, 2, SLAB, c.shape[-1]), lambda n: (n, 0, 0, 0))

    return pl.pallas_call(
        _attn_sample_kernel,
        grid=(n_seq,),
        in_specs=[
            pl.BlockSpec((None, 3, n_tok, SLAB), lambda n: (n, 0, 0, 0)),
            pl.BlockSpec((None, 6, n_tok, SLAB), lambda n: (n, 0, 0, 0)),
            cache_spec(c0), cache_spec(c1), cache_spec(c2),
        ],
        out_specs=pl.BlockSpec((None, n_tok, SLAB), lambda n: (n, 0, 0)),
        out_shape=jax.ShapeDtypeStruct((n_seq, n_tok, SLAB), F32),
        compiler_params=_params("arbitrary"),
        name="attn_sample",
    )(qn, kvn, c0, c1, c2)


def _even_out_sample_kernel(a_ref, ctx_ref, yb_ref, pw_ref, ps_ref, wo_ref, x_ref, out_ref,
                            *, n_seq, n_tok):
    def ext_row(e, cols):
        if e >= POOL_STATE:
            t = e - POOL_STATE
            return a_ref[t * n_seq:(t + 1) * n_seq, cols]
        return ctx_ref[e, :, cols]

    for t in range(n_tok):
        rows = slice(t * n_seq, (t + 1) * n_seq)
        ya = []
        for gi, w in enumerate(POOL_WINDOWS):
            cols = slice(gi * POOL_GROUP, (gi + 1) * POOL_GROUP)
            terms = [ext_row(POOL_STATE + t - k, cols) for k in range(1, w)]
            ya.append(_pool_mix(terms, a_ref[rows, cols], float(w), pw_ref, ps_ref, gi))
        y = jnp.zeros((n_seq, D_MODEL), F32)
        for gi in range(len(POOL_WINDOWS)):
            y = y + jnp.dot(ya[gi].astype(BF16), wo_ref[gi * POOL_GROUP:(gi + 1) * POOL_GROUP, :],
                            preferred_element_type=F32)
        yb = yb_ref[:, t * SLAB:(t + 1) * SLAB].astype(BF16)
        y = y + jnp.dot(yb, wo_ref[SLAB:, :], preferred_element_type=F32)
        out_ref[rows, :] = x_ref[rows, :] + y


def _even_out_sample(z, ctx, yb, pool_w, pool_scale, w_out, x, n_seq, n_tok):
    T = x.shape[0]
    full = lambda *shape: pl.BlockSpec(shape, lambda i: (0,) * len(shape))
    return pl.pallas_call(
        functools.partial(_even_out_sample_kernel, n_seq=n_seq, n_tok=n_tok),
        grid=(1,),
        in_specs=[
            full(T, SLAB),
            full(POOL_STATE, n_seq, SLAB),
            full(n_seq, n_tok * SLAB),
            full(len(POOL_WINDOWS), POOL_GROUP, POOL_GROUP),
            full(1, SLAB),
            full(2 * SLAB, D_MODEL),
            full(T, D_MODEL),
        ],
        out_specs=full(T, D_MODEL),
        out_shape=jax.ShapeDtypeStruct((T, D_MODEL), F32),
        compiler_params=_params("arbitrary"),
        name="even_out_sample",
    )(z, ctx, yb, pool_w, pool_scale, w_out, x)


def _odd_out_sample_kernel(z_ref, ctx_ref, coef_ref, bias_ref, cw_ref, wo_ref, x_ref,
                           out_ref, *, n_seq, n_tok, mix_terms):
    def ext_row(e):
        if e >= CONV_W - 1:
            t = e - (CONV_W - 1)
            return z_ref[3, t * n_seq:(t + 1) * n_seq, :]
        return ctx_ref[:, e * SLAB:(e + 1) * SLAB]

    for t in range(n_tok):
        rows = slice(t * n_seq, (t + 1) * n_seq)
        sp = jnp.zeros((n_seq, SLAB), F32) + bias_ref[t:t + 1, :]
        for s in mix_terms[t]:
            r = t * n_tok + s
            sp = sp + coef_ref[r:r + 1, :] * z_ref[1, s * n_seq:(s + 1) * n_seq, :]
        yc = z_ref[0, rows, :] * sp
        conv = jnp.zeros((n_seq, SLAB), F32)
        for j in range(CONV_W):
            conv = conv + cw_ref[j:j + 1, :] * ext_row(t + j)
        yd = z_ref[2, rows, :] * conv
        y = jnp.dot(yc.astype(BF16), wo_ref[0:SLAB, :], preferred_element_type=F32)
        y = y + jnp.dot(yd.astype(BF16), wo_ref[SLAB:, :], preferred_element_type=F32)
        out_ref[rows, :] = x_ref[rows, :] + y


def _odd_out_sample(z, ctx, coef, bias, conv_w, w_out, x, n_seq, n_tok, mix_terms):
    T = x.shape[0]
    full = lambda *shape: pl.BlockSpec(shape, lambda i: (0,) * len(shape))
    return pl.pallas_call(
        functools.partial(_odd_out_sample_kernel, n_seq=n_seq, n_tok=n_tok, mix_terms=mix_terms),
        grid=(1,),
        in_specs=[
            full(4, T, SLAB),
            full(n_seq, (CONV_W - 1) * SLAB),
            full(n_tok * n_tok, SLAB),
            full(n_tok, SLAB),
            full(CONV_W, SLAB),
            full(2 * SLAB, D_MODEL),
            full(T, D_MODEL),
        ],
        out_specs=full(T, D_MODEL),
        out_shape=jax.ShapeDtypeStruct((T, D_MODEL), F32),
        compiler_params=_params("arbitrary"),
        name="odd_out_sample",
    )(z, ctx, coef, bias, conv_w, w_out, x)


def _rope_tables(pos, split):
    half = HEAD_DIM // 2
    inv = jnp.power(jnp.float32(ROPE_THETA), -jnp.arange(half, dtype=F32) / half)
    ang = pos.astype(F32)[:, None] * inv[None, :]
    cos = jnp.cos(ang)
    sin = jnp.sin(ang)
    if split:
        return jnp.tile(cos, (1, LANES // half)), jnp.tile(sin, (1, LANES // half))
    cos_t = jnp.tile(jnp.concatenate([cos, cos], axis=-1), (1, LANES // HEAD_DIM))
    sin_t = jnp.tile(jnp.concatenate([-sin, sin], axis=-1), (1, LANES // HEAD_DIM))
    return cos_t, sin_t


def _split_qk_columns(w):
    d = w.shape[0]
    half = HEAD_DIM // 2
    rest = w[:, SLAB:].reshape(d, len(DIL_CFG), 3, HEADS, 2, half)
    qk = rest[:, :, :2].transpose(0, 1, 2, 4, 3, 5)
    rest = jnp.concatenate([qk.reshape(d, len(DIL_CFG), 2, SLAB),
                            rest[:, :, 2:].reshape(d, len(DIL_CFG), 1, SLAB)], axis=2)
    return jnp.concatenate([w[:, :SLAB], rest.reshape(d, -1)], axis=1)


def kernel(x_prompt, x_sample, cache_kv_w128, cache_kv_w512, cache_kv_w2048, state_pool, state_conv,
           p_prompt, p_sample, ev_w_in, ev_pool_w, ev_pool_scale, ev_w_out, od_w_in, od_ln_g, od_ln_b,
           od_ws, od_bs, od_conv_w, od_w_out, norm_mix, norm_ffn, norm_ple, ffn_w1, ffn_w2,
           ple_w_proj, ple_w_gate, norm_final):
    n_p, seq, _ = x_prompt.shape
    n_s, n_tok, _ = x_sample.shape
    depth = norm_mix.shape[0]
    tp = n_p * seq
    ts = n_s * n_tok

    bf = lambda w: w.astype(BF16)
    row = lambda v: v.reshape(1, -1)
    ev_w_in_b, ev_pool_w_b, ev_w_out_b = bf(ev_w_in), bf(ev_pool_w), bf(ev_w_out)
    od_w_in_b, od_w_out_b = bf(od_w_in), bf(od_w_out)
    w1_b, w2_b, wg_b, wp_b = bf(ffn_w1), bf(ffn_w2), bf(ple_w_gate), bf(ple_w_proj)

    cos_p, sin_p = _rope_tables(jnp.arange(seq), True)
    pos_s = [PAST_LEN + t for t in range(n_tok)]
    cos_s, sin_s = _rope_tables(jnp.repeat(jnp.asarray(pos_s, jnp.int32), n_s), False)

    mix_terms = tuple(
        tuple(s for s in range(n_tok)
              if pos_s[s] // CHUNK == pos_s[t] // CHUNK and pos_s[s] % CHUNK <= pos_s[t] % CHUNK)
        for t in range(n_tok))
    local = [p % CHUNK for p in pos_s]

    rp = x_prompt.reshape(tp, D_MODEL)
    rs = x_sample.transpose(1, 0, 2).reshape(ts, D_MODEL)
    pp = p_prompt.reshape(depth, tp, PLE_DIM)
    ps = p_sample.transpose(0, 2, 1, 3).reshape(depth, ts, PLE_DIM)

    tm_p, tm_mix, tm_ffn, tf = 512, 512, 512, 512
    kv_p = [[] for _ in DIL_CFG]
    kv_s = [[] for _ in DIL_CFG]
    pool_p, pool_s, conv_p, conv_s, cv_s = [], [], [], [], []

    for i in range(depth):
        g_mix = row(norm_mix[i])
        if i % 2 == 0:
            e = i // 2
            pscale = row(ev_pool_scale[e])
            dils = tuple(d for _, d in DIL_CFG)
            w_split = bf(_split_qk_columns(ev_w_in[e]))
            za, *groups = _proj_even(rp, g_mix, w_split, cos_p, sin_p, tm_p, n_p, dils, True)
            attn = []
            for grp, dil in zip(groups, dils):
                if dil == 1:
                    grp = grp.reshape(3, n_p, 1, seq, SLAB)
                attn.append(_attn_prompt(grp, dil))
            rp = _even_out_prompt(za, attn, ev_pool_w_b[e], pscale, ev_w_out_b[e], rp, seq, tm_mix)
            for g, ((win, dil), grp) in enumerate(zip(DIL_CFG, groups)):
                keep = min(win, seq)
                if dil == 1:
                    tail = grp.reshape(3, n_p, seq, SLAB)[1:, :, seq - keep:]
                else:
                    tail = grp[1:, :, :, (seq - keep) // dil:].transpose(0, 1, 3, 2, 4)
                k = tail[0].reshape(n_p, keep, 2, HEADS, HEAD_DIM // 2).transpose(0, 1, 3, 2, 4)
                k = k.reshape(n_p, keep, HEADS, HEAD_DIM)
                v = tail[1].reshape(n_p, keep, HEADS, HEAD_DIM)
                kv_p[g].append(jnp.stack([k, v], axis=2))
            pool_p.append(za.reshape(n_p, seq, SLAB)[:, seq - POOL_STATE:])
            zas, *sgroups = _proj_even(rs, g_mix, ev_w_in_b[e], cos_s, sin_s, ts, 1, (1, 1, 1),
                                       False)
            zn = jnp.stack(sgroups).reshape(3, 3, n_tok, n_s, SLAB).transpose(3, 0, 1, 2, 4)
            qn = zn[:, :, 0]
            kvn = zn[:, :, 1:].reshape(n_s, 6, n_tok, SLAB)
            native = lambda c: c.transpose(0, 2, 3, 4, 1).reshape(n_s, 2, SLAB, c.shape[1])
            yb = _attn_sample(qn, kvn, native(cache_kv_w128[e]), native(cache_kv_w512[e]),
                              native(cache_kv_w2048[e]))
            rs = _even_out_sample(zas, state_pool[e].transpose(1, 0, 2),
                                  yb.reshape(n_s, n_tok * SLAB), ev_pool_w_b[e], pscale,
                                  ev_w_out_b[e], rs, n_s, n_tok)
            for g in range(len(DIL_CFG)):
                k = zn[:, g, 1].reshape(n_s, n_tok, HEADS, HEAD_DIM)
                v = zn[:, g, 2].reshape(n_s, n_tok, HEADS, HEAD_DIM)
                kv_s[g].append(jnp.stack([k, v], axis=2))
            a_n = zas.reshape(n_tok, n_s, SLAB).transpose(1, 0, 2)
            pool_s.append(jnp.concatenate([state_pool[e], a_n], axis=1)[:, -POOL_STATE:])
        else:
            o = i // 2
            ln_g, ln_b = row(od_ln_g[o]), row(od_ln_b[o])
            bs_rows = jnp.repeat(od_bs[o].T, LANES, axis=1)
            z = _proj_odd(rp, g_mix, od_w_in_b[o], ln_g, ln_b, tm_p)
            rp = _odd_out_prompt(z, od_ws[o], bs_rows, od_conv_w[o], od_w_out_b[o], rp, seq, tm_mix)
            conv_p.append(z[3].reshape(n_p, seq, SLAB)[:, seq - (CONV_W - 1):])
            zs = _proj_odd(rs, g_mix, od_w_in_b[o], ln_g, ln_b, ts)
            coef = jnp.stack([jnp.repeat(od_ws[o][:, local[t], local[s]], LANES)
                              for t in range(n_tok) for s in range(n_tok)])
            bias = jnp.stack([bs_rows[local[t]] for t in range(n_tok)])
            rs = _odd_out_sample(zs, state_conv[o].reshape(n_s, (CONV_W - 1) * SLAB), coef, bias,
                                 od_conv_w[o], od_w_out_b[o], rs, n_s, n_tok, mix_terms)
            hd_n = zs[3].reshape(n_tok, n_s, SLAB).transpose(1, 0, 2)
            conv_s.append(jnp.concatenate([state_conv[o], hd_n], axis=1)[:, -(CONV_W - 1):])
            cv_s.append(zs[1].reshape(n_tok, n_s, SLAB).transpose(1, 0, 2))

        last = i == depth - 1
        args = (row(norm_ffn[i]), w1_b[i], w2_b[i], row(norm_ple[i]), wg_b[i], wp_b[i],
                row(norm_final), last)
        rp = _ffn_ple(rp, pp[i], *args, tm_ffn, tf)
        rs = _ffn_ple(rs, ps[i], *args, ts, tf)

    y_prompt = rp.reshape(n_p, seq, D_MODEL)
    y_sample = rs.reshape(n_tok, n_s, D_MODEL).transpose(1, 0, 2)
    st = lambda lst: jnp.stack(lst, axis=0)
    return (y_prompt, y_sample, st(kv_p[0]), st(kv_p[1]), st(kv_p[2]),
            st(kv_s[0]), st(kv_s[1]), st(kv_s[2]),
            st(pool_p), st(pool_s), st(conv_p), st(conv_s), st(cv_s))
```

```python
import functools
import math

import jax
import jax.numpy as jnp
from jax import lax
from jax.experimental import pallas as pl
from jax.experimental.pallas import tpu as pltpu

F32 = jnp.float32
BF16 = jnp.bfloat16

D_MODEL = 1024
D_FF = 4 * D_MODEL
PLE_DIM = 256
EPS = 1e-6
ROPE_THETA = 10000.0
PAST_LEN = 2048

SLAB = 512
POOL_WINDOWS = (2, 4, 8, 16)
POOL_GROUP = 128
POOL_STATE = 15
POOL_HALO = 16
DIL_CFG = ((128, 1), (512, 4), (2048, 16))
HEADS = 8
HEAD_DIM = 64
ATT_BLOCK = 128
ATT_ROWS = 512
CHUNK = 128
C_GROUPS = 4
CONV_W = 3
CONV_HALO = 8
LANES = 128
NEG = -1e30

VMEM_LIMIT = 52 * 1024 * 1024


def _params(*sem):
    return pltpu.CompilerParams(dimension_semantics=sem, vmem_limit_bytes=VMEM_LIMIT)


def _rms(x, g):
    ms = jnp.mean(x * x, axis=-1, keepdims=True)
    return x * lax.rsqrt(ms + EPS) * g


def _gelu(x):
    c = math.sqrt(2.0 / math.pi)
    return 0.5 * x * (1.0 + jnp.tanh(c * (x + 0.044715 * (x * x * x))))


def _proj_even_kernel(x_ref, g_ref, w_ref, cos_ref, sin_ref, za_ref, g0_ref, g1_ref, g2_ref,
                      zs_ref, *, dils, split):
    tm = x_ref.shape[0]
    nl = SLAB // LANES
    hn = _rms(x_ref[...], g_ref[...]).astype(BF16)
    za_ref[...] = jnp.dot(hn, w_ref[:, 0:SLAB], preferred_element_type=F32)

    def rotate(chunks):
        cos = cos_ref[...]
        sin = sin_ref[...]
        if split:
            h = nl // 2
            return ([chunks[i] * cos - chunks[i + h] * sin for i in range(h)]
                    + [chunks[i] * cos + chunks[i - h] * sin for i in range(h, nl)])
        lane = lax.broadcasted_iota(jnp.int32, cos.shape, 1)
        first_half = jnp.bitwise_and(lane, HEAD_DIM - 1) < (HEAD_DIM // 2)
        out = []
        for zc in chunks:
            partner = jnp.where(first_half,
                                pltpu.roll(zc, LANES - HEAD_DIM // 2, 1),
                                pltpu.roll(zc, HEAD_DIM // 2, 1))
            out.append(zc * cos + partner * sin)
        return out

    slot = 0
    for g, (out_ref, dil) in enumerate(zip((g0_ref, g1_ref, g2_ref), dils)):
        for c in range(3):
            col0 = (1 + 3 * g + c) * SLAB
            z = jnp.dot(hn, w_ref[:, col0:col0 + SLAB], preferred_element_type=F32)
            chunks = [z[:, i * LANES:(i + 1) * LANES] for i in range(nl)]
            if c < 2:
                chunks = rotate(chunks)
            for i, zc in enumerate(chunks):
                cols = slice(i * LANES, (i + 1) * LANES)
                if dil == 1:
                    out_ref[c, :, cols] = zc
                else:
                    zs_ref[slot] = zc
                    for r in range(dil):
                        out_ref[c, r, :, cols] = zs_ref[slot, pl.ds(r, tm // dil, stride=dil), :]
                    slot += 1


def _resident(shape):
    return pl.BlockSpec(shape, lambda *_: (0,) * len(shape), pipeline_mode=pl.Buffered(1))


def _proj_even(x, g, w, cos, sin, tm, n_seq, dils, split):
    T = x.shape[0]
    ntab = cos.shape[0] // tm
    tps = T // n_seq // tm

    def group_spec(dil):
        if dil == 1:
            return pl.BlockSpec((3, tm, SLAB), lambda i: (0, i, 0))
        return pl.BlockSpec((3, None, dil, tm // dil, SLAB), lambda i: (0, i // tps, 0, i % tps, 0))

    def group_shape(dil):
        if dil == 1:
            return jax.ShapeDtypeStruct((3, T, SLAB), F32)
        return jax.ShapeDtypeStruct((3, n_seq, dil, T // n_seq // dil, SLAB), F32)

    n_slots = max(1, 3 * (SLAB // LANES) * sum(d > 1 for d in dils))
    return pl.pallas_call(
        functools.partial(_proj_even_kernel, dils=dils, split=split),
        grid=(T // tm,),
        in_specs=[
            pl.BlockSpec((tm, D_MODEL), lambda i: (i, 0)),
            _resident((1, D_MODEL)),
            _resident(w.shape),
            pl.BlockSpec((tm, LANES), lambda i: (i % ntab, 0)),
            pl.BlockSpec((tm, LANES), lambda i: (i % ntab, 0)),
        ],
        out_specs=[pl.BlockSpec((tm, SLAB), lambda i: (i, 0))] + [group_spec(d) for d in dils],
        out_shape=[jax.ShapeDtypeStruct((T, SLAB), F32)] + [group_shape(d) for d in dils],
        scratch_shapes=[pltpu.VMEM((n_slots, tm, LANES), F32)],
        compiler_params=_params("arbitrary"),
        name="proj_even",
    )(x, g, w, cos, sin)


def _proj_odd_kernel(x_ref, g_ref, w_ref, lng_ref, lnb_ref, o_ref):
    hn = _rms(x_ref[...], g_ref[...]).astype(BF16)

    def slab(s):
        return jnp.dot(hn, w_ref[:, s * SLAB:(s + 1) * SLAB], preferred_element_type=F32)

    o_ref[0] = _gelu(slab(0))
    zv = slab(1)
    for c in range(C_GROUPS):
        sl = slice(c * LANES, (c + 1) * LANES)
        v = _gelu(zv[:, sl])
        mu = jnp.mean(v, axis=-1, keepdims=True)
        dv = v - mu
        var = jnp.mean(dv * dv, axis=-1, keepdims=True)
        o_ref[1, :, sl] = dv * lax.rsqrt(var + EPS) * lng_ref[:, sl] + lnb_ref[:, sl]
    o_ref[2] = slab(2)
    o_ref[3] = slab(3) * slab(4)


def _proj_odd(x, g, w, ln_g, ln_b, tm):
    T = x.shape[0]
    return pl.pallas_call(
        _proj_odd_kernel,
        grid=(T // tm,),
        in_specs=[
            pl.BlockSpec((tm, D_MODEL), lambda i: (i, 0)),
            _resident((1, D_MODEL)),
            _resident(w.shape),
            _resident((1, SLAB)),
            _resident((1, SLAB)),
        ],
        out_specs=pl.BlockSpec((4, tm, SLAB), lambda i: (0, i, 0)),
        out_shape=jax.ShapeDtypeStruct((4, T, SLAB), F32),
        compiler_params=_params("arbitrary"),
        name="proj_odd",
    )(x, g, w, ln_g, ln_b)


def _attn_prompt_kernel(q_ref, kp_ref, kc_ref, vp_ref, vc_ref, o_ref, lse_ref,
                        qst_ref, ks_ref, vs_ref, bias_ref, s_ref, e_ref):
    step = pl.program_id(2)
    sub = ATT_ROWS // ATT_BLOCK
    half_rot = HEAD_DIM // 2
    heads_per_blk = LANES // half_rot
    n_blk = HEADS // heads_per_blk
    log2e = 1.4426950408889634
    ln2 = 0.6931471805599453
    scale = HEAD_DIM ** -0.5 * log2e

    qlane = lax.broadcasted_iota(jnp.int32, (ATT_ROWS, 2 * LANES), 1)
    head_in_blk = jnp.bitwise_and(qlane, LANES - 1) // half_rot
    for j in range(n_blk):
        lo = slice(j * LANES, (j + 1) * LANES)
        hi = slice((n_blk + j) * LANES, (n_blk + j + 1) * LANES)
        dst = slice(2 * j * LANES, (2 * j + 2) * LANES)
        q2 = (jnp.concatenate([q_ref[:, lo], q_ref[:, hi]], axis=1) * scale).astype(BF16)
        for hq in range(heads_per_blk):
            qm = jnp.where(head_in_blk == hq, q2, jnp.zeros_like(q2))
            for b in range(sub):
                r0 = ((j * sub + b) * heads_per_blk + hq) * ATT_BLOCK
                qst_ref[r0:r0 + ATT_BLOCK, :] = qm[b * ATT_BLOCK:(b + 1) * ATT_BLOCK, :]
        ks_ref[0:ATT_BLOCK, dst] = jnp.concatenate([kp_ref[:, lo], kp_ref[:, hi]],
                                                   axis=1).astype(BF16)
        ks_ref[ATT_BLOCK:, dst] = jnp.concatenate([kc_ref[:, lo], kc_ref[:, hi]],
                                                  axis=1).astype(BF16)
    vs_ref[0:ATT_BLOCK, :] = vp_ref[...].astype(BF16)
    vs_ref[ATT_BLOCK:, :] = vc_ref[...].astype(BF16)

    qi = lax.broadcasted_iota(jnp.int32, (ATT_BLOCK, 2 * ATT_BLOCK), 0)
    ki = lax.broadcasted_iota(jnp.int32, (ATT_BLOCK, 2 * ATT_BLOCK), 1)
    rel = qi + ATT_BLOCK - ki
    band = jnp.logical_and(rel >= 0, rel <= ATT_BLOCK)
    bias_ref[0] = jnp.where(band, 0.0, NEG)
    bias_ref[1] = jnp.where(jnp.logical_and(band, ki >= ATT_BLOCK), 0.0, NEG)
    lane = lax.broadcasted_iota(jnp.int32, (ATT_BLOCK, LANES), 1)
    low_head = lane < HEAD_DIM
    blk_rows = heads_per_blk * ATT_BLOCK

    first = (step == 0).astype(jnp.int32)
    hb = ATT_BLOCK // 2
    lane_hb = lax.broadcasted_iota(jnp.int32, (hb, LANES), 1)

    def scores(b):
        for j in range(n_blk):
            base = (j * sub + b) * blk_rows
            k2 = ks_ref[b * ATT_BLOCK:(b + 2) * ATT_BLOCK, 2 * j * LANES:(2 * j + 2) * LANES]
            s_ref[b, j * blk_rows:(j + 1) * blk_rows, :] = lax.dot_general(
                qst_ref[base:base + blk_rows, :], k2, (((1,), (1,)), ((), ())),
                preferred_element_type=F32)

    def softmax(b):
        which = first if b == 0 else 0
        for rh in range(2):
            lse_tile = jnp.zeros((hb, LANES), F32)
            for h in range(HEADS):
                rows = slice(h * ATT_BLOCK + rh * hb, h * ATT_BLOCK + (rh + 1) * hb)
                s = s_ref[b, rows, :] + bias_ref[which, rh * hb:(rh + 1) * hb, :]
                m = jnp.max(s, axis=-1, keepdims=True)
                e = jnp.exp2(s - m)
                l = jnp.sum(e, axis=-1, keepdims=True)
                e_ref[b, rows, :] = (e * (1.0 / l)).astype(BF16)
                lse_tile = jnp.where(lane_hb == h, m * ln2 + jnp.log(l), lse_tile)
            lse_ref[b * ATT_BLOCK + rh * hb:b * ATT_BLOCK + (rh + 1) * hb, :] = lse_tile

    def weighted_values(b):
        for hp in range(HEADS // 2):
            cols = slice(hp * LANES, (hp + 1) * LANES)
            v2 = vs_ref[b * ATT_BLOCK:(b + 2) * ATT_BLOCK, cols]
            pv = jnp.dot(e_ref[b, 2 * hp * ATT_BLOCK:(2 * hp + 2) * ATT_BLOCK, :], v2,
                         preferred_element_type=F32)
            o_ref[b * ATT_BLOCK:(b + 1) * ATT_BLOCK, cols] = jnp.where(
                low_head, pv[0:ATT_BLOCK], pv[ATT_BLOCK:])

    for t in range(sub + 2):
        if t < sub:
            scores(t)
        if 0 <= t - 1 < sub:
            softmax(t - 1)
        if 0 <= t - 2 < sub:
            weighted_values(t - 2)


def _attn_prompt(qkv, dil):
    _, n_seq, _, m_rows, _ = qkv.shape
    steps = m_rows // ATT_ROWS
    sub = ATT_ROWS // ATT_BLOCK

    def cur(slab):
        return pl.BlockSpec((None, None, None, ATT_ROWS, SLAB), lambda n, r, b: (slab, n, r, b, 0))

    def prev(slab):
        return pl.BlockSpec((None, None, None, ATT_BLOCK, SLAB),
                            lambda n, r, b: (slab, n, r, jnp.maximum(b * sub - 1, 0), 0))

    return pl.pallas_call(
        _attn_prompt_kernel,
        grid=(n_seq, dil, steps),
        in_specs=[cur(0), prev(1), cur(1), prev(2), cur(2)],
        out_specs=[
            pl.BlockSpec((None, None, ATT_ROWS, SLAB), lambda n, r, b: (n, r, b, 0)),
            pl.BlockSpec((None, None, ATT_ROWS, LANES), lambda n, r, b: (n, r, b, 0)),
        ],
        out_shape=[
            jax.ShapeDtypeStruct((n_seq, dil, m_rows, SLAB), F32),
            jax.ShapeDtypeStruct((n_seq, dil, m_rows, LANES), F32),
        ],
        scratch_shapes=[
            pltpu.VMEM((HEADS * ATT_ROWS, 2 * LANES), BF16),
            pltpu.VMEM((ATT_ROWS + ATT_BLOCK, SLAB), BF16),
            pltpu.VMEM((ATT_ROWS + ATT_BLOCK, SLAB), BF16),
            pltpu.VMEM((2, ATT_BLOCK, 2 * ATT_BLOCK), F32),
            pltpu.VMEM((sub, HEADS * ATT_BLOCK, 2 * ATT_BLOCK), F32),
            pltpu.VMEM((sub, HEADS * ATT_BLOCK, 2 * ATT_BLOCK), BF16),
        ],
        compiler_params=_params("arbitrary", "arbitrary", "arbitrary"),
        name=f"attn_prompt_d{dil}",
    )(qkv, qkv, qkv, qkv, qkv)


def _pool_mix(window_terms, a_cols, cnt, pw_ref, scale_ref, gi):
    acc = a_cols
    for term in window_terms:
        acc = acc + term
    pooled = acc / cnt - a_cols
    cols = slice(gi * POOL_GROUP, (gi + 1) * POOL_GROUP)
    mixed = jnp.dot(pooled.astype(BF16), pw_ref[gi], preferred_element_type=F32)
    return mixed * scale_ref[:, cols]


def _even_out_prompt_kernel(a_ref, halo_ref, o0_ref, o1_ref, o2_ref, l0_ref, l1_ref, l2_ref,
                            pw_ref, ps_ref, wo_ref, x_ref, out_ref, ext_ref, ya_ref, yb_ref,
                            oi_ref, li_ref, *, tiles_per_seq):
    tm = a_ref.shape[0]
    it = pl.program_id(0) % tiles_per_seq
    halo = halo_ref[...]
    ext_ref[0:POOL_HALO, :] = jnp.where(it == 0, jnp.zeros_like(halo), halo)
    ext_ref[POOL_HALO:, :] = a_ref[...]
    pos = it * tm + lax.broadcasted_iota(jnp.int32, (tm, 1), 0)
    for gi, w in enumerate(POOL_WINDOWS):
        cols = slice(gi * POOL_GROUP, (gi + 1) * POOL_GROUP)
        terms = [ext_ref[POOL_HALO - k:POOL_HALO - k + tm, cols] for k in range(1, w)]
        cnt = jnp.minimum(w, pos + 1).astype(F32)
        ya_ref[:, cols] = _pool_mix(terms, a_ref[:, cols], cnt, pw_ref, ps_ref, gi).astype(BF16)

    for g, (o_ref, l_ref) in enumerate(((o0_ref, l0_ref), (o1_ref, l1_ref), (o2_ref, l2_ref))):
        dil = o_ref.shape[0]
        for r in range(dil):
            rows = pl.ds(r, tm // dil, stride=dil)
            li_ref[g, rows, :] = l_ref[r]
            for lc in range(SLAB // LANES):
                oi_ref[g, lc, rows, :] = o_ref[r, :, lc * LANES:(lc + 1) * LANES]

    l0, l1, l2 = li_ref[0], li_ref[1], li_ref[2]
    mx = jnp.maximum(jnp.maximum(l0, l1), l2)
    e0, e1, e2 = jnp.exp(l0 - mx), jnp.exp(l1 - mx), jnp.exp(l2 - mx)
    inv = 1.0 / (e0 + e1 + e2)
    w0, w1, w2 = e0 * inv, e1 * inv, e2 * inv
    for h in range(HEADS):
        lc, lo = divmod(h * HEAD_DIM, LANES)
        sl = slice(lo, lo + HEAD_DIM)
        yb = (w0[:, h:h + 1] * oi_ref[0, lc, :, sl] + w1[:, h:h + 1] * oi_ref[1, lc, :, sl]
              + w2[:, h:h + 1] * oi_ref[2, lc, :, sl])
        yb_ref[:, h * HEAD_DIM:(h + 1) * HEAD_DIM] = yb.astype(BF16)

    y = jnp.dot(ya_ref[...], wo_ref[0:SLAB, :], preferred_element_type=F32)
    y = y + jnp.dot(yb_ref[...], wo_ref[SLAB:, :], preferred_element_type=F32)
    out_ref[...] = x_ref[...] + y


def _even_out_prompt(za, attn, pool_w, pool_scale, w_out, x, seq, tm):
    T = x.shape[0]
    tps = seq // tm
    hb = tm // POOL_HALO

    def residue_spec(dil, width):
        return pl.BlockSpec((None, dil, tm // dil, width), lambda i: (i // tps, 0, i % tps, 0))

    dils = [o.shape[1] for o, _ in attn]
    return pl.pallas_call(
        functools.partial(_even_out_prompt_kernel, tiles_per_seq=tps),
        grid=(T // tm,),
        in_specs=[
            pl.BlockSpec((tm, SLAB), lambda i: (i, 0)),
            pl.BlockSpec((POOL_HALO, SLAB), lambda i: (jnp.maximum(i * hb - 1, 0), 0)),
        ]
        + [residue_spec(d, SLAB) for d in dils] + [residue_spec(d, LANES) for d in dils]
        + [
            pl.BlockSpec((len(POOL_WINDOWS), POOL_GROUP, POOL_GROUP), lambda i: (0, 0, 0)),
            pl.BlockSpec((1, SLAB), lambda i: (0, 0)),
            pl.BlockSpec((2 * SLAB, D_MODEL), lambda i: (0, 0)),
            pl.BlockSpec((tm, D_MODEL), lambda i: (i, 0)),
        ],
        out_specs=pl.BlockSpec((tm, D_MODEL), lambda i: (i, 0)),
        out_shape=jax.ShapeDtypeStruct((T, D_MODEL), F32),
        scratch_shapes=[
            pltpu.VMEM((tm + POOL_HALO, SLAB), F32),
            pltpu.VMEM((tm, SLAB), BF16),
            pltpu.VMEM((tm, SLAB), BF16),
            pltpu.VMEM((len(dils), SLAB // LANES, tm, LANES), F32),
            pltpu.VMEM((len(dils), tm, LANES), F32),
        ],
        compiler_params=_params("arbitrary"),
        name="even_out_prompt",
    )(za, za, *[o for o, _ in attn], *[l for _, l in attn], pool_w, pool_scale, w_out, x)


def _odd_out_prompt_kernel(u_ref, vn_ref, go_ref, hd_ref, hdh_ref,
                           ws_ref, bs_ref, cw_ref, wo_ref, x_ref, out_ref,
                           ext_ref, yc_ref, yd_ref, *, tiles_per_seq):
    tm = u_ref.shape[0]
    it = pl.program_id(0) % tiles_per_seq

    ti = lax.broadcasted_iota(jnp.int32, (CHUNK, CHUNK), 0)
    si = lax.broadcasted_iota(jnp.int32, (CHUNK, CHUNK), 1)
    for g in range(C_GROUPS):
        cols = slice(g * LANES, (g + 1) * LANES)
        wm = jnp.where(si <= ti, ws_ref[g], 0.0).astype(BF16)
        for c in range(tm // CHUNK):
            rows = slice(c * CHUNK, (c + 1) * CHUNK)
            sp = jnp.dot(wm, vn_ref[rows, cols].astype(BF16), preferred_element_type=F32)
            sp = sp + bs_ref[:, cols]
            yc_ref[rows, cols] = (u_ref[rows, cols] * sp).astype(BF16)

    hd = hd_ref[...]
    halo = hdh_ref[...]
    ext_ref[0:CONV_HALO, :] = jnp.where(it == 0, jnp.zeros_like(halo), halo)
    ext_ref[CONV_HALO:, :] = hd
    conv = cw_ref[CONV_W - 1:CONV_W, :] * hd
    for j in range(CONV_W - 1):
        off = CONV_HALO - (CONV_W - 1) + j
        conv = conv + cw_ref[j:j + 1, :] * ext_ref[off:off + tm, :]
    yd_ref[...] = (go_ref[...] * conv).astype(BF16)

    y = jnp.dot(yc_ref[...], wo_ref[0:SLAB, :], preferred_element_type=F32)
    y = y + jnp.dot(yd_ref[...], wo_ref[SLAB:, :], preferred_element_type=F32)
    out_ref[...] = x_ref[...] + y


def _odd_out_prompt(z, ws, bs_rows, conv_w, w_out, x, seq, tm):
    T = x.shape[0]
    tiles_per_seq = seq // tm
    hb = tm // CONV_HALO

    def slab(s):
        return pl.BlockSpec((None, tm, SLAB), lambda i: (s, i, 0))

    return pl.pallas_call(
        functools.partial(_odd_out_prompt_kernel, tiles_per_seq=tiles_per_seq),
        grid=(T // tm,),
        in_specs=[
            slab(0), slab(1), slab(2), slab(3),
            pl.BlockSpec((None, CONV_HALO, SLAB), lambda i: (3, jnp.maximum(i * hb - 1, 0), 0)),
            _resident((C_GROUPS, CHUNK, CHUNK)),
            _resident((CHUNK, SLAB)),
            _resident((CONV_W, SLAB)),
            _resident((2 * SLAB, D_MODEL)),
            pl.BlockSpec((tm, D_MODEL), lambda i: (i, 0)),
        ],
        out_specs=pl.BlockSpec((tm, D_MODEL), lambda i: (i, 0)),
        out_shape=jax.ShapeDtypeStruct((T, D_MODEL), F32),
        scratch_shapes=[
            pltpu.VMEM((tm + CONV_HALO, SLAB), F32),
            pltpu.VMEM((tm, SLAB), BF16),
            pltpu.VMEM((tm, SLAB), BF16),
        ],
        compiler_params=_params("arbitrary"),
        name="odd_out_prompt",
    )(z, z, z, z, z, ws, bs_rows, conv_w, w_out, x)


def _ffn_ple_kernel(x_ref, p_ref, gf_ref, w1_ref, w2_ref, gp_ref, wg_ref, wp_ref, gl_ref,
                    out_ref, *, final_norm, tf):
    x = x_ref[...]
    hn = _rms(x, gf_ref[...]).astype(BF16)
    acc = None
    for c in range(D_FF // tf):
        h1 = jnp.dot(hn, w1_ref[:, c * tf:(c + 1) * tf], preferred_element_type=F32)
        h1 = jnp.square(jnp.maximum(h1, 0.0)).astype(BF16)
        part = jnp.dot(h1, w2_ref[c * tf:(c + 1) * tf, :], preferred_element_type=F32)
        acc = part if acc is None else acc + part
    r = x + acc
    hp = _rms(r, gp_ref[...]).astype(BF16)
    gate = jax.nn.sigmoid(jnp.dot(hp, wg_ref[...], preferred_element_type=F32))
    proj = jnp.dot(p_ref[...].astype(BF16), wp_ref[...], preferred_element_type=F32)
    r = r + gate * proj
    if final_norm:
        r = _rms(r, gl_ref[...])
    out_ref[...] = r


def _ffn_ple(x, p, g_ffn, w1, w2, g_ple, wg, wp, g_last, final_norm, tm, tf):
    T = x.shape[0]
    return pl.pallas_call(
        functools.partial(_ffn_ple_kernel, final_norm=final_norm, tf=tf),
        grid=(T // tm,),
        in_specs=[
            pl.BlockSpec((tm, D_MODEL), lambda i: (i, 0)),
            pl.BlockSpec((tm, PLE_DIM), lambda i: (i, 0)),
            _resident((1, D_MODEL)),
            _resident((D_MODEL, D_FF)),
            _resident((D_FF, D_MODEL)),
            _resident((1, D_MODEL)),
            _resident((D_MODEL, D_MODEL)),
            _resident((PLE_DIM, D_MODEL)),
            _resident((1, D_MODEL)),
        ],
        out_specs=pl.BlockSpec((tm, D_MODEL), lambda i: (i, 0)),
        out_shape=jax.ShapeDtypeStruct((T, D_MODEL), F32),
        compiler_params=_params("arbitrary"),
        name="ffn_ple",
    )(x, p, g_ffn, w1, w2, g_ple, wg, wp, g_last)


def _attn_sample_kernel(q_ref, kv_ref, c0_ref, c1_ref, c2_ref, o_ref):
    n_tok = q_ref.shape[1]
    n_rows = n_tok * HEADS
    scale = HEAD_DIM ** -0.5
    sub = lax.broadcasted_iota(jnp.int32, (HEADS, SLAB), 0)
    lane_head = lax.broadcasted_iota(jnp.int32, (HEADS, SLAB), 1) // HEAD_DIM
    own = sub == lane_head
    row_tok = lax.broadcasted_iota(jnp.int32, (n_rows, 1), 0) // HEADS

    caches = (c0_ref, c1_ref, c2_ref)
    dils = tuple(d for _, d in DIL_CFG)
    m = jnp.full((n_rows, 1), NEG, F32)
    s_cache, s_new = [], []
    for g, (c_ref, dil) in enumerate(zip(caches, dils)):
        qg = q_ref[g] * scale
        qbd = jnp.concatenate(
            [jnp.where(own, jnp.broadcast_to(qg[t:t + 1, :], (HEADS, SLAB)), 0.0)
             for t in range(n_tok)], axis=0)
        s = jnp.dot(qbd.astype(BF16), c_ref[0].astype(BF16), preferred_element_type=F32)
        pos = lax.broadcasted_iota(jnp.int32, s.shape, 1)
        valid = (pos >= row_tok) if dil == 1 else (jnp.bitwise_and(pos, dil - 1) == row_tok)
        s = jnp.where(valid, s, NEG)
        m = jnp.maximum(m, jnp.max(s, axis=1, keepdims=True))
        s_cache.append(s)
        kn = kv_ref[2 * g]
        for tp in range(n_tok):
            sn = jnp.sum(qbd * kn[tp:tp + 1, :], axis=1, keepdims=True)
            ok = (row_tok >= tp) if dil == 1 else (row_tok == tp)
            sn = jnp.where(ok, sn, NEG)
            m = jnp.maximum(m, sn)
            s_new.append((g, tp, sn))

    acc = jnp.zeros((n_rows, SLAB), F32)
    den = jnp.zeros((n_rows, 1), F32)
    for g, c_ref in enumerate(caches):
        e = jnp.exp(s_cache[g] - m)
        den = den + jnp.sum(e, axis=1, keepdims=True)
        acc = acc + lax.dot_general(e.astype(BF16), c_ref[1].astype(BF16),
                                    (((1,), (1,)), ((), ())), preferred_element_type=F32)
    for g, tp, sn in s_new:
        e = jnp.exp(sn - m)
        den = den + e
        acc = acc + e * kv_ref[2 * g + 1][tp:tp + 1, :]
    res = acc * (1.0 / den)
    for t in range(n_tok):
        rows = res[t * HEADS:(t + 1) * HEADS, :]
        o_ref[t:t + 1, :] = jnp.sum(jnp.where(own, rows, 0.0), axis=0, keepdims=True)


def _attn_sample(qn, kvn, c0, c1, c2):
    n_seq, _, n_tok, _ = qn.shape

    def cache_spec(c):
        return pl.BlockSpec((None, 2, SLAB, c.shape[-1]), lambda n: (n, 0, 0, 0))

    return pl.pallas_call(
        _attn_sample_kernel,
        grid=(n_seq,),
        in_specs=[
            pl.BlockSpec((None, 3, n_tok, SLAB), lambda n: (n, 0, 0, 0)),
            pl.BlockSpec((None, 6, n_tok, SLAB), lambda n: (n, 0, 0, 0)),
            cache_spec(c0), cache_spec(c1), cache_spec(c2),
        ],
        out_specs=pl.BlockSpec((None, n_tok, SLAB), lambda n: (n, 0, 0)),
        out_shape=jax.ShapeDtypeStruct((n_seq, n_tok, SLAB), F32),
        compiler_params=_params("arbitrary"),
        name="attn_sample",
    )(qn, kvn, c0, c1, c2)


def _even_out_sample_kernel(a_ref, ctx_ref, yb_ref, pw_ref, ps_ref, wo_ref, x_ref, out_ref,
                            *, n_seq, n_tok):
    def ext_row(e, cols):
        if e >= POOL_STATE:
            t = e - POOL_STATE
            return a_ref[t * n_seq:(t + 1) * n_seq, cols]
        return ctx_ref[e, :, cols]

    for t in range(n_tok):
        rows = slice(t * n_seq, (t + 1) * n_seq)
        ya = []
        for gi, w in enumerate(POOL_WINDOWS):
            cols = slice(gi * POOL_GROUP, (gi + 1) * POOL_GROUP)
            terms = [ext_row(POOL_STATE + t - k, cols) for k in range(1, w)]
            ya.append(_pool_mix(terms, a_ref[rows, cols], float(w), pw_ref, ps_ref, gi))
        y = jnp.zeros((n_seq, D_MODEL), F32)
        for gi in range(len(POOL_WINDOWS)):
            y = y + jnp.dot(ya[gi].astype(BF16), wo_ref[gi * POOL_GROUP:(gi + 1) * POOL_GROUP, :],
                            preferred_element_type=F32)
        yb = yb_ref[:, t * SLAB:(t + 1) * SLAB].astype(BF16)
        y = y + jnp.dot(yb, wo_ref[SLAB:, :], preferred_element_type=F32)
        out_ref[rows, :] = x_ref[rows, :] + y


def _even_out_sample(z, ctx, yb, pool_w, pool_scale, w_out, x, n_seq, n_tok):
    T = x.shape[0]
    full = lambda *shape: pl.BlockSpec(shape, lambda i: (0,) * len(shape))
    return pl.pallas_call(
        functools.partial(_even_out_sample_kernel, n_seq=n_seq, n_tok=n_tok),
        grid=(1,),
        in_specs=[
            full(T, SLAB),
            full(POOL_STATE, n_seq, SLAB),
            full(n_seq, n_tok * SLAB),
            full(len(POOL_WINDOWS), POOL_GROUP, POOL_GROUP),
            full(1, SLAB),
            full(2 * SLAB, D_MODEL),
            full(T, D_MODEL),
        ],
        out_specs=full(T, D_MODEL),
        out_shape=jax.ShapeDtypeStruct((T, D_MODEL), F32),
        compiler_params=_params("arbitrary"),
        name="even_out_sample",
    )(z, ctx, yb, pool_w, pool_scale, w_out, x)


def _odd_out_sample_kernel(z_ref, ctx_ref, coef_ref, bias_ref, cw_ref, wo_ref, x_ref,
                           out_ref, *, n_seq, n_tok, mix_terms):
    def ext_row(e):
        if e >= CONV_W - 1:
            t = e - (CONV_W - 1)
            return z_ref[3, t * n_seq:(t + 1) * n_seq, :]
        return ctx_ref[:, e * SLAB:(e + 1) * SLAB]

    for t in range(n_tok):
        rows = slice(t * n_seq, (t + 1) * n_seq)
        sp = jnp.zeros((n_seq, SLAB), F32) + bias_ref[t:t + 1, :]
        for s in mix_terms[t]:
            r = t * n_tok + s
            sp = sp + coef_ref[r:r + 1, :] * z_ref[1, s * n_seq:(s + 1) * n_seq, :]
        yc = z_ref[0, rows, :] * sp
        conv = jnp.zeros((n_seq, SLAB), F32)
        for j in range(CONV_W):
            conv = conv + cw_ref[j:j + 1, :] * ext_row(t + j)
        yd = z_ref[2, rows, :] * conv
        y = jnp.dot(yc.astype(BF16), wo_ref[0:SLAB, :], preferred_element_type=F32)
        y = y + jnp.dot(yd.astype(BF16), wo_ref[SLAB:, :], preferred_element_type=F32)
        out_ref[rows, :] = x_ref[rows, :] + y


def _odd_out_sample(z, ctx, coef, bias, conv_w, w_out, x, n_seq, n_tok, mix_terms):
    T = x.shape[0]
    full = lambda *shape: pl.BlockSpec(shape, lambda i: (0,) * len(shape))
    return pl.pallas_call(
        functools.partial(_odd_out_sample_kernel, n_seq=n_seq, n_tok=n_tok, mix_terms=mix_terms),
        grid=(1,),
        in_specs=[
            full(4, T, SLAB),
            full(n_seq, (CONV_W - 1) * SLAB),
            full(n_tok * n_tok, SLAB),
            full(n_tok, SLAB),
            full(CONV_W, SLAB),
            full(2 * SLAB, D_MODEL),
            full(T, D_MODEL),
        ],
        out_specs=full(T, D_MODEL),
        out_shape=jax.ShapeDtypeStruct((T, D_MODEL), F32),
        compiler_params=_params("arbitrary"),
        name="odd_out_sample",
    )(z, ctx, coef, bias, conv_w, w_out, x)


def _rope_tables(pos, split):
    half = HEAD_DIM // 2
    inv = jnp.power(jnp.float32(ROPE_THETA), -jnp.arange(half, dtype=F32) / half)
    ang = pos.astype(F32)[:, None] * inv[None, :]
    cos = jnp.cos(ang)
    sin = jnp.sin(ang)
    if split:
        return jnp.tile(cos, (1, LANES // half)), jnp.tile(sin, (1, LANES // half))
    cos_t = jnp.tile(jnp.concatenate([cos, cos], axis=-1), (1, LANES // HEAD_DIM))
    sin_t = jnp.tile(jnp.concatenate([-sin, sin], axis=-1), (1, LANES // HEAD_DIM))
    return cos_t, sin_t


def _split_qk_columns(w):
    d = w.shape[0]
    half = HEAD_DIM // 2
    rest = w[:, SLAB:].reshape(d, len(DIL_CFG), 3, HEADS, 2, half)
    qk = rest[:, :, :2].transpose(0, 1, 2, 4, 3, 5)
    rest = jnp.concatenate([qk.reshape(d, len(DIL_CFG), 2, SLAB),
                            rest[:, :, 2:].reshape(d, len(DIL_CFG), 1, SLAB)], axis=2)
    return jnp.concatenate([w[:, :SLAB], rest.reshape(d, -1)], axis=1)


def kernel(x_prompt, x_sample, cache_kv_w128, cache_kv_w512, cache_kv_w2048, state_pool, state_conv,
           p_prompt, p_sample, ev_w_in, ev_pool_w, ev_pool_scale, ev_w_out, od_w_in, od_ln_g, od_ln_b,
           od_ws, od_bs, od_conv_w, od_w_out, norm_mix, norm_ffn, norm_ple, ffn_w1, ffn_w2,
           ple_w_proj, ple_w_gate, norm_final):
    n_p, seq, _ = x_prompt.shape
    n_s, n_tok, _ = x_sample.shape
    depth = norm_mix.shape[0]
    tp = n_p * seq
    ts = n_s * n_tok

    bf = lambda w: w.astype(BF16)
    row = lambda v: v.reshape(1, -1)
    ev_w_in_b, ev_pool_w_b, ev_w_out_b = bf(ev_w_in), bf(ev_pool_w), bf(ev_w_out)
    od_w_in_b, od_w_out_b = bf(od_w_in), bf(od_w_out)
    w1_b, w2_b, wg_b, wp_b = bf(ffn_w1), bf(ffn_w2), bf(ple_w_gate), bf(ple_w_proj)

    cos_p, sin_p = _rope_tables(jnp.arange(seq), True)
    pos_s = [PAST_LEN + t for t in range(n_tok)]
    cos_s, sin_s = _rope_tables(jnp.repeat(jnp.asarray(pos_s, jnp.int32), n_s), False)

    mix_terms = tuple(
        tuple(s for s in range(n_tok)
              if pos_s[s] // CHUNK == pos_s[t] // CHUNK and pos_s[s] % CHUNK <= pos_s[t] % CHUNK)
        for t in range(n_tok))
    local = [p % CHUNK for p in pos_s]

    rp = x_prompt.reshape(tp, D_MODEL)
    rs = x_sample.transpose(1, 0, 2).reshape(ts, D_MODEL)
    pp = p_prompt.reshape(depth, tp, PLE_DIM)
    ps = p_sample.transpose(0, 2, 1, 3).reshape(depth, ts, PLE_DIM)

    tm_p, tm_mix, tm_ffn, tf = 512, 512, 512, 512
    kv_p = [[] for _ in DIL_CFG]
    kv_s = [[] for _ in DIL_CFG]
    pool_p, pool_s, conv_p, conv_s, cv_s = [], [], [], [], []

    for i in range(depth):
        g_mix = row(norm_mix[i])
        if i % 2 == 0:
            e = i // 2
            pscale = row(ev_pool_scale[e])
            dils = tuple(d for _, d in DIL_CFG)
            w_split = bf(_split_qk_columns(ev_w_in[e]))
            za, *groups = _proj_even(rp, g_mix, w_split, cos_p, sin_p, tm_p, n_p, dils, True)
            attn = []
            for grp, dil in zip(groups, dils):
                if dil == 1:
                    grp = grp.reshape(3, n_p, 1, seq, SLAB)
                attn.append(_attn_prompt(grp, dil))
            rp = _even_out_prompt(za, attn, ev_pool_w_b[e], pscale, ev_w_out_b[e], rp, seq, tm_mix)
            for g, ((win, dil), grp) in enumerate(zip(DIL_CFG, groups)):
                keep = min(win, seq)
                if dil == 1:
                    tail = grp.reshape(3, n_p, seq, SLAB)[1:, :, seq - keep:]
                else:
                    tail = grp[1:, :, :, (seq - keep) // dil:].transpose(0, 1, 3, 2, 4)
                k = tail[0].reshape(n_p, keep, 2, HEADS, HEAD_DIM // 2).transpose(0, 1, 3, 2, 4)
                k = k.reshape(n_p, keep, HEADS, HEAD_DIM)
                v = tail[1].reshape(n_p, keep, HEADS, HEAD_DIM)
                kv_p[g].append(jnp.stack([k, v], axis=2))
            pool_p.append(za.reshape(n_p, seq, SLAB)[:, seq - POOL_STATE:])
            zas, *sgroups = _proj_even(rs, g_mix, ev_w_in_b[e], cos_s, sin_s, ts, 1, (1, 1, 1),
                                       False)
            zn = jnp.stack(sgroups).reshape(3, 3, n_tok, n_s, SLAB).transpose(3, 0, 1, 2, 4)
            qn = zn[:, :, 0]
            kvn = zn[:, :, 1:].reshape(n_s, 6, n_tok, SLAB)
            native = lambda c: c.transpose(0, 2, 3, 4, 1).reshape(n_s, 2, SLAB, c.shape[1])
            yb = _attn_sample(qn, kvn, native(cache_kv_w128[e]), native(cache_kv_w512[e]),
                              native(cache_kv_w2048[e]))
            rs = _even_out_sample(zas, state_pool[e].transpose(1, 0, 2),
                                  yb.reshape(n_s, n_tok * SLAB), ev_pool_w_b[e], pscale,
                                  ev_w_out_b[e], rs, n_s, n_tok)
            for g in range(len(DIL_CFG)):
                k = zn[:, g, 1].reshape(n_s, n_tok, HEADS, HEAD_DIM)
                v = zn[:, g, 2].reshape(n_s, n_tok, HEADS, HEAD_DIM)
                kv_s[g].append(jnp.stack([k, v], axis=2))
            a_n = zas.reshape(n_tok, n_s, SLAB).transpose(1, 0, 2)
            pool_s.append(jnp.concatenate([state_pool[e], a_n], axis=1)[:, -POOL_STATE:])
        else:
            o = i // 2
            ln_g, ln_b = row(od_ln_g[o]), row(od_ln_b[o])
            bs_rows = jnp.repeat(od_bs[o].T, LANES, axis=1)
            z = _proj_odd(rp, g_mix, od_w_in_b[o], ln_g, ln_b, tm_p)
            rp = _odd_out_prompt(z, od_ws[o], bs_rows, od_conv_w[o], od_w_out_b[o], rp, seq, tm_mix)
            conv_p.append(z[3].reshape(n_p, seq, SLAB)[:, seq - (CONV_W - 1):])
            zs = _proj_odd(rs, g_mix, od_w_in_b[o], ln_g, ln_b, ts)
            coef = jnp.stack([jnp.repeat(od_ws[o][:, local[t], local[s]], LANES)
                              for t in range(n_tok) for s in range(n_tok)])
            bias = jnp.stack([bs_rows[local[t]] for t in range(n_tok)])
            rs = _odd_out_sample(zs, state_conv[o].reshape(n_s, (CONV_W - 1) * SLAB), coef, bias,
                                 od_conv_w[o], od_w_out_b[o], rs, n_s, n_tok, mix_terms)
            hd_n = zs[3].reshape(n_tok, n_s, SLAB).transpose(1, 0, 2)
            conv_s.append(jnp.concatenate([state_conv[o], hd_n], axis=1)[:, -(CONV_W - 1):])
            cv_s.append(zs[1].reshape(n_tok, n_s, SLAB).transpose(1, 0, 2))

        last = i == depth - 1
        args = (row(norm_ffn[i]), w1_b[i], w2_b[i], row(norm_ple[i]), wg_b[i], wp_b[i],
                row(norm_final), last)
        rp = _ffn_ple(rp, pp[i], *args, tm_ffn, tf)
        rs = _ffn_ple(rs, ps[i], *args, ts, tf)

    y_prompt = rp.reshape(n_p, seq, D_MODEL)
    y_sample = rs.reshape(n_tok, n_s, D_MODEL).transpose(1, 0, 2)
    st = lambda lst: jnp.stack(lst, axis=0)
    return (y_prompt, y_sample, st(kv_p[0]), st(kv_p[1]), st(kv_p[2]),
            st(kv_s[0]), st(kv_s[1]), st(kv_s[2]),
            st(pool_p), st(pool_s), st(conv_p), st(conv_s), st(cv_s))
```

```python
import functools
import math

import jax
import jax.numpy as jnp
from jax import lax
from jax.experimental import pallas as pl
from jax.experimental.pallas import tpu as pltpu

F32 = jnp.float32
BF16 = jnp.bfloat16

D_MODEL = 1024
D_FF = 4 * D_MODEL
PLE_DIM = 256
EPS = 1e-6
ROPE_THETA = 10000.0
PAST_LEN = 2048

SLAB = 512
POOL_WINDOWS = (2, 4, 8, 16)
POOL_GROUP = 128
POOL_STATE = 15
POOL_HALO = 16
DIL_CFG = ((128, 1), (512, 4), (2048, 16))
HEADS = 8
HEAD_DIM = 64
ATT_BLOCK = 128
ATT_ROWS = 512
CHUNK = 128
C_GROUPS = 4
CONV_W = 3
CONV_HALO = 8
LANES = 128
NEG = -1e30

VMEM_LIMIT = 52 * 1024 * 1024


def _params(*sem):
    return pltpu.CompilerParams(dimension_semantics=sem, vmem_limit_bytes=VMEM_LIMIT)


def _rms(x, g):
    ms = jnp.mean(x * x, axis=-1, keepdims=True)
    return x * lax.rsqrt(ms + EPS) * g


def _gelu(x):
    c = math.sqrt(2.0 / math.pi)
    return 0.5 * x * (1.0 + jnp.tanh(c * (x + 0.044715 * (x * x * x))))


def _proj_even_kernel(x_ref, g_ref, w_ref, cos_ref, sin_ref, za_ref, g0_ref, g1_ref, g2_ref,
                      zs_ref, *, dils, split):
    tm = x_ref.shape[0]
    nl = SLAB // LANES
    hn = _rms(x_ref[...], g_ref[...]).astype(BF16)
    za_ref[...] = jnp.dot(hn, w_ref[:, 0:SLAB], preferred_element_type=F32)

    def rotate(chunks):
        cos = cos_ref[...]
        sin = sin_ref[...]
        if split:
            h = nl // 2
            return ([chunks[i] * cos - chunks[i + h] * sin for i in range(h)]
                    + [chunks[i] * cos + chunks[i - h] * sin for i in range(h, nl)])
        lane = lax.broadcasted_iota(jnp.int32, cos.shape, 1)
        first_half = jnp.bitwise_and(lane, HEAD_DIM - 1) < (HEAD_DIM // 2)
        out = []
        for zc in chunks:
            partner = jnp.where(first_half,
                                pltpu.roll(zc, LANES - HEAD_DIM // 2, 1),
                                pltpu.roll(zc, HEAD_DIM // 2, 1))
            out.append(zc * cos + partner * sin)
        return out

    slot = 0
    for g, (out_ref, dil) in enumerate(zip((g0_ref, g1_ref, g2_ref), dils)):
        for c in range(3):
            col0 = (1 + 3 * g + c) * SLAB
            z = jnp.dot(hn, w_ref[:, col0:col0 + SLAB], preferred_element_type=F32)
            chunks = [z[:, i * LANES:(i + 1) * LANES] for i in range(nl)]
            if c < 2:
                chunks = rotate(chunks)
            for i, zc in enumerate(chunks):
                cols = slice(i * LANES, (i + 1) * LANES)
                if dil == 1:
                    out_ref[c, :, cols] = zc
                else:
                    zs_ref[slot] = zc
                    for r in range(dil):
                        out_ref[c, r, :, cols] = zs_ref[slot, pl.ds(r, tm // dil, stride=dil), :]
                    slot += 1


def _resident(shape):
    return pl.BlockSpec(shape, lambda *_: (0,) * len(shape), pipeline_mode=pl.Buffered(1))


def _proj_even(x, g, w, cos, sin, tm, n_seq, dils, split):
    T = x.shape[0]
    ntab = cos.shape[0] // tm
    tps = T // n_seq // tm

    def group_spec(dil):
        if dil == 1:
            return pl.BlockSpec((3, tm, SLAB), lambda i: (0, i, 0))
        return pl.BlockSpec((3, None, dil, tm // dil, SLAB), lambda i: (0, i // tps, 0, i % tps, 0))

    def group_shape(dil):
        if dil == 1:
            return jax.ShapeDtypeStruct((3, T, SLAB), F32)
        return jax.ShapeDtypeStruct((3, n_seq, dil, T // n_seq // dil, SLAB), F32)

    n_slots = max(1, 3 * (SLAB // LANES) * sum(d > 1 for d in dils))
    return pl.pallas_call(
        functools.partial(_proj_even_kernel, dils=dils, split=split),
        grid=(T // tm,),
        in_specs=[
            pl.BlockSpec((tm, D_MODEL), lambda i: (i, 0)),
            _resident((1, D_MODEL)),
            _resident(w.shape),
            pl.BlockSpec((tm, LANES), lambda i: (i % ntab, 0)),
            pl.BlockSpec((tm, LANES), lambda i: (i % ntab, 0)),
        ],
        out_specs=[pl.BlockSpec((tm, SLAB), lambda i: (i, 0))] + [group_spec(d) for d in dils],
        out_shape=[jax.ShapeDtypeStruct((T, SLAB), F32)] + [group_shape(d) for d in dils],
        scratch_shapes=[pltpu.VMEM((n_slots, tm, LANES), F32)],
        compiler_params=_params("arbitrary"),
        name="proj_even",
    )(x, g, w, cos, sin)


def _proj_odd_kernel(x_ref, g_ref, w_ref, lng_ref, lnb_ref, o_ref):
    hn = _rms(x_ref[...], g_ref[...]).astype(BF16)

    def slab(s):
        return jnp.dot(hn, w_ref[:, s * SLAB:(s + 1) * SLAB], preferred_element_type=F32)

    o_ref[0] = _gelu(slab(0))
    zv = slab(1)
    for c in range(C_GROUPS):
        sl = slice(c * LANES, (c + 1) * LANES)
        v = _gelu(zv[:, sl])
        mu = jnp.mean(v, axis=-1, keepdims=True)
        dv = v - mu
        var = jnp.mean(dv * dv, axis=-1, keepdims=True)
        o_ref[1, :, sl] = dv * lax.rsqrt(var + EPS) * lng_ref[:, sl] + lnb_ref[:, sl]
    o_ref[2] = slab(2)
    o_ref[3] = slab(3) * slab(4)


def _proj_odd(x, g, w, ln_g, ln_b, tm):
    T = x.shape[0]
    return pl.pallas_call(
        _proj_odd_kernel,
        grid=(T // tm,),
        in_specs=[
            pl.BlockSpec((tm, D_MODEL), lambda i: (i, 0)),
            _resident((1, D_MODEL)),
            _resident(w.shape),
            _resident((1, SLAB)),
            _resident((1, SLAB)),
        ],
        out_specs=pl.BlockSpec((4, tm, SLAB), lambda i: (0, i, 0)),
        out_shape=jax.ShapeDtypeStruct((4, T, SLAB), F32),
        compiler_params=_params("arbitrary"),
        name="proj_odd",
    )(x, g, w, ln_g, ln_b)


def _attn_prompt_kernel(q_ref, kp_ref, kc_ref, vp_ref, vc_ref, o_ref, lse_ref,
                        qst_ref, ks_ref, vs_ref, bias_ref, s_ref, e_ref):
    step = pl.program_id(2)
    sub = ATT_ROWS // ATT_BLOCK
    half_rot = HEAD_DIM // 2
    heads_per_blk = LANES // half_rot
    n_blk = HEADS // heads_per_blk
    log2e = 1.4426950408889634
    ln2 = 0.6931471805599453
    scale = HEAD_DIM ** -0.5 * log2e

    qlane = lax.broadcasted_iota(jnp.int32, (ATT_ROWS, 2 * LANES), 1)
    head_in_blk = jnp.bitwise_and(qlane, LANES - 1) // half_rot
    for j in range(n_blk):
        lo = slice(j * LANES, (j + 1) * LANES)
        hi = slice((n_blk + j) * LANES, (n_blk + j + 1) * LANES)
        dst = slice(2 * j * LANES, (2 * j + 2) * LANES)
        q2 = (jnp.concatenate([q_ref[:, lo], q_ref[:, hi]], axis=1) * scale).astype(BF16)
        for hq in range(heads_per_blk):
            qm = jnp.where(head_in_blk == hq, q2, jnp.zeros_like(q2))
            for b in range(sub):
                r0 = ((j * sub + b) * heads_per_blk + hq) * ATT_BLOCK
                qst_ref[r0:r0 + ATT_BLOCK, :] = qm[b * ATT_BLOCK:(b + 1) * ATT_BLOCK, :]
        ks_ref[0:ATT_BLOCK, dst] = jnp.concatenate([kp_ref[:, lo], kp_ref[:, hi]],
                                                   axis=1).astype(BF16)
        ks_ref[ATT_BLOCK:, dst] = jnp.concatenate([kc_ref[:, lo], kc_ref[:, hi]],
                                                  axis=1).astype(BF16)
    vs_ref[0:ATT_BLOCK, :] = vp_ref[...].astype(BF16)
    vs_ref[ATT_BLOCK:, :] = vc_ref[...].astype(BF16)

    qi = lax.broadcasted_iota(jnp.int32, (ATT_BLOCK, 2 * ATT_BLOCK), 0)
    ki = lax.broadcasted_iota(jnp.int32, (ATT_BLOCK, 2 * ATT_BLOCK), 1)
    rel = qi + ATT_BLOCK - ki
    band = jnp.logical_and(rel >= 0, rel <= ATT_BLOCK)
    bias_ref[0] = jnp.where(band, 0.0, NEG)
    bias_ref[1] = jnp.where(jnp.logical_and(band, ki >= ATT_BLOCK), 0.0, NEG)
    lane = lax.broadcasted_iota(jnp.int32, (ATT_BLOCK, LANES), 1)
    low_head = lane < HEAD_DIM
    blk_rows = heads_per_blk * ATT_BLOCK

    first = (step == 0).astype(jnp.int32)
    hb = ATT_BLOCK // 2
    lane_hb = lax.broadcasted_iota(jnp.int32, (hb, LANES), 1)

    def scores(b):
        for j in range(n_blk):
            base = (j * sub + b) * blk_rows
            k2 = ks_ref[b * ATT_BLOCK:(b + 2) * ATT_BLOCK, 2 * j * LANES:(2 * j + 2) * LANES]
            s_ref[b, j * blk_rows:(j + 1) * blk_rows, :] = lax.dot_general(
                qst_ref[base:base + blk_rows, :], k2, (((1,), (1,)), ((), ())),
                preferred_element_type=F32)

    def softmax(b):
        which = first if b == 0 else 0
        for rh in range(2):
            lse_tile = jnp.zeros((hb, LANES), F32)
            for h in range(HEADS):
                rows = slice(h * ATT_BLOCK + rh * hb, h * ATT_BLOCK + (rh + 1) * hb)
                s = s_ref[b, rows, :] + bias_ref[which, rh * hb:(rh + 1) * hb, :]
                m = jnp.max(s, axis=-1, keepdims=True)
                e = jnp.exp2(s - m)
                l = jnp.sum(e, axis=-1, keepdims=True)
                e_ref[b, rows, :] = (e * (1.0 / l)).astype(BF16)
                lse_tile = jnp.where(lane_hb == h, m * ln2 + jnp.log(l), lse_tile)
            lse_ref[b * ATT_BLOCK + rh * hb:b * ATT_BLOCK + (rh + 1) * hb, :] = lse_tile

    def weighted_values(b):
        for hp in range(HEADS // 2):
            cols = slice(hp * LANES, (hp + 1) * LANES)
            v2 = vs_ref[b * ATT_BLOCK:(b + 2) * ATT_BLOCK, cols]
            pv = jnp.dot(e_ref[b, 2 * hp * ATT_BLOCK:(2 * hp + 2) * ATT_BLOCK, :], v2,
                         preferred_element_type=F32)
            o_ref[b * ATT_BLOCK:(b + 1) * ATT_BLOCK, cols] = jnp.where(
                low_head, pv[0:ATT_BLOCK], pv[ATT_BLOCK:])

    for t in range(sub + 2):
        if t < sub:
            scores(t)
        if 0 <= t - 1 < sub:
            softmax(t - 1)
        if 0 <= t - 2 < sub:
            weighted_values(t - 2)


def _attn_prompt(qkv, dil):
    _, n_seq, _, m_rows, _ = qkv.shape
    steps = m_rows // ATT_ROWS
    sub = ATT_ROWS // ATT_BLOCK

    def cur(slab):
        return pl.BlockSpec((None, None, None, ATT_ROWS, SLAB), lambda n, r, b: (slab, n, r, b, 0))

    def prev(slab):
        return pl.BlockSpec((None, None, None, ATT_BLOCK, SLAB),
                            lambda n, r, b: (slab, n, r, jnp.maximum(b * sub - 1, 0), 0))

    return pl.pallas_call(
        _attn_prompt_kernel,
        grid=(n_seq, dil, steps),
        in_specs=[cur(0), prev(1), cur(1), prev(2), cur(2)],
        out_specs=[
            pl.BlockSpec((None, None, ATT_ROWS, SLAB), lambda n, r, b: (n, r, b, 0)),
            pl.BlockSpec((None, None, ATT_ROWS, LANES), lambda n, r, b: (n, r, b, 0)),
        ],
        out_shape=[
            jax.ShapeDtypeStruct((n_seq, dil, m_rows, SLAB), F32),
            jax.ShapeDtypeStruct((n_seq, dil, m_rows, LANES), F32),
        ],
        scratch_shapes=[
            pltpu.VMEM((HEADS * ATT_ROWS, 2 * LANES), BF16),
            pltpu.VMEM((ATT_ROWS + ATT_BLOCK, SLAB), BF16),
            pltpu.VMEM((ATT_ROWS + ATT_BLOCK, SLAB), BF16),
            pltpu.VMEM((2, ATT_BLOCK, 2 * ATT_BLOCK), F32),
            pltpu.VMEM((sub, HEADS * ATT_BLOCK, 2 * ATT_BLOCK), F32),
            pltpu.VMEM((sub, HEADS * ATT_BLOCK, 2 * ATT_BLOCK), BF16),
        ],
        compiler_params=_params("arbitrary", "arbitrary", "arbitrary"),
        name=f"attn_prompt_d{dil}",
    )(qkv, qkv, qkv, qkv, qkv)


def _kv_tail_kernel(k_ref, v_ref, o_ref, tok_ref):
    dil, rows, _ = k_ref.shape
    keep = dil * rows
    nl = SLAB // LANES
    half_rot = HEAD_DIM // 2
    heads_per_chunk = LANES // half_rot
    for kv, src in enumerate((k_ref, v_ref)):
        for lc in range(nl):
            for r in range(dil):
                tok_ref[lc, pl.ds(r, rows, stride=dil), :] = src[r, :, lc * LANES:(lc + 1) * LANES]
            for pc in range(keep // LANES):
                pcols = slice(pc * LANES, (pc + 1) * LANES)
                t = tok_ref[lc, pcols, :].T
                if kv == 0:
                    half, hblk = divmod(lc, nl // 2)
                    for hq in range(heads_per_chunk):
                        r0 = (hblk * heads_per_chunk + hq) * HEAD_DIM + half * half_rot
                        o_ref[0, r0:r0 + half_rot, pcols] = t[hq * half_rot:(hq + 1) * half_rot, :]
                else:
                    o_ref[1, lc * LANES:(lc + 1) * LANES, pcols] = t


def _kv_tail(grp, keep):
    _, n_seq, dil, m_rows, _ = grp.shape
    rows = keep // dil
    last = m_rows // rows - 1

    def spec(slab):
        return pl.BlockSpec((None, None, dil, rows, SLAB), lambda n: (slab, n, 0, last, 0))

    return pl.pallas_call(
        _kv_tail_kernel,
        grid=(n_seq,),
        in_specs=[spec(1), spec(2)],
        out_specs=pl.BlockSpec((None, 2, SLAB, keep), lambda n: (n, 0, 0, 0)),
        out_shape=jax.ShapeDtypeStruct((n_seq, 2, SLAB, keep), F32),
        scratch_shapes=[pltpu.VMEM((SLAB // LANES, keep, LANES), F32)],
        compiler_params=_params("arbitrary"),
        name=f"kv_tail_d{dil}",
    )(grp, grp)


def _pool_mix(window_terms, a_cols, cnt, pw_ref, scale_ref, gi):
    acc = a_cols
    for term in window_terms:
        acc = acc + term
    pooled = acc / cnt - a_cols
    cols = slice(gi * POOL_GROUP, (gi + 1) * POOL_GROUP)
    mixed = jnp.dot(pooled.astype(BF16), pw_ref[gi], preferred_element_type=F32)
    return mixed * scale_ref[:, cols]


def _even_out_prompt_kernel(a_ref, halo_ref, o0_ref, o1_ref, o2_ref, l0_ref, l1_ref, l2_ref,
                            pw_ref, ps_ref, wo_ref, x_ref, out_ref, ext_ref, ya_ref, yb_ref,
                            oi_ref, li_ref, *, tiles_per_seq):
    tm = a_ref.shape[0]
    it = pl.program_id(0) % tiles_per_seq
    halo = halo_ref[...]
    ext_ref[0:POOL_HALO, :] = jnp.where(it == 0, jnp.zeros_like(halo), halo)
    ext_ref[POOL_HALO:, :] = a_ref[...]
    pos = it * tm + lax.broadcasted_iota(jnp.int32, (tm, 1), 0)
    for gi, w in enumerate(POOL_WINDOWS):
        cols = slice(gi * POOL_GROUP, (gi + 1) * POOL_GROUP)
        terms = [ext_ref[POOL_HALO - k:POOL_HALO - k + tm, cols] for k in range(1, w)]
        cnt = jnp.minimum(w, pos + 1).astype(F32)
        ya_ref[:, cols] = _pool_mix(terms, a_ref[:, cols], cnt, pw_ref, ps_ref, gi).astype(BF16)

    for g, (o_ref, l_ref) in enumerate(((o0_ref, l0_ref), (o1_ref, l1_ref), (o2_ref, l2_ref))):
        dil = o_ref.shape[0]
        for r in range(dil):
            rows = pl.ds(r, tm // dil, stride=dil)
            li_ref[g, rows, :] = l_ref[r]
            for lc in range(SLAB // LANES):
                oi_ref[g, lc, rows, :] = o_ref[r, :, lc * LANES:(lc + 1) * LANES]

    l0, l1, l2 = li_ref[0], li_ref[1], li_ref[2]
    mx = jnp.maximum(jnp.maximum(l0, l1), l2)
    e0, e1, e2 = jnp.exp(l0 - mx), jnp.exp(l1 - mx), jnp.exp(l2 - mx)
    inv = 1.0 / (e0 + e1 + e2)
    w0, w1 = e0 * inv, e1 * inv
    for h in range(HEADS):
        lc, lo = divmod(h * HEAD_DIM, LANES)
        sl = slice(lo, lo + HEAD_DIM)
        o2 = oi_ref[2, lc, :, sl]
        yb = (o2 + w0[:, h:h + 1] * (oi_ref[0, lc, :, sl] - o2)
              + w1[:, h:h + 1] * (oi_ref[1, lc, :, sl] - o2))
        yb_ref[:, h * HEAD_DIM:(h + 1) * HEAD_DIM] = yb.astype(BF16)

    y = jnp.dot(ya_ref[...], wo_ref[0:SLAB, :], preferred_element_type=F32)
    y = y + jnp.dot(yb_ref[...], wo_ref[SLAB:, :], preferred_element_type=F32)
    out_ref[...] = x_ref[...] + y


def _even_out_prompt(za, attn, pool_w, pool_scale, w_out, x, seq, tm):
    T = x.shape[0]
    tps = seq // tm
    hb = tm // POOL_HALO

    def residue_spec(dil, width):
        return pl.BlockSpec((None, dil, tm // dil, width), lambda i: (i // tps, 0, i % tps, 0))

    dils = [o.shape[1] for o, _ in attn]
    return pl.pallas_call(
        functools.partial(_even_out_prompt_kernel, tiles_per_seq=tps),
        grid=(T // tm,),
        in_specs=[
            pl.BlockSpec((tm, SLAB), lambda i: (i, 0)),
            pl.BlockSpec((POOL_HALO, SLAB), lambda i: (jnp.maximum(i * hb - 1, 0), 0)),
        ]
        + [residue_spec(d, SLAB) for d in dils] + [residue_spec(d, LANES) for d in dils]
        + [
            pl.BlockSpec((len(POOL_WINDOWS), POOL_GROUP, POOL_GROUP), lambda i: (0, 0, 0)),
            pl.BlockSpec((1, SLAB), lambda i: (0, 0)),
            pl.BlockSpec((2 * SLAB, D_MODEL), lambda i: (0, 0)),
            pl.BlockSpec((tm, D_MODEL), lambda i: (i, 0)),
        ],
        out_specs=pl.BlockSpec((tm, D_MODEL), lambda i: (i, 0)),
        out_shape=jax.ShapeDtypeStruct((T, D_MODEL), F32),
        scratch_shapes=[
            pltpu.VMEM((tm + POOL_HALO, SLAB), F32),
            pltpu.VMEM((tm, SLAB), BF16),
            pltpu.VMEM((tm, SLAB), BF16),
            pltpu.VMEM((len(dils), SLAB // LANES, tm, LANES), F32),
            pltpu.VMEM((len(dils), tm, LANES), F32),
        ],
        compiler_params=_params("arbitrary"),
        name="even_out_prompt",
    )(za, za, *[o for o, _ in attn], *[l for _, l in attn], pool_w, pool_scale, w_out, x)


def _odd_out_prompt_kernel(u_ref, vn_ref, go_ref, hd_ref, hdh_ref,
                           ws_ref, bs_ref, cw_ref, wo_ref, x_ref, out_ref,
                           ext_ref, yc_ref, yd_ref, *, tiles_per_seq):
    tm = u_ref.shape[0]
    it = pl.program_id(0) % tiles_per_seq

    ti = lax.broadcasted_iota(jnp.int32, (CHUNK, CHUNK), 0)
    si = lax.broadcasted_iota(jnp.int32, (CHUNK, CHUNK), 1)
    for g in range(C_GROUPS):
        cols = slice(g * LANES, (g + 1) * LANES)
        wm = jnp.where(si <= ti, ws_ref[g], 0.0).astype(BF16)
        for c in range(tm // CHUNK):
            rows = slice(c * CHUNK, (c + 1) * CHUNK)
            sp = jnp.dot(wm, vn_ref[rows, cols].astype(BF16), preferred_element_type=F32)
            sp = sp + bs_ref[:, cols]
            yc_ref[rows, cols] = (u_ref[rows, cols] * sp).astype(BF16)

    hd = hd_ref[...]
    halo = hdh_ref[...]
    ext_ref[0:CONV_HALO, :] = jnp.where(it == 0, jnp.zeros_like(halo), halo)
    ext_ref[CONV_HALO:, :] = hd
    conv = cw_ref[CONV_W - 1:CONV_W, :] * hd
    for j in range(CONV_W - 1):
        off = CONV_HALO - (CONV_W - 1) + j
        conv = conv + cw_ref[j:j + 1, :] * ext_ref[off:off + tm, :]
    yd_ref[...] = (go_ref[...] * conv).astype(BF16)

    y = jnp.dot(yc_ref[...], wo_ref[0:SLAB, :], preferred_element_type=F32)
    y = y + jnp.dot(yd_ref[...], wo_ref[SLAB:, :], preferred_element_type=F32)
    out_ref[...] = x_ref[...] + y


def _odd_out_prompt(z, ws, bs_rows, conv_w, w_out, x, seq, tm):
    T = x.shape[0]
    tiles_per_seq = seq // tm
    hb = tm // CONV_HALO

    def slab(s):
        return pl.BlockSpec((None, tm, SLAB), lambda i: (s, i, 0))

    return pl.pallas_call(
        functools.partial(_odd_out_prompt_kernel, tiles_per_seq=tiles_per_seq),
        grid=(T // tm,),
        in_specs=[
            slab(0), slab(1), slab(2), slab(3),
            pl.BlockSpec((None, CONV_HALO, SLAB), lambda i: (3, jnp.maximum(i * hb - 1, 0), 0)),
            _resident((C_GROUPS, CHUNK, CHUNK)),
            _resident((CHUNK, SLAB)),
            _resident((CONV_W, SLAB)),
            _resident((2 * SLAB, D_MODEL)),
            pl.BlockSpec((tm, D_MODEL), lambda i: (i, 0)),
        ],
        out_specs=pl.BlockSpec((tm, D_MODEL), lambda i: (i, 0)),
        out_shape=jax.ShapeDtypeStruct((T, D_MODEL), F32),
        scratch_shapes=[
            pltpu.VMEM((tm + CONV_HALO, SLAB), F32),
            pltpu.VMEM((tm, SLAB), BF16),
            pltpu.VMEM((tm, SLAB), BF16),
        ],
        compiler_params=_params("arbitrary"),
        name="odd_out_prompt",
    )(z, z, z, z, z, ws, bs_rows, conv_w, w_out, x)


def _ffn_ple_kernel(x_ref, p_ref, gf_ref, w1_ref, w2_ref, gp_ref, wg_ref, wp_ref, gl_ref,
                    out_ref, *, final_norm, tf):
    x = x_ref[...]
    hn = _rms(x, gf_ref[...]).astype(BF16)
    acc = None
    for c in range(D_FF // tf):
        h1 = jnp.dot(hn, w1_ref[:, c * tf:(c + 1) * tf], preferred_element_type=F32)
        h1 = jnp.square(jnp.maximum(h1, 0.0)).astype(BF16)
        part = jnp.dot(h1, w2_ref[c * tf:(c + 1) * tf, :], preferred_element_type=F32)
        acc = part if acc is None else acc + part
    r = x + acc
    hp = _rms(r, gp_ref[...]).astype(BF16)
    gate = jax.nn.sigmoid(jnp.dot(hp, wg_ref[...], preferred_element_type=F32))
    proj = jnp.dot(p_ref[...].astype(BF16), wp_ref[...], preferred_element_type=F32)
    r = r + gate * proj
    if final_norm:
        r = _rms(r, gl_ref[...])
    out_ref[...] = r


def _layer_resident(layer, shape):
    return pl.BlockSpec((None,) + shape, lambda *_: (layer,) + (0,) * len(shape),
                        pipeline_mode=pl.Buffered(1))


def _ffn_ple(layer, x, p, g_ffn, w1, w2, g_ple, wg, wp, g_last, final_norm, tm, tf):
    T = x.shape[0]
    return pl.pallas_call(
        functools.partial(_ffn_ple_kernel, final_norm=final_norm, tf=tf),
        grid=(T // tm,),
        in_specs=[
            pl.BlockSpec((tm, D_MODEL), lambda i: (i, 0)),
            pl.BlockSpec((None, tm, PLE_DIM), lambda i: (layer, i, 0)),
            _layer_resident(layer, (1, D_MODEL)),
            _layer_resident(layer, (D_MODEL, D_FF)),
            _layer_resident(layer, (D_FF, D_MODEL)),
            _layer_resident(layer, (1, D_MODEL)),
            _layer_resident(layer, (D_MODEL, D_MODEL)),
            _layer_resident(layer, (PLE_DIM, D_MODEL)),
            _resident((1, D_MODEL)),
        ],
        out_specs=pl.BlockSpec((tm, D_MODEL), lambda i: (i, 0)),
        out_shape=jax.ShapeDtypeStruct((T, D_MODEL), F32),
        compiler_params=_params("arbitrary"),
        name="ffn_ple",
    )(x, p, g_ffn, w1, w2, g_ple, wg, wp, g_last)


def _attn_sample_kernel(q_ref, kv_ref, c0_ref, c1_ref, c2_ref, o_ref):
    n_tok = q_ref.shape[1]
    n_rows = n_tok * HEADS
    scale = HEAD_DIM ** -0.5
    sub = lax.broadcasted_iota(jnp.int32, (HEADS, SLAB), 0)
    lane_head = lax.broadcasted_iota(jnp.int32, (HEADS, SLAB), 1) // HEAD_DIM
    own = sub == lane_head
    row_tok = lax.broadcasted_iota(jnp.int32, (n_rows, 1), 0) // HEADS

    caches = (c0_ref, c1_ref, c2_ref)
    dils = tuple(d for _, d in DIL_CFG)
    m = jnp.full((n_rows, 1), NEG, F32)
    s_cache, s_new = [], []
    for g, (c_ref, dil) in enumerate(zip(caches, dils)):
        qg = q_ref[g] * scale
        qbd = jnp.concatenate(
            [jnp.where(own, jnp.broadcast_to(qg[t:t + 1, :], (HEADS, SLAB)), 0.0)
             for t in range(n_tok)], axis=0)
        s = jnp.dot(qbd.astype(BF16), c_ref[0].astype(BF16), preferred_element_type=F32)
        pos = lax.broadcasted_iota(jnp.int32, s.shape, 1)
        valid = (pos >= row_tok) if dil == 1 else (jnp.bitwise_and(pos, dil - 1) == row_tok)
        s = jnp.where(valid, s, NEG)
        m = jnp.maximum(m, jnp.max(s, axis=1, keepdims=True))
        s_cache.append(s)
        kn = kv_ref[2 * g]
        for tp in range(n_tok):
            sn = jnp.sum(qbd * kn[tp:tp + 1, :], axis=1, keepdims=True)
            ok = (row_tok >= tp) if dil == 1 else (row_tok == tp)
            sn = jnp.where(ok, sn, NEG)
            m = jnp.maximum(m, sn)
            s_new.append((g, tp, sn))

    acc = jnp.zeros((n_rows, SLAB), F32)
    den = jnp.zeros((n_rows, 1), F32)
    for g, c_ref in enumerate(caches):
        e = jnp.exp(s_cache[g] - m)
        den = den + jnp.sum(e, axis=1, keepdims=True)
        acc = acc + lax.dot_general(e.astype(BF16), c_ref[1].astype(BF16),
                                    (((1,), (1,)), ((), ())), preferred_element_type=F32)
    for g, tp, sn in s_new:
        e = jnp.exp(sn - m)
        den = den + e
        acc = acc + e * kv_ref[2 * g + 1][tp:tp + 1, :]
    res = acc * (1.0 / den)
    for t in range(n_tok):
        rows = res[t * HEADS:(t + 1) * HEADS, :]
        o_ref[t:t + 1, :] = jnp.sum(jnp.where(own, rows, 0.0), axis=0, keepdims=True)


def _attn_sample(qn, kvn, c0, c1, c2):
    n_seq, _, n_tok, _ = qn.shape

    def cache_spec(c):
        return pl.BlockSpec((None, 2, SLAB, c.shape[-1]), lambda n: (n, 0, 0, 0))

    return pl.pallas_call(
        _attn_sample_kernel,
        grid=(n_seq,),
        in_specs=[
            pl.BlockSpec((None, 3, n_tok, SLAB), lambda n: (n, 0, 0, 0)),
            pl.BlockSpec((None, 6, n_tok, SLAB), lambda n: (n, 0, 0, 0)),
            cache_spec(c0), cache_spec(c1), cache_spec(c2),
        ],
        out_specs=pl.BlockSpec((None, n_tok, SLAB), lambda n: (n, 0, 0)),
        out_shape=jax.ShapeDtypeStruct((n_seq, n_tok, SLAB), F32),
        compiler_params=_params("arbitrary"),
        name="attn_sample",
    )(qn, kvn, c0, c1, c2)


def _even_out_sample_kernel(a_ref, ctx_ref, yb_ref, pw_ref, ps_ref, wo_ref, x_ref, out_ref,
                            *, n_seq, n_tok):
    def ext_row(e, cols):
        if e >= POOL_STATE:
            t = e - POOL_STATE
            return a_ref[t * n_seq:(t + 1) * n_seq, cols]
        return ctx_ref[e, :, cols]

    for t in range(n_tok):
        rows = slice(t * n_seq, (t + 1) * n_seq)
        ya = []
        for gi, w in enumerate(POOL_WINDOWS):
            cols = slice(gi * POOL_GROUP, (gi + 1) * POOL_GROUP)
            terms = [ext_row(POOL_STATE + t - k, cols) for k in range(1, w)]
            ya.append(_pool_mix(terms, a_ref[rows, cols], float(w), pw_ref, ps_ref, gi))
        y = jnp.zeros((n_seq, D_MODEL), F32)
        for gi in range(len(POOL_WINDOWS)):
            y = y + jnp.dot(ya[gi].astype(BF16), wo_ref[gi * POOL_GROUP:(gi + 1) * POOL_GROUP, :],
                            preferred_element_type=F32)
        yb = yb_ref[:, t * SLAB:(t + 1) * SLAB].astype(BF16)
        y = y + jnp.dot(yb, wo_ref[SLAB:, :], preferred_element_type=F32)
        out_ref[rows, :] = x_ref[rows, :] + y


def _even_out_sample(z, ctx, yb, pool_w, pool_scale, w_out, x, n_seq, n_tok):
    T = x.shape[0]
    full = lambda *shape: pl.BlockSpec(shape, lambda i: (0,) * len(shape))
    return pl.pallas_call(
        functools.partial(_even_out_sample_kernel, n_seq=n_seq, n_tok=n_tok),
        grid=(1,),
        in_specs=[
            full(T, SLAB),
            full(POOL_STATE, n_seq, SLAB),
            full(n_seq, n_tok * SLAB),
            full(len(POOL_WINDOWS), POOL_GROUP, POOL_GROUP),
            full(1, SLAB),
            full(2 * SLAB, D_MODEL),
            full(T, D_MODEL),
        ],
        out_specs=full(T, D_MODEL),
        out_shape=jax.ShapeDtypeStruct((T, D_MODEL), F32),
        compiler_params=_params("arbitrary"),
        name="even_out_sample",
    )(z, ctx, yb, pool_w, pool_scale, w_out, x)


def _odd_out_sample_kernel(z_ref, ctx_ref, coef_ref, bias_ref, cw_ref, wo_ref, x_ref,
                           out_ref, *, n_seq, n_tok, mix_terms):
    def ext_row(e):
        if e >= CONV_W - 1:
            t = e - (CONV_W - 1)
            return z_ref[3, t * n_seq:(t + 1) * n_seq, :]
        return ctx_ref[:, e * SLAB:(e + 1) * SLAB]

    for t in range(n_tok):
        rows = slice(t * n_seq, (t + 1) * n_seq)
        sp = jnp.zeros((n_seq, SLAB), F32) + bias_ref[t:t + 1, :]
        for s in mix_terms[t]:
            r = t * n_tok + s
            sp = sp + coef_ref[r:r + 1, :] * z_ref[1, s * n_seq:(s + 1) * n_seq, :]
        yc = z_ref[0, rows, :] * sp
        conv = jnp.zeros((n_seq, SLAB), F32)
        for j in range(CONV_W):
            conv = conv + cw_ref[j:j + 1, :] * ext_row(t + j)
        yd = z_ref[2, rows, :] * conv
        y = jnp.dot(yc.astype(BF16), wo_ref[0:SLAB, :], preferred_element_type=F32)
        y = y + jnp.dot(yd.astype(BF16), wo_ref[SLAB:, :], preferred_element_type=F32)
        out_ref[rows, :] = x_ref[rows, :] + y


def _odd_out_sample(z, ctx, coef, bias, conv_w, w_out, x, n_seq, n_tok, mix_terms):
    T = x.shape[0]
    full = lambda *shape: pl.BlockSpec(shape, lambda i: (0,) * len(shape))
    return pl.pallas_call(
        functools.partial(_odd_out_sample_kernel, n_seq=n_seq, n_tok=n_tok, mix_terms=mix_terms),
        grid=(1,),
        in_specs=[
            full(4, T, SLAB),
            full(n_seq, (CONV_W - 1) * SLAB),
            full(n_tok * n_tok, SLAB),
            full(n_tok, SLAB),
            full(CONV_W, SLAB),
            full(2 * SLAB, D_MODEL),
            full(T, D_MODEL),
        ],
        out_specs=full(T, D_MODEL),
        out_shape=jax.ShapeDtypeStruct((T, D_MODEL), F32),
        compiler_params=_params("arbitrary"),
        name="odd_out_sample",
    )(z, ctx, coef, bias, conv_w, w_out, x)


def _rope_tables(pos, split):
    half = HEAD_DIM // 2
    inv = jnp.power(jnp.float32(ROPE_THETA), -jnp.arange(half, dtype=F32) / half)
    ang = pos.astype(F32)[:, None] * inv[None, :]
    cos = jnp.cos(ang)
    sin = jnp.sin(ang)
    if split:
        return jnp.tile(cos, (1, LANES // half)), jnp.tile(sin, (1, LANES // half))
    cos_t = jnp.tile(jnp.concatenate([cos, cos], axis=-1), (1, LANES // HEAD_DIM))
    sin_t = jnp.tile(jnp.concatenate([-sin, sin], axis=-1), (1, LANES // HEAD_DIM))
    return cos_t, sin_t


def _split_qk_columns(w):
    d = w.shape[0]
    half = HEAD_DIM // 2
    rest = w[:, SLAB:].reshape(d, len(DIL_CFG), 3, HEADS, 2, half)
    qk = rest[:, :, :2].transpose(0, 1, 2, 4, 3, 5)
    rest = jnp.concatenate([qk.reshape(d, len(DIL_CFG), 2, SLAB),
                            rest[:, :, 2:].reshape(d, len(DIL_CFG), 1, SLAB)], axis=2)
    return jnp.concatenate([w[:, :SLAB], rest.reshape(d, -1)], axis=1)


def kernel(x_prompt, x_sample, cache_kv_w128, cache_kv_w512, cache_kv_w2048, state_pool, state_conv,
           p_prompt, p_sample, ev_w_in, ev_pool_w, ev_pool_scale, ev_w_out, od_w_in, od_ln_g, od_ln_b,
           od_ws, od_bs, od_conv_w, od_w_out, norm_mix, norm_ffn, norm_ple, ffn_w1, ffn_w2,
           ple_w_proj, ple_w_gate, norm_final):
    n_p, seq, _ = x_prompt.shape
    n_s, n_tok, _ = x_sample.shape
    depth = norm_mix.shape[0]
    tp = n_p * seq
    ts = n_s * n_tok

    bf = lambda w: w.astype(BF16)
    row = lambda v: v.reshape(1, -1)
    ev_w_in_b, ev_pool_w_b, ev_w_out_b = bf(ev_w_in), bf(ev_pool_w), bf(ev_w_out)
    od_w_in_b, od_w_out_b = bf(od_w_in), bf(od_w_out)
    w1_b, w2_b, wg_b, wp_b = bf(ffn_w1), bf(ffn_w2), bf(ple_w_gate), bf(ple_w_proj)

    cos_p, sin_p = _rope_tables(jnp.arange(seq), True)
    pos_s = [PAST_LEN + t for t in range(n_tok)]
    cos_s, sin_s = _rope_tables(jnp.repeat(jnp.asarray(pos_s, jnp.int32), n_s), False)

    mix_terms = tuple(
        tuple(s for s in range(n_tok)
              if pos_s[s] // CHUNK == pos_s[t] // CHUNK and pos_s[s] % CHUNK <= pos_s[t] % CHUNK)
        for t in range(n_tok))
    local = [p % CHUNK for p in pos_s]

    rp = x_prompt.reshape(tp, D_MODEL)
    rs = x_sample.transpose(1, 0, 2).reshape(ts, D_MODEL)
    pp = p_prompt.reshape(depth, tp, PLE_DIM)
    ps = p_sample.transpose(0, 2, 1, 3).reshape(depth, ts, PLE_DIM)

    tm_p, tm_mix, tm_ffn, tf = 512, 512, 512, 512
    kv_p = [[] for _ in DIL_CFG]
    kv_s = [[] for _ in DIL_CFG]
    pool_p, pool_s, conv_p, conv_s, cv_s = [], [], [], [], []

    for i in range(depth):
        g_mix = row(norm_mix[i])
        if i % 2 == 0:
            e = i // 2
            pscale = row(ev_pool_scale[e])
            dils = tuple(d for _, d in DIL_CFG)
            w_split = bf(_split_qk_columns(ev_w_in[e]))
            za, *groups = _proj_even(rp, g_mix, w_split, cos_p, sin_p, tm_p, n_p, dils, True)
            groups = [grp.reshape(3, n_p, dil, seq // dil, SLAB) for grp, dil in zip(groups, dils)]
            attn = [_attn_prompt(grp, dil) for grp, dil in zip(groups, dils)]
            rp = _even_out_prompt(za, attn, ev_pool_w_b[e], pscale, ev_w_out_b[e], rp, seq, tm_mix)
            for g, ((win, _), grp) in enumerate(zip(DIL_CFG, groups)):
                keep = min(win, seq)
                tail = _kv_tail(grp, keep).reshape(n_p, 2, HEADS, HEAD_DIM, keep)
                kv_p[g].append(tail.transpose(0, 4, 1, 2, 3))
            pool_p.append(za.reshape(n_p, seq, SLAB)[:, seq - POOL_STATE:])
            zas, *sgroups = _proj_even(rs, g_mix, ev_w_in_b[e], cos_s, sin_s, ts, 1, (1, 1, 1),
                                       False)
            zn = jnp.stack(sgroups).reshape(3, 3, n_tok, n_s, SLAB).transpose(3, 0, 1, 2, 4)
            qn = zn[:, :, 0]
            kvn = zn[:, :, 1:].reshape(n_s, 6, n_tok, SLAB)
            native = lambda c: c.transpose(0, 2, 3, 4, 1).reshape(n_s, 2, SLAB, c.shape[1])
            yb = _attn_sample(qn, kvn, native(cache_kv_w128[e]), native(cache_kv_w512[e]),
                              native(cache_kv_w2048[e]))
            rs = _even_out_sample(zas, state_pool[e].transpose(1, 0, 2),
                                  yb.reshape(n_s, n_tok * SLAB), ev_pool_w_b[e], pscale,
                                  ev_w_out_b[e], rs, n_s, n_tok)
            for g in range(len(DIL_CFG)):
                k = zn[:, g, 1].reshape(n_s, n_tok, HEADS, HEAD_DIM)
                v = zn[:, g, 2].reshape(n_s, n_tok, HEADS, HEAD_DIM)
                kv_s[g].append(jnp.stack([k, v], axis=2))
            a_n = zas.reshape(n_tok, n_s, SLAB).transpose(1, 0, 2)
            pool_s.append(jnp.concatenate([state_pool[e], a_n], axis=1)[:, -POOL_STATE:])
        else:
            o = i // 2
            ln_g, ln_b = row(od_ln_g[o]), row(od_ln_b[o])
            bs_rows = jnp.repeat(od_bs[o].T, LANES, axis=1)
            z = _proj_odd(rp, g_mix, od_w_in_b[o], ln_g, ln_b, tm_p)
            rp = _odd_out_prompt(z, od_ws[o], bs_rows, od_conv_w[o], od_w_out_b[o], rp, seq, tm_mix)
            conv_p.append(z.reshape(4, n_p, seq, SLAB)[3, :, seq - (CONV_W - 1):])
            zs = _proj_odd(rs, g_mix, od_w_in_b[o], ln_g, ln_b, ts)
            coef = jnp.stack([jnp.repeat(od_ws[o][:, local[t], local[s]], LANES)
                              for t in range(n_tok) for s in range(n_tok)])
            bias = jnp.stack([bs_rows[local[t]] for t in range(n_tok)])
            rs = _odd_out_sample(zs, state_conv[o].reshape(n_s, (CONV_W - 1) * SLAB), coef, bias,
                                 od_conv_w[o], od_w_out_b[o], rs, n_s, n_tok, mix_terms)
            hd_n = zs[3].reshape(n_tok, n_s, SLAB).transpose(1, 0, 2)
            conv_s.append(jnp.concatenate([state_conv[o], hd_n], axis=1)[:, -(CONV_W - 1):])
            cv_s.append(zs[1].reshape(n_tok, n_s, SLAB).transpose(1, 0, 2))

        last = i == depth - 1
        args = (norm_ffn[:, None, :], w1_b, w2_b, norm_ple[:, None, :], wg_b, wp_b,
                row(norm_final), last)
        rp = _ffn_ple(i, rp, pp, *args, tm_ffn, tf)
        rs = _ffn_ple(i, rs, ps, *args, ts, tf)

    y_prompt = rp.reshape(n_p, seq, D_MODEL)
    y_sample = rs.reshape(n_tok, n_s, D_MODEL).transpose(1, 0, 2)
    st = lambda lst: jnp.stack(lst, axis=0)
    return (y_prompt, y_sample, st(kv_p[0]), st(kv_p[1]), st(kv_p[2]),
            st(kv_s[0]), st(kv_s[1]), st(kv_s[2]),
            st(pool_p), st(pool_s), st(conv_p), st(conv_s), st(cv_s))
```

```python
import functools
import math

import jax
import jax.numpy as jnp
from jax import lax
from jax.experimental import pallas as pl
from jax.experimental.pallas import tpu as pltpu

F32 = jnp.float32
BF16 = jnp.bfloat16

D_MODEL = 1024
D_FF = 4 * D_MODEL
PLE_DIM = 256
EPS = 1e-6
ROPE_THETA = 10000.0
PAST_LEN = 2048

SLAB = 512
POOL_WINDOWS = (2, 4, 8, 16)
POOL_GROUP = 128
POOL_STATE = 15
POOL_HALO = 16
DIL_CFG = ((128, 1), (512, 4), (2048, 16))
HEADS = 8
HEAD_DIM = 64
ATT_BLOCK = 128
ATT_ROWS = 512
CHUNK = 128
C_GROUPS = 4
CONV_W = 3
CONV_HALO = 8
LANES = 128
NEG = -1e30

VMEM_LIMIT = 52 * 1024 * 1024
HOSTING_VMEM_LIMIT = 60 * 1024 * 1024
HOST_SPLITS = 4


def _params(*sem):
    return pltpu.CompilerParams(dimension_semantics=sem, vmem_limit_bytes=VMEM_LIMIT)


def _rms(x, g):
    ms = jnp.mean(x * x, axis=-1, keepdims=True)
    return x * lax.rsqrt(ms + EPS) * g


def _gelu(x):
    c = math.sqrt(2.0 / math.pi)
    return 0.5 * x * (1.0 + jnp.tanh(c * (x + 0.044715 * (x * x * x))))


def _proj_even_kernel(x_ref, g_ref, w_ref, cos_ref, sin_ref, za_ref, g0_ref, g1_ref, g2_ref,
                      zs_ref, *, dils, split):
    tm = x_ref.shape[0]
    nl = SLAB // LANES
    hn = _rms(x_ref[...], g_ref[...]).astype(BF16)
    za_ref[...] = jnp.dot(hn, w_ref[:, 0:SLAB], preferred_element_type=F32)

    def rotate(chunks):
        cos = cos_ref[...]
        sin = sin_ref[...]
        if split:
            h = nl // 2
            return ([chunks[i] * cos - chunks[i + h] * sin for i in range(h)]
                    + [chunks[i] * cos + chunks[i - h] * sin for i in range(h, nl)])
        lane = lax.broadcasted_iota(jnp.int32, cos.shape, 1)
        first_half = jnp.bitwise_and(lane, HEAD_DIM - 1) < (HEAD_DIM // 2)
        out = []
        for zc in chunks:
            partner = jnp.where(first_half,
                                pltpu.roll(zc, LANES - HEAD_DIM // 2, 1),
                                pltpu.roll(zc, HEAD_DIM // 2, 1))
            out.append(zc * cos + partner * sin)
        return out

    slot = 0
    for g, (out_ref, dil) in enumerate(zip((g0_ref, g1_ref, g2_ref), dils)):
        for c in range(3):
            col0 = (1 + 3 * g + c) * SLAB
            z = jnp.dot(hn, w_ref[:, col0:col0 + SLAB], preferred_element_type=F32)
            chunks = [z[:, i * LANES:(i + 1) * LANES] for i in range(nl)]
            if c < 2:
                chunks = rotate(chunks)
            for i, zc in enumerate(chunks):
                cols = slice(i * LANES, (i + 1) * LANES)
                if dil == 1:
                    out_ref[c, :, cols] = zc
                else:
                    zs_ref[slot] = zc
                    for r in range(dil):
                        out_ref[c, r, :, cols] = zs_ref[slot, pl.ds(r, tm // dil, stride=dil), :]
                    slot += 1


def _resident(shape):
    return pl.BlockSpec(shape, lambda *_: (0,) * len(shape), pipeline_mode=pl.Buffered(1))


def _proj_even(x, g, w, cos, sin, tm, n_seq, dils, split):
    T = x.shape[0]
    ntab = cos.shape[0] // tm
    tps = T // n_seq // tm

    def group_spec(dil):
        if dil == 1:
            return pl.BlockSpec((3, tm, SLAB), lambda i: (0, i, 0))
        return pl.BlockSpec((3, None, dil, tm // dil, SLAB), lambda i: (0, i // tps, 0, i % tps, 0))

    def group_shape(dil):
        if dil == 1:
            return jax.ShapeDtypeStruct((3, T, SLAB), F32)
        return jax.ShapeDtypeStruct((3, n_seq, dil, T // n_seq // dil, SLAB), F32)

    n_slots = max(1, 3 * (SLAB // LANES) * sum(d > 1 for d in dils))
    return pl.pallas_call(
        functools.partial(_proj_even_kernel, dils=dils, split=split),
        grid=(T // tm,),
        in_specs=[
            pl.BlockSpec((tm, D_MODEL), lambda i: (i, 0)),
            _resident((1, D_MODEL)),
            _resident(w.shape),
            pl.BlockSpec((tm, LANES), lambda i: (i % ntab, 0)),
            pl.BlockSpec((tm, LANES), lambda i: (i % ntab, 0)),
        ],
        out_specs=[pl.BlockSpec((tm, SLAB), lambda i: (i, 0))] + [group_spec(d) for d in dils],
        out_shape=[jax.ShapeDtypeStruct((T, SLAB), F32)] + [group_shape(d) for d in dils],
        scratch_shapes=[pltpu.VMEM((n_slots, tm, LANES), F32)],
        compiler_params=_params("arbitrary"),
        name="proj_even",
    )(x, g, w, cos, sin)


def _proj_odd_kernel(x_ref, g_ref, w_ref, lng_ref, lnb_ref, o_ref):
    hn = _rms(x_ref[...], g_ref[...]).astype(BF16)

    def slab(s):
        return jnp.dot(hn, w_ref[:, s * SLAB:(s + 1) * SLAB], preferred_element_type=F32)

    o_ref[0] = _gelu(slab(0))
    zv = slab(1)
    for c in range(C_GROUPS):
        sl = slice(c * LANES, (c + 1) * LANES)
        v = _gelu(zv[:, sl])
        mu = jnp.mean(v, axis=-1, keepdims=True)
        dv = v - mu
        var = jnp.mean(dv * dv, axis=-1, keepdims=True)
        o_ref[1, :, sl] = dv * lax.rsqrt(var + EPS) * lng_ref[:, sl] + lnb_ref[:, sl]
    o_ref[2] = slab(2)
    o_ref[3] = slab(3) * slab(4)


def _proj_odd(x, g, w, ln_g, ln_b, tm):
    T = x.shape[0]
    return pl.pallas_call(
        _proj_odd_kernel,
        grid=(T // tm,),
        in_specs=[
            pl.BlockSpec((tm, D_MODEL), lambda i: (i, 0)),
            _resident((1, D_MODEL)),
            _resident(w.shape),
            _resident((1, SLAB)),
            _resident((1, SLAB)),
        ],
        out_specs=pl.BlockSpec((4, tm, SLAB), lambda i: (0, i, 0)),
        out_shape=jax.ShapeDtypeStruct((4, T, SLAB), F32),
        compiler_params=_params("arbitrary"),
        name="proj_odd",
    )(x, g, w, ln_g, ln_b)


def _attn_prompt_kernel(q_ref, kp_ref, kc_ref, vp_ref, vc_ref, o_ref, lse_ref,
                        qst_ref, ks_ref, vs_ref, bias_ref, s_ref, e_ref):
    step = pl.program_id(2)
    sub = ATT_ROWS // ATT_BLOCK
    half_rot = HEAD_DIM // 2
    heads_per_blk = LANES // half_rot
    n_blk = HEADS // heads_per_blk
    log2e = 1.4426950408889634
    ln2 = 0.6931471805599453
    scale = HEAD_DIM ** -0.5 * log2e

    qlane = lax.broadcasted_iota(jnp.int32, (ATT_ROWS, 2 * LANES), 1)
    head_in_blk = jnp.bitwise_and(qlane, LANES - 1) // half_rot
    for j in range(n_blk):
        lo = slice(j * LANES, (j + 1) * LANES)
        hi = slice((n_blk + j) * LANES, (n_blk + j + 1) * LANES)
        dst = slice(2 * j * LANES, (2 * j + 2) * LANES)
        q2 = (jnp.concatenate([q_ref[:, lo], q_ref[:, hi]], axis=1) * scale).astype(BF16)
        for hq in range(heads_per_blk):
            qm = jnp.where(head_in_blk == hq, q2, jnp.zeros_like(q2))
            for b in range(sub):
                r0 = ((j * sub + b) * heads_per_blk + hq) * ATT_BLOCK
                qst_ref[r0:r0 + ATT_BLOCK, :] = qm[b * ATT_BLOCK:(b + 1) * ATT_BLOCK, :]
        ks_ref[0:ATT_BLOCK, dst] = jnp.concatenate([kp_ref[:, lo], kp_ref[:, hi]],
                                                   axis=1).astype(BF16)
        ks_ref[ATT_BLOCK:, dst] = jnp.concatenate([kc_ref[:, lo], kc_ref[:, hi]],
                                                  axis=1).astype(BF16)
    vs_ref[0:ATT_BLOCK, :] = vp_ref[...].astype(BF16)
    vs_ref[ATT_BLOCK:, :] = vc_ref[...].astype(BF16)

    qi = lax.broadcasted_iota(jnp.int32, (ATT_BLOCK, 2 * ATT_BLOCK), 0)
    ki = lax.broadcasted_iota(jnp.int32, (ATT_BLOCK, 2 * ATT_BLOCK), 1)
    rel = qi + ATT_BLOCK - ki
    band = jnp.logical_and(rel >= 0, rel <= ATT_BLOCK)
    bias_ref[0] = jnp.where(band, 0.0, NEG)
    bias_ref[1] = jnp.where(jnp.logical_and(band, ki >= ATT_BLOCK), 0.0, NEG)
    lane = lax.broadcasted_iota(jnp.int32, (ATT_BLOCK, LANES), 1)
    low_head = lane < HEAD_DIM
    blk_rows = heads_per_blk * ATT_BLOCK

    first = (step == 0).astype(jnp.int32)
    hb = ATT_BLOCK // 2
    lane_hb = lax.broadcasted_iota(jnp.int32, (hb, LANES), 1)

    def scores(b):
        for j in range(n_blk):
            base = (j * sub + b) * blk_rows
            k2 = ks_ref[b * ATT_BLOCK:(b + 2) * ATT_BLOCK, 2 * j * LANES:(2 * j + 2) * LANES]
            s_ref[b, j * blk_rows:(j + 1) * blk_rows, :] = lax.dot_general(
                qst_ref[base:base + blk_rows, :], k2, (((1,), (1,)), ((), ())),
                preferred_element_type=F32)

    def softmax(b):
        which = first if b == 0 else 0
        for rh in range(2):
            lse_tile = jnp.zeros((hb, LANES), F32)
            for h in range(HEADS):
                rows = slice(h * ATT_BLOCK + rh * hb, h * ATT_BLOCK + (rh + 1) * hb)
                s = s_ref[b, rows, :] + bias_ref[which, rh * hb:(rh + 1) * hb, :]
                m = jnp.max(s, axis=-1, keepdims=True)
                e = jnp.exp2(s - m)
                l = jnp.sum(e, axis=-1, keepdims=True)
                e_ref[b, rows, :] = (e * (1.0 / l)).astype(BF16)
                lse_tile = jnp.where(lane_hb == h, m * ln2 + jnp.log(l), lse_tile)
            lse_ref[b * ATT_BLOCK + rh * hb:b * ATT_BLOCK + (rh + 1) * hb, :] = lse_tile

    def weighted_values(b):
        for hp in range(HEADS // 2):
            cols = slice(hp * LANES, (hp + 1) * LANES)
            v2 = vs_ref[b * ATT_BLOCK:(b + 2) * ATT_BLOCK, cols]
            pv = jnp.dot(e_ref[b, 2 * hp * ATT_BLOCK:(2 * hp + 2) * ATT_BLOCK, :], v2,
                         preferred_element_type=F32)
            o_ref[b * ATT_BLOCK:(b + 1) * ATT_BLOCK, cols] = jnp.where(
                low_head, pv[0:ATT_BLOCK], pv[ATT_BLOCK:])

    for t in range(sub + 2):
        if t < sub:
            scores(t)
        if 0 <= t - 1 < sub:
            softmax(t - 1)
        if 0 <= t - 2 < sub:
            weighted_values(t - 2)


def _attn_prompt(qkv, dil):
    _, n_seq, _, m_rows, _ = qkv.shape
    steps = m_rows // ATT_ROWS
    sub = ATT_ROWS // ATT_BLOCK

    def cur(slab):
        return pl.BlockSpec((None, None, None, ATT_ROWS, SLAB), lambda n, r, b: (slab, n, r, b, 0))

    def prev(slab):
        return pl.BlockSpec((None, None, None, ATT_BLOCK, SLAB),
                            lambda n, r, b: (slab, n, r, jnp.maximum(b * sub - 1, 0), 0))

    return pl.pallas_call(
        _attn_prompt_kernel,
        grid=(n_seq, dil, steps),
        in_specs=[cur(0), prev(1), cur(1), prev(2), cur(2)],
        out_specs=[
            pl.BlockSpec((None, None, ATT_ROWS, SLAB), lambda n, r, b: (n, r, b, 0)),
            pl.BlockSpec((None, None, ATT_ROWS, LANES), lambda n, r, b: (n, r, b, 0)),
        ],
        out_shape=[
            jax.ShapeDtypeStruct((n_seq, dil, m_rows, SLAB), F32),
            jax.ShapeDtypeStruct((n_seq, dil, m_rows, LANES), F32),
        ],
        scratch_shapes=[
            pltpu.VMEM((HEADS * ATT_ROWS, 2 * LANES), BF16),
            pltpu.VMEM((ATT_ROWS + ATT_BLOCK, SLAB), BF16),
            pltpu.VMEM((ATT_ROWS + ATT_BLOCK, SLAB), BF16),
            pltpu.VMEM((2, ATT_BLOCK, 2 * ATT_BLOCK), F32),
            pltpu.VMEM((sub, HEADS * ATT_BLOCK, 2 * ATT_BLOCK), F32),
            pltpu.VMEM((sub, HEADS * ATT_BLOCK, 2 * ATT_BLOCK), BF16),
        ],
        compiler_params=_params("arbitrary", "arbitrary", "arbitrary"),
        name=f"attn_prompt_d{dil}",
    )(qkv, qkv, qkv, qkv, qkv)


def _kv_tail_kernel(k_ref, v_ref, o_ref, tok_ref):
    dil, rows, _ = k_ref.shape
    keep = dil * rows
    nl = SLAB // LANES
    half_rot = HEAD_DIM // 2
    heads_per_chunk = LANES // half_rot
    for kv, src in enumerate((k_ref, v_ref)):
        for lc in range(nl):
            for r in range(dil):
                tok_ref[lc, pl.ds(r, rows, stride=dil), :] = src[r, :, lc * LANES:(lc + 1) * LANES]
            for pc in range(keep // LANES):
                pcols = slice(pc * LANES, (pc + 1) * LANES)
                t = tok_ref[lc, pcols, :].T
                if kv == 0:
                    half, hblk = divmod(lc, nl // 2)
                    for hq in range(heads_per_chunk):
                        r0 = (hblk * heads_per_chunk + hq) * HEAD_DIM + half * half_rot
                        o_ref[0, r0:r0 + half_rot, pcols] = t[hq * half_rot:(hq + 1) * half_rot, :]
                else:
                    o_ref[1, lc * LANES:(lc + 1) * LANES, pcols] = t


def _kv_tail(grp, keep):
    _, n_seq, dil, m_rows, _ = grp.shape
    rows = keep // dil
    last = m_rows // rows - 1

    def spec(slab):
        return pl.BlockSpec((None, None, dil, rows, SLAB), lambda n: (slab, n, 0, last, 0))

    return pl.pallas_call(
        _kv_tail_kernel,
        grid=(n_seq,),
        in_specs=[spec(1), spec(2)],
        out_specs=pl.BlockSpec((None, 2, SLAB, keep), lambda n: (n, 0, 0, 0)),
        out_shape=jax.ShapeDtypeStruct((n_seq, 2, SLAB, keep), F32),
        scratch_shapes=[pltpu.VMEM((SLAB // LANES, keep, LANES), F32)],
        compiler_params=_params("arbitrary"),
        name=f"kv_tail_d{dil}",
    )(grp, grp)


def _pool_mix(window_terms, a_cols, cnt, pw_ref, scale_ref, gi):
    acc = a_cols
    for term in window_terms:
        acc = acc + term
    pooled = acc / cnt - a_cols
    cols = slice(gi * POOL_GROUP, (gi + 1) * POOL_GROUP)
    mixed = jnp.dot(pooled.astype(BF16), pw_ref[gi], preferred_element_type=F32)
    return mixed * scale_ref[:, cols]


def _even_out_prompt_kernel(a_ref, halo_ref, o0_ref, o1_ref, o2_ref, l0_ref, l1_ref, l2_ref,
                            pw_ref, ps_ref, wo_ref, x_ref, out_ref, ext_ref, ya_ref, yb_ref,
                            oi_ref, li_ref, *, tiles_per_seq):
    tm = a_ref.shape[0]
    it = pl.program_id(0) % tiles_per_seq
    halo = halo_ref[...]
    ext_ref[0:POOL_HALO, :] = jnp.where(it == 0, jnp.zeros_like(halo), halo)
    ext_ref[POOL_HALO:, :] = a_ref[...]
    pos = it * tm + lax.broadcasted_iota(jnp.int32, (tm, 1), 0)
    for gi, w in enumerate(POOL_WINDOWS):
        cols = slice(gi * POOL_GROUP, (gi + 1) * POOL_GROUP)
        terms = [ext_ref[POOL_HALO - k:POOL_HALO - k + tm, cols] for k in range(1, w)]
        cnt = jnp.minimum(w, pos + 1).astype(F32)
        ya_ref[:, cols] = _pool_mix(terms, a_ref[:, cols], cnt, pw_ref, ps_ref, gi).astype(BF16)

    for g, (o_ref, l_ref) in enumerate(((o0_ref, l0_ref), (o1_ref, l1_ref), (o2_ref, l2_ref))):
        dil = o_ref.shape[0]
        for r in range(dil):
            rows = pl.ds(r, tm // dil, stride=dil)
            li_ref[g, rows, :] = l_ref[r]
            for lc in range(SLAB // LANES):
                oi_ref[g, lc, rows, :] = o_ref[r, :, lc * LANES:(lc + 1) * LANES]

    l0, l1, l2 = li_ref[0], li_ref[1], li_ref[2]
    mx = jnp.maximum(jnp.maximum(l0, l1), l2)
    e0, e1, e2 = jnp.exp(l0 - mx), jnp.exp(l1 - mx), jnp.exp(l2 - mx)
    inv = 1.0 / (e0 + e1 + e2)
    w0, w1 = e0 * inv, e1 * inv
    for h in range(HEADS):
        lc, lo = divmod(h * HEAD_DIM, LANES)
        sl = slice(lo, lo + HEAD_DIM)
        o2 = oi_ref[2, lc, :, sl]
        yb = (o2 + w0[:, h:h + 1] * (oi_ref[0, lc, :, sl] - o2)
              + w1[:, h:h + 1] * (oi_ref[1, lc, :, sl] - o2))
        yb_ref[:, h * HEAD_DIM:(h + 1) * HEAD_DIM] = yb.astype(BF16)

    y = jnp.dot(ya_ref[...], wo_ref[0:SLAB, :], preferred_element_type=F32)
    y = y + jnp.dot(yb_ref[...], wo_ref[SLAB:, :], preferred_element_type=F32)
    out_ref[...] = x_ref[...] + y


def _even_out_prompt(za, attn, pool_w, pool_scale, w_out, x, seq, tm):
    T = x.shape[0]
    tps = seq // tm
    hb = tm // POOL_HALO

    def residue_spec(dil, width):
        return pl.BlockSpec((None, dil, tm // dil, width), lambda i: (i // tps, 0, i % tps, 0))

    dils = [o.shape[1] for o, _ in attn]
    return pl.pallas_call(
        functools.partial(_even_out_prompt_kernel, tiles_per_seq=tps),
        grid=(T // tm,),
        in_specs=[
            pl.BlockSpec((tm, SLAB), lambda i: (i, 0)),
            pl.BlockSpec((POOL_HALO, SLAB), lambda i: (jnp.maximum(i * hb - 1, 0), 0)),
        ]
        + [residue_spec(d, SLAB) for d in dils] + [residue_spec(d, LANES) for d in dils]
        + [
            pl.BlockSpec((len(POOL_WINDOWS), POOL_GROUP, POOL_GROUP), lambda i: (0, 0, 0)),
            pl.BlockSpec((1, SLAB), lambda i: (0, 0)),
            pl.BlockSpec((2 * SLAB, D_MODEL), lambda i: (0, 0)),
            pl.BlockSpec((tm, D_MODEL), lambda i: (i, 0)),
        ],
        out_specs=pl.BlockSpec((tm, D_MODEL), lambda i: (i, 0)),
        out_shape=jax.ShapeDtypeStruct((T, D_MODEL), F32),
        scratch_shapes=[
            pltpu.VMEM((tm + POOL_HALO, SLAB), F32),
            pltpu.VMEM((tm, SLAB), BF16),
            pltpu.VMEM((tm, SLAB), BF16),
            pltpu.VMEM((len(dils), SLAB // LANES, tm, LANES), F32),
            pltpu.VMEM((len(dils), tm, LANES), F32),
        ],
        compiler_params=_params("arbitrary"),
        name="even_out_prompt",
    )(za, za, *[o for o, _ in attn], *[l for _, l in attn], pool_w, pool_scale, w_out, x)


def _odd_out_prompt_kernel(u_ref, vn_ref, go_ref, hd_ref, hdh_ref,
                           ws_ref, bs_ref, cw_ref, wo_ref, x_ref, out_ref,
                           ext_ref, yc_ref, yd_ref, *, tiles_per_seq):
    tm = u_ref.shape[0]
    it = pl.program_id(0) % tiles_per_seq

    ti = lax.broadcasted_iota(jnp.int32, (CHUNK, CHUNK), 0)
    si = lax.broadcasted_iota(jnp.int32, (CHUNK, CHUNK), 1)
    for g in range(C_GROUPS):
        cols = slice(g * LANES, (g + 1) * LANES)
        wm = jnp.where(si <= ti, ws_ref[g], 0.0).astype(BF16)
        for c in range(tm // CHUNK):
            rows = slice(c * CHUNK, (c + 1) * CHUNK)
            sp = jnp.dot(wm, vn_ref[rows, cols].astype(BF16), preferred_element_type=F32)
            sp = sp + bs_ref[:, cols]
            yc_ref[rows, cols] = (u_ref[rows, cols] * sp).astype(BF16)

    hd = hd_ref[...]
    halo = hdh_ref[...]
    ext_ref[0:CONV_HALO, :] = jnp.where(it == 0, jnp.zeros_like(halo), halo)
    ext_ref[CONV_HALO:, :] = hd
    conv = cw_ref[CONV_W - 1:CONV_W, :] * hd
    for j in range(CONV_W - 1):
        off = CONV_HALO - (CONV_W - 1) + j
        conv = conv + cw_ref[j:j + 1, :] * ext_ref[off:off + tm, :]
    yd_ref[...] = (go_ref[...] * conv).astype(BF16)

    y = jnp.dot(yc_ref[...], wo_ref[0:SLAB, :], preferred_element_type=F32)
    y = y + jnp.dot(yd_ref[...], wo_ref[SLAB:, :], preferred_element_type=F32)
    out_ref[...] = x_ref[...] + y


def _odd_out_prompt(z, ws, bs_rows, conv_w, w_out, x, seq, tm):
    T = x.shape[0]
    tiles_per_seq = seq // tm
    hb = tm // CONV_HALO

    def slab(s):
        return pl.BlockSpec((None, tm, SLAB), lambda i: (s, i, 0))

    return pl.pallas_call(
        functools.partial(_odd_out_prompt_kernel, tiles_per_seq=tiles_per_seq),
        grid=(T // tm,),
        in_specs=[
            slab(0), slab(1), slab(2), slab(3),
            pl.BlockSpec((None, CONV_HALO, SLAB), lambda i: (3, jnp.maximum(i * hb - 1, 0), 0)),
            _resident((C_GROUPS, CHUNK, CHUNK)),
            _resident((CHUNK, SLAB)),
            _resident((CONV_W, SLAB)),
            _resident((2 * SLAB, D_MODEL)),
            pl.BlockSpec((tm, D_MODEL), lambda i: (i, 0)),
        ],
        out_specs=pl.BlockSpec((tm, D_MODEL), lambda i: (i, 0)),
        out_shape=jax.ShapeDtypeStruct((T, D_MODEL), F32),
        scratch_shapes=[
            pltpu.VMEM((tm + CONV_HALO, SLAB), F32),
            pltpu.VMEM((tm, SLAB), BF16),
            pltpu.VMEM((tm, SLAB), BF16),
        ],
        compiler_params=_params("arbitrary"),
        name="odd_out_prompt",
    )(z, z, z, z, z, ws, bs_rows, conv_w, w_out, x)


def _ffn_ple_kernel(x_ref, p_ref, gf_ref, w1_ref, w2_ref, gp_ref, wg_ref, wp_ref, gl_ref,
                    out_ref, *, final_norm, tf):
    x = x_ref[...]
    hn = _rms(x, gf_ref[...]).astype(BF16)
    acc = None
    for c in range(D_FF // tf):
        h1 = jnp.dot(hn, w1_ref[:, c * tf:(c + 1) * tf], preferred_element_type=F32)
        h1 = jnp.square(jnp.maximum(h1, 0.0)).astype(BF16)
        part = jnp.dot(h1, w2_ref[c * tf:(c + 1) * tf, :], preferred_element_type=F32)
        acc = part if acc is None else acc + part
    r = x + acc
    hp = _rms(r, gp_ref[...]).astype(BF16)
    gate = jax.nn.sigmoid(jnp.dot(hp, wg_ref[...], preferred_element_type=F32))
    proj = jnp.dot(p_ref[...].astype(BF16), wp_ref[...], preferred_element_type=F32)
    r = r + gate * proj
    if final_norm:
        r = _rms(r, gl_ref[...])
    out_ref[...] = r


def _layer_resident(layer, shape):
    return pl.BlockSpec((None,) + shape, lambda *_: (layer,) + (0,) * len(shape),
                        pipeline_mode=pl.Buffered(1))


def _ffn_ple(layer, x, p, g_ffn, w1, w2, g_ple, wg, wp, g_last, final_norm, tm, tf):
    T = x.shape[0]
    return pl.pallas_call(
        functools.partial(_ffn_ple_kernel, final_norm=final_norm, tf=tf),
        grid=(T // tm,),
        in_specs=[
            pl.BlockSpec((tm, D_MODEL), lambda i: (i, 0)),
            pl.BlockSpec((None, tm, PLE_DIM), lambda i: (layer, i, 0)),
            _layer_resident(layer, (1, D_MODEL)),
            _layer_resident(layer, (D_MODEL, D_FF)),
            _layer_resident(layer, (D_FF, D_MODEL)),
            _layer_resident(layer, (1, D_MODEL)),
            _layer_resident(layer, (D_MODEL, D_MODEL)),
            _layer_resident(layer, (PLE_DIM, D_MODEL)),
            _resident((1, D_MODEL)),
        ],
        out_specs=pl.BlockSpec((tm, D_MODEL), lambda i: (i, 0)),
        out_shape=jax.ShapeDtypeStruct((T, D_MODEL), F32),
        compiler_params=_params("arbitrary"),
        name="ffn_ple",
    )(x, p, g_ffn, w1, w2, g_ple, wg, wp, g_last)


def _ffn_ple_hosting_kernel(x_ref, p_ref, gf_ref, w1_ref, w2_ref, gp_ref, wg_ref, wp_ref, gl_ref,
                            q_ref, c0_ref, c1_ref, c2_ref, out_ref, pacc_ref, pml_ref,
                            hn_ref, acc_ref, *, final_norm, tf, splits):
    j = pl.program_id(1)
    share = D_FF // splits

    def mlp_share(k):
        acc = None
        for c in range(share // tf):
            cols = slice(k * share + c * tf, k * share + (c + 1) * tf)
            h1 = jnp.dot(hn_ref[...], w1_ref[:, cols], preferred_element_type=F32)
            h1 = jnp.square(jnp.maximum(h1, 0.0)).astype(BF16)
            part = jnp.dot(h1, w2_ref[cols, :], preferred_element_type=F32)
            acc = part if acc is None else acc + part
        return acc

    @pl.when(j == 0)
    def _():
        hn_ref[...] = _rms(x_ref[...], gf_ref[...]).astype(BF16)
        acc_ref[...] = mlp_share(0)

    for k in range(1, splits - 1):
        @pl.when(j == k)
        def _(k=k):
            acc_ref[...] += mlp_share(k)

    @pl.when(j == splits - 1)
    def _():
        r = x_ref[...] + (acc_ref[...] + mlp_share(splits - 1))
        hp = _rms(r, gp_ref[...]).astype(BF16)
        gate = jax.nn.sigmoid(jnp.dot(hp, wg_ref[...], preferred_element_type=F32))
        proj = jnp.dot(p_ref[...].astype(BF16), wp_ref[...], preferred_element_type=F32)
        r = r + gate * proj
        if final_norm:
            r = _rms(r, gl_ref[...])
        out_ref[...] = r

    _partial_cached_attention(q_ref, (c0_ref, c1_ref, c2_ref), pacc_ref, pml_ref)


def _ffn_ple_hosting(layer, x, p, g_ffn, w1, w2, g_ple, wg, wp, g_last, final_norm, tf,
                     qn, caches, splits):
    T = x.shape[0]
    n_seq, n_grp, n_tok, _ = qn.shape
    tm = splits * T // n_seq
    seq_of = lambda i, j: splits * i + j

    def cache_spec(c):
        return pl.BlockSpec((None, 2, SLAB, c.shape[-1]), lambda i, j: (seq_of(i, j), 0, 0, 0))

    return pl.pallas_call(
        functools.partial(_ffn_ple_hosting_kernel, final_norm=final_norm, tf=tf, splits=splits),
        grid=(T // tm, splits),
        in_specs=[
            pl.BlockSpec((tm, D_MODEL), lambda i, j: (i, 0)),
            pl.BlockSpec((None, tm, PLE_DIM), lambda i, j: (layer, i, 0)),
            _layer_resident(layer, (1, D_MODEL)),
            _layer_resident(layer, (D_MODEL, D_FF)),
            _layer_resident(layer, (D_FF, D_MODEL)),
            _layer_resident(layer, (1, D_MODEL)),
            _layer_resident(layer, (D_MODEL, D_MODEL)),
            _layer_resident(layer, (PLE_DIM, D_MODEL)),
            _resident((1, D_MODEL)),
            pl.BlockSpec((None, n_grp, n_tok, SLAB), lambda i, j: (seq_of(i, j), 0, 0, 0)),
        ] + [cache_spec(c) for c in caches],
        out_specs=[
            pl.BlockSpec((tm, D_MODEL), lambda i, j: (i, 0)),
            pl.BlockSpec((None, n_tok * HEADS, SLAB), lambda i, j: (seq_of(i, j), 0, 0)),
            pl.BlockSpec((None, n_tok * HEADS, LANES), lambda i, j: (seq_of(i, j), 0, 0)),
        ],
        out_shape=[
            jax.ShapeDtypeStruct((T, D_MODEL), F32),
            jax.ShapeDtypeStruct((n_seq, n_tok * HEADS, SLAB), F32),
            jax.ShapeDtypeStruct((n_seq, n_tok * HEADS, LANES), F32),
        ],
        scratch_shapes=[pltpu.VMEM((tm, D_MODEL), BF16), pltpu.VMEM((tm, D_MODEL), F32)],
        compiler_params=pltpu.CompilerParams(dimension_semantics=("arbitrary", "arbitrary"),
                                             vmem_limit_bytes=HOSTING_VMEM_LIMIT),
        name="ffn_ple_hosting",
    )(x, p, g_ffn, w1, w2, g_ple, wg, wp, g_last, qn, *caches)


def _own_lanes():
    sub = lax.broadcasted_iota(jnp.int32, (HEADS, SLAB), 0)
    lane_head = lax.broadcasted_iota(jnp.int32, (HEADS, SLAB), 1) // HEAD_DIM
    return sub == lane_head


def _block_diag_queries(qg):
    own = _own_lanes()
    return jnp.concatenate(
        [jnp.where(own, jnp.broadcast_to(qg[t:t + 1, :], (HEADS, SLAB)), 0.0)
         for t in range(qg.shape[0])], axis=0)


def _cached_scores(qbd, kt, dil):
    s = jnp.dot(qbd.astype(BF16), kt.astype(BF16), preferred_element_type=F32)
    pos = lax.broadcasted_iota(jnp.int32, s.shape, 1)
    row_tok = lax.broadcasted_iota(jnp.int32, (s.shape[0], 1), 0) // HEADS
    valid = (pos >= row_tok) if dil == 1 else (jnp.bitwise_and(pos, dil - 1) == row_tok)
    return jnp.where(valid, s, NEG)


def _weighted_cached_values(e, vt):
    return lax.dot_general(e.astype(BF16), vt.astype(BF16), (((1,), (1,)), ((), ())),
                           preferred_element_type=F32)


def _partial_cached_attention(q_ref, cache_refs, acc_ref, ml_ref):
    scale = HEAD_DIM ** -0.5
    scores = []
    m = None
    for g, (c_ref, (_, dil)) in enumerate(zip(cache_refs, DIL_CFG)):
        s = _cached_scores(_block_diag_queries(q_ref[g] * scale), c_ref[0], dil)
        row_max = jnp.max(s, axis=1, keepdims=True)
        m = row_max if m is None else jnp.maximum(m, row_max)
        scores.append(s)
    acc, l = None, None
    for s, c_ref in zip(scores, cache_refs):
        e = jnp.exp(s - m)
        part = _weighted_cached_values(e, c_ref[1])
        row_sum = jnp.sum(e, axis=1, keepdims=True)
        acc = part if acc is None else acc + part
        l = row_sum if l is None else l + row_sum
    acc_ref[...] = acc
    lane = lax.broadcasted_iota(jnp.int32, ml_ref.shape, 1)
    ml_ref[...] = jnp.where(lane == 0, m, jnp.where(lane == 1, l, 0.0))


def _attn_sample_kernel(q_ref, kv_ref, pacc_ref, pml_ref, o_ref, *, n_seq):
    n_tok = q_ref.shape[2]
    n_rows = n_tok * HEADS
    scale = HEAD_DIM ** -0.5
    own = _own_lanes()
    row_tok = lax.broadcasted_iota(jnp.int32, (n_rows, 1), 0) // HEADS

    def one_sequence(n, carry):
        m_part = pml_ref[n, :, 0:1]
        m = m_part
        s_new = []
        for g, (_, dil) in enumerate(DIL_CFG):
            qbd = _block_diag_queries(q_ref[n, g] * scale)
            kn = kv_ref[n, 2 * g]
            for tp in range(n_tok):
                sn = jnp.sum(qbd * kn[tp:tp + 1, :], axis=1, keepdims=True)
                ok = (row_tok >= tp) if dil == 1 else (row_tok == tp)
                sn = jnp.where(ok, sn, NEG)
                m = jnp.maximum(m, sn)
                s_new.append((g, tp, sn))
        w_part = jnp.exp(m_part - m)
        acc = w_part * pacc_ref[n]
        den = w_part * pml_ref[n, :, 1:2]
        for g, tp, sn in s_new:
            e = jnp.exp(sn - m)
            den = den + e
            acc = acc + e * kv_ref[n, 2 * g + 1][tp:tp + 1, :]
        res = acc * (1.0 / den)
        for t in range(n_tok):
            rows = res[t * HEADS:(t + 1) * HEADS, :]
            o_ref[n, t:t + 1, :] = jnp.sum(jnp.where(own, rows, 0.0), axis=0, keepdims=True)
        return carry

    lax.fori_loop(0, n_seq, one_sequence, 0)


def _attn_sample(qn, kvn, pacc, pml, seqs_per_step=16):
    n_seq, _, n_tok, _ = qn.shape
    nb = seqs_per_step

    def spec(*tail):
        return pl.BlockSpec((nb,) + tail, lambda i: (i,) + (0,) * len(tail))

    return pl.pallas_call(
        functools.partial(_attn_sample_kernel, n_seq=nb),
        grid=(n_seq // nb,),
        in_specs=[spec(3, n_tok, SLAB), spec(6, n_tok, SLAB),
                  spec(n_tok * HEADS, SLAB), spec(n_tok * HEADS, LANES)],
        out_specs=spec(n_tok, SLAB),
        out_shape=jax.ShapeDtypeStruct((n_seq, n_tok, SLAB), F32),
        compiler_params=_params("arbitrary"),
        name="attn_sample",
    )(qn, kvn, pacc, pml)


def _even_out_sample_kernel(a_ref, ctx_ref, yb_ref, pw_ref, ps_ref, wo_ref, x_ref, out_ref,
                            *, n_seq, n_tok):
    def ext_row(e, cols):
        if e >= POOL_STATE:
            t = e - POOL_STATE
            return a_ref[t * n_seq:(t + 1) * n_seq, cols]
        return ctx_ref[e, :, cols]

    for t in range(n_tok):
        rows = slice(t * n_seq, (t + 1) * n_seq)
        ya = []
        for gi, w in enumerate(POOL_WINDOWS):
            cols = slice(gi * POOL_GROUP, (gi + 1) * POOL_GROUP)
            terms = [ext_row(POOL_STATE + t - k, cols) for k in range(1, w)]
            ya.append(_pool_mix(terms, a_ref[rows, cols], float(w), pw_ref, ps_ref, gi))
        y = jnp.zeros((n_seq, D_MODEL), F32)
        for gi in range(len(POOL_WINDOWS)):
            y = y + jnp.dot(ya[gi].astype(BF16), wo_ref[gi * POOL_GROUP:(gi + 1) * POOL_GROUP, :],
                            preferred_element_type=F32)
        yb = yb_ref[:, t * SLAB:(t + 1) * SLAB].astype(BF16)
        y = y + jnp.dot(yb, wo_ref[SLAB:, :], preferred_element_type=F32)
        out_ref[rows, :] = x_ref[rows, :] + y


def _even_out_sample(z, ctx, yb, pool_w, pool_scale, w_out, x, n_seq, n_tok):
    T = x.shape[0]
    full = lambda *shape: pl.BlockSpec(shape, lambda i: (0,) * len(shape))
    return pl.pallas_call(
        functools.partial(_even_out_sample_kernel, n_seq=n_seq, n_tok=n_tok),
        grid=(1,),
        in_specs=[
            full(T, SLAB),
            full(POOL_STATE, n_seq, SLAB),
            full(n_seq, n_tok * SLAB),
            full(len(POOL_WINDOWS), POOL_GROUP, POOL_GROUP),
            full(1, SLAB),
            full(2 * SLAB, D_MODEL),
            full(T, D_MODEL),
        ],
        out_specs=full(T, D_MODEL),
        out_shape=jax.ShapeDtypeStruct((T, D_MODEL), F32),
        compiler_params=_params("arbitrary"),
        name="even_out_sample",
    )(z, ctx, yb, pool_w, pool_scale, w_out, x)


def _odd_out_sample_kernel(z_ref, ctx_ref, coef_ref, bias_ref, cw_ref, wo_ref, x_ref,
                           out_ref, *, n_seq, n_tok, mix_terms):
    def ext_row(e):
        if e >= CONV_W - 1:
            t = e - (CONV_W - 1)
            return z_ref[3, t * n_seq:(t + 1) * n_seq, :]
        return ctx_ref[:, e * SLAB:(e + 1) * SLAB]

    for t in range(n_tok):
        rows = slice(t * n_seq, (t + 1) * n_seq)
        sp = jnp.zeros((n_seq, SLAB), F32) + bias_ref[t:t + 1, :]
        for s in mix_terms[t]:
            r = t * n_tok + s
            sp = sp + coef_ref[r:r + 1, :] * z_ref[1, s * n_seq:(s + 1) * n_seq, :]
        yc = z_ref[0, rows, :] * sp
        conv = jnp.zeros((n_seq, SLAB), F32)
        for j in range(CONV_W):
            conv = conv + cw_ref[j:j + 1, :] * ext_row(t + j)
        yd = z_ref[2, rows, :] * conv
        y = jnp.dot(yc.astype(BF16), wo_ref[0:SLAB, :], preferred_element_type=F32)
        y = y + jnp.dot(yd.astype(BF16), wo_ref[SLAB:, :], preferred_element_type=F32)
        out_ref[rows, :] = x_ref[rows, :] + y


def _odd_out_sample(z, ctx, coef, bias, conv_w, w_out, x, n_seq, n_tok, mix_terms):
    T = x.shape[0]
    full = lambda *shape: pl.BlockSpec(shape, lambda i: (0,) * len(shape))
    return pl.pallas_call(
        functools.partial(_odd_out_sample_kernel, n_seq=n_seq, n_tok=n_tok, mix_terms=mix_terms),
        grid=(1,),
        in_specs=[
            full(4, T, SLAB),
            full(n_seq, (CONV_W - 1) * SLAB),
            full(n_tok * n_tok, SLAB),
            full(n_tok, SLAB),
            full(CONV_W, SLAB),
            full(2 * SLAB, D_MODEL),
            full(T, D_MODEL),
        ],
        out_specs=full(T, D_MODEL),
        out_shape=jax.ShapeDtypeStruct((T, D_MODEL), F32),
        compiler_params=_params("arbitrary"),
        name="odd_out_sample",
    )(z, ctx, coef, bias, conv_w, w_out, x)


def _rope_tables(pos, split):
    half = HEAD_DIM // 2
    inv = jnp.power(jnp.float32(ROPE_THETA), -jnp.arange(half, dtype=F32) / half)
    ang = pos.astype(F32)[:, None] * inv[None, :]
    cos = jnp.cos(ang)
    sin = jnp.sin(ang)
    if split:
        return jnp.tile(cos, (1, LANES // half)), jnp.tile(sin, (1, LANES // half))
    cos_t = jnp.tile(jnp.concatenate([cos, cos], axis=-1), (1, LANES // HEAD_DIM))
    sin_t = jnp.tile(jnp.concatenate([-sin, sin], axis=-1), (1, LANES // HEAD_DIM))
    return cos_t, sin_t


def _split_qk_columns(w):
    d = w.shape[0]
    half = HEAD_DIM // 2
    rest = w[:, SLAB:].reshape(d, len(DIL_CFG), 3, HEADS, 2, half)
    qk = rest[:, :, :2].transpose(0, 1, 2, 4, 3, 5)
    rest = jnp.concatenate([qk.reshape(d, len(DIL_CFG), 2, SLAB),
                            rest[:, :, 2:].reshape(d, len(DIL_CFG), 1, SLAB)], axis=2)
    return jnp.concatenate([w[:, :SLAB], rest.reshape(d, -1)], axis=1)


def kernel(x_prompt, x_sample, cache_kv_w128, cache_kv_w512, cache_kv_w2048, state_pool, state_conv,
           p_prompt, p_sample, ev_w_in, ev_pool_w, ev_pool_scale, ev_w_out, od_w_in, od_ln_g, od_ln_b,
           od_ws, od_bs, od_conv_w, od_w_out, norm_mix, norm_ffn, norm_ple, ffn_w1, ffn_w2,
           ple_w_proj, ple_w_gate, norm_final):
    n_p, seq, _ = x_prompt.shape
    n_s, n_tok, _ = x_sample.shape
    depth = norm_mix.shape[0]
    tp = n_p * seq
    ts = n_s * n_tok

    bf = lambda w: w.astype(BF16)
    row = lambda v: v.reshape(1, -1)
    ev_w_in_b, ev_pool_w_b, ev_w_out_b = bf(ev_w_in), bf(ev_pool_w), bf(ev_w_out)
    od_w_in_b, od_w_out_b = bf(od_w_in), bf(od_w_out)
    w1_b, w2_b, wg_b, wp_b = bf(ffn_w1), bf(ffn_w2), bf(ple_w_gate), bf(ple_w_proj)

    cos_p, sin_p = _rope_tables(jnp.arange(seq), True)
    pos_s = [PAST_LEN + t for t in range(n_tok)]
    cos_s, sin_s = _rope_tables(jnp.repeat(jnp.asarray(pos_s, jnp.int32), n_s), False)

    mix_terms = tuple(
        tuple(s for s in range(n_tok)
              if pos_s[s] // CHUNK == pos_s[t] // CHUNK and pos_s[s] % CHUNK <= pos_s[t] % CHUNK)
        for t in range(n_tok))
    local = [p % CHUNK for p in pos_s]

    rp = x_prompt.reshape(tp, D_MODEL)
    rs = x_sample.transpose(1, 0, 2).reshape(ts, D_MODEL)
    pp = p_prompt.reshape(depth, tp, PLE_DIM)
    ps = p_sample.transpose(0, 2, 1, 3).reshape(depth, ts, PLE_DIM)

    tm_p, tm_mix, tm_ffn, tf = 512, 512, 512, 512
    kv_p = [[] for _ in DIL_CFG]
    kv_s = [[] for _ in DIL_CFG]
    pool_p, pool_s, conv_p, conv_s, cv_s = [], [], [], [], []

    for i in range(depth):
        g_mix = row(norm_mix[i])
        ffn_args = (norm_ffn[:, None, :], w1_b, w2_b, norm_ple[:, None, :], wg_b, wp_b,
                    row(norm_final), i == depth - 1)
        prompt_mlp_done = False
        if i % 2 == 0:
            e = i // 2
            pscale = row(ev_pool_scale[e])
            dils = tuple(d for _, d in DIL_CFG)
            w_split = bf(_split_qk_columns(ev_w_in[e]))
            za, *groups = _proj_even(rp, g_mix, w_split, cos_p, sin_p, tm_p, n_p, dils, True)
            groups = [grp.reshape(3, n_p, dil, seq // dil, SLAB) for grp, dil in zip(groups, dils)]
            attn = [_attn_prompt(grp, dil) for grp, dil in zip(groups, dils)]
            rp = _even_out_prompt(za, attn, ev_pool_w_b[e], pscale, ev_w_out_b[e], rp, seq, tm_mix)
            for g, ((win, _), grp) in enumerate(zip(DIL_CFG, groups)):
                keep = min(win, seq)
                tail = _kv_tail(grp, keep).reshape(n_p, 2, HEADS, HEAD_DIM, keep)
                kv_p[g].append(tail.transpose(0, 4, 1, 2, 3))
            pool_p.append(za.reshape(n_p, seq, SLAB)[:, seq - POOL_STATE:])
            zas, *sgroups = _proj_even(rs, g_mix, ev_w_in_b[e], cos_s, sin_s, ts, 1, (1, 1, 1),
                                       False)
            zn = jnp.stack(sgroups).reshape(3, 3, n_tok, n_s, SLAB).transpose(3, 0, 1, 2, 4)
            qn = zn[:, :, 0]
            kvn = zn[:, :, 1:].reshape(n_s, 6, n_tok, SLAB)
            native = lambda c: c.transpose(0, 2, 3, 4, 1).reshape(n_s, 2, SLAB, c.shape[1])
            caches = [native(c[e]) for c in (cache_kv_w128, cache_kv_w512, cache_kv_w2048)]
            rp, pacc, pml = _ffn_ple_hosting(i, rp, pp, *ffn_args, tf, qn, caches, HOST_SPLITS)
            prompt_mlp_done = True
            yb = _attn_sample(qn, kvn, pacc, pml)
            rs = _even_out_sample(zas, state_pool[e].transpose(1, 0, 2),
                                  yb.reshape(n_s, n_tok * SLAB), ev_pool_w_b[e], pscale,
                                  ev_w_out_b[e], rs, n_s, n_tok)
            for g in range(len(DIL_CFG)):
                k = zn[:, g, 1].reshape(n_s, n_tok, HEADS, HEAD_DIM)
                v = zn[:, g, 2].reshape(n_s, n_tok, HEADS, HEAD_DIM)
                kv_s[g].append(jnp.stack([k, v], axis=2))
            a_n = zas.reshape(n_tok, n_s, SLAB).transpose(1, 0, 2)
            pool_s.append(jnp.concatenate([state_pool[e], a_n], axis=1)[:, -POOL_STATE:])
        else:
            o = i // 2
            ln_g, ln_b = row(od_ln_g[o]), row(od_ln_b[o])
            bs_rows = jnp.repeat(od_bs[o].T, LANES, axis=1)
            z = _proj_odd(rp, g_mix, od_w_in_b[o], ln_g, ln_b, tm_p)
            rp = _odd_out_prompt(z, od_ws[o], bs_rows, od_conv_w[o], od_w_out_b[o], rp, seq, tm_mix)
            conv_p.append(z.reshape(4, n_p, seq, SLAB)[3, :, seq - (CONV_W - 1):])
            zs = _proj_odd(rs, g_mix, od_w_in_b[o], ln_g, ln_b, ts)
            coef = jnp.stack([jnp.repeat(od_ws[o][:, local[t], local[s]], LANES)
                              for t in range(n_tok) for s in range(n_tok)])
            bias = jnp.stack([bs_rows[local[t]] for t in range(n_tok)])
            rs = _odd_out_sample(zs, state_conv[o].reshape(n_s, (CONV_W - 1) * SLAB), coef, bias,
                                 od_conv_w[o], od_w_out_b[o], rs, n_s, n_tok, mix_terms)
            hd_n = zs[3].reshape(n_tok, n_s, SLAB).transpose(1, 0, 2)
            conv_s.append(jnp.concatenate([state_conv[o], hd_n], axis=1)[:, -(CONV_W - 1):])
            cv_s.append(zs[1].reshape(n_tok, n_s, SLAB).transpose(1, 0, 2))

        if not prompt_mlp_done:
            rp = _ffn_ple(i, rp, pp, *ffn_args, tm_ffn, tf)
        rs = _ffn_ple(i, rs, ps, *ffn_args, ts, tf)

    y_prompt = rp.reshape(n_p, seq, D_MODEL)
    y_sample = rs.reshape(n_tok, n_s, D_MODEL).transpose(1, 0, 2)
    st = lambda lst: jnp.stack(lst, axis=0)
    return (y_prompt, y_sample, st(kv_p[0]), st(kv_p[1]), st(kv_p[2]),
            st(kv_s[0]), st(kv_s[1]), st(kv_s[2]),
            st(pool_p), st(pool_s), st(conv_p), st(conv_s), st(cv_s))
```

```python
import functools
import math

import jax
import jax.numpy as jnp
from jax import lax
from jax.experimental import pallas as pl
from jax.experimental.pallas import tpu as pltpu

F32 = jnp.float32
BF16 = jnp.bfloat16

D_MODEL = 1024
D_FF = 4 * D_MODEL
PLE_DIM = 256
EPS = 1e-6
ROPE_THETA = 10000.0
PAST_LEN = 2048

SLAB = 512
POOL_WINDOWS = (2, 4, 8, 16)
POOL_GROUP = 128
POOL_STATE = 15
POOL_HALO = 16
DIL_CFG = ((128, 1), (512, 4), (2048, 16))
HEADS = 8
HEAD_DIM = 64
ATT_BLOCK = 128
ATT_ROWS = 512
CHUNK = 128
C_GROUPS = 4
CONV_W = 3
CONV_HALO = 8
LANES = 128
NEG = -1e30

VMEM_LIMIT = 52 * 1024 * 1024
HOSTING_VMEM_LIMIT = 60 * 1024 * 1024
HOST_SPLITS = 4


def _params(*sem):
    return pltpu.CompilerParams(dimension_semantics=sem, vmem_limit_bytes=VMEM_LIMIT)


def _rms(x, g):
    ms = jnp.mean(x * x, axis=-1, keepdims=True)
    return x * lax.rsqrt(ms + EPS) * g


def _gelu(x):
    c = math.sqrt(2.0 / math.pi)
    return 0.5 * x * (1.0 + jnp.tanh(c * (x + 0.044715 * (x * x * x))))


def _proj_even_kernel(x_ref, g_ref, w_ref, cos_ref, sin_ref, za_ref, g0_ref, g1_ref, g2_ref,
                      zs_ref, *, dils, split):
    tm = x_ref.shape[0]
    nl = SLAB // LANES
    hn = _rms(x_ref[...], g_ref[...]).astype(BF16)
    za_ref[...] = jnp.dot(hn, w_ref[:, 0:SLAB], preferred_element_type=F32)

    def rotate(chunks):
        cos = cos_ref[...]
        sin = sin_ref[...]
        if split:
            h = nl // 2
            return ([chunks[i] * cos - chunks[i + h] * sin for i in range(h)]
                    + [chunks[i] * cos + chunks[i - h] * sin for i in range(h, nl)])
        lane = lax.broadcasted_iota(jnp.int32, cos.shape, 1)
        first_half = jnp.bitwise_and(lane, HEAD_DIM - 1) < (HEAD_DIM // 2)
        out = []
        for zc in chunks:
            partner = jnp.where(first_half,
                                pltpu.roll(zc, LANES - HEAD_DIM // 2, 1),
                                pltpu.roll(zc, HEAD_DIM // 2, 1))
            out.append(zc * cos + partner * sin)
        return out

    slot = 0
    for g, (out_ref, dil) in enumerate(zip((g0_ref, g1_ref, g2_ref), dils)):
        for c in range(3):
            col0 = (1 + 3 * g + c) * SLAB
            z = jnp.dot(hn, w_ref[:, col0:col0 + SLAB], preferred_element_type=F32)
            chunks = [z[:, i * LANES:(i + 1) * LANES] for i in range(nl)]
            if c < 2:
                chunks = rotate(chunks)
            for i, zc in enumerate(chunks):
                cols = slice(i * LANES, (i + 1) * LANES)
                if dil == 1:
                    out_ref[c, :, cols] = zc
                else:
                    zs_ref[slot] = zc
                    for r in range(dil):
                        out_ref[c, r, :, cols] = zs_ref[slot, pl.ds(r, tm // dil, stride=dil), :]
                    slot += 1


def _resident(shape):
    return pl.BlockSpec(shape, lambda *_: (0,) * len(shape), pipeline_mode=pl.Buffered(1))


def _proj_even(x, g, w, cos, sin, tm, n_seq, dils, split):
    T = x.shape[0]
    ntab = cos.shape[0] // tm
    tps = T // n_seq // tm

    def group_spec(dil):
        if dil == 1:
            return pl.BlockSpec((3, tm, SLAB), lambda i: (0, i, 0))
        return pl.BlockSpec((3, None, dil, tm // dil, SLAB), lambda i: (0, i // tps, 0, i % tps, 0))

    def group_shape(dil):
        if dil == 1:
            return jax.ShapeDtypeStruct((3, T, SLAB), F32)
        return jax.ShapeDtypeStruct((3, n_seq, dil, T // n_seq // dil, SLAB), F32)

    n_slots = max(1, 3 * (SLAB // LANES) * sum(d > 1 for d in dils))
    return pl.pallas_call(
        functools.partial(_proj_even_kernel, dils=dils, split=split),
        grid=(T // tm,),
        in_specs=[
            pl.BlockSpec((tm, D_MODEL), lambda i: (i, 0)),
            _resident((1, D_MODEL)),
            _resident(w.shape),
            pl.BlockSpec((tm, LANES), lambda i: (i % ntab, 0)),
            pl.BlockSpec((tm, LANES), lambda i: (i % ntab, 0)),
        ],
        out_specs=[pl.BlockSpec((tm, SLAB), lambda i: (i, 0))] + [group_spec(d) for d in dils],
        out_shape=[jax.ShapeDtypeStruct((T, SLAB), F32)] + [group_shape(d) for d in dils],
        scratch_shapes=[pltpu.VMEM((n_slots, tm, LANES), F32)],
        compiler_params=_params("arbitrary"),
        name="proj_even",
    )(x, g, w, cos, sin)


def _proj_odd_kernel(x_ref, g_ref, w_ref, lng_ref, lnb_ref, o_ref):
    hn = _rms(x_ref[...], g_ref[...]).astype(BF16)

    def slab(s):
        return jnp.dot(hn, w_ref[:, s * SLAB:(s + 1) * SLAB], preferred_element_type=F32)

    o_ref[0] = _gelu(slab(0))
    zv = slab(1)
    for c in range(C_GROUPS):
        sl = slice(c * LANES, (c + 1) * LANES)
        v = _gelu(zv[:, sl])
        mu = jnp.mean(v, axis=-1, keepdims=True)
        dv = v - mu
        var = jnp.mean(dv * dv, axis=-1, keepdims=True)
        o_ref[1, :, sl] = dv * lax.rsqrt(var + EPS) * lng_ref[:, sl] + lnb_ref[:, sl]
    o_ref[2] = slab(2)
    o_ref[3] = slab(3) * slab(4)


def _proj_odd(x, g, w, ln_g, ln_b, tm):
    T = x.shape[0]
    return pl.pallas_call(
        _proj_odd_kernel,
        grid=(T // tm,),
        in_specs=[
            pl.BlockSpec((tm, D_MODEL), lambda i: (i, 0)),
            _resident((1, D_MODEL)),
            _resident(w.shape),
            _resident((1, SLAB)),
            _resident((1, SLAB)),
        ],
        out_specs=pl.BlockSpec((4, tm, SLAB), lambda i: (0, i, 0)),
        out_shape=jax.ShapeDtypeStruct((4, T, SLAB), F32),
        compiler_params=_params("arbitrary"),
        name="proj_odd",
    )(x, g, w, ln_g, ln_b)


def _attn_prompt_kernel(q_ref, kp_ref, kc_ref, vp_ref, vc_ref, o_ref, lse_ref,
                        qst_ref, ks_ref, vs_ref, bias_ref, s_ref, e_ref):
    step = pl.program_id(2)
    att_rows = q_ref.shape[0]
    sub = att_rows // ATT_BLOCK
    half_rot = HEAD_DIM // 2
    heads_per_blk = LANES // half_rot
    n_blk = HEADS // heads_per_blk
    log2e = 1.4426950408889634
    ln2 = 0.6931471805599453
    scale = HEAD_DIM ** -0.5 * log2e

    qlane = lax.broadcasted_iota(jnp.int32, (att_rows, 2 * LANES), 1)
    head_in_blk = jnp.bitwise_and(qlane, LANES - 1) // half_rot
    for j in range(n_blk):
        lo = slice(j * LANES, (j + 1) * LANES)
        hi = slice((n_blk + j) * LANES, (n_blk + j + 1) * LANES)
        dst = slice(2 * j * LANES, (2 * j + 2) * LANES)
        q2 = (jnp.concatenate([q_ref[:, lo], q_ref[:, hi]], axis=1) * scale).astype(BF16)
        for hq in range(heads_per_blk):
            qm = jnp.where(head_in_blk == hq, q2, jnp.zeros_like(q2))
            for b in range(sub):
                r0 = ((j * sub + b) * heads_per_blk + hq) * ATT_BLOCK
                qst_ref[r0:r0 + ATT_BLOCK, :] = qm[b * ATT_BLOCK:(b + 1) * ATT_BLOCK, :]
        ks_ref[0:ATT_BLOCK, dst] = jnp.concatenate([kp_ref[:, lo], kp_ref[:, hi]],
                                                   axis=1).astype(BF16)
        ks_ref[ATT_BLOCK:, dst] = jnp.concatenate([kc_ref[:, lo], kc_ref[:, hi]],
                                                  axis=1).astype(BF16)
    vs_ref[0:ATT_BLOCK, :] = vp_ref[...].astype(BF16)
    vs_ref[ATT_BLOCK:, :] = vc_ref[...].astype(BF16)

    qi = lax.broadcasted_iota(jnp.int32, (ATT_BLOCK, 2 * ATT_BLOCK), 0)
    ki = lax.broadcasted_iota(jnp.int32, (ATT_BLOCK, 2 * ATT_BLOCK), 1)
    rel = qi + ATT_BLOCK - ki
    band = jnp.logical_and(rel >= 0, rel <= ATT_BLOCK)
    bias_ref[0] = jnp.where(band, 0.0, NEG)
    bias_ref[1] = jnp.where(jnp.logical_and(band, ki >= ATT_BLOCK), 0.0, NEG)
    lane = lax.broadcasted_iota(jnp.int32, (ATT_BLOCK, LANES), 1)
    low_head = lane < HEAD_DIM
    blk_rows = heads_per_blk * ATT_BLOCK

    first = (step == 0).astype(jnp.int32)
    hb = ATT_BLOCK // 2
    lane_hb = lax.broadcasted_iota(jnp.int32, (hb, LANES), 1)

    def scores(b):
        for j in range(n_blk):
            base = (j * sub + b) * blk_rows
            k2 = ks_ref[b * ATT_BLOCK:(b + 2) * ATT_BLOCK, 2 * j * LANES:(2 * j + 2) * LANES]
            s_ref[b, j * blk_rows:(j + 1) * blk_rows, :] = lax.dot_general(
                qst_ref[base:base + blk_rows, :], k2, (((1,), (1,)), ((), ())),
                preferred_element_type=F32)

    def softmax(b):
        which = first if b == 0 else 0
        for rh in range(2):
            lse_tile = jnp.zeros((hb, LANES), F32)
            for h in range(HEADS):
                rows = slice(h * ATT_BLOCK + rh * hb, h * ATT_BLOCK + (rh + 1) * hb)
                s = s_ref[b, rows, :] + bias_ref[which, rh * hb:(rh + 1) * hb, :]
                m = jnp.max(s, axis=-1, keepdims=True)
                e = jnp.exp2(s - m)
                l = jnp.sum(e, axis=-1, keepdims=True)
                e_ref[b, rows, :] = (e * (1.0 / l)).astype(BF16)
                lse_tile = jnp.where(lane_hb == h, m * ln2 + jnp.log(l), lse_tile)
            lse_ref[b * ATT_BLOCK + rh * hb:b * ATT_BLOCK + (rh + 1) * hb, :] = lse_tile

    def weighted_values(b):
        for hp in range(HEADS // 2):
            cols = slice(hp * LANES, (hp + 1) * LANES)
            v2 = vs_ref[b * ATT_BLOCK:(b + 2) * ATT_BLOCK, cols]
            pv = jnp.dot(e_ref[b, 2 * hp * ATT_BLOCK:(2 * hp + 2) * ATT_BLOCK, :], v2,
                         preferred_element_type=F32)
            o_ref[b * ATT_BLOCK:(b + 1) * ATT_BLOCK, cols] = jnp.where(
                low_head, pv[0:ATT_BLOCK], pv[ATT_BLOCK:])

    for t in range(sub + 2):
        if t < sub:
            scores(t)
        if 0 <= t - 1 < sub:
            softmax(t - 1)
        if 0 <= t - 2 < sub:
            weighted_values(t - 2)


def _attn_prompt(qkv, dil):
    _, n_seq, _, m_rows, _ = qkv.shape
    att_rows = min(ATT_ROWS, m_rows)
    steps = m_rows // att_rows
    sub = att_rows // ATT_BLOCK

    def cur(slab):
        return pl.BlockSpec((None, None, None, att_rows, SLAB), lambda n, r, b: (slab, n, r, b, 0))

    def prev(slab):
        return pl.BlockSpec((None, None, None, ATT_BLOCK, SLAB),
                            lambda n, r, b: (slab, n, r, jnp.maximum(b * sub - 1, 0), 0))

    return pl.pallas_call(
        _attn_prompt_kernel,
        grid=(n_seq, dil, steps),
        in_specs=[cur(0), prev(1), cur(1), prev(2), cur(2)],
        out_specs=[
            pl.BlockSpec((None, None, att_rows, SLAB), lambda n, r, b: (n, r, b, 0)),
            pl.BlockSpec((None, None, att_rows, LANES), lambda n, r, b: (n, r, b, 0)),
        ],
        out_shape=[
            jax.ShapeDtypeStruct((n_seq, dil, m_rows, SLAB), F32),
            jax.ShapeDtypeStruct((n_seq, dil, m_rows, LANES), F32),
        ],
        scratch_shapes=[
            pltpu.VMEM((HEADS * att_rows, 2 * LANES), BF16),
            pltpu.VMEM((att_rows + ATT_BLOCK, SLAB), BF16),
            pltpu.VMEM((att_rows + ATT_BLOCK, SLAB), BF16),
            pltpu.VMEM((2, ATT_BLOCK, 2 * ATT_BLOCK), F32),
            pltpu.VMEM((sub, HEADS * ATT_BLOCK, 2 * ATT_BLOCK), F32),
            pltpu.VMEM((sub, HEADS * ATT_BLOCK, 2 * ATT_BLOCK), BF16),
        ],
        compiler_params=_params("arbitrary", "arbitrary", "arbitrary"),
        name=f"attn_prompt_d{dil}",
    )(qkv, qkv, qkv, qkv, qkv)


def _kv_tail_kernel(k_ref, v_ref, o_ref, tok_ref):
    dil, rows, _ = k_ref.shape
    keep = dil * rows
    nl = SLAB // LANES
    half_rot = HEAD_DIM // 2
    heads_per_chunk = LANES // half_rot
    for kv, src in enumerate((k_ref, v_ref)):
        for lc in range(nl):
            for r in range(dil):
                tok_ref[lc, pl.ds(r, rows, stride=dil), :] = src[r, :, lc * LANES:(lc + 1) * LANES]
            for pc in range(keep // LANES):
                pcols = slice(pc * LANES, (pc + 1) * LANES)
                t = tok_ref[lc, pcols, :].T
                if kv == 0:
                    half, hblk = divmod(lc, nl // 2)
                    for hq in range(heads_per_chunk):
                        r0 = (hblk * heads_per_chunk + hq) * HEAD_DIM + half * half_rot
                        o_ref[0, r0:r0 + half_rot, pcols] = t[hq * half_rot:(hq + 1) * half_rot, :]
                else:
                    o_ref[1, lc * LANES:(lc + 1) * LANES, pcols] = t


def _kv_tail(grp, keep):
    _, n_seq, dil, m_rows, _ = grp.shape
    rows = keep // dil
    last = m_rows // rows - 1

    def spec(slab):
        return pl.BlockSpec((None, None, dil, rows, SLAB), lambda n: (slab, n, 0, last, 0))

    return pl.pallas_call(
        _kv_tail_kernel,
        grid=(n_seq,),
        in_specs=[spec(1), spec(2)],
        out_specs=pl.BlockSpec((None, 2, SLAB, keep), lambda n: (n, 0, 0, 0)),
        out_shape=jax.ShapeDtypeStruct((n_seq, 2, SLAB, keep), F32),
        scratch_shapes=[pltpu.VMEM((SLAB // LANES, keep, LANES), F32)],
        compiler_params=_params("arbitrary"),
        name=f"kv_tail_d{dil}",
    )(grp, grp)


def _pool_mix(window_terms, a_cols, cnt, pw_ref, scale_ref, gi):
    acc = a_cols
    for term in window_terms:
        acc = acc + term
    pooled = acc / cnt - a_cols
    cols = slice(gi * POOL_GROUP, (gi + 1) * POOL_GROUP)
    mixed = jnp.dot(pooled.astype(BF16), pw_ref[gi], preferred_element_type=F32)
    return mixed * scale_ref[:, cols]


def _even_out_prompt_kernel(a_ref, halo_ref, o0_ref, o1_ref, o2_ref, l0_ref, l1_ref, l2_ref,
                            pw_ref, ps_ref, wo_ref, x_ref, out_ref, ext_ref, ya_ref, yb_ref,
                            oi_ref, li_ref, *, tiles_per_seq):
    tm = a_ref.shape[0]
    it = pl.program_id(0) % tiles_per_seq
    halo = halo_ref[...]
    ext_ref[0:POOL_HALO, :] = jnp.where(it == 0, jnp.zeros_like(halo), halo)
    ext_ref[POOL_HALO:, :] = a_ref[...]
    pos = it * tm + lax.broadcasted_iota(jnp.int32, (tm, 1), 0)
    for gi, w in enumerate(POOL_WINDOWS):
        cols = slice(gi * POOL_GROUP, (gi + 1) * POOL_GROUP)
        terms = [ext_ref[POOL_HALO - k:POOL_HALO - k + tm, cols] for k in range(1, w)]
        cnt = jnp.minimum(w, pos + 1).astype(F32)
        ya_ref[:, cols] = _pool_mix(terms, a_ref[:, cols], cnt, pw_ref, ps_ref, gi).astype(BF16)

    for g, (o_ref, l_ref) in enumerate(((o0_ref, l0_ref), (o1_ref, l1_ref), (o2_ref, l2_ref))):
        dil = o_ref.shape[0]
        for r in range(dil):
            rows = pl.ds(r, tm // dil, stride=dil)
            li_ref[g, rows, :] = l_ref[r]
            for lc in range(SLAB // LANES):
                oi_ref[g, lc, rows, :] = o_ref[r, :, lc * LANES:(lc + 1) * LANES]

    l0, l1, l2 = li_ref[0], li_ref[1], li_ref[2]
    mx = jnp.maximum(jnp.maximum(l0, l1), l2)
    e0, e1, e2 = jnp.exp(l0 - mx), jnp.exp(l1 - mx), jnp.exp(l2 - mx)
    inv = 1.0 / (e0 + e1 + e2)
    w0, w1 = e0 * inv, e1 * inv
    for h in range(HEADS):
        lc, lo = divmod(h * HEAD_DIM, LANES)
        sl = slice(lo, lo + HEAD_DIM)
        o2 = oi_ref[2, lc, :, sl]
        yb = (o2 + w0[:, h:h + 1] * (oi_ref[0, lc, :, sl] - o2)
              + w1[:, h:h + 1] * (oi_ref[1, lc, :, sl] - o2))
        yb_ref[:, h * HEAD_DIM:(h + 1) * HEAD_DIM] = yb.astype(BF16)

    y = jnp.dot(ya_ref[...], wo_ref[0:SLAB, :], preferred_element_type=F32)
    y = y + jnp.dot(yb_ref[...], wo_ref[SLAB:, :], preferred_element_type=F32)
    out_ref[...] = x_ref[...] + y


def _even_out_prompt(za, attn, pool_w, pool_scale, w_out, x, seq, tm):
    T = x.shape[0]
    tps = seq // tm
    hb = tm // POOL_HALO

    def residue_spec(dil, width):
        return pl.BlockSpec((None, dil, tm // dil, width), lambda i: (i // tps, 0, i % tps, 0))

    dils = [o.shape[1] for o, _ in attn]
    return pl.pallas_call(
        functools.partial(_even_out_prompt_kernel, tiles_per_seq=tps),
        grid=(T // tm,),
        in_specs=[
            pl.BlockSpec((tm, SLAB), lambda i: (i, 0)),
            pl.BlockSpec((POOL_HALO, SLAB), lambda i: (jnp.maximum(i * hb - 1, 0), 0)),
        ]
        + [residue_spec(d, SLAB) for d in dils] + [residue_spec(d, LANES) for d in dils]
        + [
            pl.BlockSpec((len(POOL_WINDOWS), POOL_GROUP, POOL_GROUP), lambda i: (0, 0, 0)),
            pl.BlockSpec((1, SLAB), lambda i: (0, 0)),
            pl.BlockSpec((2 * SLAB, D_MODEL), lambda i: (0, 0)),
            pl.BlockSpec((tm, D_MODEL), lambda i: (i, 0)),
        ],
        out_specs=pl.BlockSpec((tm, D_MODEL), lambda i: (i, 0)),
        out_shape=jax.ShapeDtypeStruct((T, D_MODEL), F32),
        scratch_shapes=[
            pltpu.VMEM((tm + POOL_HALO, SLAB), F32),
            pltpu.VMEM((tm, SLAB), BF16),
            pltpu.VMEM((tm, SLAB), BF16),
            pltpu.VMEM((len(dils), SLAB // LANES, tm, LANES), F32),
            pltpu.VMEM((len(dils), tm, LANES), F32),
        ],
        compiler_params=_params("arbitrary"),
        name="even_out_prompt",
    )(za, za, *[o for o, _ in attn], *[l for _, l in attn], pool_w, pool_scale, w_out, x)


def _odd_mix(u_ref, vn_ref, go_ref, hd_ref, hdh_ref, ws_ref, bs_ref, cw_ref, wo_ref, x_ref,
             ext_ref, yc_ref, yd_ref, tiles_per_seq):
    tm = u_ref.shape[0]
    it = pl.program_id(0) % tiles_per_seq

    ti = lax.broadcasted_iota(jnp.int32, (CHUNK, CHUNK), 0)
    si = lax.broadcasted_iota(jnp.int32, (CHUNK, CHUNK), 1)
    for g in range(C_GROUPS):
        cols = slice(g * LANES, (g + 1) * LANES)
        wm = jnp.where(si <= ti, ws_ref[g], 0.0).astype(BF16)
        for c in range(tm // CHUNK):
            rows = slice(c * CHUNK, (c + 1) * CHUNK)
            sp = jnp.dot(wm, vn_ref[rows, cols].astype(BF16), preferred_element_type=F32)
            sp = sp + bs_ref[:, cols]
            yc_ref[rows, cols] = (u_ref[rows, cols] * sp).astype(BF16)

    hd = hd_ref[...]
    halo = hdh_ref[...]
    ext_ref[0:CONV_HALO, :] = jnp.where(it == 0, jnp.zeros_like(halo), halo)
    ext_ref[CONV_HALO:, :] = hd
    conv = cw_ref[CONV_W - 1:CONV_W, :] * hd
    for j in range(CONV_W - 1):
        off = CONV_HALO - (CONV_W - 1) + j
        conv = conv + cw_ref[j:j + 1, :] * ext_ref[off:off + tm, :]
    yd_ref[...] = (go_ref[...] * conv).astype(BF16)

    y = jnp.dot(yc_ref[...], wo_ref[0:SLAB, :], preferred_element_type=F32)
    y = y + jnp.dot(yd_ref[...], wo_ref[SLAB:, :], preferred_element_type=F32)
    return x_ref[...] + y


def _odd_out_ffn_kernel(u_ref, vn_ref, go_ref, hd_ref, hdh_ref, ws_ref, bs_ref, cw_ref, wo_ref,
                        x_ref, p_ref, gf_ref, w1_ref, w2_ref, gp_ref, wg_ref, wp_ref, gl_ref,
                        out_ref, ext_ref, yc_ref, yd_ref, *, tiles_per_seq, final_norm, tf):
    r = _odd_mix(u_ref, vn_ref, go_ref, hd_ref, hdh_ref, ws_ref, bs_ref, cw_ref, wo_ref, x_ref,
                 ext_ref, yc_ref, yd_ref, tiles_per_seq)
    out_ref[...] = _mlp_ple(r, p_ref, gf_ref, w1_ref, w2_ref, gp_ref, wg_ref, wp_ref, gl_ref,
                            final_norm, tf)


def _odd_out_ffn(layer, z, ws, bs_rows, conv_w, w_out, x, seq, p, g_ffn, w1, w2, g_ple, wg, wp,
                 g_last, final_norm, tm, tf):
    T = x.shape[0]
    tiles_per_seq = seq // tm
    hb = tm // CONV_HALO

    def slab(s):
        return pl.BlockSpec((None, tm, SLAB), lambda i: (s, i, 0))

    return pl.pallas_call(
        functools.partial(_odd_out_ffn_kernel, tiles_per_seq=tiles_per_seq,
                          final_norm=final_norm, tf=tf),
        grid=(T // tm,),
        in_specs=[
            slab(0), slab(1), slab(2), slab(3),
            pl.BlockSpec((None, CONV_HALO, SLAB), lambda i: (3, jnp.maximum(i * hb - 1, 0), 0)),
            _resident((C_GROUPS, CHUNK, CHUNK)),
            _resident((CHUNK, SLAB)),
            _resident((CONV_W, SLAB)),
            _resident((2 * SLAB, D_MODEL)),
            pl.BlockSpec((tm, D_MODEL), lambda i: (i, 0)),
            pl.BlockSpec((None, tm, PLE_DIM), lambda i: (layer, i, 0)),
            _layer_resident(layer, (1, D_MODEL)),
            _layer_resident(layer, (D_MODEL, D_FF)),
            _layer_resident(layer, (D_FF, D_MODEL)),
            _layer_resident(layer, (1, D_MODEL)),
            _layer_resident(layer, (D_MODEL, D_MODEL)),
            _layer_resident(layer, (PLE_DIM, D_MODEL)),
            _resident((1, D_MODEL)),
        ],
        out_specs=pl.BlockSpec((tm, D_MODEL), lambda i: (i, 0)),
        out_shape=jax.ShapeDtypeStruct((T, D_MODEL), F32),
        scratch_shapes=[
            pltpu.VMEM((tm + CONV_HALO, SLAB), F32),
            pltpu.VMEM((tm, SLAB), BF16),
            pltpu.VMEM((tm, SLAB), BF16),
        ],
        compiler_params=_params("arbitrary"),
        name="odd_out_ffn",
    )(z, z, z, z, z, ws, bs_rows, conv_w, w_out, x, p, g_ffn, w1, w2, g_ple, wg, wp, g_last)


def _mlp_ple(x, p_ref, gf_ref, w1_ref, w2_ref, gp_ref, wg_ref, wp_ref, gl_ref, final_norm, tf):
    hn = _rms(x, gf_ref[...]).astype(BF16)
    acc = None
    for c in range(D_FF // tf):
        h1 = jnp.dot(hn, w1_ref[:, c * tf:(c + 1) * tf], preferred_element_type=F32)
        h1 = jnp.square(jnp.maximum(h1, 0.0)).astype(BF16)
        part = jnp.dot(h1, w2_ref[c * tf:(c + 1) * tf, :], preferred_element_type=F32)
        acc = part if acc is None else acc + part
    r = x + acc
    hp = _rms(r, gp_ref[...]).astype(BF16)
    gate = jax.nn.sigmoid(jnp.dot(hp, wg_ref[...], preferred_element_type=F32))
    proj = jnp.dot(p_ref[...].astype(BF16), wp_ref[...], preferred_element_type=F32)
    r = r + gate * proj
    if final_norm:
        r = _rms(r, gl_ref[...])
    return r


def _ffn_ple_kernel(x_ref, p_ref, gf_ref, w1_ref, w2_ref, gp_ref, wg_ref, wp_ref, gl_ref,
                    out_ref, *, final_norm, tf):
    out_ref[...] = _mlp_ple(x_ref[...], p_ref, gf_ref, w1_ref, w2_ref, gp_ref, wg_ref, wp_ref,
                            gl_ref, final_norm, tf)


def _layer_resident(layer, shape):
    return pl.BlockSpec((None,) + shape, lambda *_: (layer,) + (0,) * len(shape),
                        pipeline_mode=pl.Buffered(1))


def _ffn_ple(layer, x, p, g_ffn, w1, w2, g_ple, wg, wp, g_last, final_norm, tm, tf):
    T = x.shape[0]
    return pl.pallas_call(
        functools.partial(_ffn_ple_kernel, final_norm=final_norm, tf=tf),
        grid=(T // tm,),
        in_specs=[
            pl.BlockSpec((tm, D_MODEL), lambda i: (i, 0)),
            pl.BlockSpec((None, tm, PLE_DIM), lambda i: (layer, i, 0)),
            _layer_resident(layer, (1, D_MODEL)),
            _layer_resident(layer, (D_MODEL, D_FF)),
            _layer_resident(layer, (D_FF, D_MODEL)),
            _layer_resident(layer, (1, D_MODEL)),
            _layer_resident(layer, (D_MODEL, D_MODEL)),
            _layer_resident(layer, (PLE_DIM, D_MODEL)),
            _resident((1, D_MODEL)),
        ],
        out_specs=pl.BlockSpec((tm, D_MODEL), lambda i: (i, 0)),
        out_shape=jax.ShapeDtypeStruct((T, D_MODEL), F32),
        compiler_params=_params("arbitrary"),
        name="ffn_ple",
    )(x, p, g_ffn, w1, w2, g_ple, wg, wp, g_last)


def _ffn_ple_hosting_kernel(x_ref, p_ref, gf_ref, w1_ref, w2_ref, gp_ref, wg_ref, wp_ref, gl_ref,
                            q_ref, c0_ref, c1_ref, c2_ref, out_ref, pacc_ref, pml_ref,
                            hn_ref, acc_ref, *, final_norm, tf, splits):
    j = pl.program_id(1)
    share = D_FF // splits

    def mlp_share(k):
        acc = None
        for c in range(share // tf):
            cols = slice(k * share + c * tf, k * share + (c + 1) * tf)
            h1 = jnp.dot(hn_ref[...], w1_ref[:, cols], preferred_element_type=F32)
            h1 = jnp.square(jnp.maximum(h1, 0.0)).astype(BF16)
            part = jnp.dot(h1, w2_ref[cols, :], preferred_element_type=F32)
            acc = part if acc is None else acc + part
        return acc

    @pl.when(j == 0)
    def _():
        hn_ref[...] = _rms(x_ref[...], gf_ref[...]).astype(BF16)
        acc_ref[...] = mlp_share(0)

    for k in range(1, splits - 1):
        @pl.when(j == k)
        def _(k=k):
            acc_ref[...] += mlp_share(k)

    @pl.when(j == splits - 1)
    def _():
        r = x_ref[...] + (acc_ref[...] + mlp_share(splits - 1))
        hp = _rms(r, gp_ref[...]).astype(BF16)
        gate = jax.nn.sigmoid(jnp.dot(hp, wg_ref[...], preferred_element_type=F32))
        proj = jnp.dot(p_ref[...].astype(BF16), wp_ref[...], preferred_element_type=F32)
        r = r + gate * proj
        if final_norm:
            r = _rms(r, gl_ref[...])
        out_ref[...] = r

    _partial_cached_attention(q_ref, (c0_ref, c1_ref, c2_ref), pacc_ref, pml_ref)


def _ffn_ple_hosting(layer, x, p, g_ffn, w1, w2, g_ple, wg, wp, g_last, final_norm, tf,
                     qn, caches, splits):
    T = x.shape[0]
    n_seq, n_grp, n_tok, _ = qn.shape
    tm = splits * T // n_seq
    seq_of = lambda i, j: splits * i + j

    def cache_spec(c):
        return pl.BlockSpec((None, 2, SLAB, c.shape[-1]), lambda i, j: (seq_of(i, j), 0, 0, 0))

    return pl.pallas_call(
        functools.partial(_ffn_ple_hosting_kernel, final_norm=final_norm, tf=tf, splits=splits),
        grid=(T // tm, splits),
        in_specs=[
            pl.BlockSpec((tm, D_MODEL), lambda i, j: (i, 0)),
            pl.BlockSpec((None, tm, PLE_DIM), lambda i, j: (layer, i, 0)),
            _layer_resident(layer, (1, D_MODEL)),
            _layer_resident(layer, (D_MODEL, D_FF)),
            _layer_resident(layer, (D_FF, D_MODEL)),
            _layer_resident(layer, (1, D_MODEL)),
            _layer_resident(layer, (D_MODEL, D_MODEL)),
            _layer_resident(layer, (PLE_DIM, D_MODEL)),
            _resident((1, D_MODEL)),
            pl.BlockSpec((None, n_grp, n_tok, SLAB), lambda i, j: (seq_of(i, j), 0, 0, 0)),
        ] + [cache_spec(c) for c in caches],
        out_specs=[
            pl.BlockSpec((tm, D_MODEL), lambda i, j: (i, 0)),
            pl.BlockSpec((None, n_tok * HEADS, SLAB), lambda i, j: (seq_of(i, j), 0, 0)),
            pl.BlockSpec((None, n_tok * HEADS, LANES), lambda i, j: (seq_of(i, j), 0, 0)),
        ],
        out_shape=[
            jax.ShapeDtypeStruct((T, D_MODEL), F32),
            jax.ShapeDtypeStruct((n_seq, n_tok * HEADS, SLAB), F32),
            jax.ShapeDtypeStruct((n_seq, n_tok * HEADS, LANES), F32),
        ],
        scratch_shapes=[pltpu.VMEM((tm, D_MODEL), BF16), pltpu.VMEM((tm, D_MODEL), F32)],
        compiler_params=pltpu.CompilerParams(dimension_semantics=("arbitrary", "arbitrary"),
                                             vmem_limit_bytes=HOSTING_VMEM_LIMIT),
        name="ffn_ple_hosting",
    )(x, p, g_ffn, w1, w2, g_ple, wg, wp, g_last, qn, *caches)


def _own_lanes():
    sub = lax.broadcasted_iota(jnp.int32, (HEADS, SLAB), 0)
    lane_head = lax.broadcasted_iota(jnp.int32, (HEADS, SLAB), 1) // HEAD_DIM
    return sub == lane_head


def _block_diag_queries(qg):
    own = _own_lanes()
    return jnp.concatenate(
        [jnp.where(own, jnp.broadcast_to(qg[t:t + 1, :], (HEADS, SLAB)), 0.0)
         for t in range(qg.shape[0])], axis=0)


def _cached_scores(qbd, kt, dil):
    s = jnp.dot(qbd.astype(BF16), kt.astype(BF16), preferred_element_type=F32)
    pos = lax.broadcasted_iota(jnp.int32, s.shape, 1)
    row_tok = lax.broadcasted_iota(jnp.int32, (s.shape[0], 1), 0) // HEADS
    valid = (pos >= row_tok) if dil == 1 else (jnp.bitwise_and(pos, dil - 1) == row_tok)
    return jnp.where(valid, s, NEG)


def _weighted_cached_values(e, vt):
    return lax.dot_general(e.astype(BF16), vt.astype(BF16), (((1,), (1,)), ((), ())),
                           preferred_element_type=F32)


def _partial_cached_attention(q_ref, cache_refs, acc_ref, ml_ref):
    scale = HEAD_DIM ** -0.5
    scores = []
    m = None
    for g, (c_ref, (_, dil)) in enumerate(zip(cache_refs, DIL_CFG)):
        s = _cached_scores(_block_diag_queries(q_ref[g] * scale), c_ref[0], dil)
        row_max = jnp.max(s, axis=1, keepdims=True)
        m = row_max if m is None else jnp.maximum(m, row_max)
        scores.append(s)
    acc, l = None, None
    for s, c_ref in zip(scores, cache_refs):
        e = jnp.exp(s - m)
        part = _weighted_cached_values(e, c_ref[1])
        row_sum = jnp.sum(e, axis=1, keepdims=True)
        acc = part if acc is None else acc + part
        l = row_sum if l is None else l + row_sum
    acc_ref[...] = acc
    lane = lax.broadcasted_iota(jnp.int32, ml_ref.shape, 1)
    ml_ref[...] = jnp.where(lane == 0, m, jnp.where(lane == 1, l, 0.0))


def _attn_sample_kernel(q_ref, kv_ref, pacc_ref, pml_ref, o_ref, *, n_seq):
    n_tok = q_ref.shape[2]
    n_rows = n_tok * HEADS
    scale = HEAD_DIM ** -0.5
    own = _own_lanes()
    row_tok = lax.broadcasted_iota(jnp.int32, (n_rows, 1), 0) // HEADS

    def one_sequence(n, carry):
        m_part = pml_ref[n, :, 0:1]
        m = m_part
        s_new = []
        for g, (_, dil) in enumerate(DIL_CFG):
            qbd = _block_diag_queries(q_ref[n, g] * scale)
            kn = kv_ref[n, 2 * g]
            for tp in range(n_tok):
                sn = jnp.sum(qbd * kn[tp:tp + 1, :], axis=1, keepdims=True)
                ok = (row_tok >= tp) if dil == 1 else (row_tok == tp)
                sn = jnp.where(ok, sn, NEG)
                m = jnp.maximum(m, sn)
                s_new.append((g, tp, sn))
        w_part = jnp.exp(m_part - m)
        acc = w_part * pacc_ref[n]
        den = w_part * pml_ref[n, :, 1:2]
        for g, tp, sn in s_new:
            e = jnp.exp(sn - m)
            den = den + e
            acc = acc + e * kv_ref[n, 2 * g + 1][tp:tp + 1, :]
        res = acc * (1.0 / den)
        for t in range(n_tok):
            rows = res[t * HEADS:(t + 1) * HEADS, :]
            o_ref[n, t:t + 1, :] = jnp.sum(jnp.where(own, rows, 0.0), axis=0, keepdims=True)
        return carry

    lax.fori_loop(0, n_seq, one_sequence, 0, unroll=4)


def _attn_sample(qn, kvn, pacc, pml, seqs_per_step=16):
    n_seq, _, n_tok, _ = qn.shape
    nb = seqs_per_step

    def spec(*tail):
        return pl.BlockSpec((nb,) + tail, lambda i: (i,) + (0,) * len(tail))

    return pl.pallas_call(
        functools.partial(_attn_sample_kernel, n_seq=nb),
        grid=(n_seq // nb,),
        in_specs=[spec(3, n_tok, SLAB), spec(6, n_tok, SLAB),
                  spec(n_tok * HEADS, SLAB), spec(n_tok * HEADS, LANES)],
        out_specs=spec(n_tok, SLAB),
        out_shape=jax.ShapeDtypeStruct((n_seq, n_tok, SLAB), F32),
        compiler_params=_params("arbitrary"),
        name="attn_sample",
    )(qn, kvn, pacc, pml)


def _even_out_sample_kernel(a_ref, ctx_ref, yb_ref, pw_ref, ps_ref, wo_ref, x_ref, out_ref,
                            *, n_seq, n_tok):
    def ext_row(e, cols):
        if e >= POOL_STATE:
            t = e - POOL_STATE
            return a_ref[t * n_seq:(t + 1) * n_seq, cols]
        return ctx_ref[e, :, cols]

    for t in range(n_tok):
        rows = slice(t * n_seq, (t + 1) * n_seq)
        ya = []
        for gi, w in enumerate(POOL_WINDOWS):
            cols = slice(gi * POOL_GROUP, (gi + 1) * POOL_GROUP)
            terms = [ext_row(POOL_STATE + t - k, cols) for k in range(1, w)]
            ya.append(_pool_mix(terms, a_ref[rows, cols], float(w), pw_ref, ps_ref, gi))
        y = jnp.zeros((n_seq, D_MODEL), F32)
        for gi in range(len(POOL_WINDOWS)):
            y = y + jnp.dot(ya[gi].astype(BF16), wo_ref[gi * POOL_GROUP:(gi + 1) * POOL_GROUP, :],
                            preferred_element_type=F32)
        yb = yb_ref[:, t * SLAB:(t + 1) * SLAB].astype(BF16)
        y = y + jnp.dot(yb, wo_ref[SLAB:, :], preferred_element_type=F32)
        out_ref[rows, :] = x_ref[rows, :] + y


def _even_out_sample(z, ctx, yb, pool_w, pool_scale, w_out, x, n_seq, n_tok):
    T = x.shape[0]
    full = lambda *shape: pl.BlockSpec(shape, lambda i: (0,) * len(shape))
    return pl.pallas_call(
        functools.partial(_even_out_sample_kernel, n_seq=n_seq, n_tok=n_tok),
        grid=(1,),
        in_specs=[
            full(T, SLAB),
            full(POOL_STATE, n_seq, SLAB),
            full(n_seq, n_tok * SLAB),
            full(len(POOL_WINDOWS), POOL_GROUP, POOL_GROUP),
            full(1, SLAB),
            full(2 * SLAB, D_MODEL),
            full(T, D_MODEL),
        ],
        out_specs=full(T, D_MODEL),
        out_shape=jax.ShapeDtypeStruct((T, D_MODEL), F32),
        compiler_params=_params("arbitrary"),
        name="even_out_sample",
    )(z, ctx, yb, pool_w, pool_scale, w_out, x)


def _odd_out_sample_kernel(z_ref, ctx_ref, coef_ref, bias_ref, cw_ref, wo_ref, x_ref,
                           out_ref, *, n_seq, n_tok, mix_terms):
    def ext_row(e):
        if e >= CONV_W - 1:
            t = e - (CONV_W - 1)
            return z_ref[3, t * n_seq:(t + 1) * n_seq, :]
        return ctx_ref[:, e * SLAB:(e + 1) * SLAB]

    for t in range(n_tok):
        rows = slice(t * n_seq, (t + 1) * n_seq)
        sp = jnp.zeros((n_seq, SLAB), F32) + bias_ref[t:t + 1, :]
        for s in mix_terms[t]:
            r = t * n_tok + s
            sp = sp + coef_ref[r:r + 1, :] * z_ref[1, s * n_seq:(s + 1) * n_seq, :]
        yc = z_ref[0, rows, :] * sp
        conv = jnp.zeros((n_seq, SLAB), F32)
        for j in range(CONV_W):
            conv = conv + cw_ref[j:j + 1, :] * ext_row(t + j)
        yd = z_ref[2, rows, :] * conv
        y = jnp.dot(yc.astype(BF16), wo_ref[0:SLAB, :], preferred_element_type=F32)
        y = y + jnp.dot(yd.astype(BF16), wo_ref[SLAB:, :], preferred_element_type=F32)
        out_ref[rows, :] = x_ref[rows, :] + y


def _odd_out_sample(z, ctx, coef, bias, conv_w, w_out, x, n_seq, n_tok, mix_terms):
    T = x.shape[0]
    full = lambda *shape: pl.BlockSpec(shape, lambda i: (0,) * len(shape))
    return pl.pallas_call(
        functools.partial(_odd_out_sample_kernel, n_seq=n_seq, n_tok=n_tok, mix_terms=mix_terms),
        grid=(1,),
        in_specs=[
            full(4, T, SLAB),
            full(n_seq, (CONV_W - 1) * SLAB),
            full(n_tok * n_tok, SLAB),
            full(n_tok, SLAB),
            full(CONV_W, SLAB),
            full(2 * SLAB, D_MODEL),
            full(T, D_MODEL),
        ],
        out_specs=full(T, D_MODEL),
        out_shape=jax.ShapeDtypeStruct((T, D_MODEL), F32),
        compiler_params=_params("arbitrary"),
        name="odd_out_sample",
    )(z, ctx, coef, bias, conv_w, w_out, x)


def _rope_tables(pos, split):
    half = HEAD_DIM // 2
    inv = jnp.power(jnp.float32(ROPE_THETA), -jnp.arange(half, dtype=F32) / half)
    ang = pos.astype(F32)[:, None] * inv[None, :]
    cos = jnp.cos(ang)
    sin = jnp.sin(ang)
    if split:
        return jnp.tile(cos, (1, LANES // half)), jnp.tile(sin, (1, LANES // half))
    cos_t = jnp.tile(jnp.concatenate([cos, cos], axis=-1), (1, LANES // HEAD_DIM))
    sin_t = jnp.tile(jnp.concatenate([-sin, sin], axis=-1), (1, LANES // HEAD_DIM))
    return cos_t, sin_t


def _split_qk_columns(w):
    d = w.shape[0]
    half = HEAD_DIM // 2
    rest = w[:, SLAB:].reshape(d, len(DIL_CFG), 3, HEADS, 2, half)
    qk = rest[:, :, :2].transpose(0, 1, 2, 4, 3, 5)
    rest = jnp.concatenate([qk.reshape(d, len(DIL_CFG), 2, SLAB),
                            rest[:, :, 2:].reshape(d, len(DIL_CFG), 1, SLAB)], axis=2)
    return jnp.concatenate([w[:, :SLAB], rest.reshape(d, -1)], axis=1)


def kernel(x_prompt, x_sample, cache_kv_w128, cache_kv_w512, cache_kv_w2048, state_pool, state_conv,
           p_prompt, p_sample, ev_w_in, ev_pool_w, ev_pool_scale, ev_w_out, od_w_in, od_ln_g, od_ln_b,
           od_ws, od_bs, od_conv_w, od_w_out, norm_mix, norm_ffn, norm_ple, ffn_w1, ffn_w2,
           ple_w_proj, ple_w_gate, norm_final):
    n_p, seq, _ = x_prompt.shape
    n_s, n_tok, _ = x_sample.shape
    depth = norm_mix.shape[0]
    tp = n_p * seq
    ts = n_s * n_tok

    bf = lambda w: w.astype(BF16)
    row = lambda v: v.reshape(1, -1)
    ev_w_in_b, ev_pool_w_b, ev_w_out_b = bf(ev_w_in), bf(ev_pool_w), bf(ev_w_out)
    od_w_in_b, od_w_out_b = bf(od_w_in), bf(od_w_out)
    w1_b, w2_b, wg_b, wp_b = bf(ffn_w1), bf(ffn_w2), bf(ple_w_gate), bf(ple_w_proj)

    cos_p, sin_p = _rope_tables(jnp.arange(seq), True)
    pos_s = [PAST_LEN + t for t in range(n_tok)]
    cos_s, sin_s = _rope_tables(jnp.repeat(jnp.asarray(pos_s, jnp.int32), n_s), False)

    mix_terms = tuple(
        tuple(s for s in range(n_tok)
              if pos_s[s] // CHUNK == pos_s[t] // CHUNK and pos_s[s] % CHUNK <= pos_s[t] % CHUNK)
        for t in range(n_tok))
    local = [p % CHUNK for p in pos_s]

    rp = x_prompt.reshape(tp, D_MODEL)
    rs = x_sample.transpose(1, 0, 2).reshape(ts, D_MODEL)
    pp = p_prompt.reshape(depth, tp, PLE_DIM)
    ps = p_sample.transpose(0, 2, 1, 3).reshape(depth, ts, PLE_DIM)

    tm_p, tm_mix, tm_ffn, tf = 512, 512, 512, 512
    kv_p = [[] for _ in DIL_CFG]
    kv_s = [[] for _ in DIL_CFG]
    pool_p, pool_s, conv_p, conv_s, cv_s = [], [], [], [], []

    for i in range(depth):
        g_mix = row(norm_mix[i])
        ffn_args = (norm_ffn[:, None, :], w1_b, w2_b, norm_ple[:, None, :], wg_b, wp_b,
                    row(norm_final), i == depth - 1)
        if i % 2 == 0:
            e = i // 2
            pscale = row(ev_pool_scale[e])
            dils = tuple(d for _, d in DIL_CFG)
            w_split = bf(_split_qk_columns(ev_w_in[e]))
            za, *groups = _proj_even(rp, g_mix, w_split, cos_p, sin_p, tm_p, n_p, dils, True)
            groups = [grp.reshape(3, n_p, dil, seq // dil, SLAB) for grp, dil in zip(groups, dils)]
            attn = [_attn_prompt(grp, dil) for grp, dil in zip(groups, dils)]
            rp = _even_out_prompt(za, attn, ev_pool_w_b[e], pscale, ev_w_out_b[e], rp, seq, tm_mix)
            for g, ((win, _), grp) in enumerate(zip(DIL_CFG, groups)):
                keep = min(win, seq)
                tail = _kv_tail(grp, keep).reshape(n_p, 2, HEADS, HEAD_DIM, keep)
                kv_p[g].append(tail.transpose(0, 4, 1, 2, 3))
            pool_p.append(za.reshape(n_p, seq, SLAB)[:, seq - POOL_STATE:])
            zas, *sgroups = _proj_even(rs, g_mix, ev_w_in_b[e], cos_s, sin_s, ts, 1, (1, 1, 1),
                                       False)
            zn = jnp.stack(sgroups).reshape(3, 3, n_tok, n_s, SLAB).transpose(3, 0, 1, 2, 4)
            qn = zn[:, :, 0]
            kvn = zn[:, :, 1:].reshape(n_s, 6, n_tok, SLAB)
            native = lambda c: c.transpose(0, 2, 3, 4, 1).reshape(n_s, 2, SLAB, c.shape[1])
            caches = [native(c[e]) for c in (cache_kv_w128, cache_kv_w512, cache_kv_w2048)]
            rp, pacc, pml = _ffn_ple_hosting(i, rp, pp, *ffn_args, tf, qn, caches, HOST_SPLITS)
            yb = _attn_sample(qn, kvn, pacc, pml)
            rs = _even_out_sample(zas, state_pool[e].transpose(1, 0, 2),
                                  yb.reshape(n_s, n_tok * SLAB), ev_pool_w_b[e], pscale,
                                  ev_w_out_b[e], rs, n_s, n_tok)
            for g in range(len(DIL_CFG)):
                k = zn[:, g, 1].reshape(n_s, n_tok, HEADS, HEAD_DIM)
                v = zn[:, g, 2].reshape(n_s, n_tok, HEADS, HEAD_DIM)
                kv_s[g].append(jnp.stack([k, v], axis=2))
            a_n = zas.reshape(n_tok, n_s, SLAB).transpose(1, 0, 2)
            pool_s.append(jnp.concatenate([state_pool[e], a_n], axis=1)[:, -POOL_STATE:])
        else:
            o = i // 2
            ln_g, ln_b = row(od_ln_g[o]), row(od_ln_b[o])
            bs_rows = jnp.repeat(od_bs[o].T, LANES, axis=1)
            z = _proj_odd(rp, g_mix, od_w_in_b[o], ln_g, ln_b, tm_p)
            rp = _odd_out_ffn(i, z, od_ws[o], bs_rows, od_conv_w[o], od_w_out_b[o], rp, seq, pp,
                              *ffn_args, tm_ffn, tf)
            conv_p.append(z.reshape(4, n_p, seq, SLAB)[3, :, seq - (CONV_W - 1):])
            zs = _proj_odd(rs, g_mix, od_w_in_b[o], ln_g, ln_b, ts)
            coef = jnp.stack([jnp.repeat(od_ws[o][:, local[t], local[s]], LANES)
                              for t in range(n_tok) for s in range(n_tok)])
            bias = jnp.stack([bs_rows[local[t]] for t in range(n_tok)])
            rs = _odd_out_sample(zs, state_conv[o].reshape(n_s, (CONV_W - 1) * SLAB), coef, bias,
                                 od_conv_w[o], od_w_out_b[o], rs, n_s, n_tok, mix_terms)
            hd_n = zs[3].reshape(n_tok, n_s, SLAB).transpose(1, 0, 2)
            conv_s.append(jnp.concatenate([state_conv[o], hd_n], axis=1)[:, -(CONV_W - 1):])
            cv_s.append(zs[1].reshape(n_tok, n_s, SLAB).transpose(1, 0, 2))

        rs = _ffn_ple(i, rs, ps, *ffn_args, ts, tf)

    y_prompt = rp.reshape(n_p, seq, D_MODEL)
    y_sample = rs.reshape(n_tok, n_s, D_MODEL).transpose(1, 0, 2)
    st = lambda lst: jnp.stack(lst, axis=0)
    return (y_prompt, y_sample, st(kv_p[0]), st(kv_p[1]), st(kv_p[2]),
            st(kv_s[0]), st(kv_s[1]), st(kv_s[2]),
            st(pool_p), st(pool_s), st(conv_p), st(conv_s), st(cv_s))
```

```python
import functools
import math

import jax
import jax.numpy as jnp
from jax import lax
from jax.experimental import pallas as pl
from jax.experimental.pallas import tpu as pltpu

F32 = jnp.float32
BF16 = jnp.bfloat16

D_MODEL = 1024
D_FF = 4 * D_MODEL
PLE_DIM = 256
EPS = 1e-6
ROPE_THETA = 10000.0
PAST_LEN = 2048

SLAB = 512
POOL_WINDOWS = (2, 4, 8, 16)
POOL_GROUP = 128
POOL_STATE = 15
POOL_HALO = 16
DIL_CFG = ((128, 1), (512, 4), (2048, 16))
HEADS = 8
HEAD_DIM = 64
ATT_BLOCK = 128
ATT_ROWS = 512
CHUNK = 128
C_GROUPS = 4
CONV_W = 3
CONV_HALO = 8
LANES = 128
NEG = -1e30

VMEM_LIMIT = 52 * 1024 * 1024
HOSTING_VMEM_LIMIT = 60 * 1024 * 1024
HOST_SPLITS = 4


def _params(*sem):
    return pltpu.CompilerParams(dimension_semantics=sem, vmem_limit_bytes=VMEM_LIMIT)


def _rms(x, g):
    ms = jnp.mean(x * x, axis=-1, keepdims=True)
    return x * lax.rsqrt(ms + EPS) * g


def _gelu(x):
    c = math.sqrt(2.0 / math.pi)
    return 0.5 * x * (1.0 + jnp.tanh(c * (x + 0.044715 * (x * x * x))))


def _proj_even_kernel(x_ref, g_ref, w_ref, cos_ref, sin_ref, za_ref, g0_ref, g1_ref, g2_ref,
                      zs_ref, *, dils, split):
    tm = x_ref.shape[0]
    nl = SLAB // LANES
    hn = _rms(x_ref[...], g_ref[...]).astype(BF16)
    za_ref[...] = jnp.dot(hn, w_ref[:, 0:SLAB], preferred_element_type=F32)

    def rotate(chunks):
        cos = cos_ref[...]
        sin = sin_ref[...]
        if split:
            h = nl // 2
            return ([chunks[i] * cos - chunks[i + h] * sin for i in range(h)]
                    + [chunks[i] * cos + chunks[i - h] * sin for i in range(h, nl)])
        lane = lax.broadcasted_iota(jnp.int32, cos.shape, 1)
        first_half = jnp.bitwise_and(lane, HEAD_DIM - 1) < (HEAD_DIM // 2)
        out = []
        for zc in chunks:
            partner = jnp.where(first_half,
                                pltpu.roll(zc, LANES - HEAD_DIM // 2, 1),
                                pltpu.roll(zc, HEAD_DIM // 2, 1))
            out.append(zc * cos + partner * sin)
        return out

    slot = 0
    for g, (out_ref, dil) in enumerate(zip((g0_ref, g1_ref, g2_ref), dils)):
        for c in range(3):
            col0 = (1 + 3 * g + c) * SLAB
            z = jnp.dot(hn, w_ref[:, col0:col0 + SLAB], preferred_element_type=F32)
            chunks = [z[:, i * LANES:(i + 1) * LANES] for i in range(nl)]
            if c < 2:
                chunks = rotate(chunks)
            for i, zc in enumerate(chunks):
                cols = slice(i * LANES, (i + 1) * LANES)
                if dil == 1:
                    out_ref[c, :, cols] = zc
                else:
                    zs_ref[slot] = zc
                    for r in range(dil):
                        out_ref[c, r, :, cols] = zs_ref[slot, pl.ds(r, tm // dil, stride=dil), :]
                    slot += 1


def _resident(shape):
    return pl.BlockSpec(shape, lambda *_: (0,) * len(shape), pipeline_mode=pl.Buffered(1))


def _proj_even(x, g, w, cos, sin, tm, n_seq, dils, split):
    T = x.shape[0]
    ntab = cos.shape[0] // tm
    tps = T // n_seq // tm

    def group_spec(dil):
        if dil == 1:
            return pl.BlockSpec((3, tm, SLAB), lambda i: (0, i, 0))
        return pl.BlockSpec((3, None, dil, tm // dil, SLAB), lambda i: (0, i // tps, 0, i % tps, 0))

    def group_shape(dil):
        if dil == 1:
            return jax.ShapeDtypeStruct((3, T, SLAB), F32)
        return jax.ShapeDtypeStruct((3, n_seq, dil, T // n_seq // dil, SLAB), F32)

    n_slots = max(1, 3 * (SLAB // LANES) * sum(d > 1 for d in dils))
    return pl.pallas_call(
        functools.partial(_proj_even_kernel, dils=dils, split=split),
        grid=(T // tm,),
        in_specs=[
            pl.BlockSpec((tm, D_MODEL), lambda i: (i, 0)),
            _resident((1, D_MODEL)),
            _resident(w.shape),
            pl.BlockSpec((tm, LANES), lambda i: (i % ntab, 0)),
            pl.BlockSpec((tm, LANES), lambda i: (i % ntab, 0)),
        ],
        out_specs=[pl.BlockSpec((tm, SLAB), lambda i: (i, 0))] + [group_spec(d) for d in dils],
        out_shape=[jax.ShapeDtypeStruct((T, SLAB), F32)] + [group_shape(d) for d in dils],
        scratch_shapes=[pltpu.VMEM((n_slots, tm, LANES), F32)],
        compiler_params=_params("arbitrary"),
        name="proj_even",
    )(x, g, w, cos, sin)


def _proj_odd_kernel(x_ref, g_ref, w_ref, lng_ref, lnb_ref, o_ref):
    hn = _rms(x_ref[...], g_ref[...]).astype(BF16)

    def slab(s):
        return jnp.dot(hn, w_ref[:, s * SLAB:(s + 1) * SLAB], preferred_element_type=F32)

    o_ref[0] = _gelu(slab(0))
    zv = slab(1)
    for c in range(C_GROUPS):
        sl = slice(c * LANES, (c + 1) * LANES)
        v = _gelu(zv[:, sl])
        mu = jnp.mean(v, axis=-1, keepdims=True)
        dv = v - mu
        var = jnp.mean(dv * dv, axis=-1, keepdims=True)
        o_ref[1, :, sl] = dv * lax.rsqrt(var + EPS) * lng_ref[:, sl] + lnb_ref[:, sl]
    o_ref[2] = slab(2)
    o_ref[3] = slab(3) * slab(4)


def _proj_odd(x, g, w, ln_g, ln_b, tm):
    T = x.shape[0]
    return pl.pallas_call(
        _proj_odd_kernel,
        grid=(T // tm,),
        in_specs=[
            pl.BlockSpec((tm, D_MODEL), lambda i: (i, 0)),
            _resident((1, D_MODEL)),
            _resident(w.shape),
            _resident((1, SLAB)),
            _resident((1, SLAB)),
        ],
        out_specs=pl.BlockSpec((4, tm, SLAB), lambda i: (0, i, 0)),
        out_shape=jax.ShapeDtypeStruct((4, T, SLAB), F32),
        compiler_params=_params("arbitrary"),
        name="proj_odd",
    )(x, g, w, ln_g, ln_b)


def _attn_prompt_kernel(q_ref, kp_ref, kc_ref, vp_ref, vc_ref, o_ref, lse_ref,
                        qst_ref, ks_ref, vs_ref, bias_ref, s_ref, e_ref):
    step = pl.program_id(2)
    att_rows = q_ref.shape[0]
    sub = att_rows // ATT_BLOCK
    half_rot = HEAD_DIM // 2
    heads_per_blk = LANES // half_rot
    n_blk = HEADS // heads_per_blk
    log2e = 1.4426950408889634
    ln2 = 0.6931471805599453
    scale = HEAD_DIM ** -0.5 * log2e

    qlane = lax.broadcasted_iota(jnp.int32, (att_rows, 2 * LANES), 1)
    head_in_blk = jnp.bitwise_and(qlane, LANES - 1) // half_rot
    for j in range(n_blk):
        lo = slice(j * LANES, (j + 1) * LANES)
        hi = slice((n_blk + j) * LANES, (n_blk + j + 1) * LANES)
        dst = slice(2 * j * LANES, (2 * j + 2) * LANES)
        q2 = (jnp.concatenate([q_ref[:, lo], q_ref[:, hi]], axis=1) * scale).astype(BF16)
        for hq in range(heads_per_blk):
            qm = jnp.where(head_in_blk == hq, q2, jnp.zeros_like(q2))
            for b in range(sub):
                r0 = ((j * sub + b) * heads_per_blk + hq) * ATT_BLOCK
                qst_ref[r0:r0 + ATT_BLOCK, :] = qm[b * ATT_BLOCK:(b + 1) * ATT_BLOCK, :]
        ks_ref[0:ATT_BLOCK, dst] = jnp.concatenate([kp_ref[:, lo], kp_ref[:, hi]],
                                                   axis=1).astype(BF16)
        ks_ref[ATT_BLOCK:, dst] = jnp.concatenate([kc_ref[:, lo], kc_ref[:, hi]],
                                                  axis=1).astype(BF16)
    vs_ref[0:ATT_BLOCK, :] = vp_ref[...].astype(BF16)
    vs_ref[ATT_BLOCK:, :] = vc_ref[...].astype(BF16)

    qi = lax.broadcasted_iota(jnp.int32, (ATT_BLOCK, 2 * ATT_BLOCK), 0)
    ki = lax.broadcasted_iota(jnp.int32, (ATT_BLOCK, 2 * ATT_BLOCK), 1)
    rel = qi + ATT_BLOCK - ki
    band = jnp.logical_and(rel >= 0, rel <= ATT_BLOCK)
    bias_ref[0] = jnp.where(band, 0.0, NEG)
    bias_ref[1] = jnp.where(jnp.logical_and(band, ki >= ATT_BLOCK), 0.0, NEG)
    lane = lax.broadcasted_iota(jnp.int32, (ATT_BLOCK, LANES), 1)
    low_head = lane < HEAD_DIM
    blk_rows = heads_per_blk * ATT_BLOCK

    first = (step == 0).astype(jnp.int32)
    hb = ATT_BLOCK // 2
    lane_hb = lax.broadcasted_iota(jnp.int32, (hb, LANES), 1)

    def scores(b):
        for j in range(n_blk):
            base = (j * sub + b) * blk_rows
            k2 = ks_ref[b * ATT_BLOCK:(b + 2) * ATT_BLOCK, 2 * j * LANES:(2 * j + 2) * LANES]
            s_ref[b, j * blk_rows:(j + 1) * blk_rows, :] = lax.dot_general(
                qst_ref[base:base + blk_rows, :], k2, (((1,), (1,)), ((), ())),
                preferred_element_type=F32)

    def softmax(b):
        which = first if b == 0 else 0
        for rh in range(2):
            lse_tile = jnp.zeros((hb, LANES), F32)
            for h in range(HEADS):
                rows = slice(h * ATT_BLOCK + rh * hb, h * ATT_BLOCK + (rh + 1) * hb)
                s = s_ref[b, rows, :] + bias_ref[which, rh * hb:(rh + 1) * hb, :]
                m = jnp.max(s, axis=-1, keepdims=True)
                e = jnp.exp2(s - m)
                l = jnp.sum(e, axis=-1, keepdims=True)
                e_ref[b, rows, :] = (e * (1.0 / l)).astype(BF16)
                lse_tile = jnp.where(lane_hb == h, m * ln2 + jnp.log(l), lse_tile)
            lse_ref[b * ATT_BLOCK + rh * hb:b * ATT_BLOCK + (rh + 1) * hb, :] = lse_tile

    def weighted_values(b):
        for hp in range(HEADS // 2):
            cols = slice(hp * LANES, (hp + 1) * LANES)
            v2 = vs_ref[b * ATT_BLOCK:(b + 2) * ATT_BLOCK, cols]
            pv = jnp.dot(e_ref[b, 2 * hp * ATT_BLOCK:(2 * hp + 2) * ATT_BLOCK, :], v2,
                         preferred_element_type=F32)
            o_ref[b * ATT_BLOCK:(b + 1) * ATT_BLOCK, cols] = jnp.where(
                low_head, pv[0:ATT_BLOCK], pv[ATT_BLOCK:])

    for t in range(sub + 2):
        if t < sub:
            scores(t)
        if 0 <= t - 1 < sub:
            softmax(t - 1)
        if 0 <= t - 2 < sub:
            weighted_values(t - 2)


def _attn_prompt(qkv, dil):
    _, n_seq, _, m_rows, _ = qkv.shape
    att_rows = min(ATT_ROWS, m_rows)
    steps = m_rows // att_rows
    sub = att_rows // ATT_BLOCK

    def cur(slab):
        return pl.BlockSpec((None, None, None, att_rows, SLAB), lambda n, r, b: (slab, n, r, b, 0))

    def prev(slab):
        return pl.BlockSpec((None, None, None, ATT_BLOCK, SLAB),
                            lambda n, r, b: (slab, n, r, jnp.maximum(b * sub - 1, 0), 0))

    return pl.pallas_call(
        _attn_prompt_kernel,
        grid=(n_seq, dil, steps),
        in_specs=[cur(0), prev(1), cur(1), prev(2), cur(2)],
        out_specs=[
            pl.BlockSpec((None, None, att_rows, SLAB), lambda n, r, b: (n, r, b, 0)),
            pl.BlockSpec((None, None, att_rows, LANES), lambda n, r, b: (n, r, b, 0)),
        ],
        out_shape=[
            jax.ShapeDtypeStruct((n_seq, dil, m_rows, SLAB), F32),
            jax.ShapeDtypeStruct((n_seq, dil, m_rows, LANES), F32),
        ],
        scratch_shapes=[
            pltpu.VMEM((HEADS * att_rows, 2 * LANES), BF16),
            pltpu.VMEM((att_rows + ATT_BLOCK, SLAB), BF16),
            pltpu.VMEM((att_rows + ATT_BLOCK, SLAB), BF16),
            pltpu.VMEM((2, ATT_BLOCK, 2 * ATT_BLOCK), F32),
            pltpu.VMEM((sub, HEADS * ATT_BLOCK, 2 * ATT_BLOCK), F32),
            pltpu.VMEM((sub, HEADS * ATT_BLOCK, 2 * ATT_BLOCK), BF16),
        ],
        compiler_params=_params("arbitrary", "arbitrary", "arbitrary"),
        name=f"attn_prompt_d{dil}",
    )(qkv, qkv, qkv, qkv, qkv)


def _kv_tail_kernel(k_ref, v_ref, o_ref, tok_ref):
    dil, rows, _ = k_ref.shape
    keep = dil * rows
    nl = SLAB // LANES
    half_rot = HEAD_DIM // 2
    heads_per_chunk = LANES // half_rot
    for kv, src in enumerate((k_ref, v_ref)):
        for lc in range(nl):
            for r in range(dil):
                tok_ref[lc, pl.ds(r, rows, stride=dil), :] = src[r, :, lc * LANES:(lc + 1) * LANES]
            for pc in range(keep // LANES):
                pcols = slice(pc * LANES, (pc + 1) * LANES)
                t = tok_ref[lc, pcols, :].T
                if kv == 0:
                    half, hblk = divmod(lc, nl // 2)
                    for hq in range(heads_per_chunk):
                        r0 = (hblk * heads_per_chunk + hq) * HEAD_DIM + half * half_rot
                        o_ref[0, r0:r0 + half_rot, pcols] = t[hq * half_rot:(hq + 1) * half_rot, :]
                else:
                    o_ref[1, lc * LANES:(lc + 1) * LANES, pcols] = t


def _kv_tail(grp, keep):
    _, n_seq, dil, m_rows, _ = grp.shape
    rows = keep // dil
    last = m_rows // rows - 1

    def spec(slab):
        return pl.BlockSpec((None, None, dil, rows, SLAB), lambda n: (slab, n, 0, last, 0))

    return pl.pallas_call(
        _kv_tail_kernel,
        grid=(n_seq,),
        in_specs=[spec(1), spec(2)],
        out_specs=pl.BlockSpec((None, 2, SLAB, keep), lambda n: (n, 0, 0, 0)),
        out_shape=jax.ShapeDtypeStruct((n_seq, 2, SLAB, keep), F32),
        scratch_shapes=[pltpu.VMEM((SLAB // LANES, keep, LANES), F32)],
        compiler_params=_params("arbitrary"),
        name=f"kv_tail_d{dil}",
    )(grp, grp)


def _pool_mix(window_terms, a_cols, cnt, pw_ref, scale_ref, gi):
    acc = a_cols
    for term in window_terms:
        acc = acc + term
    pooled = acc / cnt - a_cols
    cols = slice(gi * POOL_GROUP, (gi + 1) * POOL_GROUP)
    mixed = jnp.dot(pooled.astype(BF16), pw_ref[gi], preferred_element_type=F32)
    return mixed * scale_ref[:, cols]


def _even_out_prompt_kernel(a_ref, halo_ref, o0_ref, o1_ref, o2_ref, l0_ref, l1_ref, l2_ref,
                            pw_ref, ps_ref, wo_ref, x_ref, out_ref, ext_ref, ya_ref, yb_ref,
                            oi_ref, li_ref, *, tiles_per_seq):
    tm = a_ref.shape[0]
    it = pl.program_id(0) % tiles_per_seq
    halo = halo_ref[...]
    ext_ref[0:POOL_HALO, :] = jnp.where(it == 0, jnp.zeros_like(halo), halo)
    ext_ref[POOL_HALO:, :] = a_ref[...]
    pos = it * tm + lax.broadcasted_iota(jnp.int32, (tm, 1), 0)
    for gi, w in enumerate(POOL_WINDOWS):
        cols = slice(gi * POOL_GROUP, (gi + 1) * POOL_GROUP)
        terms = [ext_ref[POOL_HALO - k:POOL_HALO - k + tm, cols] for k in range(1, w)]
        cnt = jnp.minimum(w, pos + 1).astype(F32)
        ya_ref[:, cols] = _pool_mix(terms, a_ref[:, cols], cnt, pw_ref, ps_ref, gi).astype(BF16)

    for g, (o_ref, l_ref) in enumerate(((o0_ref, l0_ref), (o1_ref, l1_ref), (o2_ref, l2_ref))):
        dil = o_ref.shape[0]
        for r in range(dil):
            rows = pl.ds(r, tm // dil, stride=dil)
            li_ref[g, rows, :] = l_ref[r]
            for lc in range(SLAB // LANES):
                oi_ref[g, lc, rows, :] = o_ref[r, :, lc * LANES:(lc + 1) * LANES]

    l0, l1, l2 = li_ref[0], li_ref[1], li_ref[2]
    mx = jnp.maximum(jnp.maximum(l0, l1), l2)
    e0, e1, e2 = jnp.exp(l0 - mx), jnp.exp(l1 - mx), jnp.exp(l2 - mx)
    inv = 1.0 / (e0 + e1 + e2)
    w0, w1 = e0 * inv, e1 * inv
    for h in range(HEADS):
        lc, lo = divmod(h * HEAD_DIM, LANES)
        sl = slice(lo, lo + HEAD_DIM)
        o2 = oi_ref[2, lc, :, sl]
        yb = (o2 + w0[:, h:h + 1] * (oi_ref[0, lc, :, sl] - o2)
              + w1[:, h:h + 1] * (oi_ref[1, lc, :, sl] - o2))
        yb_ref[:, h * HEAD_DIM:(h + 1) * HEAD_DIM] = yb.astype(BF16)

    y = jnp.dot(ya_ref[...], wo_ref[0:SLAB, :], preferred_element_type=F32)
    y = y + jnp.dot(yb_ref[...], wo_ref[SLAB:, :], preferred_element_type=F32)
    out_ref[...] = x_ref[...] + y


def _even_out_prompt(za, attn, pool_w, pool_scale, w_out, x, seq, tm):
    T = x.shape[0]
    tps = seq // tm
    hb = tm // POOL_HALO

    def residue_spec(dil, width):
        return pl.BlockSpec((None, dil, tm // dil, width), lambda i: (i // tps, 0, i % tps, 0))

    dils = [o.shape[1] for o, _ in attn]
    return pl.pallas_call(
        functools.partial(_even_out_prompt_kernel, tiles_per_seq=tps),
        grid=(T // tm,),
        in_specs=[
            pl.BlockSpec((tm, SLAB), lambda i: (i, 0)),
            pl.BlockSpec((POOL_HALO, SLAB), lambda i: (jnp.maximum(i * hb - 1, 0), 0)),
        ]
        + [residue_spec(d, SLAB) for d in dils] + [residue_spec(d, LANES) for d in dils]
        + [
            pl.BlockSpec((len(POOL_WINDOWS), POOL_GROUP, POOL_GROUP), lambda i: (0, 0, 0)),
            pl.BlockSpec((1, SLAB), lambda i: (0, 0)),
            pl.BlockSpec((2 * SLAB, D_MODEL), lambda i: (0, 0)),
            pl.BlockSpec((tm, D_MODEL), lambda i: (i, 0)),
        ],
        out_specs=pl.BlockSpec((tm, D_MODEL), lambda i: (i, 0)),
        out_shape=jax.ShapeDtypeStruct((T, D_MODEL), F32),
        scratch_shapes=[
            pltpu.VMEM((tm + POOL_HALO, SLAB), F32),
            pltpu.VMEM((tm, SLAB), BF16),
            pltpu.VMEM((tm, SLAB), BF16),
            pltpu.VMEM((len(dils), SLAB // LANES, tm, LANES), F32),
            pltpu.VMEM((len(dils), tm, LANES), F32),
        ],
        compiler_params=_params("arbitrary"),
        name="even_out_prompt",
    )(za, za, *[o for o, _ in attn], *[l for _, l in attn], pool_w, pool_scale, w_out, x)


def _odd_mix(u_ref, vn_ref, go_ref, hd_ref, hdh_ref, ws_ref, bs_ref, cw_ref, wo_ref, x_ref,
             ext_ref, yc_ref, yd_ref, tiles_per_seq):
    tm = u_ref.shape[0]
    it = pl.program_id(0) % tiles_per_seq

    ti = lax.broadcasted_iota(jnp.int32, (CHUNK, CHUNK), 0)
    si = lax.broadcasted_iota(jnp.int32, (CHUNK, CHUNK), 1)
    for g in range(C_GROUPS):
        cols = slice(g * LANES, (g + 1) * LANES)
        wm = jnp.where(si <= ti, ws_ref[g], 0.0).astype(BF16)
        for c in range(tm // CHUNK):
            rows = slice(c * CHUNK, (c + 1) * CHUNK)
            sp = jnp.dot(wm, vn_ref[rows, cols].astype(BF16), preferred_element_type=F32)
            sp = sp + bs_ref[:, cols]
            yc_ref[rows, cols] = (u_ref[rows, cols] * sp).astype(BF16)

    hd = hd_ref[...]
    halo = hdh_ref[...]
    ext_ref[0:CONV_HALO, :] = jnp.where(it == 0, jnp.zeros_like(halo), halo)
    ext_ref[CONV_HALO:, :] = hd
    conv = cw_ref[CONV_W - 1:CONV_W, :] * hd
    for j in range(CONV_W - 1):
        off = CONV_HALO - (CONV_W - 1) + j
        conv = conv + cw_ref[j:j + 1, :] * ext_ref[off:off + tm, :]
    yd_ref[...] = (go_ref[...] * conv).astype(BF16)

    y = jnp.dot(yc_ref[...], wo_ref[0:SLAB, :], preferred_element_type=F32)
    y = y + jnp.dot(yd_ref[...], wo_ref[SLAB:, :], preferred_element_type=F32)
    return x_ref[...] + y


def _odd_out_ffn_kernel(u_ref, vn_ref, go_ref, hd_ref, hdh_ref, ws_ref, bs_ref, cw_ref, wo_ref,
                        x_ref, p_ref, gf_ref, w1_ref, w2_ref, gp_ref, wg_ref, wp_ref, gl_ref,
                        out_ref, ext_ref, yc_ref, yd_ref, *, tiles_per_seq, final_norm, tf):
    r = _odd_mix(u_ref, vn_ref, go_ref, hd_ref, hdh_ref, ws_ref, bs_ref, cw_ref, wo_ref, x_ref,
                 ext_ref, yc_ref, yd_ref, tiles_per_seq)
    out_ref[...] = _mlp_ple(r, p_ref, gf_ref, w1_ref, w2_ref, gp_ref, wg_ref, wp_ref, gl_ref,
                            final_norm, tf)


def _odd_out_ffn(layer, z, ws, bs_rows, conv_w, w_out, x, seq, p, g_ffn, w1, w2, g_ple, wg, wp,
                 g_last, final_norm, tm, tf):
    T = x.shape[0]
    tiles_per_seq = seq // tm
    hb = tm // CONV_HALO

    def slab(s):
        return pl.BlockSpec((None, tm, SLAB), lambda i: (s, i, 0))

    return pl.pallas_call(
        functools.partial(_odd_out_ffn_kernel, tiles_per_seq=tiles_per_seq,
                          final_norm=final_norm, tf=tf),
        grid=(T // tm,),
        in_specs=[
            slab(0), slab(1), slab(2), slab(3),
            pl.BlockSpec((None, CONV_HALO, SLAB), lambda i: (3, jnp.maximum(i * hb - 1, 0), 0)),
            _resident((C_GROUPS, CHUNK, CHUNK)),
            _resident((CHUNK, SLAB)),
            _resident((CONV_W, SLAB)),
            _resident((2 * SLAB, D_MODEL)),
            pl.BlockSpec((tm, D_MODEL), lambda i: (i, 0)),
            pl.BlockSpec((None, tm, PLE_DIM), lambda i: (layer, i, 0)),
            _layer_resident(layer, (1, D_MODEL)),
            _layer_resident(layer, (D_MODEL, D_FF)),
            _layer_resident(layer, (D_FF, D_MODEL)),
            _layer_resident(layer, (1, D_MODEL)),
            _layer_resident(layer, (D_MODEL, D_MODEL)),
            _layer_resident(layer, (PLE_DIM, D_MODEL)),
            _resident((1, D_MODEL)),
        ],
        out_specs=pl.BlockSpec((tm, D_MODEL), lambda i: (i, 0)),
        out_shape=jax.ShapeDtypeStruct((T, D_MODEL), F32),
        scratch_shapes=[
            pltpu.VMEM((tm + CONV_HALO, SLAB), F32),
            pltpu.VMEM((tm, SLAB), BF16),
            pltpu.VMEM((tm, SLAB), BF16),
        ],
        compiler_params=_params("arbitrary"),
        name="odd_out_ffn",
    )(z, z, z, z, z, ws, bs_rows, conv_w, w_out, x, p, g_ffn, w1, w2, g_ple, wg, wp, g_last)


def _mlp_ple(x, p_ref, gf_ref, w1_ref, w2_ref, gp_ref, wg_ref, wp_ref, gl_ref, final_norm, tf):
    hn = _rms(x, gf_ref[...]).astype(BF16)
    acc = None
    for c in range(D_FF // tf):
        h1 = jnp.dot(hn, w1_ref[:, c * tf:(c + 1) * tf], preferred_element_type=F32)
        h1 = jnp.square(jnp.maximum(h1, 0.0)).astype(BF16)
        part = jnp.dot(h1, w2_ref[c * tf:(c + 1) * tf, :], preferred_element_type=F32)
        acc = part if acc is None else acc + part
    r = x + acc
    hp = _rms(r, gp_ref[...]).astype(BF16)
    gate = jax.nn.sigmoid(jnp.dot(hp, wg_ref[...], preferred_element_type=F32))
    proj = jnp.dot(p_ref[...].astype(BF16), wp_ref[...], preferred_element_type=F32)
    r = r + gate * proj
    if final_norm:
        r = _rms(r, gl_ref[...])
    return r


def _ffn_ple_kernel(x_ref, p_ref, gf_ref, w1_ref, w2_ref, gp_ref, wg_ref, wp_ref, gl_ref,
                    out_ref, *, final_norm, tf):
    out_ref[...] = _mlp_ple(x_ref[...], p_ref, gf_ref, w1_ref, w2_ref, gp_ref, wg_ref, wp_ref,
                            gl_ref, final_norm, tf)


def _layer_resident(layer, shape):
    return pl.BlockSpec((None,) + shape, lambda *_: (layer,) + (0,) * len(shape),
                        pipeline_mode=pl.Buffered(1))


def _ffn_ple(layer, x, p, g_ffn, w1, w2, g_ple, wg, wp, g_last, final_norm, tm, tf):
    T = x.shape[0]
    return pl.pallas_call(
        functools.partial(_ffn_ple_kernel, final_norm=final_norm, tf=tf),
        grid=(T // tm,),
        in_specs=[
            pl.BlockSpec((tm, D_MODEL), lambda i: (i, 0)),
            pl.BlockSpec((None, tm, PLE_DIM), lambda i: (layer, i, 0)),
            _layer_resident(layer, (1, D_MODEL)),
            _layer_resident(layer, (D_MODEL, D_FF)),
            _layer_resident(layer, (D_FF, D_MODEL)),
            _layer_resident(layer, (1, D_MODEL)),
            _layer_resident(layer, (D_MODEL, D_MODEL)),
            _layer_resident(layer, (PLE_DIM, D_MODEL)),
            _resident((1, D_MODEL)),
        ],
        out_specs=pl.BlockSpec((tm, D_MODEL), lambda i: (i, 0)),
        out_shape=jax.ShapeDtypeStruct((T, D_MODEL), F32),
        compiler_params=_params("arbitrary"),
        name="ffn_ple",
    )(x, p, g_ffn, w1, w2, g_ple, wg, wp, g_last)


def _ffn_ple_hosting_kernel(x_ref, p_ref, gf_ref, w1_ref, w2_ref, gp_ref, wg_ref, wp_ref, gl_ref,
                            q_ref, c0_ref, c1_ref, c2_ref, out_ref, pacc_ref, pml_ref,
                            hn_ref, acc_ref, *, final_norm, tf, splits):
    j = pl.program_id(1)
    share = D_FF // splits

    def mlp_share(k):
        acc = None
        for c in range(share // tf):
            cols = slice(k * share + c * tf, k * share + (c + 1) * tf)
            h1 = jnp.dot(hn_ref[...], w1_ref[:, cols], preferred_element_type=F32)
            h1 = jnp.square(jnp.maximum(h1, 0.0)).astype(BF16)
            part = jnp.dot(h1, w2_ref[cols, :], preferred_element_type=F32)
            acc = part if acc is None else acc + part
        return acc

    @pl.when(j == 0)
    def _():
        hn_ref[...] = _rms(x_ref[...], gf_ref[...]).astype(BF16)
        acc_ref[...] = mlp_share(0)

    for k in range(1, splits - 1):
        @pl.when(j == k)
        def _(k=k):
            acc_ref[...] += mlp_share(k)

    @pl.when(j == splits - 1)
    def _():
        r = x_ref[...] + (acc_ref[...] + mlp_share(splits - 1))
        hp = _rms(r, gp_ref[...]).astype(BF16)
        gate = jax.nn.sigmoid(jnp.dot(hp, wg_ref[...], preferred_element_type=F32))
        proj = jnp.dot(p_ref[...].astype(BF16), wp_ref[...], preferred_element_type=F32)
        r = r + gate * proj
        if final_norm:
            r = _rms(r, gl_ref[...])
        out_ref[...] = r

    _partial_cached_attention(q_ref, (c0_ref, c1_ref, c2_ref), pacc_ref, pml_ref)


def _ffn_ple_hosting(layer, x, p, g_ffn, w1, w2, g_ple, wg, wp, g_last, final_norm, tf,
                     qn, caches, splits):
    T = x.shape[0]
    n_seq, n_grp, n_tok, _ = qn.shape
    tm = splits * T // n_seq
    seq_of = lambda i, j: splits * i + j

    def cache_spec(c):
        return pl.BlockSpec((None, 2, SLAB, c.shape[-1]), lambda i, j: (seq_of(i, j), 0, 0, 0))

    return pl.pallas_call(
        functools.partial(_ffn_ple_hosting_kernel, final_norm=final_norm, tf=tf, splits=splits),
        grid=(T // tm, splits),
        in_specs=[
            pl.BlockSpec((tm, D_MODEL), lambda i, j: (i, 0)),
            pl.BlockSpec((None, tm, PLE_DIM), lambda i, j: (layer, i, 0)),
            _layer_resident(layer, (1, D_MODEL)),
            _layer_resident(layer, (D_MODEL, D_FF)),
            _layer_resident(layer, (D_FF, D_MODEL)),
            _layer_resident(layer, (1, D_MODEL)),
            _layer_resident(layer, (D_MODEL, D_MODEL)),
            _layer_resident(layer, (PLE_DIM, D_MODEL)),
            _resident((1, D_MODEL)),
            pl.BlockSpec((None, n_grp, n_tok, SLAB), lambda i, j: (seq_of(i, j), 0, 0, 0)),
        ] + [cache_spec(c) for c in caches],
        out_specs=[
            pl.BlockSpec((tm, D_MODEL), lambda i, j: (i, 0)),
            pl.BlockSpec((None, n_tok * HEADS, SLAB), lambda i, j: (seq_of(i, j), 0, 0)),
            pl.BlockSpec((None, n_tok * HEADS, LANES), lambda i, j: (seq_of(i, j), 0, 0)),
        ],
        out_shape=[
            jax.ShapeDtypeStruct((T, D_MODEL), F32),
            jax.ShapeDtypeStruct((n_seq, n_tok * HEADS, SLAB), F32),
            jax.ShapeDtypeStruct((n_seq, n_tok * HEADS, LANES), F32),
        ],
        scratch_shapes=[pltpu.VMEM((tm, D_MODEL), BF16), pltpu.VMEM((tm, D_MODEL), F32)],
        compiler_params=pltpu.CompilerParams(dimension_semantics=("arbitrary", "arbitrary"),
                                             vmem_limit_bytes=HOSTING_VMEM_LIMIT),
        name="ffn_ple_hosting",
    )(x, p, g_ffn, w1, w2, g_ple, wg, wp, g_last, qn, *caches)


def _own_lanes():
    sub = lax.broadcasted_iota(jnp.int32, (HEADS, SLAB), 0)
    lane_head = lax.broadcasted_iota(jnp.int32, (HEADS, SLAB), 1) // HEAD_DIM
    return sub == lane_head


def _block_diag_queries(qg):
    own = _own_lanes()
    return jnp.concatenate(
        [jnp.where(own, jnp.broadcast_to(qg[t:t + 1, :], (HEADS, SLAB)), 0.0)
         for t in range(qg.shape[0])], axis=0)


def _cached_scores(qbd, kt, dil):
    s = jnp.dot(qbd.astype(BF16), kt.astype(BF16), preferred_element_type=F32)
    pos = lax.broadcasted_iota(jnp.int32, s.shape, 1)
    row_tok = lax.broadcasted_iota(jnp.int32, (s.shape[0], 1), 0) // HEADS
    valid = (pos >= row_tok) if dil == 1 else (jnp.bitwise_and(pos, dil - 1) == row_tok)
    return jnp.where(valid, s, NEG)


def _weighted_cached_values(e, vt):
    return lax.dot_general(e.astype(BF16), vt.astype(BF16), (((1,), (1,)), ((), ())),
                           preferred_element_type=F32)


def _partial_cached_attention(q_ref, cache_refs, acc_ref, ml_ref):
    scale = HEAD_DIM ** -0.5
    scores = []
    m = None
    for g, (c_ref, (_, dil)) in enumerate(zip(cache_refs, DIL_CFG)):
        s = _cached_scores(_block_diag_queries(q_ref[g] * scale), c_ref[0], dil)
        row_max = jnp.max(s, axis=1, keepdims=True)
        m = row_max if m is None else jnp.maximum(m, row_max)
        scores.append(s)
    acc, l = None, None
    for s, c_ref in zip(scores, cache_refs):
        e = jnp.exp(s - m)
        part = _weighted_cached_values(e, c_ref[1])
        row_sum = jnp.sum(e, axis=1, keepdims=True)
        acc = part if acc is None else acc + part
        l = row_sum if l is None else l + row_sum
    acc_ref[...] = acc
    lane = lax.broadcasted_iota(jnp.int32, ml_ref.shape, 1)
    ml_ref[...] = jnp.where(lane == 0, m, jnp.where(lane == 1, l, 0.0))


def _attn_sample_kernel(q_ref, kv_ref, pacc_ref, pml_ref, o_ref, *, n_seq):
    n_tok = q_ref.shape[2]
    n_rows = n_tok * HEADS
    scale = HEAD_DIM ** -0.5
    own = _own_lanes()
    row_tok = lax.broadcasted_iota(jnp.int32, (n_rows, 1), 0) // HEADS

    def one_sequence(n, carry):
        m_part = pml_ref[n, :, 0:1]
        m = m_part
        s_new = []
        for g, (_, dil) in enumerate(DIL_CFG):
            qbd = _block_diag_queries(q_ref[n, g] * scale)
            kn = kv_ref[n, 2 * g]
            for tp in range(n_tok):
                sn = jnp.sum(qbd * kn[tp:tp + 1, :], axis=1, keepdims=True)
                ok = (row_tok >= tp) if dil == 1 else (row_tok == tp)
                sn = jnp.where(ok, sn, NEG)
                m = jnp.maximum(m, sn)
                s_new.append((g, tp, sn))
        w_part = jnp.exp(m_part - m)
        acc = w_part * pacc_ref[n]
        den = w_part * pml_ref[n, :, 1:2]
        for g, tp, sn in s_new:
            e = jnp.exp(sn - m)
            den = den + e
            acc = acc + e * kv_ref[n, 2 * g + 1][tp:tp + 1, :]
        res = acc * (1.0 / den)
        for t in range(n_tok):
            rows = res[t * HEADS:(t + 1) * HEADS, :]
            o_ref[n, t:t + 1, :] = jnp.sum(jnp.where(own, rows, 0.0), axis=0, keepdims=True)
        return carry

    lax.fori_loop(0, n_seq, one_sequence, 0, unroll=4)


def _attn_sample(qn, kvn, pacc, pml, seqs_per_step=16):
    n_seq, _, n_tok, _ = qn.shape
    nb = seqs_per_step

    def spec(*tail):
        return pl.BlockSpec((nb,) + tail, lambda i: (i,) + (0,) * len(tail))

    return pl.pallas_call(
        functools.partial(_attn_sample_kernel, n_seq=nb),
        grid=(n_seq // nb,),
        in_specs=[spec(3, n_tok, SLAB), spec(6, n_tok, SLAB),
                  spec(n_tok * HEADS, SLAB), spec(n_tok * HEADS, LANES)],
        out_specs=spec(n_tok, SLAB),
        out_shape=jax.ShapeDtypeStruct((n_seq, n_tok, SLAB), F32),
        compiler_params=_params("arbitrary"),
        name="attn_sample",
    )(qn, kvn, pacc, pml)


def _even_out_sample_kernel(a_ref, ctx_ref, yb_ref, pw_ref, ps_ref, wo_ref, x_ref, out_ref,
                            *, n_seq, n_tok):
    def ext_row(e, cols):
        if e >= POOL_STATE:
            t = e - POOL_STATE
            return a_ref[t * n_seq:(t + 1) * n_seq, cols]
        return ctx_ref[e, :, cols]

    for t in range(n_tok):
        rows = slice(t * n_seq, (t + 1) * n_seq)
        ya = []
        for gi, w in enumerate(POOL_WINDOWS):
            cols = slice(gi * POOL_GROUP, (gi + 1) * POOL_GROUP)
            terms = [ext_row(POOL_STATE + t - k, cols) for k in range(1, w)]
            ya.append(_pool_mix(terms, a_ref[rows, cols], float(w), pw_ref, ps_ref, gi))
        y = jnp.zeros((n_seq, D_MODEL), F32)
        for gi in range(len(POOL_WINDOWS)):
            y = y + jnp.dot(ya[gi].astype(BF16), wo_ref[gi * POOL_GROUP:(gi + 1) * POOL_GROUP, :],
                            preferred_element_type=F32)
        yb = yb_ref[:, t * SLAB:(t + 1) * SLAB].astype(BF16)
        y = y + jnp.dot(yb, wo_ref[SLAB:, :], preferred_element_type=F32)
        out_ref[rows, :] = x_ref[rows, :] + y


def _even_out_sample(z, ctx, yb, pool_w, pool_scale, w_out, x, n_seq, n_tok):
    T = x.shape[0]
    full = lambda *shape: pl.BlockSpec(shape, lambda i: (0,) * len(shape))
    return pl.pallas_call(
        functools.partial(_even_out_sample_kernel, n_seq=n_seq, n_tok=n_tok),
        grid=(1,),
        in_specs=[
            full(T, SLAB),
            full(POOL_STATE, n_seq, SLAB),
            full(n_seq, n_tok * SLAB),
            full(len(POOL_WINDOWS), POOL_GROUP, POOL_GROUP),
            full(1, SLAB),
            full(2 * SLAB, D_MODEL),
            full(T, D_MODEL),
        ],
        out_specs=full(T, D_MODEL),
        out_shape=jax.ShapeDtypeStruct((T, D_MODEL), F32),
        compiler_params=_params("arbitrary"),
        name="even_out_sample",
    )(z, ctx, yb, pool_w, pool_scale, w_out, x)


def _odd_out_sample_kernel(z_ref, ctx_ref, coef_ref, bias_ref, cw_ref, wo_ref, x_ref,
                           out_ref, *, n_seq, n_tok, mix_terms):
    def ext_row(e):
        if e >= CONV_W - 1:
            t = e - (CONV_W - 1)
            return z_ref[3, t * n_seq:(t + 1) * n_seq, :]
        return ctx_ref[:, e * SLAB:(e + 1) * SLAB]

    for t in range(n_tok):
        rows = slice(t * n_seq, (t + 1) * n_seq)
        sp = jnp.zeros((n_seq, SLAB), F32) + bias_ref[t:t + 1, :]
        for s in mix_terms[t]:
            r = t * n_tok + s
            sp = sp + coef_ref[r:r + 1, :] * z_ref[1, s * n_seq:(s + 1) * n_seq, :]
        yc = z_ref[0, rows, :] * sp
        conv = jnp.zeros((n_seq, SLAB), F32)
        for j in range(CONV_W):
            conv = conv + cw_ref[j:j + 1, :] * ext_row(t + j)
        yd = z_ref[2, rows, :] * conv
        y = jnp.dot(yc.astype(BF16), wo_ref[0:SLAB, :], preferred_element_type=F32)
        y = y + jnp.dot(yd.astype(BF16), wo_ref[SLAB:, :], preferred_element_type=F32)
        out_ref[rows, :] = x_ref[rows, :] + y


def _odd_out_sample(z, ctx, coef, bias, conv_w, w_out, x, n_seq, n_tok, mix_terms):
    T = x.shape[0]
    full = lambda *shape: pl.BlockSpec(shape, lambda i: (0,) * len(shape))
    return pl.pallas_call(
        functools.partial(_odd_out_sample_kernel, n_seq=n_seq, n_tok=n_tok, mix_terms=mix_terms),
        grid=(1,),
        in_specs=[
            full(4, T, SLAB),
            full(n_seq, (CONV_W - 1) * SLAB),
            full(n_tok * n_tok, SLAB),
            full(n_tok, SLAB),
            full(CONV_W, SLAB),
            full(2 * SLAB, D_MODEL),
            full(T, D_MODEL),
        ],
        out_specs=full(T, D_MODEL),
        out_shape=jax.ShapeDtypeStruct((T, D_MODEL), F32),
        compiler_params=_params("arbitrary"),
        name="odd_out_sample",
    )(z, ctx, coef, bias, conv_w, w_out, x)


def _rope_tables(pos, split):
    half = HEAD_DIM // 2
    inv = jnp.power(jnp.float32(ROPE_THETA), -jnp.arange(half, dtype=F32) / half)
    ang = pos.astype(F32)[:, None] * inv[None, :]
    cos = jnp.cos(ang)
    sin = jnp.sin(ang)
    if split:
        return jnp.tile(cos, (1, LANES // half)), jnp.tile(sin, (1, LANES // half))
    cos_t = jnp.tile(jnp.concatenate([cos, cos], axis=-1), (1, LANES // HEAD_DIM))
    sin_t = jnp.tile(jnp.concatenate([-sin, sin], axis=-1), (1, LANES // HEAD_DIM))
    return cos_t, sin_t


def _split_qk_columns_kernel(w_ref, o_ref):
    half = HEAD_DIM // 2
    n_slabs = w_ref.shape[1] // SLAB
    for s in range(n_slabs):
        cols = slice(s * SLAB, (s + 1) * SLAB)
        x = w_ref[:, cols]
        if s >= 1 and (s - 1) % 3 < 2:
            xt = x.T
            xt = jnp.concatenate(
                [xt[h * HEAD_DIM + part * half:h * HEAD_DIM + (part + 1) * half, :]
                 for part in range(2) for h in range(HEADS)], axis=0)
            x = xt.T
        o_ref[:, cols] = x.astype(BF16)


def _split_qk_columns(w, tr=256):
    d, n = w.shape
    return pl.pallas_call(
        _split_qk_columns_kernel,
        grid=(d // tr,),
        in_specs=[pl.BlockSpec((tr, n), lambda i: (i, 0))],
        out_specs=pl.BlockSpec((tr, n), lambda i: (i, 0)),
        out_shape=jax.ShapeDtypeStruct((d, n), BF16),
        compiler_params=_params("arbitrary"),
        name="split_qk_columns",
    )(w)


def kernel(x_prompt, x_sample, cache_kv_w128, cache_kv_w512, cache_kv_w2048, state_pool, state_conv,
           p_prompt, p_sample, ev_w_in, ev_pool_w, ev_pool_scale, ev_w_out, od_w_in, od_ln_g, od_ln_b,
           od_ws, od_bs, od_conv_w, od_w_out, norm_mix, norm_ffn, norm_ple, ffn_w1, ffn_w2,
           ple_w_proj, ple_w_gate, norm_final):
    n_p, seq, _ = x_prompt.shape
    n_s, n_tok, _ = x_sample.shape
    depth = norm_mix.shape[0]
    tp = n_p * seq
    ts = n_s * n_tok

    bf = lambda w: w.astype(BF16)
    row = lambda v: v.reshape(1, -1)
    ev_w_in_b, ev_pool_w_b, ev_w_out_b = bf(ev_w_in), bf(ev_pool_w), bf(ev_w_out)
    od_w_in_b, od_w_out_b = bf(od_w_in), bf(od_w_out)
    w1_b, w2_b, wg_b, wp_b = bf(ffn_w1), bf(ffn_w2), bf(ple_w_gate), bf(ple_w_proj)

    cos_p, sin_p = _rope_tables(jnp.arange(seq), True)
    pos_s = [PAST_LEN + t for t in range(n_tok)]
    cos_s, sin_s = _rope_tables(jnp.repeat(jnp.asarray(pos_s, jnp.int32), n_s), False)

    mix_terms = tuple(
        tuple(s for s in range(n_tok)
              if pos_s[s] // CHUNK == pos_s[t] // CHUNK and pos_s[s] % CHUNK <= pos_s[t] % CHUNK)
        for t in range(n_tok))
    local = [p % CHUNK for p in pos_s]

    rp = x_prompt.reshape(tp, D_MODEL)
    rs = x_sample.transpose(1, 0, 2).reshape(ts, D_MODEL)
    pp = p_prompt.reshape(depth, tp, PLE_DIM)
    ps = p_sample.transpose(0, 2, 1, 3).reshape(depth, ts, PLE_DIM)

    tm_p, tm_mix, tm_ffn, tf = 512, 512, 512, 512
    kv_p = [[] for _ in DIL_CFG]
    kv_s = [[] for _ in DIL_CFG]
    pool_p, pool_s, conv_p, conv_s, cv_s = [], [], [], [], []

    for i in range(depth):
        g_mix = row(norm_mix[i])
        ffn_args = (norm_ffn[:, None, :], w1_b, w2_b, norm_ple[:, None, :], wg_b, wp_b,
                    row(norm_final), i == depth - 1)
        if i % 2 == 0:
            e = i // 2
            pscale = row(ev_pool_scale[e])
            dils = tuple(d for _, d in DIL_CFG)
            w_split = _split_qk_columns(ev_w_in[e])
            za, *groups = _proj_even(rp, g_mix, w_split, cos_p, sin_p, tm_p, n_p, dils, True)
            groups = [grp.reshape(3, n_p, dil, seq // dil, SLAB) for grp, dil in zip(groups, dils)]
            attn = [_attn_prompt(grp, dil) for grp, dil in zip(groups, dils)]
            rp = _even_out_prompt(za, attn, ev_pool_w_b[e], pscale, ev_w_out_b[e], rp, seq, tm_mix)
            for g, ((win, _), grp) in enumerate(zip(DIL_CFG, groups)):
                keep = min(win, seq)
                tail = _kv_tail(grp, keep).reshape(n_p, 2, HEADS, HEAD_DIM, keep)
                kv_p[g].append(tail.transpose(0, 4, 1, 2, 3))
            pool_p.append(za.reshape(n_p, seq, SLAB)[:, seq - POOL_STATE:])
            zas, *sgroups = _proj_even(rs, g_mix, ev_w_in_b[e], cos_s, sin_s, ts, 1, (1, 1, 1),
                                       False)
            zn = jnp.stack(sgroups).reshape(3, 3, n_tok, n_s, SLAB).transpose(3, 0, 1, 2, 4)
            qn = zn[:, :, 0]
            kvn = zn[:, :, 1:].reshape(n_s, 6, n_tok, SLAB)
            native = lambda c: c.transpose(0, 2, 3, 4, 1).reshape(n_s, 2, SLAB, c.shape[1])
            caches = [native(c[e]) for c in (cache_kv_w128, cache_kv_w512, cache_kv_w2048)]
            rp, pacc, pml = _ffn_ple_hosting(i, rp, pp, *ffn_args, tf, qn, caches, HOST_SPLITS)
            yb = _attn_sample(qn, kvn, pacc, pml)
            rs = _even_out_sample(zas, state_pool[e].transpose(1, 0, 2),
                                  yb.reshape(n_s, n_tok * SLAB), ev_pool_w_b[e], pscale,
                                  ev_w_out_b[e], rs, n_s, n_tok)
            for g in range(len(DIL_CFG)):
                k = zn[:, g, 1].reshape(n_s, n_tok, HEADS, HEAD_DIM)
                v = zn[:, g, 2].reshape(n_s, n_tok, HEADS, HEAD_DIM)
                kv_s[g].append(jnp.stack([k, v], axis=2))
            a_n = zas.reshape(n_tok, n_s, SLAB).transpose(1, 0, 2)
            pool_s.append(jnp.concatenate([state_pool[e], a_n], axis=1)[:, -POOL_STATE:])
        else:
            o = i // 2
            ln_g, ln_b = row(od_ln_g[o]), row(od_ln_b[o])
            bs_rows = jnp.repeat(od_bs[o].T, LANES, axis=1)
            z = _proj_odd(rp, g_mix, od_w_in_b[o], ln_g, ln_b, tm_p)
            rp = _odd_out_ffn(i, z, od_ws[o], bs_rows, od_conv_w[o], od_w_out_b[o], rp, seq, pp,
                              *ffn_args, tm_ffn, tf)
            conv_p.append(z.reshape(4, n_p, seq, SLAB)[3, :, seq - (CONV_W - 1):])
            zs = _proj_odd(rs, g_mix, od_w_in_b[o], ln_g, ln_b, ts)
            coef = jnp.stack([jnp.repeat(od_ws[o][:, local[t], local[s]], LANES)
                              for t in range(n_tok) for s in range(n_tok)])
            bias = jnp.stack([bs_rows[local[t]] for t in range(n_tok)])
            rs = _odd_out_sample(zs, state_conv[o].reshape(n_s, (CONV_W - 1) * SLAB), coef, bias,
                                 od_conv_w[o], od_w_out_b[o], rs, n_s, n_tok, mix_terms)
            hd_n = zs[3].reshape(n_tok, n_s, SLAB).transpose(1, 0, 2)
            conv_s.append(jnp.concatenate([state_conv[o], hd_n], axis=1)[:, -(CONV_W - 1):])
            cv_s.append(zs[1].reshape(n_tok, n_s, SLAB).transpose(1, 0, 2))

        rs = _ffn_ple(i, rs, ps, *ffn_args, ts, tf)

    y_prompt = rp.reshape(n_p, seq, D_MODEL)
    y_sample = rs.reshape(n_tok, n_s, D_MODEL).transpose(1, 0, 2)
    st = lambda lst: jnp.stack(lst, axis=0)
    return (y_prompt, y_sample, st(kv_p[0]), st(kv_p[1]), st(kv_p[2]),
            st(kv_s[0]), st(kv_s[1]), st(kv_s[2]),
            st(pool_p), st(pool_s), st(conv_p), st(conv_s), st(cv_s))
```

```python
import functools
import math

import jax
import jax.numpy as jnp
from jax import lax
from jax.experimental import pallas as pl
from jax.experimental.pallas import tpu as pltpu

F32 = jnp.float32
BF16 = jnp.bfloat16

D_MODEL = 1024
D_FF = 4 * D_MODEL
PLE_DIM = 256
EPS = 1e-6
ROPE_THETA = 10000.0
PAST_LEN = 2048

SLAB = 512
POOL_WINDOWS = (2, 4, 8, 16)
POOL_GROUP = 128
POOL_STATE = 15
POOL_HALO = 16
DIL_CFG = ((128, 1), (512, 4), (2048, 16))
HEADS = 8
HEAD_DIM = 64
ATT_BLOCK = 128
ATT_ROWS = 512
CHUNK = 128
C_GROUPS = 4
CONV_W = 3
CONV_HALO = 8
LANES = 128
NEG = -1e30

VMEM_LIMIT = 52 * 1024 * 1024
HOSTING_VMEM_LIMIT = 60 * 1024 * 1024
HOST_SPLITS = 4
Q_ROWS = 8


def _params(*sem):
    return pltpu.CompilerParams(dimension_semantics=sem, vmem_limit_bytes=VMEM_LIMIT)


def _rms(x, g):
    ms = jnp.mean(x * x, axis=-1, keepdims=True)
    return x * lax.rsqrt(ms + EPS) * g


def _gelu(x):
    c = math.sqrt(2.0 / math.pi)
    return 0.5 * x * (1.0 + jnp.tanh(c * (x + 0.044715 * (x * x * x))))


def _proj_even_kernel(x_ref, g_ref, w_ref, cos_ref, sin_ref, za_ref, g0_ref, g1_ref, g2_ref,
                      zs_ref, *, dils, split):
    tm = x_ref.shape[0]
    nl = SLAB // LANES
    hn = _rms(x_ref[...], g_ref[...]).astype(BF16)
    za_ref[...] = jnp.dot(hn, w_ref[:, 0:SLAB], preferred_element_type=F32)

    def rotate(chunks):
        cos = cos_ref[...]
        sin = sin_ref[...]
        if split:
            h = nl // 2
            return ([chunks[i] * cos - chunks[i + h] * sin for i in range(h)]
                    + [chunks[i] * cos + chunks[i - h] * sin for i in range(h, nl)])
        lane = lax.broadcasted_iota(jnp.int32, cos.shape, 1)
        first_half = jnp.bitwise_and(lane, HEAD_DIM - 1) < (HEAD_DIM // 2)
        out = []
        for zc in chunks:
            partner = jnp.where(first_half,
                                pltpu.roll(zc, LANES - HEAD_DIM // 2, 1),
                                pltpu.roll(zc, HEAD_DIM // 2, 1))
            out.append(zc * cos + partner * sin)
        return out

    slot = 0
    for g, (out_ref, dil) in enumerate(zip((g0_ref, g1_ref, g2_ref), dils)):
        for c in range(3):
            col0 = (1 + 3 * g + c) * SLAB
            z = jnp.dot(hn, w_ref[:, col0:col0 + SLAB], preferred_element_type=F32)
            chunks = [z[:, i * LANES:(i + 1) * LANES] for i in range(nl)]
            if c < 2:
                chunks = rotate(chunks)
            for i, zc in enumerate(chunks):
                cols = slice(i * LANES, (i + 1) * LANES)
                if dil == 1:
                    out_ref[c, :, cols] = zc
                else:
                    zs_ref[slot] = zc
                    for r in range(dil):
                        out_ref[c, r, :, cols] = zs_ref[slot, pl.ds(r, tm // dil, stride=dil), :]
                    slot += 1


def _resident(shape):
    return pl.BlockSpec(shape, lambda *_: (0,) * len(shape), pipeline_mode=pl.Buffered(1))


def _proj_even(x, g, w, cos, sin, tm, n_seq, dils, split):
    T = x.shape[0]
    ntab = cos.shape[0] // tm
    tps = T // n_seq // tm

    def group_spec(dil):
        if dil == 1:
            return pl.BlockSpec((3, tm, SLAB), lambda i: (0, i, 0))
        return pl.BlockSpec((3, None, dil, tm // dil, SLAB), lambda i: (0, i // tps, 0, i % tps, 0))

    def group_shape(dil):
        if dil == 1:
            return jax.ShapeDtypeStruct((3, T, SLAB), F32)
        return jax.ShapeDtypeStruct((3, n_seq, dil, T // n_seq // dil, SLAB), F32)

    n_slots = max(1, 3 * (SLAB // LANES) * sum(d > 1 for d in dils))
    return pl.pallas_call(
        functools.partial(_proj_even_kernel, dils=dils, split=split),
        grid=(T // tm,),
        in_specs=[
            pl.BlockSpec((tm, D_MODEL), lambda i: (i, 0)),
            _resident((1, D_MODEL)),
            _resident(w.shape),
            pl.BlockSpec((tm, LANES), lambda i: (i % ntab, 0)),
            pl.BlockSpec((tm, LANES), lambda i: (i % ntab, 0)),
        ],
        out_specs=[pl.BlockSpec((tm, SLAB), lambda i: (i, 0))] + [group_spec(d) for d in dils],
        out_shape=[jax.ShapeDtypeStruct((T, SLAB), F32)] + [group_shape(d) for d in dils],
        scratch_shapes=[pltpu.VMEM((n_slots, tm, LANES), F32)],
        compiler_params=_params("arbitrary"),
        name="proj_even",
    )(x, g, w, cos, sin)


def _proj_odd_kernel(x_ref, g_ref, w_ref, lng_ref, lnb_ref, o_ref):
    hn = _rms(x_ref[...], g_ref[...]).astype(BF16)

    def slab(s):
        return jnp.dot(hn, w_ref[:, s * SLAB:(s + 1) * SLAB], preferred_element_type=F32)

    o_ref[0] = _gelu(slab(0))
    zv = slab(1)
    for c in range(C_GROUPS):
        sl = slice(c * LANES, (c + 1) * LANES)
        v = _gelu(zv[:, sl])
        mu = jnp.mean(v, axis=-1, keepdims=True)
        dv = v - mu
        var = jnp.mean(dv * dv, axis=-1, keepdims=True)
        o_ref[1, :, sl] = dv * lax.rsqrt(var + EPS) * lng_ref[:, sl] + lnb_ref[:, sl]
    o_ref[2] = slab(2)
    o_ref[3] = slab(3) * slab(4)


def _proj_odd(x, g, w, ln_g, ln_b, tm):
    T = x.shape[0]
    return pl.pallas_call(
        _proj_odd_kernel,
        grid=(T // tm,),
        in_specs=[
            pl.BlockSpec((tm, D_MODEL), lambda i: (i, 0)),
            _resident((1, D_MODEL)),
            _resident(w.shape),
            _resident((1, SLAB)),
            _resident((1, SLAB)),
        ],
        out_specs=pl.BlockSpec((4, tm, SLAB), lambda i: (0, i, 0)),
        out_shape=jax.ShapeDtypeStruct((4, T, SLAB), F32),
        compiler_params=_params("arbitrary"),
        name="proj_odd",
    )(x, g, w, ln_g, ln_b)


def _attn_prompt_kernel(q_ref, kp_ref, kc_ref, vp_ref, vc_ref, o_ref, lse_ref,
                        qst_ref, ks_ref, vs_ref, bias_ref, s_ref, e_ref):
    step = pl.program_id(2)
    att_rows = q_ref.shape[0]
    sub = att_rows // ATT_BLOCK
    half_rot = HEAD_DIM // 2
    heads_per_blk = LANES // half_rot
    n_blk = HEADS // heads_per_blk
    log2e = 1.4426950408889634
    ln2 = 0.6931471805599453
    scale = HEAD_DIM ** -0.5 * log2e

    qlane = lax.broadcasted_iota(jnp.int32, (att_rows, 2 * LANES), 1)
    head_in_blk = jnp.bitwise_and(qlane, LANES - 1) // half_rot
    for j in range(n_blk):
        lo = slice(j * LANES, (j + 1) * LANES)
        hi = slice((n_blk + j) * LANES, (n_blk + j + 1) * LANES)
        dst = slice(2 * j * LANES, (2 * j + 2) * LANES)
        q2 = (jnp.concatenate([q_ref[:, lo], q_ref[:, hi]], axis=1) * scale).astype(BF16)
        for hq in range(heads_per_blk):
            qm = jnp.where(head_in_blk == hq, q2, jnp.zeros_like(q2))
            for b in range(sub):
                r0 = ((j * sub + b) * heads_per_blk + hq) * ATT_BLOCK
                qst_ref[r0:r0 + ATT_BLOCK, :] = qm[b * ATT_BLOCK:(b + 1) * ATT_BLOCK, :]
        ks_ref[0:ATT_BLOCK, dst] = jnp.concatenate([kp_ref[:, lo], kp_ref[:, hi]],
                                                   axis=1).astype(BF16)
        ks_ref[ATT_BLOCK:, dst] = jnp.concatenate([kc_ref[:, lo], kc_ref[:, hi]],
                                                  axis=1).astype(BF16)
    vs_ref[0:ATT_BLOCK, :] = vp_ref[...].astype(BF16)
    vs_ref[ATT_BLOCK:, :] = vc_ref[...].astype(BF16)

    qi = lax.broadcasted_iota(jnp.int32, (ATT_BLOCK, 2 * ATT_BLOCK), 0)
    ki = lax.broadcasted_iota(jnp.int32, (ATT_BLOCK, 2 * ATT_BLOCK), 1)
    rel = qi + ATT_BLOCK - ki
    band = jnp.logical_and(rel >= 0, rel <= ATT_BLOCK)
    bias_ref[0] = jnp.where(band, 0.0, NEG)
    bias_ref[1] = jnp.where(jnp.logical_and(band, ki >= ATT_BLOCK), 0.0, NEG)
    lane = lax.broadcasted_iota(jnp.int32, (ATT_BLOCK, LANES), 1)
    low_head = lane < HEAD_DIM
    blk_rows = heads_per_blk * ATT_BLOCK

    first = (step == 0).astype(jnp.int32)
    hb = ATT_BLOCK // 2
    lane_hb = lax.broadcasted_iota(jnp.int32, (hb, LANES), 1)

    def scores(b):
        for j in range(n_blk):
            base = (j * sub + b) * blk_rows
            k2 = ks_ref[b * ATT_BLOCK:(b + 2) * ATT_BLOCK, 2 * j * LANES:(2 * j + 2) * LANES]
            s_ref[b, j * blk_rows:(j + 1) * blk_rows, :] = lax.dot_general(
                qst_ref[base:base + blk_rows, :], k2, (((1,), (1,)), ((), ())),
                preferred_element_type=F32)

    def softmax(b):
        which = first if b == 0 else 0
        for rh in range(2):
            lse_tile = jnp.zeros((hb, LANES), F32)
            for h in range(HEADS):
                rows = slice(h * ATT_BLOCK + rh * hb, h * ATT_BLOCK + (rh + 1) * hb)
                s = s_ref[b, rows, :] + bias_ref[which, rh * hb:(rh + 1) * hb, :]
                m = jnp.max(s, axis=-1, keepdims=True)
                e = jnp.exp2(s - m)
                l = jnp.sum(e, axis=-1, keepdims=True)
                e_ref[b, rows, :] = (e * (1.0 / l)).astype(BF16)
                lse_tile = jnp.where(lane_hb == h, m * ln2 + jnp.log(l), lse_tile)
            lse_ref[b * ATT_BLOCK + rh * hb:b * ATT_BLOCK + (rh + 1) * hb, :] = lse_tile

    def weighted_values(b):
        for hp in range(HEADS // 2):
            cols = slice(hp * LANES, (hp + 1) * LANES)
            v2 = vs_ref[b * ATT_BLOCK:(b + 2) * ATT_BLOCK, cols]
            pv = jnp.dot(e_ref[b, 2 * hp * ATT_BLOCK:(2 * hp + 2) * ATT_BLOCK, :], v2,
                         preferred_element_type=F32)
            o_ref[b * ATT_BLOCK:(b + 1) * ATT_BLOCK, cols] = jnp.where(
                low_head, pv[0:ATT_BLOCK], pv[ATT_BLOCK:])

    for t in range(sub + 2):
        if t < sub:
            scores(t)
        if 0 <= t - 1 < sub:
            softmax(t - 1)
        if 0 <= t - 2 < sub:
            weighted_values(t - 2)


def _attn_prompt(qkv, dil):
    _, n_seq, _, m_rows, _ = qkv.shape
    att_rows = min(ATT_ROWS, m_rows)
    steps = m_rows // att_rows
    sub = att_rows // ATT_BLOCK

    def cur(slab):
        return pl.BlockSpec((None, None, None, att_rows, SLAB), lambda n, r, b: (slab, n, r, b, 0))

    def prev(slab):
        return pl.BlockSpec((None, None, None, ATT_BLOCK, SLAB),
                            lambda n, r, b: (slab, n, r, jnp.maximum(b * sub - 1, 0), 0))

    return pl.pallas_call(
        _attn_prompt_kernel,
        grid=(n_seq, dil, steps),
        in_specs=[cur(0), prev(1), cur(1), prev(2), cur(2)],
        out_specs=[
            pl.BlockSpec((None, None, att_rows, SLAB), lambda n, r, b: (n, r, b, 0)),
            pl.BlockSpec((None, None, att_rows, LANES), lambda n, r, b: (n, r, b, 0)),
        ],
        out_shape=[
            jax.ShapeDtypeStruct((n_seq, dil, m_rows, SLAB), F32),
            jax.ShapeDtypeStruct((n_seq, dil, m_rows, LANES), F32),
        ],
        scratch_shapes=[
            pltpu.VMEM((HEADS * att_rows, 2 * LANES), BF16),
            pltpu.VMEM((att_rows + ATT_BLOCK, SLAB), BF16),
            pltpu.VMEM((att_rows + ATT_BLOCK, SLAB), BF16),
            pltpu.VMEM((2, ATT_BLOCK, 2 * ATT_BLOCK), F32),
            pltpu.VMEM((sub, HEADS * ATT_BLOCK, 2 * ATT_BLOCK), F32),
            pltpu.VMEM((sub, HEADS * ATT_BLOCK, 2 * ATT_BLOCK), BF16),
        ],
        compiler_params=_params("arbitrary", "arbitrary", "arbitrary"),
        name=f"attn_prompt_d{dil}",
    )(qkv, qkv, qkv, qkv, qkv)


def _kv_tail_kernel(k_ref, v_ref, o_ref, tok_ref):
    dil, rows, _ = k_ref.shape
    keep = dil * rows
    nl = SLAB // LANES
    half_rot = HEAD_DIM // 2
    heads_per_chunk = LANES // half_rot
    for kv, src in enumerate((k_ref, v_ref)):
        for lc in range(nl):
            for r in range(dil):
                tok_ref[lc, pl.ds(r, rows, stride=dil), :] = src[r, :, lc * LANES:(lc + 1) * LANES]
            for pc in range(keep // LANES):
                pcols = slice(pc * LANES, (pc + 1) * LANES)
                t = tok_ref[lc, pcols, :].T
                if kv == 0:
                    half, hblk = divmod(lc, nl // 2)
                    for hq in range(heads_per_chunk):
                        r0 = (hblk * heads_per_chunk + hq) * HEAD_DIM + half * half_rot
                        o_ref[0, r0:r0 + half_rot, pcols] = t[hq * half_rot:(hq + 1) * half_rot, :]
                else:
                    o_ref[1, lc * LANES:(lc + 1) * LANES, pcols] = t


def _kv_tail(grp, keep):
    _, n_seq, dil, m_rows, _ = grp.shape
    rows = keep // dil
    last = m_rows // rows - 1

    def spec(slab):
        return pl.BlockSpec((None, None, dil, rows, SLAB), lambda n: (slab, n, 0, last, 0))

    return pl.pallas_call(
        _kv_tail_kernel,
        grid=(n_seq,),
        in_specs=[spec(1), spec(2)],
        out_specs=pl.BlockSpec((None, 2, SLAB, keep), lambda n: (n, 0, 0, 0)),
        out_shape=jax.ShapeDtypeStruct((n_seq, 2, SLAB, keep), F32),
        scratch_shapes=[pltpu.VMEM((SLAB // LANES, keep, LANES), F32)],
        compiler_params=_params("arbitrary"),
        name=f"kv_tail_d{dil}",
    )(grp, grp)


def _pool_mix(window_terms, a_cols, cnt, pw_ref, scale_ref, gi):
    acc = a_cols
    for term in window_terms:
        acc = acc + term
    pooled = acc / cnt - a_cols
    cols = slice(gi * POOL_GROUP, (gi + 1) * POOL_GROUP)
    mixed = jnp.dot(pooled.astype(BF16), pw_ref[gi], preferred_element_type=F32)
    return mixed * scale_ref[:, cols]


def _even_out_prompt_kernel(a_ref, halo_ref, o0_ref, o1_ref, o2_ref, l0_ref, l1_ref, l2_ref,
                            pw_ref, ps_ref, wo_ref, x_ref, out_ref, ext_ref, ya_ref, yb_ref,
                            oi_ref, li_ref, *, tiles_per_seq):
    tm = a_ref.shape[0]
    it = pl.program_id(0) % tiles_per_seq
    halo = halo_ref[...]
    ext_ref[0:POOL_HALO, :] = jnp.where(it == 0, jnp.zeros_like(halo), halo)
    ext_ref[POOL_HALO:, :] = a_ref[...]
    pos = it * tm + lax.broadcasted_iota(jnp.int32, (tm, 1), 0)
    for gi, w in enumerate(POOL_WINDOWS):
        cols = slice(gi * POOL_GROUP, (gi + 1) * POOL_GROUP)
        terms = [ext_ref[POOL_HALO - k:POOL_HALO - k + tm, cols] for k in range(1, w)]
        cnt = jnp.minimum(w, pos + 1).astype(F32)
        ya_ref[:, cols] = _pool_mix(terms, a_ref[:, cols], cnt, pw_ref, ps_ref, gi).astype(BF16)

    for g, (o_ref, l_ref) in enumerate(((o0_ref, l0_ref), (o1_ref, l1_ref), (o2_ref, l2_ref))):
        dil = o_ref.shape[0]
        for r in range(dil):
            rows = pl.ds(r, tm // dil, stride=dil)
            li_ref[g, rows, :] = l_ref[r]
            for lc in range(SLAB // LANES):
                oi_ref[g, lc, rows, :] = o_ref[r, :, lc * LANES:(lc + 1) * LANES]

    l0, l1, l2 = li_ref[0], li_ref[1], li_ref[2]
    mx = jnp.maximum(jnp.maximum(l0, l1), l2)
    e0, e1, e2 = jnp.exp(l0 - mx), jnp.exp(l1 - mx), jnp.exp(l2 - mx)
    inv = 1.0 / (e0 + e1 + e2)
    w0, w1 = e0 * inv, e1 * inv
    for h in range(HEADS):
        lc, lo = divmod(h * HEAD_DIM, LANES)
        sl = slice(lo, lo + HEAD_DIM)
        o2 = oi_ref[2, lc, :, sl]
        yb = (o2 + w0[:, h:h + 1] * (oi_ref[0, lc, :, sl] - o2)
              + w1[:, h:h + 1] * (oi_ref[1, lc, :, sl] - o2))
        yb_ref[:, h * HEAD_DIM:(h + 1) * HEAD_DIM] = yb.astype(BF16)

    y = jnp.dot(ya_ref[...], wo_ref[0:SLAB, :], preferred_element_type=F32)
    y = y + jnp.dot(yb_ref[...], wo_ref[SLAB:, :], preferred_element_type=F32)
    out_ref[...] = x_ref[...] + y


def _even_out_prompt(za, attn, pool_w, pool_scale, w_out, x, seq, tm):
    T = x.shape[0]
    tps = seq // tm
    hb = tm // POOL_HALO

    def residue_spec(dil, width):
        return pl.BlockSpec((None, dil, tm // dil, width), lambda i: (i // tps, 0, i % tps, 0))

    dils = [o.shape[1] for o, _ in attn]
    return pl.pallas_call(
        functools.partial(_even_out_prompt_kernel, tiles_per_seq=tps),
        grid=(T // tm,),
        in_specs=[
            pl.BlockSpec((tm, SLAB), lambda i: (i, 0)),
            pl.BlockSpec((POOL_HALO, SLAB), lambda i: (jnp.maximum(i * hb - 1, 0), 0)),
        ]
        + [residue_spec(d, SLAB) for d in dils] + [residue_spec(d, LANES) for d in dils]
        + [
            pl.BlockSpec((len(POOL_WINDOWS), POOL_GROUP, POOL_GROUP), lambda i: (0, 0, 0)),
            pl.BlockSpec((1, SLAB), lambda i: (0, 0)),
            pl.BlockSpec((2 * SLAB, D_MODEL), lambda i: (0, 0)),
            pl.BlockSpec((tm, D_MODEL), lambda i: (i, 0)),
        ],
        out_specs=pl.BlockSpec((tm, D_MODEL), lambda i: (i, 0)),
        out_shape=jax.ShapeDtypeStruct((T, D_MODEL), F32),
        scratch_shapes=[
            pltpu.VMEM((tm + POOL_HALO, SLAB), F32),
            pltpu.VMEM((tm, SLAB), BF16),
            pltpu.VMEM((tm, SLAB), BF16),
            pltpu.VMEM((len(dils), SLAB // LANES, tm, LANES), F32),
            pltpu.VMEM((len(dils), tm, LANES), F32),
        ],
        compiler_params=_params("arbitrary"),
        name="even_out_prompt",
    )(za, za, *[o for o, _ in attn], *[l for _, l in attn], pool_w, pool_scale, w_out, x)


def _odd_mix(u_ref, vn_ref, go_ref, hd_ref, hdh_ref, ws_ref, bs_ref, cw_ref, wo_ref, x_ref,
             ext_ref, yc_ref, yd_ref, tiles_per_seq):
    tm = u_ref.shape[0]
    it = pl.program_id(0) % tiles_per_seq

    ti = lax.broadcasted_iota(jnp.int32, (CHUNK, CHUNK), 0)
    si = lax.broadcasted_iota(jnp.int32, (CHUNK, CHUNK), 1)
    for g in range(C_GROUPS):
        cols = slice(g * LANES, (g + 1) * LANES)
        wm = jnp.where(si <= ti, ws_ref[g], 0.0).astype(BF16)
        for c in range(tm // CHUNK):
            rows = slice(c * CHUNK, (c + 1) * CHUNK)
            sp = jnp.dot(wm, vn_ref[rows, cols].astype(BF16), preferred_element_type=F32)
            sp = sp + bs_ref[:, cols]
            yc_ref[rows, cols] = (u_ref[rows, cols] * sp).astype(BF16)

    hd = hd_ref[...]
    halo = hdh_ref[...]
    ext_ref[0:CONV_HALO, :] = jnp.where(it == 0, jnp.zeros_like(halo), halo)
    ext_ref[CONV_HALO:, :] = hd
    conv = cw_ref[CONV_W - 1:CONV_W, :] * hd
    for j in range(CONV_W - 1):
        off = CONV_HALO - (CONV_W - 1) + j
        conv = conv + cw_ref[j:j + 1, :] * ext_ref[off:off + tm, :]
    yd_ref[...] = (go_ref[...] * conv).astype(BF16)

    y = jnp.dot(yc_ref[...], wo_ref[0:SLAB, :], preferred_element_type=F32)
    y = y + jnp.dot(yd_ref[...], wo_ref[SLAB:, :], preferred_element_type=F32)
    return x_ref[...] + y


def _odd_out_ffn_kernel(u_ref, vn_ref, go_ref, hd_ref, hdh_ref, ws_ref, bs_ref, cw_ref, wo_ref,
                        x_ref, p_ref, gf_ref, w1_ref, w2_ref, gp_ref, wg_ref, wp_ref, gl_ref,
                        out_ref, ext_ref, yc_ref, yd_ref, *, tiles_per_seq, final_norm, tf):
    r = _odd_mix(u_ref, vn_ref, go_ref, hd_ref, hdh_ref, ws_ref, bs_ref, cw_ref, wo_ref, x_ref,
                 ext_ref, yc_ref, yd_ref, tiles_per_seq)
    out_ref[...] = _mlp_ple(r, p_ref, gf_ref, w1_ref, w2_ref, gp_ref, wg_ref, wp_ref, gl_ref,
                            final_norm, tf)


def _odd_out_ffn(layer, z, ws, bs_rows, conv_w, w_out, x, seq, p, g_ffn, w1, w2, g_ple, wg, wp,
                 g_last, final_norm, tm, tf):
    T = x.shape[0]
    tiles_per_seq = seq // tm
    hb = tm // CONV_HALO

    def slab(s):
        return pl.BlockSpec((None, tm, SLAB), lambda i: (s, i, 0))

    return pl.pallas_call(
        functools.partial(_odd_out_ffn_kernel, tiles_per_seq=tiles_per_seq,
                          final_norm=final_norm, tf=tf),
        grid=(T // tm,),
        in_specs=[
            slab(0), slab(1), slab(2), slab(3),
            pl.BlockSpec((None, CONV_HALO, SLAB), lambda i: (3, jnp.maximum(i * hb - 1, 0), 0)),
            _resident((C_GROUPS, CHUNK, CHUNK)),
            _resident((CHUNK, SLAB)),
            _resident((CONV_W, SLAB)),
            _resident((2 * SLAB, D_MODEL)),
            pl.BlockSpec((tm, D_MODEL), lambda i: (i, 0)),
            pl.BlockSpec((None, tm, PLE_DIM), lambda i: (layer, i, 0)),
            _layer_resident(layer, (1, D_MODEL)),
            _layer_resident(layer, (D_MODEL, D_FF)),
            _layer_resident(layer, (D_FF, D_MODEL)),
            _layer_resident(layer, (1, D_MODEL)),
            _layer_resident(layer, (D_MODEL, D_MODEL)),
            _layer_resident(layer, (PLE_DIM, D_MODEL)),
            _resident((1, D_MODEL)),
        ],
        out_specs=pl.BlockSpec((tm, D_MODEL), lambda i: (i, 0)),
        out_shape=jax.ShapeDtypeStruct((T, D_MODEL), F32),
        scratch_shapes=[
            pltpu.VMEM((tm + CONV_HALO, SLAB), F32),
            pltpu.VMEM((tm, SLAB), BF16),
            pltpu.VMEM((tm, SLAB), BF16),
        ],
        compiler_params=_params("arbitrary"),
        name="odd_out_ffn",
    )(z, z, z, z, z, ws, bs_rows, conv_w, w_out, x, p, g_ffn, w1, w2, g_ple, wg, wp, g_last)


def _mlp_ple(x, p_ref, gf_ref, w1_ref, w2_ref, gp_ref, wg_ref, wp_ref, gl_ref, final_norm, tf):
    hn = _rms(x, gf_ref[...]).astype(BF16)
    acc = None
    for c in range(D_FF // tf):
        h1 = jnp.dot(hn, w1_ref[:, c * tf:(c + 1) * tf], preferred_element_type=F32)
        h1 = jnp.square(jnp.maximum(h1, 0.0)).astype(BF16)
        part = jnp.dot(h1, w2_ref[c * tf:(c + 1) * tf, :], preferred_element_type=F32)
        acc = part if acc is None else acc + part
    r = x + acc
    hp = _rms(r, gp_ref[...]).astype(BF16)
    gate = jax.nn.sigmoid(jnp.dot(hp, wg_ref[...], preferred_element_type=F32))
    proj = jnp.dot(p_ref[...].astype(BF16), wp_ref[...], preferred_element_type=F32)
    r = r + gate * proj
    if final_norm:
        r = _rms(r, gl_ref[...])
    return r


def _ffn_ple_kernel(x_ref, p_ref, gf_ref, w1_ref, w2_ref, gp_ref, wg_ref, wp_ref, gl_ref,
                    out_ref, *, final_norm, tf):
    out_ref[...] = _mlp_ple(x_ref[...], p_ref, gf_ref, w1_ref, w2_ref, gp_ref, wg_ref, wp_ref,
                            gl_ref, final_norm, tf)


def _layer_resident(layer, shape):
    return pl.BlockSpec((None,) + shape, lambda *_: (layer,) + (0,) * len(shape),
                        pipeline_mode=pl.Buffered(1))


def _ffn_ple(layer, x, p, g_ffn, w1, w2, g_ple, wg, wp, g_last, final_norm, tm, tf):
    T = x.shape[0]
    return pl.pallas_call(
        functools.partial(_ffn_ple_kernel, final_norm=final_norm, tf=tf),
        grid=(T // tm,),
        in_specs=[
            pl.BlockSpec((tm, D_MODEL), lambda i: (i, 0)),
            pl.BlockSpec((None, tm, PLE_DIM), lambda i: (layer, i, 0)),
            _layer_resident(layer, (1, D_MODEL)),
            _layer_resident(layer, (D_MODEL, D_FF)),
            _layer_resident(layer, (D_FF, D_MODEL)),
            _layer_resident(layer, (1, D_MODEL)),
            _layer_resident(layer, (D_MODEL, D_MODEL)),
            _layer_resident(layer, (PLE_DIM, D_MODEL)),
            _resident((1, D_MODEL)),
        ],
        out_specs=pl.BlockSpec((tm, D_MODEL), lambda i: (i, 0)),
        out_shape=jax.ShapeDtypeStruct((T, D_MODEL), F32),
        compiler_params=_params("arbitrary"),
        name="ffn_ple",
    )(x, p, g_ffn, w1, w2, g_ple, wg, wp, g_last)


def _ffn_ple_hosting_kernel(x_ref, p_ref, gf_ref, w1_ref, w2_ref, gp_ref, wg_ref, wp_ref, gl_ref,
                            q0_ref, q1_ref, q2_ref, c0_ref, c1_ref, c2_ref, out_ref, pacc_ref,
                            pml_ref, hn_ref, acc_ref, *, final_norm, tf, splits):
    j = pl.program_id(1)
    share = D_FF // splits

    def mlp_share(k):
        acc = None
        for c in range(share // tf):
            cols = slice(k * share + c * tf, k * share + (c + 1) * tf)
            h1 = jnp.dot(hn_ref[...], w1_ref[:, cols], preferred_element_type=F32)
            h1 = jnp.square(jnp.maximum(h1, 0.0)).astype(BF16)
            part = jnp.dot(h1, w2_ref[cols, :], preferred_element_type=F32)
            acc = part if acc is None else acc + part
        return acc

    @pl.when(j == 0)
    def _():
        hn_ref[...] = _rms(x_ref[...], gf_ref[...]).astype(BF16)
        acc_ref[...] = mlp_share(0)

    for k in range(1, splits - 1):
        @pl.when(j == k)
        def _(k=k):
            acc_ref[...] += mlp_share(k)

    @pl.when(j == splits - 1)
    def _():
        r = x_ref[...] + (acc_ref[...] + mlp_share(splits - 1))
        hp = _rms(r, gp_ref[...]).astype(BF16)
        gate = jax.nn.sigmoid(jnp.dot(hp, wg_ref[...], preferred_element_type=F32))
        proj = jnp.dot(p_ref[...].astype(BF16), wp_ref[...], preferred_element_type=F32)
        r = r + gate * proj
        if final_norm:
            r = _rms(r, gl_ref[...])
        out_ref[...] = r

    row = (splits * pl.program_id(0) + j) % Q_ROWS
    q_rows = [[q_ref[t, pl.ds(row, 1), :] for t in range(q_ref.shape[0])]
              for q_ref in (q0_ref, q1_ref, q2_ref)]
    _partial_cached_attention(q_rows, (c0_ref, c1_ref, c2_ref), pacc_ref, pml_ref)


def _ffn_ple_hosting(layer, x, p, g_ffn, w1, w2, g_ple, wg, wp, g_last, final_norm, tf,
                     qkv, caches, splits):
    T = x.shape[0]
    _, n_tok, n_seq, _ = qkv[0].shape
    tm = splits * T // n_seq
    seq_of = lambda i, j: splits * i + j

    def cache_spec(c):
        return pl.BlockSpec((None, 2, SLAB, c.shape[-1]), lambda i, j: (seq_of(i, j), 0, 0, 0))

    return pl.pallas_call(
        functools.partial(_ffn_ple_hosting_kernel, final_norm=final_norm, tf=tf, splits=splits),
        grid=(T // tm, splits),
        in_specs=[
            pl.BlockSpec((tm, D_MODEL), lambda i, j: (i, 0)),
            pl.BlockSpec((None, tm, PLE_DIM), lambda i, j: (layer, i, 0)),
            _layer_resident(layer, (1, D_MODEL)),
            _layer_resident(layer, (D_MODEL, D_FF)),
            _layer_resident(layer, (D_FF, D_MODEL)),
            _layer_resident(layer, (1, D_MODEL)),
            _layer_resident(layer, (D_MODEL, D_MODEL)),
            _layer_resident(layer, (PLE_DIM, D_MODEL)),
            _resident((1, D_MODEL)),
        ] + [pl.BlockSpec((None, n_tok, Q_ROWS, SLAB),
                          lambda i, j: (0, 0, seq_of(i, j) // Q_ROWS, 0)) for _ in qkv]
        + [cache_spec(c) for c in caches],
        out_specs=[
            pl.BlockSpec((tm, D_MODEL), lambda i, j: (i, 0)),
            pl.BlockSpec((None, n_tok * HEADS, SLAB), lambda i, j: (seq_of(i, j), 0, 0)),
            pl.BlockSpec((None, n_tok * HEADS, LANES), lambda i, j: (seq_of(i, j), 0, 0)),
        ],
        out_shape=[
            jax.ShapeDtypeStruct((T, D_MODEL), F32),
            jax.ShapeDtypeStruct((n_seq, n_tok * HEADS, SLAB), F32),
            jax.ShapeDtypeStruct((n_seq, n_tok * HEADS, LANES), F32),
        ],
        scratch_shapes=[pltpu.VMEM((tm, D_MODEL), BF16), pltpu.VMEM((tm, D_MODEL), F32)],
        compiler_params=pltpu.CompilerParams(dimension_semantics=("arbitrary", "arbitrary"),
                                             vmem_limit_bytes=HOSTING_VMEM_LIMIT),
        name="ffn_ple_hosting",
    )(x, p, g_ffn, w1, w2, g_ple, wg, wp, g_last, *qkv, *caches)


def _own_lanes():
    sub = lax.broadcasted_iota(jnp.int32, (HEADS, SLAB), 0)
    lane_head = lax.broadcasted_iota(jnp.int32, (HEADS, SLAB), 1) // HEAD_DIM
    return sub == lane_head


def _block_diag_queries(q_rows):
    own = _own_lanes()
    scale = HEAD_DIM ** -0.5
    return jnp.concatenate(
        [jnp.where(own, jnp.broadcast_to(q * scale, (HEADS, SLAB)), 0.0) for q in q_rows], axis=0)


def _cached_scores(qbd, kt, dil):
    s = jnp.dot(qbd.astype(BF16), kt.astype(BF16), preferred_element_type=F32)
    pos = lax.broadcasted_iota(jnp.int32, s.shape, 1)
    row_tok = lax.broadcasted_iota(jnp.int32, (s.shape[0], 1), 0) // HEADS
    valid = (pos >= row_tok) if dil == 1 else (jnp.bitwise_and(pos, dil - 1) == row_tok)
    return jnp.where(valid, s, NEG)


def _weighted_cached_values(e, vt):
    return lax.dot_general(e.astype(BF16), vt.astype(BF16), (((1,), (1,)), ((), ())),
                           preferred_element_type=F32)


def _partial_cached_attention(q_rows, cache_refs, acc_ref, ml_ref):
    scores = []
    m = None
    for g, (c_ref, (_, dil)) in enumerate(zip(cache_refs, DIL_CFG)):
        s = _cached_scores(_block_diag_queries(q_rows[g]), c_ref[0], dil)
        row_max = jnp.max(s, axis=1, keepdims=True)
        m = row_max if m is None else jnp.maximum(m, row_max)
        scores.append(s)
    acc, l = None, None
    for s, c_ref in zip(scores, cache_refs):
        e = jnp.exp(s - m)
        part = _weighted_cached_values(e, c_ref[1])
        row_sum = jnp.sum(e, axis=1, keepdims=True)
        acc = part if acc is None else acc + part
        l = row_sum if l is None else l + row_sum
    acc_ref[...] = acc
    lane = lax.broadcasted_iota(jnp.int32, ml_ref.shape, 1)
    ml_ref[...] = jnp.where(lane == 0, m, jnp.where(lane == 1, l, 0.0))


def _attn_sample_kernel(g0_ref, g1_ref, g2_ref, pacc_ref, pml_ref, o_ref, *, n_seq):
    groups = (g0_ref, g1_ref, g2_ref)
    n_tok = g0_ref.shape[1]
    n_rows = n_tok * HEADS
    own = _own_lanes()
    row_tok = lax.broadcasted_iota(jnp.int32, (n_rows, 1), 0) // HEADS

    def one_sequence(n, carry):
        m_part = pml_ref[n, :, 0:1]
        m = m_part
        s_new = []
        for g, (g_ref, (_, dil)) in enumerate(zip(groups, DIL_CFG)):
            qbd = _block_diag_queries([g_ref[0, t, pl.ds(n, 1), :] for t in range(n_tok)])
            for tp in range(n_tok):
                sn = jnp.sum(qbd * g_ref[1, tp, pl.ds(n, 1), :], axis=1, keepdims=True)
                ok = (row_tok >= tp) if dil == 1 else (row_tok == tp)
                sn = jnp.where(ok, sn, NEG)
                m = jnp.maximum(m, sn)
                s_new.append((g, tp, sn))
        w_part = jnp.exp(m_part - m)
        acc = w_part * pacc_ref[n]
        den = w_part * pml_ref[n, :, 1:2]
        for g, tp, sn in s_new:
            e = jnp.exp(sn - m)
            den = den + e
            acc = acc + e * groups[g][2, tp, pl.ds(n, 1), :]
        res = acc * (1.0 / den)
        for t in range(n_tok):
            rows = res[t * HEADS:(t + 1) * HEADS, :]
            o_ref[n, t:t + 1, :] = jnp.sum(jnp.where(own, rows, 0.0), axis=0, keepdims=True)
        return carry

    lax.fori_loop(0, n_seq, one_sequence, 0, unroll=4)


def _attn_sample(qkv, pacc, pml, seqs_per_step=16):
    _, n_tok, n_seq, _ = qkv[0].shape
    nb = seqs_per_step

    def spec(*tail):
        return pl.BlockSpec((nb,) + tail, lambda i: (i,) + (0,) * len(tail))

    return pl.pallas_call(
        functools.partial(_attn_sample_kernel, n_seq=nb),
        grid=(n_seq // nb,),
        in_specs=[pl.BlockSpec((3, n_tok, nb, SLAB), lambda i: (0, 0, i, 0)) for _ in qkv]
        + [spec(n_tok * HEADS, SLAB), spec(n_tok * HEADS, LANES)],
        out_specs=spec(n_tok, SLAB),
        out_shape=jax.ShapeDtypeStruct((n_seq, n_tok, SLAB), F32),
        compiler_params=_params("arbitrary"),
        name="attn_sample",
    )(*qkv, pacc, pml)


def _even_out_sample_kernel(a_ref, ctx_ref, yb_ref, pw_ref, ps_ref, wo_ref, x_ref, out_ref,
                            *, n_seq, n_tok):
    def ext_row(e, cols):
        if e >= POOL_STATE:
            t = e - POOL_STATE
            return a_ref[t * n_seq:(t + 1) * n_seq, cols]
        return ctx_ref[e, :, cols]

    for t in range(n_tok):
        rows = slice(t * n_seq, (t + 1) * n_seq)
        ya = []
        for gi, w in enumerate(POOL_WINDOWS):
            cols = slice(gi * POOL_GROUP, (gi + 1) * POOL_GROUP)
            terms = [ext_row(POOL_STATE + t - k, cols) for k in range(1, w)]
            ya.append(_pool_mix(terms, a_ref[rows, cols], float(w), pw_ref, ps_ref, gi))
        y = jnp.zeros((n_seq, D_MODEL), F32)
        for gi in range(len(POOL_WINDOWS)):
            y = y + jnp.dot(ya[gi].astype(BF16), wo_ref[gi * POOL_GROUP:(gi + 1) * POOL_GROUP, :],
                            preferred_element_type=F32)
        yb = yb_ref[:, t * SLAB:(t + 1) * SLAB].astype(BF16)
        y = y + jnp.dot(yb, wo_ref[SLAB:, :], preferred_element_type=F32)
        out_ref[rows, :] = x_ref[rows, :] + y


def _even_out_sample(z, ctx, yb, pool_w, pool_scale, w_out, x, n_seq, n_tok):
    T = x.shape[0]
    full = lambda *shape: pl.BlockSpec(shape, lambda i: (0,) * len(shape))
    return pl.pallas_call(
        functools.partial(_even_out_sample_kernel, n_seq=n_seq, n_tok=n_tok),
        grid=(1,),
        in_specs=[
            full(T, SLAB),
            full(POOL_STATE, n_seq, SLAB),
            full(n_seq, n_tok * SLAB),
            full(len(POOL_WINDOWS), POOL_GROUP, POOL_GROUP),
            full(1, SLAB),
            full(2 * SLAB, D_MODEL),
            full(T, D_MODEL),
        ],
        out_specs=full(T, D_MODEL),
        out_shape=jax.ShapeDtypeStruct((T, D_MODEL), F32),
        compiler_params=_params("arbitrary"),
        name="even_out_sample",
    )(z, ctx, yb, pool_w, pool_scale, w_out, x)


def _odd_out_sample_kernel(z_ref, ctx_ref, coef_ref, bias_ref, cw_ref, wo_ref, x_ref,
                           out_ref, *, n_seq, n_tok, mix_terms):
    def ext_row(e):
        if e >= CONV_W - 1:
            t = e - (CONV_W - 1)
            return z_ref[3, t * n_seq:(t + 1) * n_seq, :]
        return ctx_ref[:, e * SLAB:(e + 1) * SLAB]

    for t in range(n_tok):
        rows = slice(t * n_seq, (t + 1) * n_seq)
        sp = jnp.zeros((n_seq, SLAB), F32) + bias_ref[t:t + 1, :]
        for s in mix_terms[t]:
            r = t * n_tok + s
            sp = sp + coef_ref[r:r + 1, :] * z_ref[1, s * n_seq:(s + 1) * n_seq, :]
        yc = z_ref[0, rows, :] * sp
        conv = jnp.zeros((n_seq, SLAB), F32)
        for j in range(CONV_W):
            conv = conv + cw_ref[j:j + 1, :] * ext_row(t + j)
        yd = z_ref[2, rows, :] * conv
        y = jnp.dot(yc.astype(BF16), wo_ref[0:SLAB, :], preferred_element_type=F32)
        y = y + jnp.dot(yd.astype(BF16), wo_ref[SLAB:, :], preferred_element_type=F32)
        out_ref[rows, :] = x_ref[rows, :] + y


def _odd_out_sample(z, ctx, coef, bias, conv_w, w_out, x, n_seq, n_tok, mix_terms):
    T = x.shape[0]
    full = lambda *shape: pl.BlockSpec(shape, lambda i: (0,) * len(shape))
    return pl.pallas_call(
        functools.partial(_odd_out_sample_kernel, n_seq=n_seq, n_tok=n_tok, mix_terms=mix_terms),
        grid=(1,),
        in_specs=[
            full(4, T, SLAB),
            full(n_seq, (CONV_W - 1) * SLAB),
            full(n_tok * n_tok, SLAB),
            full(n_tok, SLAB),
            full(CONV_W, SLAB),
            full(2 * SLAB, D_MODEL),
            full(T, D_MODEL),
        ],
        out_specs=full(T, D_MODEL),
        out_shape=jax.ShapeDtypeStruct((T, D_MODEL), F32),
        compiler_params=_params("arbitrary"),
        name="odd_out_sample",
    )(z, ctx, coef, bias, conv_w, w_out, x)


def _rope_tables(pos, split):
    half = HEAD_DIM // 2
    inv = jnp.power(jnp.float32(ROPE_THETA), -jnp.arange(half, dtype=F32) / half)
    ang = pos.astype(F32)[:, None] * inv[None, :]
    cos = jnp.cos(ang)
    sin = jnp.sin(ang)
    if split:
        return jnp.tile(cos, (1, LANES // half)), jnp.tile(sin, (1, LANES // half))
    cos_t = jnp.tile(jnp.concatenate([cos, cos], axis=-1), (1, LANES // HEAD_DIM))
    sin_t = jnp.tile(jnp.concatenate([-sin, sin], axis=-1), (1, LANES // HEAD_DIM))
    return cos_t, sin_t


def _split_qk_columns_kernel(w_ref, o_ref):
    half = HEAD_DIM // 2
    n_slabs = w_ref.shape[1] // SLAB
    for s in range(n_slabs):
        cols = slice(s * SLAB, (s + 1) * SLAB)
        x = w_ref[:, cols]
        if s >= 1 and (s - 1) % 3 < 2:
            xt = x.T
            xt = jnp.concatenate(
                [xt[h * HEAD_DIM + part * half:h * HEAD_DIM + (part + 1) * half, :]
                 for part in range(2) for h in range(HEADS)], axis=0)
            x = xt.T
        o_ref[:, cols] = x.astype(BF16)


def _split_qk_columns(w, tr=256):
    d, n = w.shape
    return pl.pallas_call(
        _split_qk_columns_kernel,
        grid=(d // tr,),
        in_specs=[pl.BlockSpec((tr, n), lambda i: (i, 0))],
        out_specs=pl.BlockSpec((tr, n), lambda i: (i, 0)),
        out_shape=jax.ShapeDtypeStruct((d, n), BF16),
        compiler_params=_params("arbitrary"),
        name="split_qk_columns",
    )(w)


def kernel(x_prompt, x_sample, cache_kv_w128, cache_kv_w512, cache_kv_w2048, state_pool, state_conv,
           p_prompt, p_sample, ev_w_in, ev_pool_w, ev_pool_scale, ev_w_out, od_w_in, od_ln_g, od_ln_b,
           od_ws, od_bs, od_conv_w, od_w_out, norm_mix, norm_ffn, norm_ple, ffn_w1, ffn_w2,
           ple_w_proj, ple_w_gate, norm_final):
    n_p, seq, _ = x_prompt.shape
    n_s, n_tok, _ = x_sample.shape
    depth = norm_mix.shape[0]
    tp = n_p * seq
    ts = n_s * n_tok

    bf = lambda w: w.astype(BF16)
    row = lambda v: v.reshape(1, -1)
    ev_w_in_b, ev_pool_w_b, ev_w_out_b = bf(ev_w_in), bf(ev_pool_w), bf(ev_w_out)
    od_w_in_b, od_w_out_b = bf(od_w_in), bf(od_w_out)
    w1_b, w2_b, wg_b, wp_b = bf(ffn_w1), bf(ffn_w2), bf(ple_w_gate), bf(ple_w_proj)

    cos_p, sin_p = _rope_tables(jnp.arange(seq), True)
    pos_s = [PAST_LEN + t for t in range(n_tok)]
    cos_s, sin_s = _rope_tables(jnp.repeat(jnp.asarray(pos_s, jnp.int32), n_s), False)

    mix_terms = tuple(
        tuple(s for s in range(n_tok)
              if pos_s[s] // CHUNK == pos_s[t] // CHUNK and pos_s[s] % CHUNK <= pos_s[t] % CHUNK)
        for t in range(n_tok))
    local = [p % CHUNK for p in pos_s]

    rp = x_prompt.reshape(tp, D_MODEL)
    rs = x_sample.transpose(1, 0, 2).reshape(ts, D_MODEL)
    pp = p_prompt.reshape(depth, tp, PLE_DIM)
    ps = p_sample.transpose(0, 2, 1, 3).reshape(depth, ts, PLE_DIM)

    tm_p, tm_mix, tm_ffn, tf = 512, 512, 512, 512
    kv_p = [[] for _ in DIL_CFG]
    kv_s = [[] for _ in DIL_CFG]
    pool_p, pool_s, conv_p, conv_s, cv_s = [], [], [], [], []

    for i in range(depth):
        g_mix = row(norm_mix[i])
        ffn_args = (norm_ffn[:, None, :], w1_b, w2_b, norm_ple[:, None, :], wg_b, wp_b,
                    row(norm_final), i == depth - 1)
        if i % 2 == 0:
            e = i // 2
            pscale = row(ev_pool_scale[e])
            dils = tuple(d for _, d in DIL_CFG)
            w_split = _split_qk_columns(ev_w_in[e])
            za, *groups = _proj_even(rp, g_mix, w_split, cos_p, sin_p, tm_p, n_p, dils, True)
            groups = [grp.reshape(3, n_p, dil, seq // dil, SLAB) for grp, dil in zip(groups, dils)]
            attn = [_attn_prompt(grp, dil) for grp, dil in zip(groups, dils)]
            rp = _even_out_prompt(za, attn, ev_pool_w_b[e], pscale, ev_w_out_b[e], rp, seq, tm_mix)
            for g, ((win, _), grp) in enumerate(zip(DIL_CFG, groups)):
                keep = min(win, seq)
                tail = _kv_tail(grp, keep).reshape(n_p, 2, HEADS, HEAD_DIM, keep)
                kv_p[g].append(tail.transpose(0, 4, 1, 2, 3))
            pool_p.append(za.reshape(n_p, seq, SLAB)[:, seq - POOL_STATE:])
            zas, *sgroups = _proj_even(rs, g_mix, ev_w_in_b[e], cos_s, sin_s, ts, 1, (1, 1, 1),
                                       False)
            sgroups = [g.reshape(3, n_tok, n_s, SLAB) for g in sgroups]
            native = lambda c: c.transpose(0, 2, 3, 4, 1).reshape(n_s, 2, SLAB, c.shape[1])
            caches = [native(c[e]) for c in (cache_kv_w128, cache_kv_w512, cache_kv_w2048)]
            rp, pacc, pml = _ffn_ple_hosting(i, rp, pp, *ffn_args, tf, sgroups, caches,
                                             HOST_SPLITS)
            yb = _attn_sample(sgroups, pacc, pml)
            rs = _even_out_sample(zas, state_pool[e].transpose(1, 0, 2),
                                  yb.reshape(n_s, n_tok * SLAB), ev_pool_w_b[e], pscale,
                                  ev_w_out_b[e], rs, n_s, n_tok)
            for g, grp in enumerate(sgroups):
                kv = grp[1:].transpose(2, 1, 0, 3)
                kv_s[g].append(kv.reshape(n_s, n_tok, 2, HEADS, HEAD_DIM))
            a_n = zas.reshape(n_tok, n_s, SLAB).transpose(1, 0, 2)
            pool_s.append(jnp.concatenate([state_pool[e], a_n], axis=1)[:, -POOL_STATE:])
        else:
            o = i // 2
            ln_g, ln_b = row(od_ln_g[o]), row(od_ln_b[o])
            bs_rows = jnp.repeat(od_bs[o].T, LANES, axis=1)
            z = _proj_odd(rp, g_mix, od_w_in_b[o], ln_g, ln_b, tm_p)
            rp = _odd_out_ffn(i, z, od_ws[o], bs_rows, od_conv_w[o], od_w_out_b[o], rp, seq, pp,
                              *ffn_args, tm_ffn, tf)
            conv_p.append(z.reshape(4, n_p, seq, SLAB)[3, :, seq - (CONV_W - 1):])
            zs = _proj_odd(rs, g_mix, od_w_in_b[o], ln_g, ln_b, ts)
            coef = jnp.stack([jnp.repeat(od_ws[o][:, local[t], local[s]], LANES)
                              for t in range(n_tok) for s in range(n_tok)])
            bias = jnp.stack([bs_rows[local[t]] for t in range(n_tok)])
            rs = _odd_out_sample(zs, state_conv[o].reshape(n_s, (CONV_W - 1) * SLAB), coef, bias,
                                 od_conv_w[o], od_w_out_b[o], rs, n_s, n_tok, mix_terms)
            hd_n = zs[3].reshape(n_tok, n_s, SLAB).transpose(1, 0, 2)
            conv_s.append(jnp.concatenate([state_conv[o], hd_n], axis=1)[:, -(CONV_W - 1):])
            cv_s.append(zs[1].reshape(n_tok, n_s, SLAB).transpose(1, 0, 2))

        rs = _ffn_ple(i, rs, ps, *ffn_args, ts, tf)

    y_prompt = rp.reshape(n_p, seq, D_MODEL)
    y_sample = rs.reshape(n_tok, n_s, D_MODEL).transpose(1, 0, 2)
    st = lambda lst: jnp.stack(lst, axis=0)
    return (y_prompt, y_sample, st(kv_p[0]), st(kv_p[1]), st(kv_p[2]),
            st(kv_s[0]), st(kv_s[1]), st(kv_s[2]),
            st(pool_p), st(pool_s), st(conv_p), st(conv_s), st(cv_s))
```

```python
import functools
import math

import jax
import jax.numpy as jnp
from jax import lax
from jax.experimental import pallas as pl
from jax.experimental.pallas import tpu as pltpu

F32 = jnp.float32
BF16 = jnp.bfloat16

D_MODEL = 1024
D_FF = 4 * D_MODEL
PLE_DIM = 256
EPS = 1e-6
ROPE_THETA = 10000.0
PAST_LEN = 2048

SLAB = 512
POOL_WINDOWS = (2, 4, 8, 16)
POOL_GROUP = 128
POOL_STATE = 15
POOL_HALO = 16
DIL_CFG = ((128, 1), (512, 4), (2048, 16))
HEADS = 8
HEAD_DIM = 64
ATT_BLOCK = 128
ATT_ROWS = 512
CHUNK = 128
C_GROUPS = 4
CONV_W = 3
CONV_HALO = 8
LANES = 128
NEG = -1e30

VMEM_LIMIT = 52 * 1024 * 1024
HOSTING_VMEM_LIMIT = 60 * 1024 * 1024
HOST_SPLITS = 4
Q_ROWS = 8


def _params(*sem):
    return pltpu.CompilerParams(dimension_semantics=sem, vmem_limit_bytes=VMEM_LIMIT)


def _rms(x, g):
    ms = jnp.mean(x * x, axis=-1, keepdims=True)
    return x * lax.rsqrt(ms + EPS) * g


def _gelu(x):
    c = math.sqrt(2.0 / math.pi)
    return 0.5 * x * (1.0 + jnp.tanh(c * (x + 0.044715 * (x * x * x))))


def _proj_even_kernel(x_ref, g_ref, w_ref, cos_ref, sin_ref, za_ref, g0_ref, g1_ref, g2_ref,
                      zs_ref, *, dils, split):
    tm = x_ref.shape[0]
    nl = SLAB // LANES
    hn = _rms(x_ref[...], g_ref[...]).astype(BF16)
    za_ref[...] = jnp.dot(hn, w_ref[:, 0:SLAB], preferred_element_type=F32)

    def rotate(chunks):
        cos = cos_ref[...]
        sin = sin_ref[...]
        if split:
            h = nl // 2
            return ([chunks[i] * cos - chunks[i + h] * sin for i in range(h)]
                    + [chunks[i] * cos + chunks[i - h] * sin for i in range(h, nl)])
        lane = lax.broadcasted_iota(jnp.int32, cos.shape, 1)
        first_half = jnp.bitwise_and(lane, HEAD_DIM - 1) < (HEAD_DIM // 2)
        out = []
        for zc in chunks:
            partner = jnp.where(first_half,
                                pltpu.roll(zc, LANES - HEAD_DIM // 2, 1),
                                pltpu.roll(zc, HEAD_DIM // 2, 1))
            out.append(zc * cos + partner * sin)
        return out

    slot = 0
    for g, (out_ref, dil) in enumerate(zip((g0_ref, g1_ref, g2_ref), dils)):
        for c in range(3):
            col0 = (1 + 3 * g + c) * SLAB
            z = jnp.dot(hn, w_ref[:, col0:col0 + SLAB], preferred_element_type=F32)
            chunks = [z[:, i * LANES:(i + 1) * LANES] for i in range(nl)]
            if c < 2:
                chunks = rotate(chunks)
            for i, zc in enumerate(chunks):
                cols = slice(i * LANES, (i + 1) * LANES)
                if dil == 1:
                    out_ref[c, :, cols] = zc
                else:
                    zs_ref[slot] = zc
                    for r in range(dil):
                        out_ref[c, r, :, cols] = zs_ref[slot, pl.ds(r, tm // dil, stride=dil), :]
                    slot += 1


def _resident(shape):
    return pl.BlockSpec(shape, lambda *_: (0,) * len(shape), pipeline_mode=pl.Buffered(1))


def _proj_even(x, g, w, cos, sin, tm, n_seq, dils, split):
    T = x.shape[0]
    ntab = cos.shape[0] // tm
    tps = T // n_seq // tm

    def group_spec(dil):
        if dil == 1:
            return pl.BlockSpec((3, tm, SLAB), lambda i: (0, i, 0))
        return pl.BlockSpec((3, None, dil, tm // dil, SLAB), lambda i: (0, i // tps, 0, i % tps, 0))

    def group_shape(dil):
        if dil == 1:
            return jax.ShapeDtypeStruct((3, T, SLAB), F32)
        return jax.ShapeDtypeStruct((3, n_seq, dil, T // n_seq // dil, SLAB), F32)

    n_slots = max(1, 3 * (SLAB // LANES) * sum(d > 1 for d in dils))
    return pl.pallas_call(
        functools.partial(_proj_even_kernel, dils=dils, split=split),
        grid=(T // tm,),
        in_specs=[
            pl.BlockSpec((tm, D_MODEL), lambda i: (i, 0)),
            _resident((1, D_MODEL)),
            _resident(w.shape),
            pl.BlockSpec((tm, LANES), lambda i: (i % ntab, 0)),
            pl.BlockSpec((tm, LANES), lambda i: (i % ntab, 0)),
        ],
        out_specs=[pl.BlockSpec((tm, SLAB), lambda i: (i, 0))] + [group_spec(d) for d in dils],
        out_shape=[jax.ShapeDtypeStruct((T, SLAB), F32)] + [group_shape(d) for d in dils],
        scratch_shapes=[pltpu.VMEM((n_slots, tm, LANES), F32)],
        compiler_params=_params("arbitrary"),
        name="proj_even",
    )(x, g, w, cos, sin)


def _proj_odd_kernel(x_ref, g_ref, w_ref, lng_ref, lnb_ref, o_ref):
    hn = _rms(x_ref[...], g_ref[...]).astype(BF16)

    def slab(s):
        return jnp.dot(hn, w_ref[:, s * SLAB:(s + 1) * SLAB], preferred_element_type=F32)

    o_ref[0] = _gelu(slab(0))
    zv = slab(1)
    for c in range(C_GROUPS):
        sl = slice(c * LANES, (c + 1) * LANES)
        v = _gelu(zv[:, sl])
        mu = jnp.mean(v, axis=-1, keepdims=True)
        dv = v - mu
        var = jnp.mean(dv * dv, axis=-1, keepdims=True)
        o_ref[1, :, sl] = dv * lax.rsqrt(var + EPS) * lng_ref[:, sl] + lnb_ref[:, sl]
    o_ref[2] = slab(2)
    o_ref[3] = slab(3) * slab(4)


def _proj_odd(x, g, w, ln_g, ln_b, tm):
    T = x.shape[0]
    return pl.pallas_call(
        _proj_odd_kernel,
        grid=(T // tm,),
        in_specs=[
            pl.BlockSpec((tm, D_MODEL), lambda i: (i, 0)),
            _resident((1, D_MODEL)),
            _resident(w.shape),
            _resident((1, SLAB)),
            _resident((1, SLAB)),
        ],
        out_specs=pl.BlockSpec((4, tm, SLAB), lambda i: (0, i, 0)),
        out_shape=jax.ShapeDtypeStruct((4, T, SLAB), F32),
        compiler_params=_params("arbitrary"),
        name="proj_odd",
    )(x, g, w, ln_g, ln_b)


def _attn_prompt_kernel(q_ref, kp_ref, kc_ref, vp_ref, vc_ref, o_ref, lse_ref,
                        qst_ref, ks_ref, vs_ref, bias_ref, s_ref, e_ref, mx_ref):
    step = pl.program_id(2)
    att_rows = q_ref.shape[0]
    sub = att_rows // ATT_BLOCK
    half_rot = HEAD_DIM // 2
    heads_per_blk = LANES // half_rot
    n_blk = HEADS // heads_per_blk
    log2e = 1.4426950408889634
    ln2 = 0.6931471805599453
    scale = HEAD_DIM ** -0.5 * log2e

    qlane = lax.broadcasted_iota(jnp.int32, (att_rows, 2 * LANES), 1)
    head_in_blk = jnp.bitwise_and(qlane, LANES - 1) // half_rot
    for j in range(n_blk):
        lo = slice(j * LANES, (j + 1) * LANES)
        hi = slice((n_blk + j) * LANES, (n_blk + j + 1) * LANES)
        dst = slice(2 * j * LANES, (2 * j + 2) * LANES)
        q2 = (jnp.concatenate([q_ref[:, lo], q_ref[:, hi]], axis=1) * scale).astype(BF16)
        for hq in range(heads_per_blk):
            qm = jnp.where(head_in_blk == hq, q2, jnp.zeros_like(q2))
            for b in range(sub):
                r0 = ((j * sub + b) * heads_per_blk + hq) * ATT_BLOCK
                qst_ref[r0:r0 + ATT_BLOCK, :] = qm[b * ATT_BLOCK:(b + 1) * ATT_BLOCK, :]
        ks_ref[0:ATT_BLOCK, dst] = jnp.concatenate([kp_ref[:, lo], kp_ref[:, hi]],
                                                   axis=1).astype(BF16)
        ks_ref[ATT_BLOCK:, dst] = jnp.concatenate([kc_ref[:, lo], kc_ref[:, hi]],
                                                  axis=1).astype(BF16)
    for hp in range(HEADS // 2):
        src = slice(hp * LANES, (hp + 1) * LANES)
        vs_ref[0:ATT_BLOCK, 2 * hp * LANES:(2 * hp + 1) * LANES] = vp_ref[:, src].astype(BF16)
        vs_ref[ATT_BLOCK:, 2 * hp * LANES:(2 * hp + 1) * LANES] = vc_ref[:, src].astype(BF16)
        vs_ref[:, (2 * hp + 1) * LANES:(2 * hp + 2) * LANES] = jnp.ones(
            (att_rows + ATT_BLOCK, LANES), BF16)

    qi = lax.broadcasted_iota(jnp.int32, (ATT_BLOCK, 2 * ATT_BLOCK), 0)
    ki = lax.broadcasted_iota(jnp.int32, (ATT_BLOCK, 2 * ATT_BLOCK), 1)
    rel = qi + ATT_BLOCK - ki
    band = jnp.logical_and(rel >= 0, rel <= ATT_BLOCK)
    bias_ref[0] = jnp.where(band, 0.0, NEG)
    bias_ref[1] = jnp.where(jnp.logical_and(band, ki >= ATT_BLOCK), 0.0, NEG)
    lane = lax.broadcasted_iota(jnp.int32, (ATT_BLOCK, LANES), 1)
    low_head = lane < HEAD_DIM
    blk_rows = heads_per_blk * ATT_BLOCK

    first = (step == 0).astype(jnp.int32)
    hb = ATT_BLOCK // 2
    lane_hb = lax.broadcasted_iota(jnp.int32, (hb, LANES), 1)

    def scores(b):
        for j in range(n_blk):
            base = (j * sub + b) * blk_rows
            k2 = ks_ref[b * ATT_BLOCK:(b + 2) * ATT_BLOCK, 2 * j * LANES:(2 * j + 2) * LANES]
            s_ref[b, j * blk_rows:(j + 1) * blk_rows, :] = lax.dot_general(
                qst_ref[base:base + blk_rows, :], k2, (((1,), (1,)), ((), ())),
                preferred_element_type=F32)

    def softmax(b):
        which = first if b == 0 else 0
        for rh in range(2):
            m_tile = jnp.zeros((hb, LANES), F32)
            for h in range(HEADS):
                rows = slice(h * ATT_BLOCK + rh * hb, h * ATT_BLOCK + (rh + 1) * hb)
                s = s_ref[b, rows, :] + bias_ref[which, rh * hb:(rh + 1) * hb, :]
                m = jnp.max(s, axis=-1, keepdims=True)
                e_ref[b, rows, :] = jnp.exp2((s - m).astype(BF16))
                m_tile = jnp.where(lane_hb == h, m, m_tile)
            mx_ref[b, rh * hb:(rh + 1) * hb, :] = m_tile

    def weighted_values(b):
        log_l = jnp.zeros((ATT_BLOCK, LANES), F32)
        for hp in range(HEADS // 2):
            v_ones = vs_ref[b * ATT_BLOCK:(b + 2) * ATT_BLOCK, 2 * hp * LANES:(2 * hp + 2) * LANES]
            pv = jnp.dot(e_ref[b, 2 * hp * ATT_BLOCK:(2 * hp + 2) * ATT_BLOCK, :], v_ones,
                         preferred_element_type=F32)
            l_lo, l_hi = pv[0:ATT_BLOCK, LANES:], pv[ATT_BLOCK:, LANES:]
            o_ref[b * ATT_BLOCK:(b + 1) * ATT_BLOCK, hp * LANES:(hp + 1) * LANES] = jnp.where(
                low_head, pv[0:ATT_BLOCK, 0:LANES] * (1.0 / l_lo),
                pv[ATT_BLOCK:, 0:LANES] * (1.0 / l_hi))
            log_l = jnp.where(lane == 2 * hp, jnp.log(l_lo),
                              jnp.where(lane == 2 * hp + 1, jnp.log(l_hi), log_l))
        lse_ref[b * ATT_BLOCK:(b + 1) * ATT_BLOCK, :] = mx_ref[b] * ln2 + log_l

    for t in range(sub + 2):
        if t < sub:
            scores(t)
        if 0 <= t - 1 < sub:
            softmax(t - 1)
        if 0 <= t - 2 < sub:
            weighted_values(t - 2)


def _attn_prompt(qkv, dil):
    _, n_seq, _, m_rows, _ = qkv.shape
    att_rows = min(ATT_ROWS, m_rows)
    steps = m_rows // att_rows
    sub = att_rows // ATT_BLOCK

    def cur(slab):
        return pl.BlockSpec((None, None, None, att_rows, SLAB), lambda n, r, b: (slab, n, r, b, 0))

    def prev(slab):
        return pl.BlockSpec((None, None, None, ATT_BLOCK, SLAB),
                            lambda n, r, b: (slab, n, r, jnp.maximum(b * sub - 1, 0), 0))

    return pl.pallas_call(
        _attn_prompt_kernel,
        grid=(n_seq, dil, steps),
        in_specs=[cur(0), prev(1), cur(1), prev(2), cur(2)],
        out_specs=[
            pl.BlockSpec((None, None, att_rows, SLAB), lambda n, r, b: (n, r, b, 0)),
            pl.BlockSpec((None, None, att_rows, LANES), lambda n, r, b: (n, r, b, 0)),
        ],
        out_shape=[
            jax.ShapeDtypeStruct((n_seq, dil, m_rows, SLAB), F32),
            jax.ShapeDtypeStruct((n_seq, dil, m_rows, LANES), F32),
        ],
        scratch_shapes=[
            pltpu.VMEM((HEADS * att_rows, 2 * LANES), BF16),
            pltpu.VMEM((att_rows + ATT_BLOCK, SLAB), BF16),
            pltpu.VMEM((att_rows + ATT_BLOCK, 2 * SLAB), BF16),
            pltpu.VMEM((2, ATT_BLOCK, 2 * ATT_BLOCK), F32),
            pltpu.VMEM((sub, HEADS * ATT_BLOCK, 2 * ATT_BLOCK), F32),
            pltpu.VMEM((sub, HEADS * ATT_BLOCK, 2 * ATT_BLOCK), BF16),
            pltpu.VMEM((sub, ATT_BLOCK, LANES), F32),
        ],
        compiler_params=_params("arbitrary", "arbitrary", "arbitrary"),
        name=f"attn_prompt_d{dil}",
    )(qkv, qkv, qkv, qkv, qkv)


def _kv_tail_kernel(k_ref, v_ref, o_ref, tok_ref):
    dil, rows, _ = k_ref.shape
    keep = dil * rows
    nl = SLAB // LANES
    half_rot = HEAD_DIM // 2
    heads_per_chunk = LANES // half_rot
    for kv, src in enumerate((k_ref, v_ref)):
        for lc in range(nl):
            for r in range(dil):
                tok_ref[lc, pl.ds(r, rows, stride=dil), :] = src[r, :, lc * LANES:(lc + 1) * LANES]
            for pc in range(keep // LANES):
                pcols = slice(pc * LANES, (pc + 1) * LANES)
                t = tok_ref[lc, pcols, :].T
                if kv == 0:
                    half, hblk = divmod(lc, nl // 2)
                    for hq in range(heads_per_chunk):
                        r0 = (hblk * heads_per_chunk + hq) * HEAD_DIM + half * half_rot
                        o_ref[0, r0:r0 + half_rot, pcols] = t[hq * half_rot:(hq + 1) * half_rot, :]
                else:
                    o_ref[1, lc * LANES:(lc + 1) * LANES, pcols] = t


def _kv_tail(grp, keep):
    _, n_seq, dil, m_rows, _ = grp.shape
    rows = keep // dil
    last = m_rows // rows - 1

    def spec(slab):
        return pl.BlockSpec((None, None, dil, rows, SLAB), lambda n: (slab, n, 0, last, 0))

    return pl.pallas_call(
        _kv_tail_kernel,
        grid=(n_seq,),
        in_specs=[spec(1), spec(2)],
        out_specs=pl.BlockSpec((None, 2, SLAB, keep), lambda n: (n, 0, 0, 0)),
        out_shape=jax.ShapeDtypeStruct((n_seq, 2, SLAB, keep), F32),
        scratch_shapes=[pltpu.VMEM((SLAB // LANES, keep, LANES), F32)],
        compiler_params=_params("arbitrary"),
        name=f"kv_tail_d{dil}",
    )(grp, grp)


def _pool_mix(window_terms, a_cols, cnt, pw_ref, scale_ref, gi):
    acc = a_cols
    for term in window_terms:
        acc = acc + term
    pooled = acc / cnt - a_cols
    cols = slice(gi * POOL_GROUP, (gi + 1) * POOL_GROUP)
    mixed = jnp.dot(pooled.astype(BF16), pw_ref[gi], preferred_element_type=F32)
    return mixed * scale_ref[:, cols]


def _even_out_prompt_kernel(a_ref, halo_ref, o0_ref, o1_ref, o2_ref, l0_ref, l1_ref, l2_ref,
                            pw_ref, ps_ref, wo_ref, x_ref, out_ref, ext_ref, ya_ref, yb_ref,
                            oi_ref, li_ref, *, tiles_per_seq):
    tm = a_ref.shape[0]
    it = pl.program_id(0) % tiles_per_seq
    halo = halo_ref[...]
    ext_ref[0:POOL_HALO, :] = jnp.where(it == 0, jnp.zeros_like(halo), halo)
    ext_ref[POOL_HALO:, :] = a_ref[...]
    pos = it * tm + lax.broadcasted_iota(jnp.int32, (tm, 1), 0)
    for gi, w in enumerate(POOL_WINDOWS):
        cols = slice(gi * POOL_GROUP, (gi + 1) * POOL_GROUP)
        terms = [ext_ref[POOL_HALO - k:POOL_HALO - k + tm, cols] for k in range(1, w)]
        cnt = jnp.minimum(w, pos + 1).astype(F32)
        ya_ref[:, cols] = _pool_mix(terms, a_ref[:, cols], cnt, pw_ref, ps_ref, gi).astype(BF16)

    for g, (o_ref, l_ref) in enumerate(((o0_ref, l0_ref), (o1_ref, l1_ref), (o2_ref, l2_ref))):
        dil = o_ref.shape[0]
        for r in range(dil):
            rows = pl.ds(r, tm // dil, stride=dil)
            li_ref[g, rows, :] = l_ref[r]
            for lc in range(SLAB // LANES):
                oi_ref[g, lc, rows, :] = o_ref[r, :, lc * LANES:(lc + 1) * LANES]

    l0, l1, l2 = li_ref[0], li_ref[1], li_ref[2]
    mx = jnp.maximum(jnp.maximum(l0, l1), l2)
    e0, e1, e2 = jnp.exp(l0 - mx), jnp.exp(l1 - mx), jnp.exp(l2 - mx)
    inv = 1.0 / (e0 + e1 + e2)
    w0, w1 = e0 * inv, e1 * inv
    for h in range(HEADS):
        lc, lo = divmod(h * HEAD_DIM, LANES)
        sl = slice(lo, lo + HEAD_DIM)
        o2 = oi_ref[2, lc, :, sl]
        yb = (o2 + w0[:, h:h + 1] * (oi_ref[0, lc, :, sl] - o2)
              + w1[:, h:h + 1] * (oi_ref[1, lc, :, sl] - o2))
        yb_ref[:, h * HEAD_DIM:(h + 1) * HEAD_DIM] = yb.astype(BF16)

    y = jnp.dot(ya_ref[...], wo_ref[0:SLAB, :], preferred_element_type=F32)
    y = y + jnp.dot(yb_ref[...], wo_ref[SLAB:, :], preferred_element_type=F32)
    out_ref[...] = x_ref[...] + y


def _even_out_prompt(za, attn, pool_w, pool_scale, w_out, x, seq, tm):
    T = x.shape[0]
    tps = seq // tm
    hb = tm // POOL_HALO

    def residue_spec(dil, width):
        return pl.BlockSpec((None, dil, tm // dil, width), lambda i: (i // tps, 0, i % tps, 0))

    dils = [o.shape[1] for o, _ in attn]
    return pl.pallas_call(
        functools.partial(_even_out_prompt_kernel, tiles_per_seq=tps),
        grid=(T // tm,),
        in_specs=[
            pl.BlockSpec((tm, SLAB), lambda i: (i, 0)),
            pl.BlockSpec((POOL_HALO, SLAB), lambda i: (jnp.maximum(i * hb - 1, 0), 0)),
        ]
        + [residue_spec(d, SLAB) for d in dils] + [residue_spec(d, LANES) for d in dils]
        + [
            pl.BlockSpec((len(POOL_WINDOWS), POOL_GROUP, POOL_GROUP), lambda i: (0, 0, 0)),
            pl.BlockSpec((1, SLAB), lambda i: (0, 0)),
            pl.BlockSpec((2 * SLAB, D_MODEL), lambda i: (0, 0)),
            pl.BlockSpec((tm, D_MODEL), lambda i: (i, 0)),
        ],
        out_specs=pl.BlockSpec((tm, D_MODEL), lambda i: (i, 0)),
        out_shape=jax.ShapeDtypeStruct((T, D_MODEL), F32),
        scratch_shapes=[
            pltpu.VMEM((tm + POOL_HALO, SLAB), F32),
            pltpu.VMEM((tm, SLAB), BF16),
            pltpu.VMEM((tm, SLAB), BF16),
            pltpu.VMEM((len(dils), SLAB // LANES, tm, LANES), F32),
            pltpu.VMEM((len(dils), tm, LANES), F32),
        ],
        compiler_params=_params("arbitrary"),
        name="even_out_prompt",
    )(za, za, *[o for o, _ in attn], *[l for _, l in attn], pool_w, pool_scale, w_out, x)


def _odd_mix(u_ref, vn_ref, go_ref, hd_ref, hdh_ref, ws_ref, bs_ref, cw_ref, wo_ref, x_ref,
             ext_ref, yc_ref, yd_ref, tiles_per_seq):
    tm = u_ref.shape[0]
    it = pl.program_id(0) % tiles_per_seq

    ti = lax.broadcasted_iota(jnp.int32, (CHUNK, CHUNK), 0)
    si = lax.broadcasted_iota(jnp.int32, (CHUNK, CHUNK), 1)
    for g in range(C_GROUPS):
        cols = slice(g * LANES, (g + 1) * LANES)
        wm = jnp.where(si <= ti, ws_ref[g], 0.0).astype(BF16)
        for c in range(tm // CHUNK):
            rows = slice(c * CHUNK, (c + 1) * CHUNK)
            sp = jnp.dot(wm, vn_ref[rows, cols].astype(BF16), preferred_element_type=F32)
            sp = sp + bs_ref[:, cols]
            yc_ref[rows, cols] = (u_ref[rows, cols] * sp).astype(BF16)

    hd = hd_ref[...]
    halo = hdh_ref[...]
    ext_ref[0:CONV_HALO, :] = jnp.where(it == 0, jnp.zeros_like(halo), halo)
    ext_ref[CONV_HALO:, :] = hd
    conv = cw_ref[CONV_W - 1:CONV_W, :] * hd
    for j in range(CONV_W - 1):
        off = CONV_HALO - (CONV_W - 1) + j
        conv = conv + cw_ref[j:j + 1, :] * ext_ref[off:off + tm, :]
    yd_ref[...] = (go_ref[...] * conv).astype(BF16)

    y = jnp.dot(yc_ref[...], wo_ref[0:SLAB, :], preferred_element_type=F32)
    y = y + jnp.dot(yd_ref[...], wo_ref[SLAB:, :], preferred_element_type=F32)
    return x_ref[...] + y


def _odd_out_ffn_kernel(u_ref, vn_ref, go_ref, hd_ref, hdh_ref, ws_ref, bs_ref, cw_ref, wo_ref,
                        x_ref, p_ref, gf_ref, w1_ref, w2_ref, gp_ref, wg_ref, wp_ref, gl_ref,
                        out_ref, ext_ref, yc_ref, yd_ref, *, tiles_per_seq, final_norm, tf):
    r = _odd_mix(u_ref, vn_ref, go_ref, hd_ref, hdh_ref, ws_ref, bs_ref, cw_ref, wo_ref, x_ref,
                 ext_ref, yc_ref, yd_ref, tiles_per_seq)
    out_ref[...] = _mlp_ple(r, p_ref, gf_ref, w1_ref, w2_ref, gp_ref, wg_ref, wp_ref, gl_ref,
                            final_norm, tf)


def _odd_out_ffn(layer, z, ws, bs_rows, conv_w, w_out, x, seq, p, g_ffn, w1, w2, g_ple, wg, wp,
                 g_last, final_norm, tm, tf):
    T = x.shape[0]
    tiles_per_seq = seq // tm
    hb = tm // CONV_HALO

    def slab(s):
        return pl.BlockSpec((None, tm, SLAB), lambda i: (s, i, 0))

    return pl.pallas_call(
        functools.partial(_odd_out_ffn_kernel, tiles_per_seq=tiles_per_seq,
                          final_norm=final_norm, tf=tf),
        grid=(T // tm,),
        in_specs=[
            slab(0), slab(1), slab(2), slab(3),
            pl.BlockSpec((None, CONV_HALO, SLAB), lambda i: (3, jnp.maximum(i * hb - 1, 0), 0)),
            _resident((C_GROUPS, CHUNK, CHUNK)),
            _resident((CHUNK, SLAB)),
            _resident((CONV_W, SLAB)),
            _resident((2 * SLAB, D_MODEL)),
            pl.BlockSpec((tm, D_MODEL), lambda i: (i, 0)),
            pl.BlockSpec((None, tm, PLE_DIM), lambda i: (layer, i, 0)),
            _layer_resident(layer, (1, D_MODEL)),
            _layer_resident(layer, (D_MODEL, D_FF)),
            _layer_resident(layer, (D_FF, D_MODEL)),
            _layer_resident(layer, (1, D_MODEL)),
            _layer_resident(layer, (D_MODEL, D_MODEL)),
            _layer_resident(layer, (PLE_DIM, D_MODEL)),
            _resident((1, D_MODEL)),
        ],
        out_specs=pl.BlockSpec((tm, D_MODEL), lambda i: (i, 0)),
        out_shape=jax.ShapeDtypeStruct((T, D_MODEL), F32),
        scratch_shapes=[
            pltpu.VMEM((tm + CONV_HALO, SLAB), F32),
            pltpu.VMEM((tm, SLAB), BF16),
            pltpu.VMEM((tm, SLAB), BF16),
        ],
        compiler_params=_params("arbitrary"),
        name="odd_out_ffn",
    )(z, z, z, z, z, ws, bs_rows, conv_w, w_out, x, p, g_ffn, w1, w2, g_ple, wg, wp, g_last)


def _mlp_ple(x, p_ref, gf_ref, w1_ref, w2_ref, gp_ref, wg_ref, wp_ref, gl_ref, final_norm, tf):
    hn = _rms(x, gf_ref[...]).astype(BF16)
    acc = None
    for c in range(D_FF // tf):
        h1 = jnp.dot(hn, w1_ref[:, c * tf:(c + 1) * tf], preferred_element_type=F32)
        h1 = jnp.square(jnp.maximum(h1, 0.0)).astype(BF16)
        part = jnp.dot(h1, w2_ref[c * tf:(c + 1) * tf, :], preferred_element_type=F32)
        acc = part if acc is None else acc + part
    r = x + acc
    hp = _rms(r, gp_ref[...]).astype(BF16)
    gate = jax.nn.sigmoid(jnp.dot(hp, wg_ref[...], preferred_element_type=F32))
    proj = jnp.dot(p_ref[...].astype(BF16), wp_ref[...], preferred_element_type=F32)
    r = r + gate * proj
    if final_norm:
        r = _rms(r, gl_ref[...])
    return r


def _ffn_ple_kernel(x_ref, p_ref, gf_ref, w1_ref, w2_ref, gp_ref, wg_ref, wp_ref, gl_ref,
                    out_ref, *, final_norm, tf):
    out_ref[...] = _mlp_ple(x_ref[...], p_ref, gf_ref, w1_ref, w2_ref, gp_ref, wg_ref, wp_ref,
                            gl_ref, final_norm, tf)


def _layer_resident(layer, shape):
    return pl.BlockSpec((None,) + shape, lambda *_: (layer,) + (0,) * len(shape),
                        pipeline_mode=pl.Buffered(1))


def _ffn_ple(layer, x, p, g_ffn, w1, w2, g_ple, wg, wp, g_last, final_norm, tm, tf):
    T = x.shape[0]
    return pl.pallas_call(
        functools.partial(_ffn_ple_kernel, final_norm=final_norm, tf=tf),
        grid=(T // tm,),
        in_specs=[
            pl.BlockSpec((tm, D_MODEL), lambda i: (i, 0)),
            pl.BlockSpec((None, tm, PLE_DIM), lambda i: (layer, i, 0)),
            _layer_resident(layer, (1, D_MODEL)),
            _layer_resident(layer, (D_MODEL, D_FF)),
            _layer_resident(layer, (D_FF, D_MODEL)),
            _layer_resident(layer, (1, D_MODEL)),
            _layer_resident(layer, (D_MODEL, D_MODEL)),
            _layer_resident(layer, (PLE_DIM, D_MODEL)),
            _resident((1, D_MODEL)),
        ],
        out_specs=pl.BlockSpec((tm, D_MODEL), lambda i: (i, 0)),
        out_shape=jax.ShapeDtypeStruct((T, D_MODEL), F32),
        compiler_params=_params("arbitrary"),
        name="ffn_ple",
    )(x, p, g_ffn, w1, w2, g_ple, wg, wp, g_last)


def _ffn_ple_hosting_kernel(x_ref, p_ref, gf_ref, w1_ref, w2_ref, gp_ref, wg_ref, wp_ref, gl_ref,
                            q0_ref, q1_ref, q2_ref, c0_ref, c1_ref, c2_ref, out_ref, pacc_ref,
                            pml_ref, hn_ref, acc_ref, *, final_norm, tf, splits):
    j = pl.program_id(1)
    share = D_FF // splits

    def mlp_share(k):
        acc = None
        for c in range(share // tf):
            cols = slice(k * share + c * tf, k * share + (c + 1) * tf)
            h1 = jnp.dot(hn_ref[...], w1_ref[:, cols], preferred_element_type=F32)
            h1 = jnp.square(jnp.maximum(h1, 0.0)).astype(BF16)
            part = jnp.dot(h1, w2_ref[cols, :], preferred_element_type=F32)
            acc = part if acc is None else acc + part
        return acc

    @pl.when(j == 0)
    def _():
        hn_ref[...] = _rms(x_ref[...], gf_ref[...]).astype(BF16)
        acc_ref[...] = mlp_share(0)

    for k in range(1, splits - 1):
        @pl.when(j == k)
        def _(k=k):
            acc_ref[...] += mlp_share(k)

    @pl.when(j == splits - 1)
    def _():
        r = x_ref[...] + (acc_ref[...] + mlp_share(splits - 1))
        hp = _rms(r, gp_ref[...]).astype(BF16)
        gate = jax.nn.sigmoid(jnp.dot(hp, wg_ref[...], preferred_element_type=F32))
        proj = jnp.dot(p_ref[...].astype(BF16), wp_ref[...], preferred_element_type=F32)
        r = r + gate * proj
        if final_norm:
            r = _rms(r, gl_ref[...])
        out_ref[...] = r

    row = (splits * pl.program_id(0) + j) % Q_ROWS
    q_rows = [[q_ref[t, pl.ds(row, 1), :] for t in range(q_ref.shape[0])]
              for q_ref in (q0_ref, q1_ref, q2_ref)]
    _partial_cached_attention(q_rows, (c0_ref, c1_ref, c2_ref), pacc_ref, pml_ref)


def _ffn_ple_hosting(layer, x, p, g_ffn, w1, w2, g_ple, wg, wp, g_last, final_norm, tf,
                     qkv, caches, splits):
    T = x.shape[0]
    _, n_tok, n_seq, _ = qkv[0].shape
    tm = splits * T // n_seq
    seq_of = lambda i, j: splits * i + j

    def cache_spec(c):
        return pl.BlockSpec((None, 2, SLAB, c.shape[-1]), lambda i, j: (seq_of(i, j), 0, 0, 0))

    return pl.pallas_call(
        functools.partial(_ffn_ple_hosting_kernel, final_norm=final_norm, tf=tf, splits=splits),
        grid=(T // tm, splits),
        in_specs=[
            pl.BlockSpec((tm, D_MODEL), lambda i, j: (i, 0)),
            pl.BlockSpec((None, tm, PLE_DIM), lambda i, j: (layer, i, 0)),
            _layer_resident(layer, (1, D_MODEL)),
            _layer_resident(layer, (D_MODEL, D_FF)),
            _layer_resident(layer, (D_FF, D_MODEL)),
            _layer_resident(layer, (1, D_MODEL)),
            _layer_resident(layer, (D_MODEL, D_MODEL)),
            _layer_resident(layer, (PLE_DIM, D_MODEL)),
            _resident((1, D_MODEL)),
        ] + [pl.BlockSpec((None, n_tok, Q_ROWS, SLAB),
                          lambda i, j: (0, 0, seq_of(i, j) // Q_ROWS, 0)) for _ in qkv]
        + [cache_spec(c) for c in caches],
        out_specs=[
            pl.BlockSpec((tm, D_MODEL), lambda i, j: (i, 0)),
            pl.BlockSpec((None, n_tok * HEADS, SLAB), lambda i, j: (seq_of(i, j), 0, 0)),
            pl.BlockSpec((None, n_tok * HEADS, LANES), lambda i, j: (seq_of(i, j), 0, 0)),
        ],
        out_shape=[
            jax.ShapeDtypeStruct((T, D_MODEL), F32),
            jax.ShapeDtypeStruct((n_seq, n_tok * HEADS, SLAB), F32),
            jax.ShapeDtypeStruct((n_seq, n_tok * HEADS, LANES), F32),
        ],
        scratch_shapes=[pltpu.VMEM((tm, D_MODEL), BF16), pltpu.VMEM((tm, D_MODEL), F32)],
        compiler_params=pltpu.CompilerParams(dimension_semantics=("arbitrary", "arbitrary"),
                                             vmem_limit_bytes=HOSTING_VMEM_LIMIT),
        name="ffn_ple_hosting",
    )(x, p, g_ffn, w1, w2, g_ple, wg, wp, g_last, *qkv, *caches)


def _own_lanes():
    sub = lax.broadcasted_iota(jnp.int32, (HEADS, SLAB), 0)
    lane_head = lax.broadcasted_iota(jnp.int32, (HEADS, SLAB), 1) // HEAD_DIM
    return sub == lane_head


def _block_diag_queries(q_rows):
    own = _own_lanes()
    scale = HEAD_DIM ** -0.5
    return jnp.concatenate(
        [jnp.where(own, jnp.broadcast_to(q * scale, (HEADS, SLAB)), 0.0) for q in q_rows], axis=0)


def _cached_scores(qbd, kt, dil):
    s = jnp.dot(qbd.astype(BF16), kt.astype(BF16), preferred_element_type=F32)
    pos = lax.broadcasted_iota(jnp.int32, s.shape, 1)
    row_tok = lax.broadcasted_iota(jnp.int32, (s.shape[0], 1), 0) // HEADS
    valid = (pos >= row_tok) if dil == 1 else (jnp.bitwise_and(pos, dil - 1) == row_tok)
    return jnp.where(valid, s, NEG)


def _weighted_cached_values(e, vt):
    return lax.dot_general(e.astype(BF16), vt.astype(BF16), (((1,), (1,)), ((), ())),
                           preferred_element_type=F32)


def _partial_cached_attention(q_rows, cache_refs, acc_ref, ml_ref):
    scores = []
    m = None
    for g, (c_ref, (_, dil)) in enumerate(zip(cache_refs, DIL_CFG)):
        s = _cached_scores(_block_diag_queries(q_rows[g]), c_ref[0], dil)
        row_max = jnp.max(s, axis=1, keepdims=True)
        m = row_max if m is None else jnp.maximum(m, row_max)
        scores.append(s)
    acc, l = None, None
    for s, c_ref in zip(scores, cache_refs):
        e = jnp.exp(s - m)
        part = _weighted_cached_values(e, c_ref[1])
        row_sum = jnp.sum(e, axis=1, keepdims=True)
        acc = part if acc is None else acc + part
        l = row_sum if l is None else l + row_sum
    acc_ref[...] = acc
    lane = lax.broadcasted_iota(jnp.int32, ml_ref.shape, 1)
    ml_ref[...] = jnp.where(lane == 0, m, jnp.where(lane == 1, l, 0.0))


def _attn_sample_kernel(g0_ref, g1_ref, g2_ref, pacc_ref, pml_ref, o_ref, *, n_seq):
    groups = (g0_ref, g1_ref, g2_ref)
    n_tok = g0_ref.shape[1]
    n_rows = n_tok * HEADS
    own = _own_lanes()
    row_tok = lax.broadcasted_iota(jnp.int32, (n_rows, 1), 0) // HEADS

    def one_sequence(n, carry):
        m_part = pml_ref[n, :, 0:1]
        m = m_part
        s_new = []
        for g, (g_ref, (_, dil)) in enumerate(zip(groups, DIL_CFG)):
            qbd = _block_diag_queries([g_ref[0, t, pl.ds(n, 1), :] for t in range(n_tok)])
            for tp in range(n_tok):
                sn = jnp.sum(qbd * g_ref[1, tp, pl.ds(n, 1), :], axis=1, keepdims=True)
                ok = (row_tok >= tp) if dil == 1 else (row_tok == tp)
                sn = jnp.where(ok, sn, NEG)
                m = jnp.maximum(m, sn)
                s_new.append((g, tp, sn))
        w_part = jnp.exp(m_part - m)
        acc = w_part * pacc_ref[n]
        den = w_part * pml_ref[n, :, 1:2]
        for g, tp, sn in s_new:
            e = jnp.exp(sn - m)
            den = den + e
            acc = acc + e * groups[g][2, tp, pl.ds(n, 1), :]
        res = acc * (1.0 / den)
        for t in range(n_tok):
            rows = res[t * HEADS:(t + 1) * HEADS, :]
            o_ref[n, t:t + 1, :] = jnp.sum(jnp.where(own, rows, 0.0), axis=0, keepdims=True)
        return carry

    lax.fori_loop(0, n_seq, one_sequence, 0, unroll=4)


def _attn_sample(qkv, pacc, pml, seqs_per_step=16):
    _, n_tok, n_seq, _ = qkv[0].shape
    nb = seqs_per_step

    def spec(*tail):
        return pl.BlockSpec((nb,) + tail, lambda i: (i,) + (0,) * len(tail))

    return pl.pallas_call(
        functools.partial(_attn_sample_kernel, n_seq=nb),
        grid=(n_seq // nb,),
        in_specs=[pl.BlockSpec((3, n_tok, nb, SLAB), lambda i: (0, 0, i, 0)) for _ in qkv]
        + [spec(n_tok * HEADS, SLAB), spec(n_tok * HEADS, LANES)],
        out_specs=spec(n_tok, SLAB),
        out_shape=jax.ShapeDtypeStruct((n_seq, n_tok, SLAB), F32),
        compiler_params=_params("arbitrary"),
        name="attn_sample",
    )(*qkv, pacc, pml)


def _even_out_sample_kernel(a_ref, ctx_ref, yb_ref, pw_ref, ps_ref, wo_ref, x_ref, out_ref,
                            *, n_seq, n_tok):
    def ext_row(e, cols):
        if e >= POOL_STATE:
            t = e - POOL_STATE
            return a_ref[t * n_seq:(t + 1) * n_seq, cols]
        return ctx_ref[e, :, cols]

    for t in range(n_tok):
        rows = slice(t * n_seq, (t + 1) * n_seq)
        ya = []
        for gi, w in enumerate(POOL_WINDOWS):
            cols = slice(gi * POOL_GROUP, (gi + 1) * POOL_GROUP)
            terms = [ext_row(POOL_STATE + t - k, cols) for k in range(1, w)]
            ya.append(_pool_mix(terms, a_ref[rows, cols], float(w), pw_ref, ps_ref, gi))
        y = jnp.zeros((n_seq, D_MODEL), F32)
        for gi in range(len(POOL_WINDOWS)):
            y = y + jnp.dot(ya[gi].astype(BF16), wo_ref[gi * POOL_GROUP:(gi + 1) * POOL_GROUP, :],
                            preferred_element_type=F32)
        yb = yb_ref[:, t * SLAB:(t + 1) * SLAB].astype(BF16)
        y = y + jnp.dot(yb, wo_ref[SLAB:, :], preferred_element_type=F32)
        out_ref[rows, :] = x_ref[rows, :] + y


def _even_out_sample(z, ctx, yb, pool_w, pool_scale, w_out, x, n_seq, n_tok):
    T = x.shape[0]
    full = lambda *shape: pl.BlockSpec(shape, lambda i: (0,) * len(shape))
    return pl.pallas_call(
        functools.partial(_even_out_sample_kernel, n_seq=n_seq, n_tok=n_tok),
        grid=(1,),
        in_specs=[
            full(T, SLAB),
            full(POOL_STATE, n_seq, SLAB),
            full(n_seq, n_tok * SLAB),
            full(len(POOL_WINDOWS), POOL_GROUP, POOL_GROUP),
            full(1, SLAB),
            full(2 * SLAB, D_MODEL),
            full(T, D_MODEL),
        ],
        out_specs=full(T, D_MODEL),
        out_shape=jax.ShapeDtypeStruct((T, D_MODEL), F32),
        compiler_params=_params("arbitrary"),
        name="even_out_sample",
    )(z, ctx, yb, pool_w, pool_scale, w_out, x)


def _odd_out_sample_kernel(z_ref, ctx_ref, coef_ref, bias_ref, cw_ref, wo_ref, x_ref,
                           out_ref, *, n_seq, n_tok, mix_terms):
    def ext_row(e):
        if e >= CONV_W - 1:
            t = e - (CONV_W - 1)
            return z_ref[3, t * n_seq:(t + 1) * n_seq, :]
        return ctx_ref[:, e * SLAB:(e + 1) * SLAB]

    for t in range(n_tok):
        rows = slice(t * n_seq, (t + 1) * n_seq)
        sp = jnp.zeros((n_seq, SLAB), F32) + bias_ref[t:t + 1, :]
        for s in mix_terms[t]:
            r = t * n_tok + s
            sp = sp + coef_ref[r:r + 1, :] * z_ref[1, s * n_seq:(s + 1) * n_seq, :]
        yc = z_ref[0, rows, :] * sp
        conv = jnp.zeros((n_seq, SLAB), F32)
        for j in range(CONV_W):
            conv = conv + cw_ref[j:j + 1, :] * ext_row(t + j)
        yd = z_ref[2, rows, :] * conv
        y = jnp.dot(yc.astype(BF16), wo_ref[0:SLAB, :], preferred_element_type=F32)
        y = y + jnp.dot(yd.astype(BF16), wo_ref[SLAB:, :], preferred_element_type=F32)
        out_ref[rows, :] = x_ref[rows, :] + y


def _odd_out_sample(z, ctx, coef, bias, conv_w, w_out, x, n_seq, n_tok, mix_terms):
    T = x.shape[0]
    full = lambda *shape: pl.BlockSpec(shape, lambda i: (0,) * len(shape))
    return pl.pallas_call(
        functools.partial(_odd_out_sample_kernel, n_seq=n_seq, n_tok=n_tok, mix_terms=mix_terms),
        grid=(1,),
        in_specs=[
            full(4, T, SLAB),
            full(n_seq, (CONV_W - 1) * SLAB),
            full(n_tok * n_tok, SLAB),
            full(n_tok, SLAB),
            full(CONV_W, SLAB),
            full(2 * SLAB, D_MODEL),
            full(T, D_MODEL),
        ],
        out_specs=full(T, D_MODEL),
        out_shape=jax.ShapeDtypeStruct((T, D_MODEL), F32),
        compiler_params=_params("arbitrary"),
        name="odd_out_sample",
    )(z, ctx, coef, bias, conv_w, w_out, x)


def _rope_tables(pos, split):
    half = HEAD_DIM // 2
    inv = jnp.power(jnp.float32(ROPE_THETA), -jnp.arange(half, dtype=F32) / half)
    ang = pos.astype(F32)[:, None] * inv[None, :]
    cos = jnp.cos(ang)
    sin = jnp.sin(ang)
    if split:
        return jnp.tile(cos, (1, LANES // half)), jnp.tile(sin, (1, LANES // half))
    cos_t = jnp.tile(jnp.concatenate([cos, cos], axis=-1), (1, LANES // HEAD_DIM))
    sin_t = jnp.tile(jnp.concatenate([-sin, sin], axis=-1), (1, LANES // HEAD_DIM))
    return cos_t, sin_t


def _split_qk_columns_kernel(w_ref, o_ref):
    half = HEAD_DIM // 2
    n_slabs = w_ref.shape[1] // SLAB
    for s in range(n_slabs):
        cols = slice(s * SLAB, (s + 1) * SLAB)
        x = w_ref[:, cols]
        if s >= 1 and (s - 1) % 3 < 2:
            xt = x.T
            xt = jnp.concatenate(
                [xt[h * HEAD_DIM + part * half:h * HEAD_DIM + (part + 1) * half, :]
                 for part in range(2) for h in range(HEADS)], axis=0)
            x = xt.T
        o_ref[:, cols] = x.astype(BF16)


def _split_qk_columns(w, tr=256):
    d, n = w.shape
    return pl.pallas_call(
        _split_qk_columns_kernel,
        grid=(d // tr,),
        in_specs=[pl.BlockSpec((tr, n), lambda i: (i, 0))],
        out_specs=pl.BlockSpec((tr, n), lambda i: (i, 0)),
        out_shape=jax.ShapeDtypeStruct((d, n), BF16),
        compiler_params=_params("arbitrary"),
        name="split_qk_columns",
    )(w)


def kernel(x_prompt, x_sample, cache_kv_w128, cache_kv_w512, cache_kv_w2048, state_pool, state_conv,
           p_prompt, p_sample, ev_w_in, ev_pool_w, ev_pool_scale, ev_w_out, od_w_in, od_ln_g, od_ln_b,
           od_ws, od_bs, od_conv_w, od_w_out, norm_mix, norm_ffn, norm_ple, ffn_w1, ffn_w2,
           ple_w_proj, ple_w_gate, norm_final):
    n_p, seq, _ = x_prompt.shape
    n_s, n_tok, _ = x_sample.shape
    depth = norm_mix.shape[0]
    tp = n_p * seq
    ts = n_s * n_tok

    bf = lambda w: w.astype(BF16)
    row = lambda v: v.reshape(1, -1)
    ev_w_in_b, ev_pool_w_b, ev_w_out_b = bf(ev_w_in), bf(ev_pool_w), bf(ev_w_out)
    od_w_in_b, od_w_out_b = bf(od_w_in), bf(od_w_out)
    w1_b, w2_b, wg_b, wp_b = bf(ffn_w1), bf(ffn_w2), bf(ple_w_gate), bf(ple_w_proj)

    cos_p, sin_p = _rope_tables(jnp.arange(seq), True)
    pos_s = [PAST_LEN + t for t in range(n_tok)]
    cos_s, sin_s = _rope_tables(jnp.repeat(jnp.asarray(pos_s, jnp.int32), n_s), False)

    mix_terms = tuple(
        tuple(s for s in range(n_tok)
              if pos_s[s] // CHUNK == pos_s[t] // CHUNK and pos_s[s] % CHUNK <= pos_s[t] % CHUNK)
        for t in range(n_tok))
    local = [p % CHUNK for p in pos_s]

    rp = x_prompt.reshape(tp, D_MODEL)
    rs = x_sample.transpose(1, 0, 2).reshape(ts, D_MODEL)
    pp = p_prompt.reshape(depth, tp, PLE_DIM)
    ps = p_sample.transpose(0, 2, 1, 3).reshape(depth, ts, PLE_DIM)

    tm_p, tm_mix, tm_ffn, tf = 512, 512, 512, 512
    kv_p = [[] for _ in DIL_CFG]
    kv_s = [[] for _ in DIL_CFG]
    pool_p, pool_s, conv_p, conv_s, cv_s = [], [], [], [], []

    for i in range(depth):
        g_mix = row(norm_mix[i])
        ffn_args = (norm_ffn[:, None, :], w1_b, w2_b, norm_ple[:, None, :], wg_b, wp_b,
                    row(norm_final), i == depth - 1)
        if i % 2 == 0:
            e = i // 2
            pscale = row(ev_pool_scale[e])
            dils = tuple(d for _, d in DIL_CFG)
            w_split = _split_qk_columns(ev_w_in[e])
            za, *groups = _proj_even(rp, g_mix, w_split, cos_p, sin_p, tm_p, n_p, dils, True)
            groups = [grp.reshape(3, n_p, dil, seq // dil, SLAB) for grp, dil in zip(groups, dils)]
            attn = [_attn_prompt(grp, dil) for grp, dil in zip(groups, dils)]
            rp = _even_out_prompt(za, attn, ev_pool_w_b[e], pscale, ev_w_out_b[e], rp, seq, tm_mix)
            for g, ((win, _), grp) in enumerate(zip(DIL_CFG, groups)):
                keep = min(win, seq)
                tail = _kv_tail(grp, keep).reshape(n_p, 2, HEADS, HEAD_DIM, keep)
                kv_p[g].append(tail.transpose(0, 4, 1, 2, 3))
            pool_p.append(za.reshape(n_p, seq, SLAB)[:, seq - POOL_STATE:])
            zas, *sgroups = _proj_even(rs, g_mix, ev_w_in_b[e], cos_s, sin_s, ts, 1, (1, 1, 1),
                                       False)
            sgroups = [g.reshape(3, n_tok, n_s, SLAB) for g in sgroups]
            native = lambda c: c.transpose(0, 2, 3, 4, 1).reshape(n_s, 2, SLAB, c.shape[1])
            caches = [native(c[e]) for c in (cache_kv_w128, cache_kv_w512, cache_kv_w2048)]
            rp, pacc, pml = _ffn_ple_hosting(i, rp, pp, *ffn_args, tf, sgroups, caches,
                                             HOST_SPLITS)
            yb = _attn_sample(sgroups, pacc, pml)
            rs = _even_out_sample(zas, state_pool[e].transpose(1, 0, 2),
                                  yb.reshape(n_s, n_tok * SLAB), ev_pool_w_b[e], pscale,
                                  ev_w_out_b[e], rs, n_s, n_tok)
            for g, grp in enumerate(sgroups):
                kv = grp[1:].transpose(2, 1, 0, 3)
                kv_s[g].append(kv.reshape(n_s, n_tok, 2, HEADS, HEAD_DIM))
            a_n = zas.reshape(n_tok, n_s, SLAB).transpose(1, 0, 2)
            pool_s.append(jnp.concatenate([state_pool[e], a_n], axis=1)[:, -POOL_STATE:])
        else:
            o = i // 2
            ln_g, ln_b = row(od_ln_g[o]), row(od_ln_b[o])
            bs_rows = jnp.repeat(od_bs[o].T, LANES, axis=1)
            z = _proj_odd(rp, g_mix, od_w_in_b[o], ln_g, ln_b, tm_p)
            rp = _odd_out_ffn(i, z, od_ws[o], bs_rows, od_conv_w[o], od_w_out_b[o], rp, seq, pp,
                              *ffn_args, tm_ffn, tf)
            conv_p.append(z.reshape(4, n_p, seq, SLAB)[3, :, seq - (CONV_W - 1):])
            zs = _proj_odd(rs, g_mix, od_w_in_b[o], ln_g, ln_b, ts)
            coef = jnp.stack([jnp.repeat(od_ws[o][:, local[t], local[s]], LANES)
                              for t in range(n_tok) for s in range(n_tok)])
            bias = jnp.stack([bs_rows[local[t]] for t in range(n_tok)])
            rs = _odd_out_sample(zs, state_conv[o].reshape(n_s, (CONV_W - 1) * SLAB), coef, bias,
                                 od_conv_w[o], od_w_out_b[o], rs, n_s, n_tok, mix_terms)
            hd_n = zs[3].reshape(n_tok, n_s, SLAB).transpose(1, 0, 2)
            conv_s.append(jnp.concatenate([state_conv[o], hd_n], axis=1)[:, -(CONV_W - 1):])
            cv_s.append(zs[1].reshape(n_tok, n_s, SLAB).transpose(1, 0, 2))

        rs = _ffn_ple(i, rs, ps, *ffn_args, ts, tf)

    y_prompt = rp.reshape(n_p, seq, D_MODEL)
    y_sample = rs.reshape(n_tok, n_s, D_MODEL).transpose(1, 0, 2)
    st = lambda lst: jnp.stack(lst, axis=0)
    return (y_prompt, y_sample, st(kv_p[0]), st(kv_p[1]), st(kv_p[2]),
            st(kv_s[0]), st(kv_s[1]), st(kv_s[2]),
            st(pool_p), st(pool_s), st(conv_p), st(conv_s), st(cv_s))
```

```python
import functools
import math

import jax
import jax.numpy as jnp
from jax import lax
from jax.experimental import pallas as pl
from jax.experimental.pallas import tpu as pltpu

F32 = jnp.float32
BF16 = jnp.bfloat16

D_MODEL = 1024
D_FF = 4 * D_MODEL
PLE_DIM = 256
EPS = 1e-6
ROPE_THETA = 10000.0
PAST_LEN = 2048

SLAB = 512
POOL_WINDOWS = (2, 4, 8, 16)
POOL_GROUP = 128
POOL_STATE = 15
POOL_HALO = 16
DIL_CFG = ((128, 1), (512, 4), (2048, 16))
HEADS = 8
HEAD_DIM = 64
ATT_BLOCK = 128
ATT_ROWS = 512
CHUNK = 128
C_GROUPS = 4
CONV_W = 3
CONV_HALO = 8
LANES = 128
NEG = -1e30

VMEM_LIMIT = 52 * 1024 * 1024
HOSTING_VMEM_LIMIT = 60 * 1024 * 1024
HOST_SPLITS = 4
Q_ROWS = 8


def _params(*sem):
    return pltpu.CompilerParams(dimension_semantics=sem, vmem_limit_bytes=VMEM_LIMIT)


def _rms(x, g):
    ms = jnp.mean(x * x, axis=-1, keepdims=True)
    return x * lax.rsqrt(ms + EPS) * g


def _gelu(x):
    c = math.sqrt(2.0 / math.pi)
    return 0.5 * x * (1.0 + jnp.tanh(c * (x + 0.044715 * (x * x * x))))


def _proj_even_kernel(x_ref, g_ref, w_ref, cos_ref, sin_ref, za_ref, g0_ref, g1_ref, g2_ref,
                      zs_ref, *, dils, split):
    tm = x_ref.shape[0]
    nl = SLAB // LANES
    hn = _rms(x_ref[...], g_ref[...]).astype(BF16)
    za_ref[...] = jnp.dot(hn, w_ref[:, 0:SLAB], preferred_element_type=F32)

    def rotate(chunks):
        cos = cos_ref[...]
        sin = sin_ref[...]
        if split:
            h = nl // 2
            return ([chunks[i] * cos - chunks[i + h] * sin for i in range(h)]
                    + [chunks[i] * cos + chunks[i - h] * sin for i in range(h, nl)])
        lane = lax.broadcasted_iota(jnp.int32, cos.shape, 1)
        first_half = jnp.bitwise_and(lane, HEAD_DIM - 1) < (HEAD_DIM // 2)
        out = []
        for zc in chunks:
            partner = jnp.where(first_half,
                                pltpu.roll(zc, LANES - HEAD_DIM // 2, 1),
                                pltpu.roll(zc, HEAD_DIM // 2, 1))
            out.append(zc * cos + partner * sin)
        return out

    slot = 0
    for g, (out_ref, dil) in enumerate(zip((g0_ref, g1_ref, g2_ref), dils)):
        for c in range(3):
            col0 = (1 + 3 * g + c) * SLAB
            z = jnp.dot(hn, w_ref[:, col0:col0 + SLAB], preferred_element_type=F32)
            chunks = [z[:, i * LANES:(i + 1) * LANES] for i in range(nl)]
            if c < 2:
                chunks = rotate(chunks)
            for i, zc in enumerate(chunks):
                cols = slice(i * LANES, (i + 1) * LANES)
                if dil == 1:
                    out_ref[c, :, cols] = zc
                else:
                    zs_ref[slot] = zc
                    for r in range(dil):
                        out_ref[c, r, :, cols] = zs_ref[slot, pl.ds(r, tm // dil, stride=dil), :]
                    slot += 1


def _resident(shape):
    return pl.BlockSpec(shape, lambda *_: (0,) * len(shape), pipeline_mode=pl.Buffered(1))


def _proj_even(x, g, w, cos, sin, tm, n_seq, dils, split):
    T = x.shape[0]
    ntab = cos.shape[0] // tm
    tps = T // n_seq // tm

    def group_spec(dil):
        if dil == 1:
            return pl.BlockSpec((3, tm, SLAB), lambda i: (0, i, 0))
        return pl.BlockSpec((3, None, dil, tm // dil, SLAB), lambda i: (0, i // tps, 0, i % tps, 0))

    def group_shape(dil):
        if dil == 1:
            return jax.ShapeDtypeStruct((3, T, SLAB), F32)
        return jax.ShapeDtypeStruct((3, n_seq, dil, T // n_seq // dil, SLAB), F32)

    n_slots = max(1, 3 * (SLAB // LANES) * sum(d > 1 for d in dils))
    return pl.pallas_call(
        functools.partial(_proj_even_kernel, dils=dils, split=split),
        grid=(T // tm,),
        in_specs=[
            pl.BlockSpec((tm, D_MODEL), lambda i: (i, 0)),
            _resident((1, D_MODEL)),
            _resident(w.shape),
            pl.BlockSpec((tm, LANES), lambda i: (i % ntab, 0)),
            pl.BlockSpec((tm, LANES), lambda i: (i % ntab, 0)),
        ],
        out_specs=[pl.BlockSpec((tm, SLAB), lambda i: (i, 0))] + [group_spec(d) for d in dils],
        out_shape=[jax.ShapeDtypeStruct((T, SLAB), F32)] + [group_shape(d) for d in dils],
        scratch_shapes=[pltpu.VMEM((n_slots, tm, LANES), F32)],
        compiler_params=_params("arbitrary"),
        name="proj_even",
    )(x, g, w, cos, sin)


def _proj_odd_kernel(x_ref, g_ref, w_ref, lng_ref, lnb_ref, o_ref):
    hn = _rms(x_ref[...], g_ref[...]).astype(BF16)

    def slab(s):
        return jnp.dot(hn, w_ref[:, s * SLAB:(s + 1) * SLAB], preferred_element_type=F32)

    o_ref[0] = _gelu(slab(0))
    zv = slab(1)
    for c in range(C_GROUPS):
        sl = slice(c * LANES, (c + 1) * LANES)
        v = _gelu(zv[:, sl])
        mu = jnp.mean(v, axis=-1, keepdims=True)
        dv = v - mu
        var = jnp.mean(dv * dv, axis=-1, keepdims=True)
        o_ref[1, :, sl] = dv * lax.rsqrt(var + EPS) * lng_ref[:, sl] + lnb_ref[:, sl]
    o_ref[2] = slab(2)
    o_ref[3] = slab(3) * slab(4)


def _proj_odd(x, g, w, ln_g, ln_b, tm):
    T = x.shape[0]
    return pl.pallas_call(
        _proj_odd_kernel,
        grid=(T // tm,),
        in_specs=[
            pl.BlockSpec((tm, D_MODEL), lambda i: (i, 0)),
            _resident((1, D_MODEL)),
            _resident(w.shape),
            _resident((1, SLAB)),
            _resident((1, SLAB)),
        ],
        out_specs=pl.BlockSpec((4, tm, SLAB), lambda i: (0, i, 0)),
        out_shape=jax.ShapeDtypeStruct((4, T, SLAB), F32),
        compiler_params=_params("arbitrary"),
        name="proj_odd",
    )(x, g, w, ln_g, ln_b)


def _attn_prompt_kernel(q_ref, kp_ref, kc_ref, vp_ref, vc_ref, o_ref, lse_ref,
                        qst_ref, ks_ref, vs_ref, bias_ref, s_ref, e_ref, mx_ref):
    step = pl.program_id(2)
    att_rows = q_ref.shape[0]
    sub = att_rows // ATT_BLOCK
    half_rot = HEAD_DIM // 2
    heads_per_blk = LANES // half_rot
    n_blk = HEADS // heads_per_blk
    log2e = 1.4426950408889634
    ln2 = 0.6931471805599453
    scale = HEAD_DIM ** -0.5 * log2e

    qlane = lax.broadcasted_iota(jnp.int32, (att_rows, 2 * LANES), 1)
    head_in_blk = jnp.bitwise_and(qlane, LANES - 1) // half_rot
    for j in range(n_blk):
        lo = slice(j * LANES, (j + 1) * LANES)
        hi = slice((n_blk + j) * LANES, (n_blk + j + 1) * LANES)
        dst = slice(2 * j * LANES, (2 * j + 2) * LANES)
        q2 = (jnp.concatenate([q_ref[:, lo], q_ref[:, hi]], axis=1) * scale).astype(BF16)
        for hq in range(heads_per_blk):
            qm = jnp.where(head_in_blk == hq, q2, jnp.zeros_like(q2))
            for b in range(sub):
                r0 = ((j * sub + b) * heads_per_blk + hq) * ATT_BLOCK
                qst_ref[r0:r0 + ATT_BLOCK, :] = qm[b * ATT_BLOCK:(b + 1) * ATT_BLOCK, :]
        ks_ref[0:ATT_BLOCK, dst] = jnp.concatenate([kp_ref[:, lo], kp_ref[:, hi]],
                                                   axis=1).astype(BF16)
        ks_ref[ATT_BLOCK:, dst] = jnp.concatenate([kc_ref[:, lo], kc_ref[:, hi]],
                                                  axis=1).astype(BF16)
    for hp in range(HEADS // 2):
        src = slice(hp * LANES, (hp + 1) * LANES)
        vs_ref[0:ATT_BLOCK, 2 * hp * LANES:(2 * hp + 1) * LANES] = vp_ref[:, src].astype(BF16)
        vs_ref[ATT_BLOCK:, 2 * hp * LANES:(2 * hp + 1) * LANES] = vc_ref[:, src].astype(BF16)
        vs_ref[:, (2 * hp + 1) * LANES:(2 * hp + 2) * LANES] = jnp.ones(
            (att_rows + ATT_BLOCK, LANES), BF16)

    qi = lax.broadcasted_iota(jnp.int32, (ATT_BLOCK, 2 * ATT_BLOCK), 0)
    ki = lax.broadcasted_iota(jnp.int32, (ATT_BLOCK, 2 * ATT_BLOCK), 1)
    rel = qi + ATT_BLOCK - ki
    band = jnp.logical_and(rel >= 0, rel <= ATT_BLOCK)
    bias_ref[0] = jnp.where(band, 0.0, NEG)
    bias_ref[1] = jnp.where(jnp.logical_and(band, ki >= ATT_BLOCK), 0.0, NEG)
    lane = lax.broadcasted_iota(jnp.int32, (ATT_BLOCK, LANES), 1)
    low_head = lane < HEAD_DIM
    blk_rows = heads_per_blk * ATT_BLOCK

    first = (step == 0).astype(jnp.int32)
    hb = ATT_BLOCK // 2
    lane_hb = lax.broadcasted_iota(jnp.int32, (hb, LANES), 1)

    def scores(b):
        for j in range(n_blk):
            base = (j * sub + b) * blk_rows
            k2 = ks_ref[b * ATT_BLOCK:(b + 2) * ATT_BLOCK, 2 * j * LANES:(2 * j + 2) * LANES]
            s_ref[b, j * blk_rows:(j + 1) * blk_rows, :] = lax.dot_general(
                qst_ref[base:base + blk_rows, :], k2, (((1,), (1,)), ((), ())),
                preferred_element_type=F32)

    def softmax(b):
        which = first if b == 0 else 0
        for rh in range(2):
            m_tile = jnp.zeros((hb, LANES), F32)
            for h in range(HEADS):
                rows = slice(h * ATT_BLOCK + rh * hb, h * ATT_BLOCK + (rh + 1) * hb)
                s = s_ref[b, rows, :] + bias_ref[which, rh * hb:(rh + 1) * hb, :]
                m = jnp.max(s, axis=-1, keepdims=True)
                e_ref[b, rows, :] = jnp.exp2((s - m).astype(BF16))
                m_tile = jnp.where(lane_hb == h, m, m_tile)
            mx_ref[b, rh * hb:(rh + 1) * hb, :] = m_tile

    def weighted_values(b):
        log_l = jnp.zeros((ATT_BLOCK, LANES), F32)
        for hp in range(HEADS // 2):
            v_ones = vs_ref[b * ATT_BLOCK:(b + 2) * ATT_BLOCK, 2 * hp * LANES:(2 * hp + 2) * LANES]
            pv = jnp.dot(e_ref[b, 2 * hp * ATT_BLOCK:(2 * hp + 2) * ATT_BLOCK, :], v_ones,
                         preferred_element_type=F32)
            l_lo, l_hi = pv[0:ATT_BLOCK, LANES:], pv[ATT_BLOCK:, LANES:]
            o_ref[b * ATT_BLOCK:(b + 1) * ATT_BLOCK, hp * LANES:(hp + 1) * LANES] = jnp.where(
                low_head, pv[0:ATT_BLOCK, 0:LANES] * (1.0 / l_lo),
                pv[ATT_BLOCK:, 0:LANES] * (1.0 / l_hi))
            log_l = jnp.where(lane == 2 * hp, jnp.log(l_lo),
                              jnp.where(lane == 2 * hp + 1, jnp.log(l_hi), log_l))
        lse_ref[b * ATT_BLOCK:(b + 1) * ATT_BLOCK, :] = mx_ref[b] * ln2 + log_l

    for t in range(sub + 2):
        if t < sub:
            scores(t)
        if 0 <= t - 1 < sub:
            softmax(t - 1)
        if 0 <= t - 2 < sub:
            weighted_values(t - 2)


def _attn_prompt(qkv, dil):
    _, n_seq, _, m_rows, _ = qkv.shape
    att_rows = min(ATT_ROWS, m_rows)
    steps = m_rows // att_rows
    sub = att_rows // ATT_BLOCK

    def cur(slab):
        return pl.BlockSpec((None, None, None, att_rows, SLAB), lambda n, r, b: (slab, n, r, b, 0))

    def prev(slab):
        return pl.BlockSpec((None, None, None, ATT_BLOCK, SLAB),
                            lambda n, r, b: (slab, n, r, jnp.maximum(b * sub - 1, 0), 0))

    return pl.pallas_call(
        _attn_prompt_kernel,
        grid=(n_seq, dil, steps),
        in_specs=[cur(0), prev(1), cur(1), prev(2), cur(2)],
        out_specs=[
            pl.BlockSpec((None, None, att_rows, SLAB), lambda n, r, b: (n, r, b, 0)),
            pl.BlockSpec((None, None, att_rows, LANES), lambda n, r, b: (n, r, b, 0)),
        ],
        out_shape=[
            jax.ShapeDtypeStruct((n_seq, dil, m_rows, SLAB), F32),
            jax.ShapeDtypeStruct((n_seq, dil, m_rows, LANES), F32),
        ],
        scratch_shapes=[
            pltpu.VMEM((HEADS * att_rows, 2 * LANES), BF16),
            pltpu.VMEM((att_rows + ATT_BLOCK, SLAB), BF16),
            pltpu.VMEM((att_rows + ATT_BLOCK, 2 * SLAB), BF16),
            pltpu.VMEM((2, ATT_BLOCK, 2 * ATT_BLOCK), F32),
            pltpu.VMEM((sub, HEADS * ATT_BLOCK, 2 * ATT_BLOCK), F32),
            pltpu.VMEM((sub, HEADS * ATT_BLOCK, 2 * ATT_BLOCK), BF16),
            pltpu.VMEM((sub, ATT_BLOCK, LANES), F32),
        ],
        compiler_params=_params("arbitrary", "arbitrary", "arbitrary"),
        name=f"attn_prompt_d{dil}",
    )(qkv, qkv, qkv, qkv, qkv)


def _kv_tail_kernel(k_ref, v_ref, o_ref, tok_ref):
    dil, rows, _ = k_ref.shape
    keep = dil * rows
    nl = SLAB // LANES
    half_rot = HEAD_DIM // 2
    heads_per_chunk = LANES // half_rot
    for kv, src in enumerate((k_ref, v_ref)):
        for lc in range(nl):
            for r in range(dil):
                tok_ref[lc, pl.ds(r, rows, stride=dil), :] = src[r, :, lc * LANES:(lc + 1) * LANES]
            for pc in range(keep // LANES):
                pcols = slice(pc * LANES, (pc + 1) * LANES)
                t = tok_ref[lc, pcols, :].T
                if kv == 0:
                    half, hblk = divmod(lc, nl // 2)
                    for hq in range(heads_per_chunk):
                        r0 = (hblk * heads_per_chunk + hq) * HEAD_DIM + half * half_rot
                        o_ref[0, r0:r0 + half_rot, pcols] = t[hq * half_rot:(hq + 1) * half_rot, :]
                else:
                    o_ref[1, lc * LANES:(lc + 1) * LANES, pcols] = t


def _kv_tail(grp, keep):
    _, n_seq, dil, m_rows, _ = grp.shape
    rows = keep // dil
    last = m_rows // rows - 1

    def spec(slab):
        return pl.BlockSpec((None, None, dil, rows, SLAB), lambda n: (slab, n, 0, last, 0))

    return pl.pallas_call(
        _kv_tail_kernel,
        grid=(n_seq,),
        in_specs=[spec(1), spec(2)],
        out_specs=pl.BlockSpec((None, 2, SLAB, keep), lambda n: (n, 0, 0, 0)),
        out_shape=jax.ShapeDtypeStruct((n_seq, 2, SLAB, keep), F32),
        scratch_shapes=[pltpu.VMEM((SLAB // LANES, keep, LANES), F32)],
        compiler_params=_params("arbitrary"),
        name=f"kv_tail_d{dil}",
    )(grp, grp)


def _pool_mix(window_terms, a_cols, cnt, pw_ref, scale_ref, gi):
    acc = a_cols
    for term in window_terms:
        acc = acc + term
    pooled = acc / cnt - a_cols
    cols = slice(gi * POOL_GROUP, (gi + 1) * POOL_GROUP)
    mixed = jnp.dot(pooled.astype(BF16), pw_ref[gi], preferred_element_type=F32)
    return mixed * scale_ref[:, cols]


def _even_out_prompt_kernel(a_ref, halo_ref, o0_ref, o1_ref, o2_ref, l0_ref, l1_ref, l2_ref,
                            pw_ref, ps_ref, wo_ref, x_ref, out_ref, ext_ref, ya_ref, yb_ref,
                            oi_ref, li_ref, *, tiles_per_seq):
    tm = a_ref.shape[0]
    it = pl.program_id(0) % tiles_per_seq
    halo = halo_ref[...]
    ext_ref[0:POOL_HALO, :] = jnp.where(it == 0, jnp.zeros_like(halo), halo)
    ext_ref[POOL_HALO:, :] = a_ref[...]
    pos = it * tm + lax.broadcasted_iota(jnp.int32, (tm, 1), 0)
    for gi, w in enumerate(POOL_WINDOWS):
        cols = slice(gi * POOL_GROUP, (gi + 1) * POOL_GROUP)
        terms = [ext_ref[POOL_HALO - k:POOL_HALO - k + tm, cols] for k in range(1, w)]
        cnt = jnp.minimum(w, pos + 1).astype(F32)
        ya_ref[:, cols] = _pool_mix(terms, a_ref[:, cols], cnt, pw_ref, ps_ref, gi).astype(BF16)

    for g, (o_ref, l_ref) in enumerate(((o0_ref, l0_ref), (o1_ref, l1_ref), (o2_ref, l2_ref))):
        dil = o_ref.shape[0]
        for r in range(dil):
            rows = pl.ds(r, tm // dil, stride=dil)
            li_ref[g, rows, :] = l_ref[r]
            for lc in range(SLAB // LANES):
                oi_ref[g, lc, rows, :] = o_ref[r, :, lc * LANES:(lc + 1) * LANES]

    l0, l1, l2 = li_ref[0], li_ref[1], li_ref[2]
    mx = jnp.maximum(jnp.maximum(l0, l1), l2)
    e0, e1, e2 = jnp.exp(l0 - mx), jnp.exp(l1 - mx), jnp.exp(l2 - mx)
    inv = 1.0 / (e0 + e1 + e2)
    w0, w1 = e0 * inv, e1 * inv
    for h in range(HEADS):
        lc, lo = divmod(h * HEAD_DIM, LANES)
        sl = slice(lo, lo + HEAD_DIM)
        o2 = oi_ref[2, lc, :, sl]
        yb = (o2 + w0[:, h:h + 1] * (oi_ref[0, lc, :, sl] - o2)
              + w1[:, h:h + 1] * (oi_ref[1, lc, :, sl] - o2))
        yb_ref[:, h * HEAD_DIM:(h + 1) * HEAD_DIM] = yb.astype(BF16)

    y = jnp.dot(ya_ref[...], wo_ref[0:SLAB, :], preferred_element_type=F32)
    y = y + jnp.dot(yb_ref[...], wo_ref[SLAB:, :], preferred_element_type=F32)
    out_ref[...] = x_ref[...] + y


def _even_out_prompt(za, attn, pool_w, pool_scale, w_out, x, seq, tm):
    T = x.shape[0]
    tps = seq // tm
    hb = tm // POOL_HALO

    def residue_spec(dil, width):
        return pl.BlockSpec((None, dil, tm // dil, width), lambda i: (i // tps, 0, i % tps, 0))

    dils = [o.shape[1] for o, _ in attn]
    return pl.pallas_call(
        functools.partial(_even_out_prompt_kernel, tiles_per_seq=tps),
        grid=(T // tm,),
        in_specs=[
            pl.BlockSpec((tm, SLAB), lambda i: (i, 0)),
            pl.BlockSpec((POOL_HALO, SLAB), lambda i: (jnp.maximum(i * hb - 1, 0), 0)),
        ]
        + [residue_spec(d, SLAB) for d in dils] + [residue_spec(d, LANES) for d in dils]
        + [
            pl.BlockSpec((len(POOL_WINDOWS), POOL_GROUP, POOL_GROUP), lambda i: (0, 0, 0)),
            pl.BlockSpec((1, SLAB), lambda i: (0, 0)),
            pl.BlockSpec((2 * SLAB, D_MODEL), lambda i: (0, 0)),
            pl.BlockSpec((tm, D_MODEL), lambda i: (i, 0)),
        ],
        out_specs=pl.BlockSpec((tm, D_MODEL), lambda i: (i, 0)),
        out_shape=jax.ShapeDtypeStruct((T, D_MODEL), F32),
        scratch_shapes=[
            pltpu.VMEM((tm + POOL_HALO, SLAB), F32),
            pltpu.VMEM((tm, SLAB), BF16),
            pltpu.VMEM((tm, SLAB), BF16),
            pltpu.VMEM((len(dils), SLAB // LANES, tm, LANES), F32),
            pltpu.VMEM((len(dils), tm, LANES), F32),
        ],
        compiler_params=_params("arbitrary"),
        name="even_out_prompt",
    )(za, za, *[o for o, _ in attn], *[l for _, l in attn], pool_w, pool_scale, w_out, x)


def _odd_mix(u_ref, vn_ref, go_ref, hd_ref, hdh_ref, ws_ref, bs_ref, cw_ref, wo_ref, x_ref,
             ext_ref, yc_ref, yd_ref, tiles_per_seq):
    tm = u_ref.shape[0]
    it = pl.program_id(0) % tiles_per_seq

    ti = lax.broadcasted_iota(jnp.int32, (CHUNK, CHUNK), 0)
    si = lax.broadcasted_iota(jnp.int32, (CHUNK, CHUNK), 1)
    for g in range(C_GROUPS):
        cols = slice(g * LANES, (g + 1) * LANES)
        wm = jnp.where(si <= ti, ws_ref[g], 0.0).astype(BF16)
        for c in range(tm // CHUNK):
            rows = slice(c * CHUNK, (c + 1) * CHUNK)
            sp = jnp.dot(wm, vn_ref[rows, cols].astype(BF16), preferred_element_type=F32)
            sp = sp + bs_ref[:, cols]
            yc_ref[rows, cols] = (u_ref[rows, cols] * sp).astype(BF16)

    hd = hd_ref[...]
    halo = hdh_ref[...]
    ext_ref[0:CONV_HALO, :] = jnp.where(it == 0, jnp.zeros_like(halo), halo)
    ext_ref[CONV_HALO:, :] = hd
    conv = cw_ref[CONV_W - 1:CONV_W, :] * hd
    for j in range(CONV_W - 1):
        off = CONV_HALO - (CONV_W - 1) + j
        conv = conv + cw_ref[j:j + 1, :] * ext_ref[off:off + tm, :]
    yd_ref[...] = (go_ref[...] * conv).astype(BF16)

    y = jnp.dot(yc_ref[...], wo_ref[0:SLAB, :], preferred_element_type=F32)
    y = y + jnp.dot(yd_ref[...], wo_ref[SLAB:, :], preferred_element_type=F32)
    return x_ref[...] + y


def _odd_out_ffn_kernel(u_ref, vn_ref, go_ref, hd_ref, hdh_ref, ws_ref, bs_ref, cw_ref, wo_ref,
                        x_ref, p_ref, gf_ref, w1_ref, w2_ref, gp_ref, wg_ref, wp_ref, gl_ref,
                        out_ref, ext_ref, yc_ref, yd_ref, *, tiles_per_seq, final_norm, tf):
    r = _odd_mix(u_ref, vn_ref, go_ref, hd_ref, hdh_ref, ws_ref, bs_ref, cw_ref, wo_ref, x_ref,
                 ext_ref, yc_ref, yd_ref, tiles_per_seq)
    out_ref[...] = _mlp_ple(r, p_ref, gf_ref, w1_ref, w2_ref, gp_ref, wg_ref, wp_ref, gl_ref,
                            final_norm, tf)


def _odd_out_ffn(layer, z, ws, bs_rows, conv_w, w_out, x, seq, p, g_ffn, w1, w2, g_ple, wg, wp,
                 g_last, final_norm, tm, tf):
    T = x.shape[0]
    tiles_per_seq = seq // tm
    hb = tm // CONV_HALO

    def slab(s):
        return pl.BlockSpec((None, tm, SLAB), lambda i: (s, i, 0))

    return pl.pallas_call(
        functools.partial(_odd_out_ffn_kernel, tiles_per_seq=tiles_per_seq,
                          final_norm=final_norm, tf=tf),
        grid=(T // tm,),
        in_specs=[
            slab(0), slab(1), slab(2), slab(3),
            pl.BlockSpec((None, CONV_HALO, SLAB), lambda i: (3, jnp.maximum(i * hb - 1, 0), 0)),
            _resident((C_GROUPS, CHUNK, CHUNK)),
            _resident((CHUNK, SLAB)),
            _resident((CONV_W, SLAB)),
            _resident((2 * SLAB, D_MODEL)),
            pl.BlockSpec((tm, D_MODEL), lambda i: (i, 0)),
            pl.BlockSpec((None, tm, PLE_DIM), lambda i: (layer, i, 0)),
            _layer_resident(layer, (1, D_MODEL)),
            _layer_resident(layer, (D_MODEL, D_FF)),
            _layer_resident(layer, (D_FF, D_MODEL)),
            _layer_resident(layer, (1, D_MODEL)),
            _layer_resident(layer, (D_MODEL, D_MODEL)),
            _layer_resident(layer, (PLE_DIM, D_MODEL)),
            _resident((1, D_MODEL)),
        ],
        out_specs=pl.BlockSpec((tm, D_MODEL), lambda i: (i, 0)),
        out_shape=jax.ShapeDtypeStruct((T, D_MODEL), F32),
        scratch_shapes=[
            pltpu.VMEM((tm + CONV_HALO, SLAB), F32),
            pltpu.VMEM((tm, SLAB), BF16),
            pltpu.VMEM((tm, SLAB), BF16),
        ],
        compiler_params=_params("arbitrary"),
        name="odd_out_ffn",
    )(z, z, z, z, z, ws, bs_rows, conv_w, w_out, x, p, g_ffn, w1, w2, g_ple, wg, wp, g_last)


def _mlp_ple(x, p_ref, gf_ref, w1_ref, w2_ref, gp_ref, wg_ref, wp_ref, gl_ref, final_norm, tf):
    hn = _rms(x, gf_ref[...]).astype(BF16)
    acc = None
    for c in range(D_FF // tf):
        h1 = jnp.dot(hn, w1_ref[:, c * tf:(c + 1) * tf], preferred_element_type=F32)
        h1 = jnp.square(jnp.maximum(h1, 0.0)).astype(BF16)
        part = jnp.dot(h1, w2_ref[c * tf:(c + 1) * tf, :], preferred_element_type=F32)
        acc = part if acc is None else acc + part
    r = x + acc
    hp = _rms(r, gp_ref[...]).astype(BF16)
    gate = jax.nn.sigmoid(jnp.dot(hp, wg_ref[...], preferred_element_type=F32))
    proj = jnp.dot(p_ref[...].astype(BF16), wp_ref[...], preferred_element_type=F32)
    r = r + gate * proj
    if final_norm:
        r = _rms(r, gl_ref[...])
    return r


def _ffn_ple_kernel(x_ref, p_ref, gf_ref, w1_ref, w2_ref, gp_ref, wg_ref, wp_ref, gl_ref,
                    out_ref, *, final_norm, tf):
    out_ref[...] = _mlp_ple(x_ref[...], p_ref, gf_ref, w1_ref, w2_ref, gp_ref, wg_ref, wp_ref,
                            gl_ref, final_norm, tf)


def _layer_resident(layer, shape):
    return pl.BlockSpec((None,) + shape, lambda *_: (layer,) + (0,) * len(shape),
                        pipeline_mode=pl.Buffered(1))


def _ffn_ple(layer, x, p, g_ffn, w1, w2, g_ple, wg, wp, g_last, final_norm, tm, tf):
    T = x.shape[0]
    return pl.pallas_call(
        functools.partial(_ffn_ple_kernel, final_norm=final_norm, tf=tf),
        grid=(T // tm,),
        in_specs=[
            pl.BlockSpec((tm, D_MODEL), lambda i: (i, 0)),
            pl.BlockSpec((None, tm, PLE_DIM), lambda i: (layer, i, 0)),
            _layer_resident(layer, (1, D_MODEL)),
            _layer_resident(layer, (D_MODEL, D_FF)),
            _layer_resident(layer, (D_FF, D_MODEL)),
            _layer_resident(layer, (1, D_MODEL)),
            _layer_resident(layer, (D_MODEL, D_MODEL)),
            _layer_resident(layer, (PLE_DIM, D_MODEL)),
            _resident((1, D_MODEL)),
        ],
        out_specs=pl.BlockSpec((tm, D_MODEL), lambda i: (i, 0)),
        out_shape=jax.ShapeDtypeStruct((T, D_MODEL), F32),
        compiler_params=_params("arbitrary"),
        name="ffn_ple",
    )(x, p, g_ffn, w1, w2, g_ple, wg, wp, g_last)


def _ffn_ple_hosting_kernel(x_ref, p_ref, gf_ref, w1_ref, w2_ref, gp_ref, wg_ref, wp_ref, gl_ref,
                            q0_ref, q1_ref, q2_ref, c0_ref, c1_ref, c2_ref, out_ref, pacc_ref,
                            pml_ref, hn_ref, acc_ref, *, final_norm, tf, splits):
    j = pl.program_id(1)
    share = D_FF // splits

    def mlp_share(k):
        acc = None
        for c in range(share // tf):
            cols = slice(k * share + c * tf, k * share + (c + 1) * tf)
            h1 = jnp.dot(hn_ref[...], w1_ref[:, cols], preferred_element_type=F32)
            h1 = jnp.square(jnp.maximum(h1, 0.0)).astype(BF16)
            part = jnp.dot(h1, w2_ref[cols, :], preferred_element_type=F32)
            acc = part if acc is None else acc + part
        return acc

    @pl.when(j == 0)
    def _():
        hn_ref[...] = _rms(x_ref[...], gf_ref[...]).astype(BF16)
        acc_ref[...] = mlp_share(0)

    for k in range(1, splits - 1):
        @pl.when(j == k)
        def _(k=k):
            acc_ref[...] += mlp_share(k)

    @pl.when(j == splits - 1)
    def _():
        r = x_ref[...] + (acc_ref[...] + mlp_share(splits - 1))
        hp = _rms(r, gp_ref[...]).astype(BF16)
        gate = jax.nn.sigmoid(jnp.dot(hp, wg_ref[...], preferred_element_type=F32))
        proj = jnp.dot(p_ref[...].astype(BF16), wp_ref[...], preferred_element_type=F32)
        r = r + gate * proj
        if final_norm:
            r = _rms(r, gl_ref[...])
        out_ref[...] = r

    row = (splits * pl.program_id(0) + j) % Q_ROWS
    q_rows = [[q_ref[t, pl.ds(row, 1), :] for t in range(q_ref.shape[0])]
              for q_ref in (q0_ref, q1_ref, q2_ref)]
    _partial_cached_attention(q_rows, (c0_ref, c1_ref, c2_ref), pacc_ref, pml_ref)


def _ffn_ple_hosting(layer, x, p, g_ffn, w1, w2, g_ple, wg, wp, g_last, final_norm, tf,
                     qkv, caches, splits):
    T = x.shape[0]
    _, n_tok, n_seq, _ = qkv[0].shape
    tm = splits * T // n_seq
    seq_of = lambda i, j: splits * i + j

    def cache_spec(c):
        return pl.BlockSpec((None, 2, SLAB, c.shape[-1]), lambda i, j: (seq_of(i, j), 0, 0, 0))

    return pl.pallas_call(
        functools.partial(_ffn_ple_hosting_kernel, final_norm=final_norm, tf=tf, splits=splits),
        grid=(T // tm, splits),
        in_specs=[
            pl.BlockSpec((tm, D_MODEL), lambda i, j: (i, 0)),
            pl.BlockSpec((None, tm, PLE_DIM), lambda i, j: (layer, i, 0)),
            _layer_resident(layer, (1, D_MODEL)),
            _layer_resident(layer, (D_MODEL, D_FF)),
            _layer_resident(layer, (D_FF, D_MODEL)),
            _layer_resident(layer, (1, D_MODEL)),
            _layer_resident(layer, (D_MODEL, D_MODEL)),
            _layer_resident(layer, (PLE_DIM, D_MODEL)),
            _resident((1, D_MODEL)),
        ] + [pl.BlockSpec((None, n_tok, Q_ROWS, SLAB),
                          lambda i, j: (0, 0, seq_of(i, j) // Q_ROWS, 0)) for _ in qkv]
        + [cache_spec(c) for c in caches],
        out_specs=[
            pl.BlockSpec((tm, D_MODEL), lambda i, j: (i, 0)),
            pl.BlockSpec((None, n_tok * HEADS, SLAB), lambda i, j: (seq_of(i, j), 0, 0)),
            pl.BlockSpec((None, n_tok * HEADS, LANES), lambda i, j: (seq_of(i, j), 0, 0)),
        ],
        out_shape=[
            jax.ShapeDtypeStruct((T, D_MODEL), F32),
            jax.ShapeDtypeStruct((n_seq, n_tok * HEADS, SLAB), F32),
            jax.ShapeDtypeStruct((n_seq, n_tok * HEADS, LANES), F32),
        ],
        scratch_shapes=[pltpu.VMEM((tm, D_MODEL), BF16), pltpu.VMEM((tm, D_MODEL), F32)],
        compiler_params=pltpu.CompilerParams(dimension_semantics=("arbitrary", "arbitrary"),
                                             vmem_limit_bytes=HOSTING_VMEM_LIMIT),
        name="ffn_ple_hosting",
    )(x, p, g_ffn, w1, w2, g_ple, wg, wp, g_last, *qkv, *caches)


def _own_lanes():
    sub = lax.broadcasted_iota(jnp.int32, (HEADS, SLAB), 0)
    lane_head = lax.broadcasted_iota(jnp.int32, (HEADS, SLAB), 1) // HEAD_DIM
    return sub == lane_head


def _block_diag_queries(q_rows):
    own = _own_lanes()
    scale = HEAD_DIM ** -0.5
    return jnp.concatenate(
        [jnp.where(own, jnp.broadcast_to(q * scale, (HEADS, SLAB)), 0.0) for q in q_rows], axis=0)


def _cached_scores(qbd, kt, dil):
    s = jnp.dot(qbd.astype(BF16), kt.astype(BF16), preferred_element_type=F32)
    pos = lax.broadcasted_iota(jnp.int32, s.shape, 1)
    row_tok = lax.broadcasted_iota(jnp.int32, (s.shape[0], 1), 0) // HEADS
    valid = (pos >= row_tok) if dil == 1 else (jnp.bitwise_and(pos, dil - 1) == row_tok)
    return jnp.where(valid, s, NEG)


def _weighted_cached_values(e, vt):
    return lax.dot_general(e.astype(BF16), vt.astype(BF16), (((1,), (1,)), ((), ())),
                           preferred_element_type=F32)


def _partial_cached_attention(q_rows, cache_refs, acc_ref, ml_ref):
    scores = []
    m = None
    for g, (c_ref, (_, dil)) in enumerate(zip(cache_refs, DIL_CFG)):
        s = _cached_scores(_block_diag_queries(q_rows[g]), c_ref[0], dil)
        row_max = jnp.max(s, axis=1, keepdims=True)
        m = row_max if m is None else jnp.maximum(m, row_max)
        scores.append(s)
    acc, l = None, None
    for s, c_ref in zip(scores, cache_refs):
        e = jnp.exp(s - m)
        part = _weighted_cached_values(e, c_ref[1])
        row_sum = jnp.sum(e, axis=1, keepdims=True)
        acc = part if acc is None else acc + part
        l = row_sum if l is None else l + row_sum
    acc_ref[...] = acc
    lane = lax.broadcasted_iota(jnp.int32, ml_ref.shape, 1)
    ml_ref[...] = jnp.where(lane == 0, m, jnp.where(lane == 1, l, 0.0))


def _attn_sample_kernel(g0_ref, g1_ref, g2_ref, pacc_ref, pml_ref, o_ref, *, n_seq):
    groups = (g0_ref, g1_ref, g2_ref)
    n_tok = g0_ref.shape[1]
    n_rows = n_tok * HEADS
    own = _own_lanes()
    row_tok = lax.broadcasted_iota(jnp.int32, (n_rows, 1), 0) // HEADS

    def one_sequence(n, carry):
        m_part = pml_ref[n, :, 0:1]
        m = m_part
        s_new = []
        for g, (g_ref, (_, dil)) in enumerate(zip(groups, DIL_CFG)):
            qbd = _block_diag_queries([g_ref[0, t, pl.ds(n, 1), :] for t in range(n_tok)])
            for tp in range(n_tok):
                sn = jnp.sum(qbd * g_ref[1, tp, pl.ds(n, 1), :], axis=1, keepdims=True)
                ok = (row_tok >= tp) if dil == 1 else (row_tok == tp)
                sn = jnp.where(ok, sn, NEG)
                m = jnp.maximum(m, sn)
                s_new.append((g, tp, sn))
        w_part = jnp.exp(m_part - m)
        acc = w_part * pacc_ref[n]
        den = w_part * pml_ref[n, :, 1:2]
        for g, tp, sn in s_new:
            e = jnp.exp(sn - m)
            den = den + e
            acc = acc + e * groups[g][2, tp, pl.ds(n, 1), :]
        res = acc * (1.0 / den)
        for t in range(n_tok):
            rows = res[t * HEADS:(t + 1) * HEADS, :]
            o_ref[n, t:t + 1, :] = jnp.sum(jnp.where(own, rows, 0.0), axis=0, keepdims=True)
        return carry

    lax.fori_loop(0, n_seq, one_sequence, 0, unroll=4)


def _attn_sample(qkv, pacc, pml, seqs_per_step=16):
    _, n_tok, n_seq, _ = qkv[0].shape
    nb = seqs_per_step

    def spec(*tail):
        return pl.BlockSpec((nb,) + tail, lambda i: (i,) + (0,) * len(tail))

    return pl.pallas_call(
        functools.partial(_attn_sample_kernel, n_seq=nb),
        grid=(n_seq // nb,),
        in_specs=[pl.BlockSpec((3, n_tok, nb, SLAB), lambda i: (0, 0, i, 0)) for _ in qkv]
        + [spec(n_tok * HEADS, SLAB), spec(n_tok * HEADS, LANES)],
        out_specs=spec(n_tok, SLAB),
        out_shape=jax.ShapeDtypeStruct((n_seq, n_tok, SLAB), F32),
        compiler_params=_params("arbitrary"),
        name="attn_sample",
    )(*qkv, pacc, pml)


def _even_out_sample_kernel(a_ref, ctx_ref, yb_ref, pw_ref, ps_ref, wo_ref, x_ref, out_ref,
                            *, n_seq, n_tok):
    def ext_row(e, cols):
        if e >= POOL_STATE:
            t = e - POOL_STATE
            return a_ref[t * n_seq:(t + 1) * n_seq, cols]
        return ctx_ref[e, :, cols]

    for t in range(n_tok):
        rows = slice(t * n_seq, (t + 1) * n_seq)
        ya = []
        for gi, w in enumerate(POOL_WINDOWS):
            cols = slice(gi * POOL_GROUP, (gi + 1) * POOL_GROUP)
            terms = [ext_row(POOL_STATE + t - k, cols) for k in range(1, w)]
            ya.append(_pool_mix(terms, a_ref[rows, cols], float(w), pw_ref, ps_ref, gi))
        y = jnp.zeros((n_seq, D_MODEL), F32)
        for gi in range(len(POOL_WINDOWS)):
            y = y + jnp.dot(ya[gi].astype(BF16), wo_ref[gi * POOL_GROUP:(gi + 1) * POOL_GROUP, :],
                            preferred_element_type=F32)
        yb = yb_ref[:, t * SLAB:(t + 1) * SLAB].astype(BF16)
        y = y + jnp.dot(yb, wo_ref[SLAB:, :], preferred_element_type=F32)
        out_ref[rows, :] = x_ref[rows, :] + y


def _even_out_sample(z, ctx, yb, pool_w, pool_scale, w_out, x, n_seq, n_tok):
    T = x.shape[0]
    full = lambda *shape: pl.BlockSpec(shape, lambda i: (0,) * len(shape))
    return pl.pallas_call(
        functools.partial(_even_out_sample_kernel, n_seq=n_seq, n_tok=n_tok),
        grid=(1,),
        in_specs=[
            full(T, SLAB),
            full(POOL_STATE, n_seq, SLAB),
            full(n_seq, n_tok * SLAB),
            full(len(POOL_WINDOWS), POOL_GROUP, POOL_GROUP),
            full(1, SLAB),
            full(2 * SLAB, D_MODEL),
            full(T, D_MODEL),
        ],
        out_specs=full(T, D_MODEL),
        out_shape=jax.ShapeDtypeStruct((T, D_MODEL), F32),
        compiler_params=_params("arbitrary"),
        name="even_out_sample",
    )(z, ctx, yb, pool_w, pool_scale, w_out, x)


def _odd_out_sample_kernel(z_ref, ctx_ref, coef_ref, bias_ref, cw_ref, wo_ref, x_ref,
                           out_ref, *, n_seq, n_tok, mix_terms):
    def ext_row(e):
        if e >= CONV_W - 1:
            t = e - (CONV_W - 1)
            return z_ref[3, t * n_seq:(t + 1) * n_seq, :]
        return ctx_ref[:, e * SLAB:(e + 1) * SLAB]

    for t in range(n_tok):
        rows = slice(t * n_seq, (t + 1) * n_seq)
        sp = jnp.zeros((n_seq, SLAB), F32) + bias_ref[t:t + 1, :]
        for s in mix_terms[t]:
            r = t * n_tok + s
            sp = sp + coef_ref[r:r + 1, :] * z_ref[1, s * n_seq:(s + 1) * n_seq, :]
        yc = z_ref[0, rows, :] * sp
        conv = jnp.zeros((n_seq, SLAB), F32)
        for j in range(CONV_W):
            conv = conv + cw_ref[j:j + 1, :] * ext_row(t + j)
        yd = z_ref[2, rows, :] * conv
        y = jnp.dot(yc.astype(BF16), wo_ref[0:SLAB, :], preferred_element_type=F32)
        y = y + jnp.dot(yd.astype(BF16), wo_ref[SLAB:, :], preferred_element_type=F32)
        out_ref[rows, :] = x_ref[rows, :] + y


def _odd_out_sample(z, ctx, coef, bias, conv_w, w_out, x, n_seq, n_tok, mix_terms):
    T = x.shape[0]
    full = lambda *shape: pl.BlockSpec(shape, lambda i: (0,) * len(shape))
    return pl.pallas_call(
        functools.partial(_odd_out_sample_kernel, n_seq=n_seq, n_tok=n_tok, mix_terms=mix_terms),
        grid=(1,),
        in_specs=[
            full(4, T, SLAB),
            full(n_seq, (CONV_W - 1) * SLAB),
            full(n_tok * n_tok, SLAB),
            full(n_tok, SLAB),
            full(CONV_W, SLAB),
            full(2 * SLAB, D_MODEL),
            full(T, D_MODEL),
        ],
        out_specs=full(T, D_MODEL),
        out_shape=jax.ShapeDtypeStruct((T, D_MODEL), F32),
        compiler_params=_params("arbitrary"),
        name="odd_out_sample",
    )(z, ctx, coef, bias, conv_w, w_out, x)


def _rope_tables(pos, split):
    half = HEAD_DIM // 2
    inv = jnp.power(jnp.float32(ROPE_THETA), -jnp.arange(half, dtype=F32) / half)
    ang = pos.astype(F32)[:, None] * inv[None, :]
    cos = jnp.cos(ang)
    sin = jnp.sin(ang)
    if split:
        return jnp.tile(cos, (1, LANES // half)), jnp.tile(sin, (1, LANES // half))
    cos_t = jnp.tile(jnp.concatenate([cos, cos], axis=-1), (1, LANES // HEAD_DIM))
    sin_t = jnp.tile(jnp.concatenate([-sin, sin], axis=-1), (1, LANES // HEAD_DIM))
    return cos_t, sin_t


def _split_qk_columns_kernel(w_ref, o_ref):
    half = HEAD_DIM // 2
    n_slabs = w_ref.shape[1] // SLAB
    for s in range(n_slabs):
        cols = slice(s * SLAB, (s + 1) * SLAB)
        x = w_ref[:, cols]
        if s >= 1 and (s - 1) % 3 < 2:
            xt = x.T
            xt = jnp.concatenate(
                [xt[h * HEAD_DIM + part * half:h * HEAD_DIM + (part + 1) * half, :]
                 for part in range(2) for h in range(HEADS)], axis=0)
            x = xt.T
        o_ref[:, cols] = x.astype(BF16)


def _split_qk_columns(w, tr=256):
    d, n = w.shape
    return pl.pallas_call(
        _split_qk_columns_kernel,
        grid=(d // tr,),
        in_specs=[pl.BlockSpec((tr, n), lambda i: (i, 0))],
        out_specs=pl.BlockSpec((tr, n), lambda i: (i, 0)),
        out_shape=jax.ShapeDtypeStruct((d, n), BF16),
        compiler_params=_params("arbitrary"),
        name="split_qk_columns",
    )(w)


def kernel(x_prompt, x_sample, cache_kv_w128, cache_kv_w512, cache_kv_w2048, state_pool, state_conv,
           p_prompt, p_sample, ev_w_in, ev_pool_w, ev_pool_scale, ev_w_out, od_w_in, od_ln_g, od_ln_b,
           od_ws, od_bs, od_conv_w, od_w_out, norm_mix, norm_ffn, norm_ple, ffn_w1, ffn_w2,
           ple_w_proj, ple_w_gate, norm_final):
    n_p, seq, _ = x_prompt.shape
    n_s, n_tok, _ = x_sample.shape
    depth = norm_mix.shape[0]
    tp = n_p * seq
    ts = n_s * n_tok

    bf = lambda w: w.astype(BF16)
    row = lambda v: v.reshape(1, -1)
    ev_w_in_b, ev_pool_w_b, ev_w_out_b = bf(ev_w_in), bf(ev_pool_w), bf(ev_w_out)
    od_w_in_b, od_w_out_b = bf(od_w_in), bf(od_w_out)
    w1_b, w2_b, wg_b, wp_b = bf(ffn_w1), bf(ffn_w2), bf(ple_w_gate), bf(ple_w_proj)

    cos_p, sin_p = _rope_tables(jnp.arange(seq), True)
    pos_s = [PAST_LEN + t for t in range(n_tok)]
    cos_s, sin_s = _rope_tables(jnp.repeat(jnp.asarray(pos_s, jnp.int32), n_s), False)

    mix_terms = tuple(
        tuple(s for s in range(n_tok)
              if pos_s[s] // CHUNK == pos_s[t] // CHUNK and pos_s[s] % CHUNK <= pos_s[t] % CHUNK)
        for t in range(n_tok))
    local = [p % CHUNK for p in pos_s]

    rp = x_prompt.reshape(tp, D_MODEL)
    rs = x_sample.transpose(1, 0, 2).reshape(ts, D_MODEL)
    pp = p_prompt.reshape(depth, tp, PLE_DIM)
    ps = p_sample.transpose(0, 2, 1, 3).reshape(depth, ts, PLE_DIM)

    tm_p, tm_mix, tm_ffn, tf = 512, 512, 512, 512
    assert seq % max(tm_p, tm_mix, tm_ffn) == 0 and ts % Q_ROWS == 0
    assert all(win == dil * ATT_BLOCK and (seq // dil) % min(ATT_ROWS, seq // dil) == 0
               and min(win, seq) % (dil * LANES) == 0 for win, dil in DIL_CFG)
    assert tp * HOST_SPLITS == tm_ffn * n_s, "one sample sequence per hosting-MLP grid step"
    assert PAST_LEN >= max(max(POOL_WINDOWS), CONV_W), "sample windows are full"
    assert all(cache.shape[2] == win for cache, (win, _) in
               zip((cache_kv_w128, cache_kv_w512, cache_kv_w2048), DIL_CFG))
    kv_p = [[] for _ in DIL_CFG]
    kv_s = [[] for _ in DIL_CFG]
    pool_p, pool_s, conv_p, conv_s, cv_s = [], [], [], [], []

    for i in range(depth):
        g_mix = row(norm_mix[i])
        ffn_args = (norm_ffn[:, None, :], w1_b, w2_b, norm_ple[:, None, :], wg_b, wp_b,
                    row(norm_final), i == depth - 1)
        if i % 2 == 0:
            e = i // 2
            pscale = row(ev_pool_scale[e])
            dils = tuple(d for _, d in DIL_CFG)
            w_split = _split_qk_columns(ev_w_in[e])
            za, *groups = _proj_even(rp, g_mix, w_split, cos_p, sin_p, tm_p, n_p, dils, True)
            groups = [grp.reshape(3, n_p, dil, seq // dil, SLAB) for grp, dil in zip(groups, dils)]
            attn = [_attn_prompt(grp, dil) for grp, dil in zip(groups, dils)]
            rp = _even_out_prompt(za, attn, ev_pool_w_b[e], pscale, ev_w_out_b[e], rp, seq, tm_mix)
            for g, ((win, _), grp) in enumerate(zip(DIL_CFG, groups)):
                keep = min(win, seq)
                tail = _kv_tail(grp, keep).reshape(n_p, 2, HEADS, HEAD_DIM, keep)
                kv_p[g].append(tail.transpose(0, 4, 1, 2, 3))
            pool_p.append(za.reshape(n_p, seq, SLAB)[:, seq - POOL_STATE:])
            zas, *sgroups = _proj_even(rs, g_mix, ev_w_in_b[e], cos_s, sin_s, ts, 1, (1, 1, 1),
                                       False)
            sgroups = [g.reshape(3, n_tok, n_s, SLAB) for g in sgroups]
            native = lambda c: c.transpose(0, 2, 3, 4, 1).reshape(n_s, 2, SLAB, c.shape[1])
            caches = [native(c[e]) for c in (cache_kv_w128, cache_kv_w512, cache_kv_w2048)]
            rp, pacc, pml = _ffn_ple_hosting(i, rp, pp, *ffn_args, tf, sgroups, caches,
                                             HOST_SPLITS)
            yb = _attn_sample(sgroups, pacc, pml)
            rs = _even_out_sample(zas, state_pool[e].transpose(1, 0, 2),
                                  yb.reshape(n_s, n_tok * SLAB), ev_pool_w_b[e], pscale,
                                  ev_w_out_b[e], rs, n_s, n_tok)
            for g, grp in enumerate(sgroups):
                kv = grp[1:].transpose(2, 1, 0, 3)
                kv_s[g].append(kv.reshape(n_s, n_tok, 2, HEADS, HEAD_DIM))
            a_n = zas.reshape(n_tok, n_s, SLAB).transpose(1, 0, 2)
            pool_s.append(jnp.concatenate([state_pool[e], a_n], axis=1)[:, -POOL_STATE:])
        else:
            o = i // 2
            ln_g, ln_b = row(od_ln_g[o]), row(od_ln_b[o])
            bs_rows = jnp.repeat(od_bs[o].T, LANES, axis=1)
            z = _proj_odd(rp, g_mix, od_w_in_b[o], ln_g, ln_b, tm_p)
            rp = _odd_out_ffn(i, z, od_ws[o], bs_rows, od_conv_w[o], od_w_out_b[o], rp, seq, pp,
                              *ffn_args, tm_ffn, tf)
            conv_p.append(z.reshape(4, n_p, seq, SLAB)[3, :, seq - (CONV_W - 1):])
            zs = _proj_odd(rs, g_mix, od_w_in_b[o], ln_g, ln_b, ts)
            coef = jnp.stack([jnp.repeat(od_ws[o][:, local[t], local[s]], LANES)
                              for t in range(n_tok) for s in range(n_tok)])
            bias = jnp.stack([bs_rows[local[t]] for t in range(n_tok)])
            rs = _odd_out_sample(zs, state_conv[o].reshape(n_s, (CONV_W - 1) * SLAB), coef, bias,
                                 od_conv_w[o], od_w_out_b[o], rs, n_s, n_tok, mix_terms)
            hd_n = zs[3].reshape(n_tok, n_s, SLAB).transpose(1, 0, 2)
            conv_s.append(jnp.concatenate([state_conv[o], hd_n], axis=1)[:, -(CONV_W - 1):])
            cv_s.append(zs[1].reshape(n_tok, n_s, SLAB).transpose(1, 0, 2))

        rs = _ffn_ple(i, rs, ps, *ffn_args, ts, tf)

    y_prompt = rp.reshape(n_p, seq, D_MODEL)
    y_sample = rs.reshape(n_tok, n_s, D_MODEL).transpose(1, 0, 2)
    st = lambda lst: jnp.stack(lst, axis=0)
    return (y_prompt, y_sample, st(kv_p[0]), st(kv_p[1]), st(kv_p[2]),
            st(kv_s[0]), st(kv_s[1]), st(kv_s[2]),
            st(pool_p), st(pool_s), st(conv_p), st(conv_s), st(cv_s))
```

```python
import functools
import math

import jax
import jax.numpy as jnp
from jax import lax
from jax.experimental import pallas as pl
from jax.experimental.pallas import tpu as pltpu

F32 = jnp.float32
BF16 = jnp.bfloat16

D_MODEL = 1024
D_FF = 4 * D_MODEL
PLE_DIM = 256
EPS = 1e-6
ROPE_THETA = 10000.0
PAST_LEN = 2048

SLAB = 512
POOL_WINDOWS = (2, 4, 8, 16)
POOL_GROUP = 128
POOL_STATE = 15
POOL_HALO = 16
DIL_CFG = ((128, 1), (512, 4), (2048, 16))
HEADS = 8
HEAD_DIM = 64
ATT_BLOCK = 128
ATT_ROWS = 512
CHUNK = 128
C_GROUPS = 4
CONV_W = 3
CONV_HALO = 8
LANES = 128
NEG = -1e30

VMEM_LIMIT = 52 * 1024 * 1024
HOSTING_VMEM_LIMIT = 60 * 1024 * 1024
HOST_SPLITS = 4
Q_ROWS = 8


def _params(*sem):
    return pltpu.CompilerParams(dimension_semantics=sem, vmem_limit_bytes=VMEM_LIMIT)


def _rms(x, g):
    ms = jnp.mean(x * x, axis=-1, keepdims=True)
    return x * lax.rsqrt(ms + EPS) * g


def _gelu(x):
    c = math.sqrt(2.0 / math.pi)
    return 0.5 * x * (1.0 + jnp.tanh(c * (x + 0.044715 * (x * x * x))))


def _proj_even_kernel(x_ref, g_ref, w_ref, cos_ref, sin_ref, za_ref, g0_ref, g1_ref, g2_ref,
                      zs_ref, *, dils, split):
    tm = x_ref.shape[0]
    nl = SLAB // LANES
    hn = _rms(x_ref[...], g_ref[...]).astype(BF16)
    za_ref[...] = jnp.dot(hn, w_ref[:, 0:SLAB], preferred_element_type=F32)

    def rotate(chunks):
        cos = cos_ref[...]
        sin = sin_ref[...]
        if split:
            h = nl // 2
            return ([chunks[i] * cos - chunks[i + h] * sin for i in range(h)]
                    + [chunks[i] * cos + chunks[i - h] * sin for i in range(h, nl)])
        lane = lax.broadcasted_iota(jnp.int32, cos.shape, 1)
        first_half = jnp.bitwise_and(lane, HEAD_DIM - 1) < (HEAD_DIM // 2)
        out = []
        for zc in chunks:
            partner = jnp.where(first_half,
                                pltpu.roll(zc, LANES - HEAD_DIM // 2, 1),
                                pltpu.roll(zc, HEAD_DIM // 2, 1))
            out.append(zc * cos + partner * sin)
        return out

    slot = 0
    for g, (out_ref, dil) in enumerate(zip((g0_ref, g1_ref, g2_ref), dils)):
        for c in range(3):
            col0 = (1 + 3 * g + c) * SLAB
            z = jnp.dot(hn, w_ref[:, col0:col0 + SLAB], preferred_element_type=F32)
            chunks = [z[:, i * LANES:(i + 1) * LANES] for i in range(nl)]
            if c < 2:
                chunks = rotate(chunks)
            for i, zc in enumerate(chunks):
                cols = slice(i * LANES, (i + 1) * LANES)
                if dil == 1:
                    out_ref[c, :, cols] = zc
                else:
                    zs_ref[slot] = zc
                    for r in range(dil):
                        out_ref[c, r, :, cols] = zs_ref[slot, pl.ds(r, tm // dil, stride=dil), :]
                    slot += 1


def _resident(shape):
    return pl.BlockSpec(shape, lambda *_: (0,) * len(shape), pipeline_mode=pl.Buffered(1))


def _proj_even(x, g, w, cos, sin, tm, n_seq, dils, split):
    T = x.shape[0]
    ntab = cos.shape[0] // tm
    tps = T // n_seq // tm

    def group_spec(dil):
        if dil == 1:
            return pl.BlockSpec((3, tm, SLAB), lambda i: (0, i, 0))
        return pl.BlockSpec((3, None, dil, tm // dil, SLAB), lambda i: (0, i // tps, 0, i % tps, 0))

    def group_shape(dil):
        if dil == 1:
            return jax.ShapeDtypeStruct((3, T, SLAB), F32)
        return jax.ShapeDtypeStruct((3, n_seq, dil, T // n_seq // dil, SLAB), F32)

    n_slots = max(1, 3 * (SLAB // LANES) * sum(d > 1 for d in dils))
    return pl.pallas_call(
        functools.partial(_proj_even_kernel, dils=dils, split=split),
        grid=(T // tm,),
        in_specs=[
            pl.BlockSpec((tm, D_MODEL), lambda i: (i, 0)),
            _resident((1, D_MODEL)),
            _resident(w.shape),
            pl.BlockSpec((tm, LANES), lambda i: (i % ntab, 0)),
            pl.BlockSpec((tm, LANES), lambda i: (i % ntab, 0)),
        ],
        out_specs=[pl.BlockSpec((tm, SLAB), lambda i: (i, 0))] + [group_spec(d) for d in dils],
        out_shape=[jax.ShapeDtypeStruct((T, SLAB), F32)] + [group_shape(d) for d in dils],
        scratch_shapes=[pltpu.VMEM((n_slots, tm, LANES), F32)],
        compiler_params=_params("arbitrary"),
        name="proj_even",
    )(x, g, w, cos, sin)


def _proj_odd_kernel(x_ref, g_ref, w_ref, lng_ref, lnb_ref, o_ref):
    hn = _rms(x_ref[...], g_ref[...]).astype(BF16)

    def slab(s):
        return jnp.dot(hn, w_ref[:, s * SLAB:(s + 1) * SLAB], preferred_element_type=F32)

    o_ref[0] = _gelu(slab(0))
    zv = slab(1)
    for c in range(C_GROUPS):
        sl = slice(c * LANES, (c + 1) * LANES)
        v = _gelu(zv[:, sl])
        mu = jnp.mean(v, axis=-1, keepdims=True)
        dv = v - mu
        var = jnp.mean(dv * dv, axis=-1, keepdims=True)
        o_ref[1, :, sl] = dv * lax.rsqrt(var + EPS) * lng_ref[:, sl] + lnb_ref[:, sl]
    o_ref[2] = slab(2)
    o_ref[3] = slab(3) * slab(4)


def _proj_odd(x, g, w, ln_g, ln_b, tm):
    T = x.shape[0]
    return pl.pallas_call(
        _proj_odd_kernel,
        grid=(T // tm,),
        in_specs=[
            pl.BlockSpec((tm, D_MODEL), lambda i: (i, 0)),
            _resident((1, D_MODEL)),
            _resident(w.shape),
            _resident((1, SLAB)),
            _resident((1, SLAB)),
        ],
        out_specs=pl.BlockSpec((4, tm, SLAB), lambda i: (0, i, 0)),
        out_shape=jax.ShapeDtypeStruct((4, T, SLAB), F32),
        compiler_params=_params("arbitrary"),
        name="proj_odd",
    )(x, g, w, ln_g, ln_b)


def _attn_prompt_kernel(q_ref, kp_ref, kc_ref, vp_ref, vc_ref, o_ref, lse_ref,
                        qst_ref, ks_ref, vs_ref, bias_ref, s_ref, e_ref, mx_ref):
    step = pl.program_id(2)
    att_rows = q_ref.shape[0]
    sub = att_rows // ATT_BLOCK
    half_rot = HEAD_DIM // 2
    heads_per_blk = LANES // half_rot
    n_blk = HEADS // heads_per_blk
    log2e = 1.4426950408889634
    ln2 = 0.6931471805599453
    scale = HEAD_DIM ** -0.5 * log2e

    qlane = lax.broadcasted_iota(jnp.int32, (att_rows, 2 * LANES), 1)
    head_in_blk = jnp.bitwise_and(qlane, LANES - 1) // half_rot
    for j in range(n_blk):
        lo = slice(j * LANES, (j + 1) * LANES)
        hi = slice((n_blk + j) * LANES, (n_blk + j + 1) * LANES)
        dst = slice(2 * j * LANES, (2 * j + 2) * LANES)
        q2 = (jnp.concatenate([q_ref[:, lo], q_ref[:, hi]], axis=1) * scale).astype(BF16)
        for hq in range(heads_per_blk):
            qm = jnp.where(head_in_blk == hq, q2, jnp.zeros_like(q2))
            for b in range(sub):
                r0 = ((j * sub + b) * heads_per_blk + hq) * ATT_BLOCK
                qst_ref[r0:r0 + ATT_BLOCK, :] = qm[b * ATT_BLOCK:(b + 1) * ATT_BLOCK, :]
        ks_ref[0:ATT_BLOCK, dst] = jnp.concatenate([kp_ref[:, lo], kp_ref[:, hi]],
                                                   axis=1).astype(BF16)
        ks_ref[ATT_BLOCK:, dst] = jnp.concatenate([kc_ref[:, lo], kc_ref[:, hi]],
                                                  axis=1).astype(BF16)
    for hp in range(HEADS // 2):
        src = slice(hp * LANES, (hp + 1) * LANES)
        vs_ref[0:ATT_BLOCK, 2 * hp * LANES:(2 * hp + 1) * LANES] = vp_ref[:, src].astype(BF16)
        vs_ref[ATT_BLOCK:, 2 * hp * LANES:(2 * hp + 1) * LANES] = vc_ref[:, src].astype(BF16)
        vs_ref[:, (2 * hp + 1) * LANES:(2 * hp + 2) * LANES] = jnp.ones(
            (att_rows + ATT_BLOCK, LANES), BF16)

    qi = lax.broadcasted_iota(jnp.int32, (ATT_BLOCK, 2 * ATT_BLOCK), 0)
    ki = lax.broadcasted_iota(jnp.int32, (ATT_BLOCK, 2 * ATT_BLOCK), 1)
    rel = qi + ATT_BLOCK - ki
    band = jnp.logical_and(rel >= 0, rel <= ATT_BLOCK)
    bias_ref[0] = jnp.where(band, 0.0, NEG)
    bias_ref[1] = jnp.where(jnp.logical_and(band, ki >= ATT_BLOCK), 0.0, NEG)
    lane = lax.broadcasted_iota(jnp.int32, (ATT_BLOCK, LANES), 1)
    low_head = lane < HEAD_DIM
    blk_rows = heads_per_blk * ATT_BLOCK

    first = (step == 0).astype(jnp.int32)
    hb = ATT_BLOCK // 2
    lane_hb = lax.broadcasted_iota(jnp.int32, (hb, LANES), 1)

    def scores(b):
        for j in range(n_blk):
            base = (j * sub + b) * blk_rows
            k2 = ks_ref[b * ATT_BLOCK:(b + 2) * ATT_BLOCK, 2 * j * LANES:(2 * j + 2) * LANES]
            s_ref[b, j * blk_rows:(j + 1) * blk_rows, :] = lax.dot_general(
                qst_ref[base:base + blk_rows, :], k2, (((1,), (1,)), ((), ())),
                preferred_element_type=F32)

    def softmax(b):
        which = first if b == 0 else 0
        for rh in range(2):
            m_tile = jnp.zeros((hb, LANES), F32)
            for h in range(HEADS):
                rows = slice(h * ATT_BLOCK + rh * hb, h * ATT_BLOCK + (rh + 1) * hb)
                s = s_ref[b, rows, :] + bias_ref[which, rh * hb:(rh + 1) * hb, :]
                m = jnp.max(s, axis=-1, keepdims=True)
                e_ref[b, rows, :] = jnp.exp2((s - m).astype(BF16))
                m_tile = jnp.where(lane_hb == h, m, m_tile)
            mx_ref[b, rh * hb:(rh + 1) * hb, :] = m_tile

    def weighted_values(b):
        log_l = jnp.zeros((ATT_BLOCK, LANES), F32)
        for hp in range(HEADS // 2):
            v_ones = vs_ref[b * ATT_BLOCK:(b + 2) * ATT_BLOCK, 2 * hp * LANES:(2 * hp + 2) * LANES]
            pv = jnp.dot(e_ref[b, 2 * hp * ATT_BLOCK:(2 * hp + 2) * ATT_BLOCK, :], v_ones,
                         preferred_element_type=F32)
            l_lo, l_hi = pv[0:ATT_BLOCK, LANES:], pv[ATT_BLOCK:, LANES:]
            o_ref[b * ATT_BLOCK:(b + 1) * ATT_BLOCK, hp * LANES:(hp + 1) * LANES] = jnp.where(
                low_head, pv[0:ATT_BLOCK, 0:LANES] * (1.0 / l_lo),
                pv[ATT_BLOCK:, 0:LANES] * (1.0 / l_hi))
            log_l = jnp.where(lane == 2 * hp, jnp.log(l_lo),
                              jnp.where(lane == 2 * hp + 1, jnp.log(l_hi), log_l))
        lse_ref[b * ATT_BLOCK:(b + 1) * ATT_BLOCK, :] = mx_ref[b] * ln2 + log_l

    for t in range(sub + 2):
        if t < sub:
            scores(t)
        if 0 <= t - 1 < sub:
            softmax(t - 1)
        if 0 <= t - 2 < sub:
            weighted_values(t - 2)


def _attn_prompt(qkv, dil):
    _, n_seq, _, m_rows, _ = qkv.shape
    att_rows = min(ATT_ROWS, m_rows)
    steps = m_rows // att_rows
    sub = att_rows // ATT_BLOCK

    def cur(slab):
        return pl.BlockSpec((None, None, None, att_rows, SLAB), lambda n, r, b: (slab, n, r, b, 0))

    def prev(slab):
        return pl.BlockSpec((None, None, None, ATT_BLOCK, SLAB),
                            lambda n, r, b: (slab, n, r, jnp.maximum(b * sub - 1, 0), 0))

    return pl.pallas_call(
        _attn_prompt_kernel,
        grid=(n_seq, dil, steps),
        in_specs=[cur(0), prev(1), cur(1), prev(2), cur(2)],
        out_specs=[
            pl.BlockSpec((None, None, att_rows, SLAB), lambda n, r, b: (n, r, b, 0)),
            pl.BlockSpec((None, None, att_rows, LANES), lambda n, r, b: (n, r, b, 0)),
        ],
        out_shape=[
            jax.ShapeDtypeStruct((n_seq, dil, m_rows, SLAB), F32),
            jax.ShapeDtypeStruct((n_seq, dil, m_rows, LANES), F32),
        ],
        scratch_shapes=[
            pltpu.VMEM((HEADS * att_rows, 2 * LANES), BF16),
            pltpu.VMEM((att_rows + ATT_BLOCK, SLAB), BF16),
            pltpu.VMEM((att_rows + ATT_BLOCK, 2 * SLAB), BF16),
            pltpu.VMEM((2, ATT_BLOCK, 2 * ATT_BLOCK), F32),
            pltpu.VMEM((sub, HEADS * ATT_BLOCK, 2 * ATT_BLOCK), F32),
            pltpu.VMEM((sub, HEADS * ATT_BLOCK, 2 * ATT_BLOCK), BF16),
            pltpu.VMEM((sub, ATT_BLOCK, LANES), F32),
        ],
        compiler_params=_params("arbitrary", "arbitrary", "arbitrary"),
        name=f"attn_prompt_d{dil}",
    )(qkv, qkv, qkv, qkv, qkv)


def _kv_tail_kernel(k_ref, v_ref, o_ref, tok_ref):
    dil, rows, _ = k_ref.shape
    keep = dil * rows
    nl = SLAB // LANES
    half_rot = HEAD_DIM // 2
    heads_per_chunk = LANES // half_rot
    for kv, src in enumerate((k_ref, v_ref)):
        for lc in range(nl):
            for r in range(dil):
                tok_ref[lc, pl.ds(r, rows, stride=dil), :] = src[r, :, lc * LANES:(lc + 1) * LANES]
            for pc in range(keep // LANES):
                pcols = slice(pc * LANES, (pc + 1) * LANES)
                t = tok_ref[lc, pcols, :].T
                if kv == 0:
                    half, hblk = divmod(lc, nl // 2)
                    for hq in range(heads_per_chunk):
                        r0 = (hblk * heads_per_chunk + hq) * HEAD_DIM + half * half_rot
                        o_ref[0, r0:r0 + half_rot, pcols] = t[hq * half_rot:(hq + 1) * half_rot, :]
                else:
                    o_ref[1, lc * LANES:(lc + 1) * LANES, pcols] = t


def _kv_tail(grp, keep):
    _, n_seq, dil, m_rows, _ = grp.shape
    rows = keep // dil
    last = m_rows // rows - 1

    def spec(slab):
        return pl.BlockSpec((None, None, dil, rows, SLAB), lambda n: (slab, n, 0, last, 0))

    return pl.pallas_call(
        _kv_tail_kernel,
        grid=(n_seq,),
        in_specs=[spec(1), spec(2)],
        out_specs=pl.BlockSpec((None, 2, SLAB, keep), lambda n: (n, 0, 0, 0)),
        out_shape=jax.ShapeDtypeStruct((n_seq, 2, SLAB, keep), F32),
        scratch_shapes=[pltpu.VMEM((SLAB // LANES, keep, LANES), F32)],
        compiler_params=_params("arbitrary"),
        name=f"kv_tail_d{dil}",
    )(grp, grp)


def _pool_mix(window_terms, a_cols, cnt, pw_ref, scale_ref, gi):
    acc = a_cols
    for term in window_terms:
        acc = acc + term
    pooled = acc / cnt - a_cols
    cols = slice(gi * POOL_GROUP, (gi + 1) * POOL_GROUP)
    mixed = jnp.dot(pooled.astype(BF16), pw_ref[gi], preferred_element_type=F32)
    return mixed * scale_ref[:, cols]


def _even_out_prompt_kernel(a_ref, halo_ref, o0_ref, o1_ref, o2_ref, l0_ref, l1_ref, l2_ref,
                            pw_ref, ps_ref, wo_ref, x_ref, out_ref, ext_ref, ya_ref, yb_ref,
                            oi_ref, li_ref, *, tiles_per_seq):
    tm = a_ref.shape[0]
    it = pl.program_id(0) % tiles_per_seq
    halo = halo_ref[...]
    ext_ref[0:POOL_HALO, :] = jnp.where(it == 0, jnp.zeros_like(halo), halo)
    ext_ref[POOL_HALO:, :] = a_ref[...]
    pos = it * tm + lax.broadcasted_iota(jnp.int32, (tm, 1), 0)
    for gi, w in enumerate(POOL_WINDOWS):
        cols = slice(gi * POOL_GROUP, (gi + 1) * POOL_GROUP)
        terms = [ext_ref[POOL_HALO - k:POOL_HALO - k + tm, cols] for k in range(1, w)]
        cnt = jnp.minimum(w, pos + 1).astype(F32)
        ya_ref[:, cols] = _pool_mix(terms, a_ref[:, cols], cnt, pw_ref, ps_ref, gi).astype(BF16)

    for g, (o_ref, l_ref) in enumerate(((o0_ref, l0_ref), (o1_ref, l1_ref), (o2_ref, l2_ref))):
        dil = o_ref.shape[0]
        for r in range(dil):
            rows = pl.ds(r, tm // dil, stride=dil)
            li_ref[g, rows, :] = l_ref[r]
            for lc in range(SLAB // LANES):
                oi_ref[g, lc, rows, :] = o_ref[r, :, lc * LANES:(lc + 1) * LANES]

    l0, l1, l2 = li_ref[0], li_ref[1], li_ref[2]
    mx = jnp.maximum(jnp.maximum(l0, l1), l2)
    e0, e1, e2 = jnp.exp(l0 - mx), jnp.exp(l1 - mx), jnp.exp(l2 - mx)
    inv = 1.0 / (e0 + e1 + e2)
    w0, w1 = e0 * inv, e1 * inv
    for h in range(HEADS):
        lc, lo = divmod(h * HEAD_DIM, LANES)
        sl = slice(lo, lo + HEAD_DIM)
        o2 = oi_ref[2, lc, :, sl]
        yb = (o2 + w0[:, h:h + 1] * (oi_ref[0, lc, :, sl] - o2)
              + w1[:, h:h + 1] * (oi_ref[1, lc, :, sl] - o2))
        yb_ref[:, h * HEAD_DIM:(h + 1) * HEAD_DIM] = yb.astype(BF16)

    y = jnp.dot(ya_ref[...], wo_ref[0:SLAB, :], preferred_element_type=F32)
    y = y + jnp.dot(yb_ref[...], wo_ref[SLAB:, :], preferred_element_type=F32)
    out_ref[...] = x_ref[...] + y


def _even_out_prompt(za, attn, pool_w, pool_scale, w_out, x, seq, tm):
    T = x.shape[0]
    tps = seq // tm
    hb = tm // POOL_HALO

    def residue_spec(dil, width):
        return pl.BlockSpec((None, dil, tm // dil, width), lambda i: (i // tps, 0, i % tps, 0))

    dils = [o.shape[1] for o, _ in attn]
    return pl.pallas_call(
        functools.partial(_even_out_prompt_kernel, tiles_per_seq=tps),
        grid=(T // tm,),
        in_specs=[
            pl.BlockSpec((tm, SLAB), lambda i: (i, 0)),
            pl.BlockSpec((POOL_HALO, SLAB), lambda i: (jnp.maximum(i * hb - 1, 0), 0)),
        ]
        + [residue_spec(d, SLAB) for d in dils] + [residue_spec(d, LANES) for d in dils]
        + [
            pl.BlockSpec((len(POOL_WINDOWS), POOL_GROUP, POOL_GROUP), lambda i: (0, 0, 0)),
            pl.BlockSpec((1, SLAB), lambda i: (0, 0)),
            pl.BlockSpec((2 * SLAB, D_MODEL), lambda i: (0, 0)),
            pl.BlockSpec((tm, D_MODEL), lambda i: (i, 0)),
        ],
        out_specs=pl.BlockSpec((tm, D_MODEL), lambda i: (i, 0)),
        out_shape=jax.ShapeDtypeStruct((T, D_MODEL), F32),
        scratch_shapes=[
            pltpu.VMEM((tm + POOL_HALO, SLAB), F32),
            pltpu.VMEM((tm, SLAB), BF16),
            pltpu.VMEM((tm, SLAB), BF16),
            pltpu.VMEM((len(dils), SLAB // LANES, tm, LANES), F32),
            pltpu.VMEM((len(dils), tm, LANES), F32),
        ],
        compiler_params=_params("arbitrary"),
        name="even_out_prompt",
    )(za, za, *[o for o, _ in attn], *[l for _, l in attn], pool_w, pool_scale, w_out, x)


def _odd_mix(u_ref, vn_ref, go_ref, hd_ref, hdh_ref, ws_ref, bs_ref, cw_ref, wo_ref, x_ref,
             ext_ref, yc_ref, yd_ref, tiles_per_seq):
    tm = u_ref.shape[0]
    it = pl.program_id(0) % tiles_per_seq

    ti = lax.broadcasted_iota(jnp.int32, (CHUNK, CHUNK), 0)
    si = lax.broadcasted_iota(jnp.int32, (CHUNK, CHUNK), 1)
    for g in range(C_GROUPS):
        cols = slice(g * LANES, (g + 1) * LANES)
        wm = jnp.where(si <= ti, ws_ref[g], 0.0).astype(BF16)
        for c in range(tm // CHUNK):
            rows = slice(c * CHUNK, (c + 1) * CHUNK)
            sp = jnp.dot(wm, vn_ref[rows, cols].astype(BF16), preferred_element_type=F32)
            sp = sp + bs_ref[:, cols]
            yc_ref[rows, cols] = (u_ref[rows, cols] * sp).astype(BF16)

    hd = hd_ref[...]
    halo = hdh_ref[...]
    ext_ref[0:CONV_HALO, :] = jnp.where(it == 0, jnp.zeros_like(halo), halo)
    ext_ref[CONV_HALO:, :] = hd
    conv = cw_ref[CONV_W - 1:CONV_W, :] * hd
    for j in range(CONV_W - 1):
        off = CONV_HALO - (CONV_W - 1) + j
        conv = conv + cw_ref[j:j + 1, :] * ext_ref[off:off + tm, :]
    yd_ref[...] = (go_ref[...] * conv).astype(BF16)

    y = jnp.dot(yc_ref[...], wo_ref[0:SLAB, :], preferred_element_type=F32)
    y = y + jnp.dot(yd_ref[...], wo_ref[SLAB:, :], preferred_element_type=F32)
    return x_ref[...] + y


def _odd_out_ffn_kernel(u_ref, vn_ref, go_ref, hd_ref, hdh_ref, ws_ref, bs_ref, cw_ref, wo_ref,
                        x_ref, p_ref, gf_ref, w1_ref, w2_ref, gp_ref, wg_ref, wp_ref, gl_ref,
                        out_ref, ext_ref, yc_ref, yd_ref, *, tiles_per_seq, final_norm, tf):
    r = _odd_mix(u_ref, vn_ref, go_ref, hd_ref, hdh_ref, ws_ref, bs_ref, cw_ref, wo_ref, x_ref,
                 ext_ref, yc_ref, yd_ref, tiles_per_seq)
    out_ref[...] = _mlp_ple(r, p_ref, gf_ref, w1_ref, w2_ref, gp_ref, wg_ref, wp_ref, gl_ref,
                            final_norm, tf)


def _odd_out_ffn(layer, z, ws, bs_rows, conv_w, w_out, x, seq, p, g_ffn, w1, w2, g_ple, wg, wp,
                 g_last, final_norm, tm, tf):
    T = x.shape[0]
    tiles_per_seq = seq // tm
    hb = tm // CONV_HALO

    def slab(s):
        return pl.BlockSpec((None, tm, SLAB), lambda i: (s, i, 0))

    return pl.pallas_call(
        functools.partial(_odd_out_ffn_kernel, tiles_per_seq=tiles_per_seq,
                          final_norm=final_norm, tf=tf),
        grid=(T // tm,),
        in_specs=[
            slab(0), slab(1), slab(2), slab(3),
            pl.BlockSpec((None, CONV_HALO, SLAB), lambda i: (3, jnp.maximum(i * hb - 1, 0), 0)),
            _resident((C_GROUPS, CHUNK, CHUNK)),
            _resident((CHUNK, SLAB)),
            _resident((CONV_W, SLAB)),
            _resident((2 * SLAB, D_MODEL)),
            pl.BlockSpec((tm, D_MODEL), lambda i: (i, 0)),
            pl.BlockSpec((None, tm, PLE_DIM), lambda i: (layer, i, 0)),
            _layer_resident(layer, (1, D_MODEL)),
            _layer_resident(layer, (D_MODEL, D_FF)),
            _layer_resident(layer, (D_FF, D_MODEL)),
            _layer_resident(layer, (1, D_MODEL)),
            _layer_resident(layer, (D_MODEL, D_MODEL)),
            _layer_resident(layer, (PLE_DIM, D_MODEL)),
            _resident((1, D_MODEL)),
        ],
        out_specs=pl.BlockSpec((tm, D_MODEL), lambda i: (i, 0)),
        out_shape=jax.ShapeDtypeStruct((T, D_MODEL), F32),
        scratch_shapes=[
            pltpu.VMEM((tm + CONV_HALO, SLAB), F32),
            pltpu.VMEM((tm, SLAB), BF16),
            pltpu.VMEM((tm, SLAB), BF16),
        ],
        compiler_params=_params("arbitrary"),
        name="odd_out_ffn",
    )(z, z, z, z, z, ws, bs_rows, conv_w, w_out, x, p, g_ffn, w1, w2, g_ple, wg, wp, g_last)


def _mlp_ple(x, p_ref, gf_ref, w1_ref, w2_ref, gp_ref, wg_ref, wp_ref, gl_ref, final_norm, tf):
    hn = _rms(x, gf_ref[...]).astype(BF16)
    acc = None
    for c in range(D_FF // tf):
        h1 = jnp.dot(hn, w1_ref[:, c * tf:(c + 1) * tf], preferred_element_type=F32)
        h1 = jnp.square(jnp.maximum(h1, 0.0)).astype(BF16)
        part = jnp.dot(h1, w2_ref[c * tf:(c + 1) * tf, :], preferred_element_type=F32)
        acc = part if acc is None else acc + part
    r = x + acc
    hp = _rms(r, gp_ref[...]).astype(BF16)
    gate = jax.nn.sigmoid(jnp.dot(hp, wg_ref[...], preferred_element_type=F32))
    proj = jnp.dot(p_ref[...].astype(BF16), wp_ref[...], preferred_element_type=F32)
    r = r + gate * proj
    if final_norm:
        r = _rms(r, gl_ref[...])
    return r


def _ffn_ple_kernel(x_ref, p_ref, gf_ref, w1_ref, w2_ref, gp_ref, wg_ref, wp_ref, gl_ref,
                    out_ref, *, final_norm, tf):
    out_ref[...] = _mlp_ple(x_ref[...], p_ref, gf_ref, w1_ref, w2_ref, gp_ref, wg_ref, wp_ref,
                            gl_ref, final_norm, tf)


def _layer_resident(layer, shape):
    return pl.BlockSpec((None,) + shape, lambda *_: (layer,) + (0,) * len(shape),
                        pipeline_mode=pl.Buffered(1))


def _ffn_ple(layer, x, p, g_ffn, w1, w2, g_ple, wg, wp, g_last, final_norm, tm, tf):
    T = x.shape[0]
    return pl.pallas_call(
        functools.partial(_ffn_ple_kernel, final_norm=final_norm, tf=tf),
        grid=(T // tm,),
        in_specs=[
            pl.BlockSpec((tm, D_MODEL), lambda i: (i, 0)),
            pl.BlockSpec((None, tm, PLE_DIM), lambda i: (layer, i, 0)),
            _layer_resident(layer, (1, D_MODEL)),
            _layer_resident(layer, (D_MODEL, D_FF)),
            _layer_resident(layer, (D_FF, D_MODEL)),
            _layer_resident(layer, (1, D_MODEL)),
            _layer_resident(layer, (D_MODEL, D_MODEL)),
            _layer_resident(layer, (PLE_DIM, D_MODEL)),
            _resident((1, D_MODEL)),
        ],
        out_specs=pl.BlockSpec((tm, D_MODEL), lambda i: (i, 0)),
        out_shape=jax.ShapeDtypeStruct((T, D_MODEL), F32),
        compiler_params=_params("arbitrary"),
        name="ffn_ple",
    )(x, p, g_ffn, w1, w2, g_ple, wg, wp, g_last)


def _ffn_ple_hosting_kernel(x_ref, p_ref, gf_ref, w1_ref, w2_ref, gp_ref, wg_ref, wp_ref, gl_ref,
                            q0_ref, q1_ref, q2_ref, c0_ref, c1_ref, c2_ref, out_ref, pacc_ref,
                            pml_ref, hn_ref, acc_ref, *, final_norm, tf, splits):
    j = pl.program_id(1)
    share = D_FF // splits

    def mlp_share(k):
        acc = None
        for c in range(share // tf):
            cols = slice(k * share + c * tf, k * share + (c + 1) * tf)
            h1 = jnp.dot(hn_ref[...], w1_ref[:, cols], preferred_element_type=F32)
            h1 = jnp.square(jnp.maximum(h1, 0.0)).astype(BF16)
            part = jnp.dot(h1, w2_ref[cols, :], preferred_element_type=F32)
            acc = part if acc is None else acc + part
        return acc

    @pl.when(j == 0)
    def _():
        hn_ref[...] = _rms(x_ref[...], gf_ref[...]).astype(BF16)
        acc_ref[...] = mlp_share(0)

    for k in range(1, splits - 1):
        @pl.when(j == k)
        def _(k=k):
            acc_ref[...] += mlp_share(k)

    @pl.when(j == splits - 1)
    def _():
        r = x_ref[...] + (acc_ref[...] + mlp_share(splits - 1))
        hp = _rms(r, gp_ref[...]).astype(BF16)
        gate = jax.nn.sigmoid(jnp.dot(hp, wg_ref[...], preferred_element_type=F32))
        proj = jnp.dot(p_ref[...].astype(BF16), wp_ref[...], preferred_element_type=F32)
        r = r + gate * proj
        if final_norm:
            r = _rms(r, gl_ref[...])
        out_ref[...] = r

    row = (splits * pl.program_id(0) + j) % Q_ROWS
    q_rows = [[q_ref[t, pl.ds(row, 1), :] for t in range(q_ref.shape[0])]
              for q_ref in (q0_ref, q1_ref, q2_ref)]
    _partial_cached_attention(q_rows, (c0_ref, c1_ref, c2_ref), pacc_ref, pml_ref)


def _ffn_ple_hosting(layer, x, p, g_ffn, w1, w2, g_ple, wg, wp, g_last, final_norm, tf,
                     qkv, caches, splits):
    T = x.shape[0]
    _, n_tok, n_seq, _ = qkv[0].shape
    tm = splits * T // n_seq
    seq_of = lambda i, j: splits * i + j

    def cache_spec(c):
        return pl.BlockSpec((None, 2, SLAB, c.shape[-1]), lambda i, j: (seq_of(i, j), 0, 0, 0))

    return pl.pallas_call(
        functools.partial(_ffn_ple_hosting_kernel, final_norm=final_norm, tf=tf, splits=splits),
        grid=(T // tm, splits),
        in_specs=[
            pl.BlockSpec((tm, D_MODEL), lambda i, j: (i, 0)),
            pl.BlockSpec((None, tm, PLE_DIM), lambda i, j: (layer, i, 0)),
            _layer_resident(layer, (1, D_MODEL)),
            _layer_resident(layer, (D_MODEL, D_FF)),
            _layer_resident(layer, (D_FF, D_MODEL)),
            _layer_resident(layer, (1, D_MODEL)),
            _layer_resident(layer, (D_MODEL, D_MODEL)),
            _layer_resident(layer, (PLE_DIM, D_MODEL)),
            _resident((1, D_MODEL)),
        ] + [pl.BlockSpec((None, n_tok, Q_ROWS, SLAB),
                          lambda i, j: (0, 0, seq_of(i, j) // Q_ROWS, 0)) for _ in qkv]
        + [cache_spec(c) for c in caches],
        out_specs=[
            pl.BlockSpec((tm, D_MODEL), lambda i, j: (i, 0)),
            pl.BlockSpec((None, n_tok * HEADS, SLAB), lambda i, j: (seq_of(i, j), 0, 0)),
            pl.BlockSpec((None, n_tok * HEADS, LANES), lambda i, j: (seq_of(i, j), 0, 0)),
        ],
        out_shape=[
            jax.ShapeDtypeStruct((T, D_MODEL), F32),
            jax.ShapeDtypeStruct((n_seq, n_tok * HEADS, SLAB), F32),
            jax.ShapeDtypeStruct((n_seq, n_tok * HEADS, LANES), F32),
        ],
        scratch_shapes=[pltpu.VMEM((tm, D_MODEL), BF16), pltpu.VMEM((tm, D_MODEL), F32)],
        compiler_params=pltpu.CompilerParams(dimension_semantics=("arbitrary", "arbitrary"),
                                             vmem_limit_bytes=HOSTING_VMEM_LIMIT),
        name="ffn_ple_hosting",
    )(x, p, g_ffn, w1, w2, g_ple, wg, wp, g_last, *qkv, *caches)


def _own_lanes():
    sub = lax.broadcasted_iota(jnp.int32, (HEADS, SLAB), 0)
    lane_head = lax.broadcasted_iota(jnp.int32, (HEADS, SLAB), 1) // HEAD_DIM
    return sub == lane_head


def _block_diag_queries(q_rows):
    own = _own_lanes()
    scale = HEAD_DIM ** -0.5
    return jnp.concatenate(
        [jnp.where(own, jnp.broadcast_to(q * scale, (HEADS, SLAB)), 0.0) for q in q_rows], axis=0)


def _cached_scores(qbd, kt, dil):
    s = jnp.dot(qbd.astype(BF16), kt.astype(BF16), preferred_element_type=F32)
    pos = lax.broadcasted_iota(jnp.int32, s.shape, 1)
    row_tok = lax.broadcasted_iota(jnp.int32, (s.shape[0], 1), 0) // HEADS
    valid = (pos >= row_tok) if dil == 1 else (jnp.bitwise_and(pos, dil - 1) == row_tok)
    return jnp.where(valid, s, NEG)


def _weighted_cached_values(e, vt):
    return lax.dot_general(e.astype(BF16), vt.astype(BF16), (((1,), (1,)), ((), ())),
                           preferred_element_type=F32)


def _partial_cached_attention(q_rows, cache_refs, acc_ref, ml_ref):
    scores = []
    m = None
    for g, (c_ref, (_, dil)) in enumerate(zip(cache_refs, DIL_CFG)):
        s = _cached_scores(_block_diag_queries(q_rows[g]), c_ref[0], dil)
        row_max = jnp.max(s, axis=1, keepdims=True)
        m = row_max if m is None else jnp.maximum(m, row_max)
        scores.append(s)
    acc, l = None, None
    for s, c_ref in zip(scores, cache_refs):
        e = jnp.exp(s - m)
        part = _weighted_cached_values(e, c_ref[1])
        row_sum = jnp.sum(e, axis=1, keepdims=True)
        acc = part if acc is None else acc + part
        l = row_sum if l is None else l + row_sum
    acc_ref[...] = acc
    lane = lax.broadcasted_iota(jnp.int32, ml_ref.shape, 1)
    ml_ref[...] = jnp.where(lane == 0, m, jnp.where(lane == 1, l, 0.0))


def _attn_sample_kernel(g0_ref, g1_ref, g2_ref, pacc_ref, pml_ref, o_ref, *, n_seq):
    groups = (g0_ref, g1_ref, g2_ref)
    n_tok = g0_ref.shape[1]
    n_rows = n_tok * HEADS
    own = _own_lanes()
    row_tok = lax.broadcasted_iota(jnp.int32, (n_rows, 1), 0) // HEADS

    def one_sequence(n, carry):
        m_part = pml_ref[n, :, 0:1]
        m = m_part
        s_new = []
        for g, (g_ref, (_, dil)) in enumerate(zip(groups, DIL_CFG)):
            qbd = _block_diag_queries([g_ref[0, t, pl.ds(n, 1), :] for t in range(n_tok)])
            for tp in range(n_tok):
                sn = jnp.sum(qbd * g_ref[1, tp, pl.ds(n, 1), :], axis=1, keepdims=True)
                ok = (row_tok >= tp) if dil == 1 else (row_tok == tp)
                sn = jnp.where(ok, sn, NEG)
                m = jnp.maximum(m, sn)
                s_new.append((g, tp, sn))
        w_part = jnp.exp(m_part - m)
        acc = w_part * pacc_ref[n]
        den = w_part * pml_ref[n, :, 1:2]
        for g, tp, sn in s_new:
            e = jnp.exp(sn - m)
            den = den + e
            acc = acc + e * groups[g][2, tp, pl.ds(n, 1), :]
        res = acc * (1.0 / den)
        for t in range(n_tok):
            rows = res[t * HEADS:(t + 1) * HEADS, :]
            o_ref[n, t:t + 1, :] = jnp.sum(jnp.where(own, rows, 0.0), axis=0, keepdims=True)
        return carry

    lax.fori_loop(0, n_seq, one_sequence, 0, unroll=4)


def _attn_sample(qkv, pacc, pml, seqs_per_step=16):
    _, n_tok, n_seq, _ = qkv[0].shape
    nb = seqs_per_step

    def spec(*tail):
        return pl.BlockSpec((nb,) + tail, lambda i: (i,) + (0,) * len(tail))

    return pl.pallas_call(
        functools.partial(_attn_sample_kernel, n_seq=nb),
        grid=(n_seq // nb,),
        in_specs=[pl.BlockSpec((3, n_tok, nb, SLAB), lambda i: (0, 0, i, 0)) for _ in qkv]
        + [spec(n_tok * HEADS, SLAB), spec(n_tok * HEADS, LANES)],
        out_specs=spec(n_tok, SLAB),
        out_shape=jax.ShapeDtypeStruct((n_seq, n_tok, SLAB), F32),
        compiler_params=_params("arbitrary"),
        name="attn_sample",
    )(*qkv, pacc, pml)


def _even_out_sample_kernel(a_ref, ctx_ref, yb_ref, pw_ref, ps_ref, wo_ref, x_ref, out_ref,
                            *, n_seq, n_tok):
    def ext_row(e, cols):
        if e >= POOL_STATE:
            t = e - POOL_STATE
            return a_ref[t * n_seq:(t + 1) * n_seq, cols]
        return ctx_ref[e, :, cols]

    for t in range(n_tok):
        rows = slice(t * n_seq, (t + 1) * n_seq)
        ya = []
        for gi, w in enumerate(POOL_WINDOWS):
            cols = slice(gi * POOL_GROUP, (gi + 1) * POOL_GROUP)
            terms = [ext_row(POOL_STATE + t - k, cols) for k in range(1, w)]
            ya.append(_pool_mix(terms, a_ref[rows, cols], float(w), pw_ref, ps_ref, gi))
        y = jnp.zeros((n_seq, D_MODEL), F32)
        for gi in range(len(POOL_WINDOWS)):
            y = y + jnp.dot(ya[gi].astype(BF16), wo_ref[gi * POOL_GROUP:(gi + 1) * POOL_GROUP, :],
                            preferred_element_type=F32)
        yb = yb_ref[:, t * SLAB:(t + 1) * SLAB].astype(BF16)
        y = y + jnp.dot(yb, wo_ref[SLAB:, :], preferred_element_type=F32)
        out_ref[rows, :] = x_ref[rows, :] + y


def _even_out_sample(z, ctx, yb, pool_w, pool_scale, w_out, x, n_seq, n_tok):
    T = x.shape[0]
    full = lambda *shape: pl.BlockSpec(shape, lambda i: (0,) * len(shape))
    return pl.pallas_call(
        functools.partial(_even_out_sample_kernel, n_seq=n_seq, n_tok=n_tok),
        grid=(1,),
        in_specs=[
            full(T, SLAB),
            full(POOL_STATE, n_seq, SLAB),
            full(n_seq, n_tok * SLAB),
            full(len(POOL_WINDOWS), POOL_GROUP, POOL_GROUP),
            full(1, SLAB),
            full(2 * SLAB, D_MODEL),
            full(T, D_MODEL),
        ],
        out_specs=full(T, D_MODEL),
        out_shape=jax.ShapeDtypeStruct((T, D_MODEL), F32),
        compiler_params=_params("arbitrary"),
        name="even_out_sample",
    )(z, ctx, yb, pool_w, pool_scale, w_out, x)


def _odd_out_sample_kernel(z_ref, ctx_ref, coef_ref, bias_ref, cw_ref, wo_ref, x_ref,
                           out_ref, *, n_seq, n_tok, mix_terms):
    def ext_row(e):
        if e >= CONV_W - 1:
            t = e - (CONV_W - 1)
            return z_ref[3, t * n_seq:(t + 1) * n_seq, :]
        return ctx_ref[:, e * SLAB:(e + 1) * SLAB]

    for t in range(n_tok):
        rows = slice(t * n_seq, (t + 1) * n_seq)
        sp = jnp.zeros((n_seq, SLAB), F32) + bias_ref[t:t + 1, :]
        for s in mix_terms[t]:
            r = t * n_tok + s
            sp = sp + coef_ref[r:r + 1, :] * z_ref[1, s * n_seq:(s + 1) * n_seq, :]
        yc = z_ref[0, rows, :] * sp
        conv = jnp.zeros((n_seq, SLAB), F32)
        for j in range(CONV_W):
            conv = conv + cw_ref[j:j + 1, :] * ext_row(t + j)
        yd = z_ref[2, rows, :] * conv
        y = jnp.dot(yc.astype(BF16), wo_ref[0:SLAB, :], preferred_element_type=F32)
        y = y + jnp.dot(yd.astype(BF16), wo_ref[SLAB:, :], preferred_element_type=F32)
        out_ref[rows, :] = x_ref[rows, :] + y


def _odd_out_sample(z, ctx, coef, bias, conv_w, w_out, x, n_seq, n_tok, mix_terms):
    T = x.shape[0]
    full = lambda *shape: pl.BlockSpec(shape, lambda i: (0,) * len(shape))
    return pl.pallas_call(
        functools.partial(_odd_out_sample_kernel, n_seq=n_seq, n_tok=n_tok, mix_terms=mix_terms),
        grid=(1,),
        in_specs=[
            full(4, T, SLAB),
            full(n_seq, (CONV_W - 1) * SLAB),
            full(n_tok * n_tok, SLAB),
            full(n_tok, SLAB),
            full(CONV_W, SLAB),
            full(2 * SLAB, D_MODEL),
            full(T, D_MODEL),
        ],
        out_specs=full(T, D_MODEL),
        out_shape=jax.ShapeDtypeStruct((T, D_MODEL), F32),
        compiler_params=_params("arbitrary"),
        name="odd_out_sample",
    )(z, ctx, coef, bias, conv_w, w_out, x)


def _rope_tables(pos, split):
    half = HEAD_DIM // 2
    inv = jnp.power(jnp.float32(ROPE_THETA), -jnp.arange(half, dtype=F32) / half)
    ang = pos.astype(F32)[:, None] * inv[None, :]
    cos = jnp.cos(ang)
    sin = jnp.sin(ang)
    if split:
        return jnp.tile(cos, (1, LANES // half)), jnp.tile(sin, (1, LANES // half))
    cos_t = jnp.tile(jnp.concatenate([cos, cos], axis=-1), (1, LANES // HEAD_DIM))
    sin_t = jnp.tile(jnp.concatenate([-sin, sin], axis=-1), (1, LANES // HEAD_DIM))
    return cos_t, sin_t


def _split_qk_columns_kernel(w_ref, o_ref):
    half = HEAD_DIM // 2
    n_slabs = w_ref.shape[1] // SLAB
    for s in range(n_slabs):
        cols = slice(s * SLAB, (s + 1) * SLAB)
        x = w_ref[:, cols]
        if s >= 1 and (s - 1) % 3 < 2:
            xt = x.T
            xt = jnp.concatenate(
                [xt[h * HEAD_DIM + part * half:h * HEAD_DIM + (part + 1) * half, :]
                 for part in range(2) for h in range(HEADS)], axis=0)
            x = xt.T
        o_ref[:, cols] = x.astype(BF16)


def _split_qk_columns(w, tr=256):
    d, n = w.shape
    return pl.pallas_call(
        _split_qk_columns_kernel,
        grid=(d // tr,),
        in_specs=[pl.BlockSpec((tr, n), lambda i: (i, 0))],
        out_specs=pl.BlockSpec((tr, n), lambda i: (i, 0)),
        out_shape=jax.ShapeDtypeStruct((d, n), BF16),
        compiler_params=_params("arbitrary"),
        name="split_qk_columns",
    )(w)


def kernel(x_prompt, x_sample, cache_kv_w128, cache_kv_w512, cache_kv_w2048, state_pool, state_conv,
           p_prompt, p_sample, ev_w_in, ev_pool_w, ev_pool_scale, ev_w_out, od_w_in, od_ln_g, od_ln_b,
           od_ws, od_bs, od_conv_w, od_w_out, norm_mix, norm_ffn, norm_ple, ffn_w1, ffn_w2,
           ple_w_proj, ple_w_gate, norm_final):
    n_p, seq, _ = x_prompt.shape
    n_s, n_tok, _ = x_sample.shape
    depth = norm_mix.shape[0]
    tp = n_p * seq
    ts = n_s * n_tok

    bf = lambda w: w.astype(BF16)
    row = lambda v: v.reshape(1, -1)
    ev_w_in_b, ev_pool_w_b, ev_w_out_b = bf(ev_w_in), bf(ev_pool_w), bf(ev_w_out)
    od_w_in_b, od_w_out_b = bf(od_w_in), bf(od_w_out)
    w1_b, w2_b, wg_b, wp_b = bf(ffn_w1), bf(ffn_w2), bf(ple_w_gate), bf(ple_w_proj)

    cos_p, sin_p = _rope_tables(jnp.arange(seq), True)
    pos_s = [PAST_LEN + t for t in range(n_tok)]
    cos_s, sin_s = _rope_tables(jnp.repeat(jnp.asarray(pos_s, jnp.int32), n_s), False)

    mix_terms = tuple(
        tuple(s for s in range(n_tok)
              if pos_s[s] // CHUNK == pos_s[t] // CHUNK and pos_s[s] % CHUNK <= pos_s[t] % CHUNK)
        for t in range(n_tok))
    local = [p % CHUNK for p in pos_s]

    rp = x_prompt.reshape(tp, D_MODEL)
    rs = x_sample.transpose(1, 0, 2).reshape(ts, D_MODEL)
    pp = p_prompt.reshape(depth, tp, PLE_DIM)
    ps = p_sample.transpose(0, 2, 1, 3).reshape(depth, ts, PLE_DIM)

    tm_p, tm_mix, tm_ffn, tf = 512, 512, 512, 512
    tm_odd = 1024
    assert seq % max(tm_p, tm_mix, tm_ffn, tm_odd) == 0 and ts % Q_ROWS == 0
    assert all(win == dil * ATT_BLOCK and (seq // dil) % min(ATT_ROWS, seq // dil) == 0
               and min(win, seq) % (dil * LANES) == 0 for win, dil in DIL_CFG)
    assert tp * HOST_SPLITS == tm_ffn * n_s, "one sample sequence per hosting-MLP grid step"
    assert PAST_LEN >= max(max(POOL_WINDOWS), CONV_W), "sample windows are full"
    assert all(cache.shape[2] == win for cache, (win, _) in
               zip((cache_kv_w128, cache_kv_w512, cache_kv_w2048), DIL_CFG))
    kv_p = [[] for _ in DIL_CFG]
    kv_s = [[] for _ in DIL_CFG]
    pool_p, pool_s, conv_p, conv_s, cv_s = [], [], [], [], []

    for i in range(depth):
        g_mix = row(norm_mix[i])
        ffn_args = (norm_ffn[:, None, :], w1_b, w2_b, norm_ple[:, None, :], wg_b, wp_b,
                    row(norm_final), i == depth - 1)
        if i % 2 == 0:
            e = i // 2
            pscale = row(ev_pool_scale[e])
            dils = tuple(d for _, d in DIL_CFG)
            w_split = _split_qk_columns(ev_w_in[e])
            za, *groups = _proj_even(rp, g_mix, w_split, cos_p, sin_p, tm_p, n_p, dils, True)
            groups = [grp.reshape(3, n_p, dil, seq // dil, SLAB) for grp, dil in zip(groups, dils)]
            attn = [_attn_prompt(grp, dil) for grp, dil in zip(groups, dils)]
            rp = _even_out_prompt(za, attn, ev_pool_w_b[e], pscale, ev_w_out_b[e], rp, seq, tm_mix)
            for g, ((win, _), grp) in enumerate(zip(DIL_CFG, groups)):
                keep = min(win, seq)
                tail = _kv_tail(grp, keep).reshape(n_p, 2, HEADS, HEAD_DIM, keep)
                kv_p[g].append(tail.transpose(0, 4, 1, 2, 3))
            pool_p.append(za.reshape(n_p, seq, SLAB)[:, seq - POOL_STATE:])
            zas, *sgroups = _proj_even(rs, g_mix, ev_w_in_b[e], cos_s, sin_s, ts, 1, (1, 1, 1),
                                       False)
            sgroups = [g.reshape(3, n_tok, n_s, SLAB) for g in sgroups]
            native = lambda c: c.transpose(0, 2, 3, 4, 1).reshape(n_s, 2, SLAB, c.shape[1])
            caches = [native(c[e]) for c in (cache_kv_w128, cache_kv_w512, cache_kv_w2048)]
            rp, pacc, pml = _ffn_ple_hosting(i, rp, pp, *ffn_args, tf, sgroups, caches,
                                             HOST_SPLITS)
            yb = _attn_sample(sgroups, pacc, pml)
            rs = _even_out_sample(zas, state_pool[e].transpose(1, 0, 2),
                                  yb.reshape(n_s, n_tok * SLAB), ev_pool_w_b[e], pscale,
                                  ev_w_out_b[e], rs, n_s, n_tok)
            for g, grp in enumerate(sgroups):
                kv = grp[1:].transpose(2, 1, 0, 3)
                kv_s[g].append(kv.reshape(n_s, n_tok, 2, HEADS, HEAD_DIM))
            a_n = zas.reshape(n_tok, n_s, SLAB).transpose(1, 0, 2)
            pool_s.append(jnp.concatenate([state_pool[e], a_n], axis=1)[:, -POOL_STATE:])
        else:
            o = i // 2
            ln_g, ln_b = row(od_ln_g[o]), row(od_ln_b[o])
            bs_rows = jnp.repeat(od_bs[o].T, LANES, axis=1)
            z = _proj_odd(rp, g_mix, od_w_in_b[o], ln_g, ln_b, tm_odd)
            rp = _odd_out_ffn(i, z, od_ws[o], bs_rows, od_conv_w[o], od_w_out_b[o], rp, seq, pp,
                              *ffn_args, tm_ffn, tf)
            conv_p.append(z.reshape(4, n_p, seq, SLAB)[3, :, seq - (CONV_W - 1):])
            zs = _proj_odd(rs, g_mix, od_w_in_b[o], ln_g, ln_b, ts)
            coef = jnp.stack([jnp.repeat(od_ws[o][:, local[t], local[s]], LANES)
                              for t in range(n_tok) for s in range(n_tok)])
            bias = jnp.stack([bs_rows[local[t]] for t in range(n_tok)])
            rs = _odd_out_sample(zs, state_conv[o].reshape(n_s, (CONV_W - 1) * SLAB), coef, bias,
                                 od_conv_w[o], od_w_out_b[o], rs, n_s, n_tok, mix_terms)
            hd_n = zs[3].reshape(n_tok, n_s, SLAB).transpose(1, 0, 2)
            conv_s.append(jnp.concatenate([state_conv[o], hd_n], axis=1)[:, -(CONV_W - 1):])
            cv_s.append(zs[1].reshape(n_tok, n_s, SLAB).transpose(1, 0, 2))

        rs = _ffn_ple(i, rs, ps, *ffn_args, ts, tf)

    y_prompt = rp.reshape(n_p, seq, D_MODEL)
    y_sample = rs.reshape(n_tok, n_s, D_MODEL).transpose(1, 0, 2)
    st = lambda lst: jnp.stack(lst, axis=0)
    return (y_prompt, y_sample, st(kv_p[0]), st(kv_p[1]), st(kv_p[2]),
            st(kv_s[0]), st(kv_s[1]), st(kv_s[2]),
            st(pool_p), st(pool_s), st(conv_p), st(conv_s), st(cv_s))
```

```python
import functools
import math

import jax
import jax.numpy as jnp
from jax import lax
from jax.experimental import pallas as pl
from jax.experimental.pallas import tpu as pltpu

F32 = jnp.float32
BF16 = jnp.bfloat16

D_MODEL = 1024
D_FF = 4 * D_MODEL
PLE_DIM = 256
EPS = 1e-6
ROPE_THETA = 10000.0
PAST_LEN = 2048

SLAB = 512
POOL_WINDOWS = (2, 4, 8, 16)
POOL_GROUP = 128
POOL_STATE = 15
POOL_HALO = 16
DIL_CFG = ((128, 1), (512, 4), (2048, 16))
HEADS = 8
HEAD_DIM = 64
ATT_BLOCK = 128
ATT_ROWS = 512
CHUNK = 128
C_GROUPS = 4
CONV_W = 3
CONV_HALO = 8
LANES = 128
NEG = -1e30

VMEM_LIMIT = 52 * 1024 * 1024
HOSTING_VMEM_LIMIT = 60 * 1024 * 1024
HOST_SPLITS = 4
Q_ROWS = 8


def _params(*sem):
    return pltpu.CompilerParams(dimension_semantics=sem, vmem_limit_bytes=VMEM_LIMIT)


def _rms(x, g):
    ms = jnp.mean(x * x, axis=-1, keepdims=True)
    return x * lax.rsqrt(ms + EPS) * g


def _gelu(x):
    c = math.sqrt(2.0 / math.pi)
    return 0.5 * x * (1.0 + jnp.tanh(c * (x + 0.044715 * (x * x * x))))


def _proj_even_kernel(x_ref, g_ref, w_ref, cos_ref, sin_ref, za_ref, g0_ref, g1_ref, g2_ref,
                      zs_ref, *, dils, split):
    tm = x_ref.shape[0]
    nl = SLAB // LANES
    hn = _rms(x_ref[...], g_ref[...]).astype(BF16)
    za_ref[...] = jnp.dot(hn, w_ref[:, 0:SLAB], preferred_element_type=F32)

    def rotate(chunks):
        cos = cos_ref[...]
        sin = sin_ref[...]
        if split:
            h = nl // 2
            return ([chunks[i] * cos - chunks[i + h] * sin for i in range(h)]
                    + [chunks[i] * cos + chunks[i - h] * sin for i in range(h, nl)])
        lane = lax.broadcasted_iota(jnp.int32, cos.shape, 1)
        first_half = jnp.bitwise_and(lane, HEAD_DIM - 1) < (HEAD_DIM // 2)
        out = []
        for zc in chunks:
            partner = jnp.where(first_half,
                                pltpu.roll(zc, LANES - HEAD_DIM // 2, 1),
                                pltpu.roll(zc, HEAD_DIM // 2, 1))
            out.append(zc * cos + partner * sin)
        return out

    slot = 0
    for g, (out_ref, dil) in enumerate(zip((g0_ref, g1_ref, g2_ref), dils)):
        for c in range(3):
            col0 = (1 + 3 * g + c) * SLAB
            z = jnp.dot(hn, w_ref[:, col0:col0 + SLAB], preferred_element_type=F32)
            chunks = [z[:, i * LANES:(i + 1) * LANES] for i in range(nl)]
            if c < 2:
                chunks = rotate(chunks)
            for i, zc in enumerate(chunks):
                cols = slice(i * LANES, (i + 1) * LANES)
                if dil == 1:
                    out_ref[c, :, cols] = zc
                else:
                    zs_ref[slot] = zc
                    for r in range(dil):
                        out_ref[c, r, :, cols] = zs_ref[slot, pl.ds(r, tm // dil, stride=dil), :]
                    slot += 1


def _resident(shape):
    return pl.BlockSpec(shape, lambda *_: (0,) * len(shape), pipeline_mode=pl.Buffered(1))


def _proj_even(x, g, w, cos, sin, tm, n_seq, dils, split):
    T = x.shape[0]
    ntab = cos.shape[0] // tm
    tps = T // n_seq // tm

    def group_spec(dil):
        if dil == 1:
            return pl.BlockSpec((3, tm, SLAB), lambda i: (0, i, 0))
        return pl.BlockSpec((3, None, dil, tm // dil, SLAB), lambda i: (0, i // tps, 0, i % tps, 0))

    def group_shape(dil):
        if dil == 1:
            return jax.ShapeDtypeStruct((3, T, SLAB), F32)
        return jax.ShapeDtypeStruct((3, n_seq, dil, T // n_seq // dil, SLAB), F32)

    n_slots = max(1, 3 * (SLAB // LANES) * sum(d > 1 for d in dils))
    return pl.pallas_call(
        functools.partial(_proj_even_kernel, dils=dils, split=split),
        grid=(T // tm,),
        in_specs=[
            pl.BlockSpec((tm, D_MODEL), lambda i: (i, 0)),
            _resident((1, D_MODEL)),
            _resident(w.shape),
            pl.BlockSpec((tm, LANES), lambda i: (i % ntab, 0)),
            pl.BlockSpec((tm, LANES), lambda i: (i % ntab, 0)),
        ],
        out_specs=[pl.BlockSpec((tm, SLAB), lambda i: (i, 0))] + [group_spec(d) for d in dils],
        out_shape=[jax.ShapeDtypeStruct((T, SLAB), F32)] + [group_shape(d) for d in dils],
        scratch_shapes=[pltpu.VMEM((n_slots, tm, LANES), F32)],
        compiler_params=_params("arbitrary"),
        name="proj_even",
    )(x, g, w, cos, sin)


def _proj_odd_kernel(x_ref, g_ref, w_ref, lng_ref, lnb_ref, o_ref):
    hn = _rms(x_ref[...], g_ref[...]).astype(BF16)

    def slab(s):
        return jnp.dot(hn, w_ref[:, s * SLAB:(s + 1) * SLAB], preferred_element_type=F32)

    o_ref[0] = _gelu(slab(0))
    zv = slab(1)
    for c in range(C_GROUPS):
        sl = slice(c * LANES, (c + 1) * LANES)
        v = _gelu(zv[:, sl])
        mu = jnp.mean(v, axis=-1, keepdims=True)
        dv = v - mu
        var = jnp.mean(dv * dv, axis=-1, keepdims=True)
        o_ref[1, :, sl] = dv * lax.rsqrt(var + EPS) * lng_ref[:, sl] + lnb_ref[:, sl]
    o_ref[2] = slab(2)
    o_ref[3] = slab(3) * slab(4)


def _proj_odd(x, g, w, ln_g, ln_b, tm):
    T = x.shape[0]
    return pl.pallas_call(
        _proj_odd_kernel,
        grid=(T // tm,),
        in_specs=[
            pl.BlockSpec((tm, D_MODEL), lambda i: (i, 0)),
            _resident((1, D_MODEL)),
            _resident(w.shape),
            _resident((1, SLAB)),
            _resident((1, SLAB)),
        ],
        out_specs=pl.BlockSpec((4, tm, SLAB), lambda i: (0, i, 0)),
        out_shape=jax.ShapeDtypeStruct((4, T, SLAB), F32),
        compiler_params=_params("arbitrary"),
        name="proj_odd",
    )(x, g, w, ln_g, ln_b)


def _attn_prompt_kernel(q_ref, kp_ref, kc_ref, vp_ref, vc_ref, o_ref, lse_ref,
                        qst_ref, ks_ref, vs_ref, bias_ref, s_ref, e_ref, mx_ref):
    step = pl.program_id(2)
    att_rows = q_ref.shape[0]
    sub = att_rows // ATT_BLOCK
    half_rot = HEAD_DIM // 2
    heads_per_blk = LANES // half_rot
    n_blk = HEADS // heads_per_blk
    log2e = 1.4426950408889634
    ln2 = 0.6931471805599453
    scale = HEAD_DIM ** -0.5 * log2e

    qlane = lax.broadcasted_iota(jnp.int32, (att_rows, 2 * LANES), 1)
    head_in_blk = jnp.bitwise_and(qlane, LANES - 1) // half_rot
    for j in range(n_blk):
        lo = slice(j * LANES, (j + 1) * LANES)
        hi = slice((n_blk + j) * LANES, (n_blk + j + 1) * LANES)
        dst = slice(2 * j * LANES, (2 * j + 2) * LANES)
        q2 = (jnp.concatenate([q_ref[:, lo], q_ref[:, hi]], axis=1) * scale).astype(BF16)
        for hq in range(heads_per_blk):
            qm = jnp.where(head_in_blk == hq, q2, jnp.zeros_like(q2))
            for b in range(sub):
                r0 = ((j * sub + b) * heads_per_blk + hq) * ATT_BLOCK
                qst_ref[r0:r0 + ATT_BLOCK, :] = qm[b * ATT_BLOCK:(b + 1) * ATT_BLOCK, :]
        ks_ref[0:ATT_BLOCK, dst] = jnp.concatenate([kp_ref[:, lo], kp_ref[:, hi]],
                                                   axis=1).astype(BF16)
        ks_ref[ATT_BLOCK:, dst] = jnp.concatenate([kc_ref[:, lo], kc_ref[:, hi]],
                                                  axis=1).astype(BF16)
    for hp in range(HEADS // 2):
        src = slice(hp * LANES, (hp + 1) * LANES)
        vs_ref[0:ATT_BLOCK, 2 * hp * LANES:(2 * hp + 1) * LANES] = vp_ref[:, src].astype(BF16)
        vs_ref[ATT_BLOCK:, 2 * hp * LANES:(2 * hp + 1) * LANES] = vc_ref[:, src].astype(BF16)
        vs_ref[:, (2 * hp + 1) * LANES:(2 * hp + 2) * LANES] = jnp.ones(
            (att_rows + ATT_BLOCK, LANES), BF16)

    qi = lax.broadcasted_iota(jnp.int32, (ATT_BLOCK, 2 * ATT_BLOCK), 0)
    ki = lax.broadcasted_iota(jnp.int32, (ATT_BLOCK, 2 * ATT_BLOCK), 1)
    rel = qi + ATT_BLOCK - ki
    band = jnp.logical_and(rel >= 0, rel <= ATT_BLOCK)
    bias_ref[0] = jnp.where(band, 0.0, NEG)
    bias_ref[1] = jnp.where(jnp.logical_and(band, ki >= ATT_BLOCK), 0.0, NEG)
    lane = lax.broadcasted_iota(jnp.int32, (ATT_BLOCK, LANES), 1)
    low_head = lane < HEAD_DIM
    blk_rows = heads_per_blk * ATT_BLOCK

    first = (step == 0).astype(jnp.int32)
    hb = ATT_BLOCK // 2
    lane_hb = lax.broadcasted_iota(jnp.int32, (hb, LANES), 1)

    def scores(b):
        for j in range(n_blk):
            base = (j * sub + b) * blk_rows
            k2 = ks_ref[b * ATT_BLOCK:(b + 2) * ATT_BLOCK, 2 * j * LANES:(2 * j + 2) * LANES]
            s_ref[b, j * blk_rows:(j + 1) * blk_rows, :] = lax.dot_general(
                qst_ref[base:base + blk_rows, :], k2, (((1,), (1,)), ((), ())),
                preferred_element_type=F32)

    def softmax(b):
        which = first if b == 0 else 0
        for rh in range(2):
            m_tile = jnp.zeros((hb, LANES), F32)
            for h in range(HEADS):
                rows = slice(h * ATT_BLOCK + rh * hb, h * ATT_BLOCK + (rh + 1) * hb)
                s = s_ref[b, rows, :] + bias_ref[which, rh * hb:(rh + 1) * hb, :]
                m = jnp.max(s, axis=-1, keepdims=True)
                e_ref[b, rows, :] = jnp.exp2((s - m).astype(BF16))
                m_tile = jnp.where(lane_hb == h, m, m_tile)
            mx_ref[b, rh * hb:(rh + 1) * hb, :] = m_tile

    def weighted_values(b):
        log_l = jnp.zeros((ATT_BLOCK, LANES), F32)
        for hp in range(HEADS // 2):
            v_ones = vs_ref[b * ATT_BLOCK:(b + 2) * ATT_BLOCK, 2 * hp * LANES:(2 * hp + 2) * LANES]
            pv = jnp.dot(e_ref[b, 2 * hp * ATT_BLOCK:(2 * hp + 2) * ATT_BLOCK, :], v_ones,
                         preferred_element_type=F32)
            l_lo, l_hi = pv[0:ATT_BLOCK, LANES:], pv[ATT_BLOCK:, LANES:]
            o_ref[b * ATT_BLOCK:(b + 1) * ATT_BLOCK, hp * LANES:(hp + 1) * LANES] = jnp.where(
                low_head, pv[0:ATT_BLOCK, 0:LANES] * (1.0 / l_lo),
                pv[ATT_BLOCK:, 0:LANES] * (1.0 / l_hi)).astype(o_ref.dtype)
            log_l = jnp.where(lane == 2 * hp, jnp.log(l_lo),
                              jnp.where(lane == 2 * hp + 1, jnp.log(l_hi), log_l))
        lse_ref[b * ATT_BLOCK:(b + 1) * ATT_BLOCK, :] = mx_ref[b] * ln2 + log_l

    for t in range(sub + 2):
        if t < sub:
            scores(t)
        if 0 <= t - 1 < sub:
            softmax(t - 1)
        if 0 <= t - 2 < sub:
            weighted_values(t - 2)


def _attn_prompt(qkv, dil):
    _, n_seq, _, m_rows, _ = qkv.shape
    att_rows = min(ATT_ROWS, m_rows)
    steps = m_rows // att_rows
    sub = att_rows // ATT_BLOCK

    def cur(slab):
        return pl.BlockSpec((None, None, None, att_rows, SLAB), lambda n, r, b: (slab, n, r, b, 0))

    def prev(slab):
        return pl.BlockSpec((None, None, None, ATT_BLOCK, SLAB),
                            lambda n, r, b: (slab, n, r, jnp.maximum(b * sub - 1, 0), 0))

    return pl.pallas_call(
        _attn_prompt_kernel,
        grid=(n_seq, dil, steps),
        in_specs=[cur(0), prev(1), cur(1), prev(2), cur(2)],
        out_specs=[
            pl.BlockSpec((None, None, att_rows, SLAB), lambda n, r, b: (n, r, b, 0)),
            pl.BlockSpec((None, None, att_rows, LANES), lambda n, r, b: (n, r, b, 0)),
        ],
        out_shape=[
            jax.ShapeDtypeStruct((n_seq, dil, m_rows, SLAB), BF16),
            jax.ShapeDtypeStruct((n_seq, dil, m_rows, LANES), F32),
        ],
        scratch_shapes=[
            pltpu.VMEM((HEADS * att_rows, 2 * LANES), BF16),
            pltpu.VMEM((att_rows + ATT_BLOCK, SLAB), BF16),
            pltpu.VMEM((att_rows + ATT_BLOCK, 2 * SLAB), BF16),
            pltpu.VMEM((2, ATT_BLOCK, 2 * ATT_BLOCK), F32),
            pltpu.VMEM((sub, HEADS * ATT_BLOCK, 2 * ATT_BLOCK), F32),
            pltpu.VMEM((sub, HEADS * ATT_BLOCK, 2 * ATT_BLOCK), BF16),
            pltpu.VMEM((sub, ATT_BLOCK, LANES), F32),
        ],
        compiler_params=_params("arbitrary", "arbitrary", "arbitrary"),
        name=f"attn_prompt_d{dil}",
    )(qkv, qkv, qkv, qkv, qkv)


def _kv_tail_kernel(k_ref, v_ref, o_ref, tok_ref):
    dil, rows, _ = k_ref.shape
    keep = dil * rows
    nl = SLAB // LANES
    half_rot = HEAD_DIM // 2
    heads_per_chunk = LANES // half_rot
    for kv, src in enumerate((k_ref, v_ref)):
        for lc in range(nl):
            for r in range(dil):
                tok_ref[lc, pl.ds(r, rows, stride=dil), :] = src[r, :, lc * LANES:(lc + 1) * LANES]
            for pc in range(keep // LANES):
                pcols = slice(pc * LANES, (pc + 1) * LANES)
                t = tok_ref[lc, pcols, :].T
                if kv == 0:
                    half, hblk = divmod(lc, nl // 2)
                    for hq in range(heads_per_chunk):
                        r0 = (hblk * heads_per_chunk + hq) * HEAD_DIM + half * half_rot
                        o_ref[0, r0:r0 + half_rot, pcols] = t[hq * half_rot:(hq + 1) * half_rot, :]
                else:
                    o_ref[1, lc * LANES:(lc + 1) * LANES, pcols] = t


def _kv_tail(grp, keep):
    _, n_seq, dil, m_rows, _ = grp.shape
    rows = keep // dil
    last = m_rows // rows - 1

    def spec(slab):
        return pl.BlockSpec((None, None, dil, rows, SLAB), lambda n: (slab, n, 0, last, 0))

    return pl.pallas_call(
        _kv_tail_kernel,
        grid=(n_seq,),
        in_specs=[spec(1), spec(2)],
        out_specs=pl.BlockSpec((None, 2, SLAB, keep), lambda n: (n, 0, 0, 0)),
        out_shape=jax.ShapeDtypeStruct((n_seq, 2, SLAB, keep), F32),
        scratch_shapes=[pltpu.VMEM((SLAB // LANES, keep, LANES), F32)],
        compiler_params=_params("arbitrary"),
        name=f"kv_tail_d{dil}",
    )(grp, grp)


def _pool_mix(window_terms, a_cols, cnt, pw_ref, scale_ref, gi):
    acc = a_cols
    for term in window_terms:
        acc = acc + term
    pooled = acc / cnt - a_cols
    cols = slice(gi * POOL_GROUP, (gi + 1) * POOL_GROUP)
    mixed = jnp.dot(pooled.astype(BF16), pw_ref[gi], preferred_element_type=F32)
    return mixed * scale_ref[:, cols]


def _even_out_prompt_kernel(a_ref, halo_ref, o0_ref, o1_ref, o2_ref, l0_ref, l1_ref, l2_ref,
                            pw_ref, ps_ref, wo_ref, x_ref, out_ref, ext_ref, ya_ref, yb_ref,
                            oi_ref, li_ref, *, tiles_per_seq):
    tm = a_ref.shape[0]
    it = pl.program_id(0) % tiles_per_seq
    halo = halo_ref[...]
    ext_ref[0:POOL_HALO, :] = jnp.where(it == 0, jnp.zeros_like(halo), halo)
    ext_ref[POOL_HALO:, :] = a_ref[...]
    pos = it * tm + lax.broadcasted_iota(jnp.int32, (tm, 1), 0)
    for gi, w in enumerate(POOL_WINDOWS):
        cols = slice(gi * POOL_GROUP, (gi + 1) * POOL_GROUP)
        terms = [ext_ref[POOL_HALO - k:POOL_HALO - k + tm, cols] for k in range(1, w)]
        cnt = jnp.minimum(w, pos + 1).astype(F32)
        ya_ref[:, cols] = _pool_mix(terms, a_ref[:, cols], cnt, pw_ref, ps_ref, gi).astype(BF16)

    for g, (o_ref, l_ref) in enumerate(((o0_ref, l0_ref), (o1_ref, l1_ref), (o2_ref, l2_ref))):
        dil = o_ref.shape[0]
        for r in range(dil):
            rows = pl.ds(r, tm // dil, stride=dil)
            li_ref[g, rows, :] = l_ref[r]
            for lc in range(SLAB // LANES):
                oi_ref[g, lc, rows, :] = o_ref[r, :, lc * LANES:(lc + 1) * LANES].astype(F32)

    l0, l1, l2 = li_ref[0], li_ref[1], li_ref[2]
    mx = jnp.maximum(jnp.maximum(l0, l1), l2)
    e0, e1, e2 = jnp.exp(l0 - mx), jnp.exp(l1 - mx), jnp.exp(l2 - mx)
    inv = 1.0 / (e0 + e1 + e2)
    w0, w1 = e0 * inv, e1 * inv
    for h in range(HEADS):
        lc, lo = divmod(h * HEAD_DIM, LANES)
        sl = slice(lo, lo + HEAD_DIM)
        o2 = oi_ref[2, lc, :, sl]
        yb = (o2 + w0[:, h:h + 1] * (oi_ref[0, lc, :, sl] - o2)
              + w1[:, h:h + 1] * (oi_ref[1, lc, :, sl] - o2))
        yb_ref[:, h * HEAD_DIM:(h + 1) * HEAD_DIM] = yb.astype(BF16)

    y = jnp.dot(ya_ref[...], wo_ref[0:SLAB, :], preferred_element_type=F32)
    y = y + jnp.dot(yb_ref[...], wo_ref[SLAB:, :], preferred_element_type=F32)
    out_ref[...] = x_ref[...] + y


def _even_out_prompt(za, attn, pool_w, pool_scale, w_out, x, seq, tm):
    T = x.shape[0]
    tps = seq // tm
    hb = tm // POOL_HALO

    def residue_spec(dil, width):
        return pl.BlockSpec((None, dil, tm // dil, width), lambda i: (i // tps, 0, i % tps, 0))

    dils = [o.shape[1] for o, _ in attn]
    return pl.pallas_call(
        functools.partial(_even_out_prompt_kernel, tiles_per_seq=tps),
        grid=(T // tm,),
        in_specs=[
            pl.BlockSpec((tm, SLAB), lambda i: (i, 0)),
            pl.BlockSpec((POOL_HALO, SLAB), lambda i: (jnp.maximum(i * hb - 1, 0), 0)),
        ]
        + [residue_spec(d, SLAB) for d in dils] + [residue_spec(d, LANES) for d in dils]
        + [
            pl.BlockSpec((len(POOL_WINDOWS), POOL_GROUP, POOL_GROUP), lambda i: (0, 0, 0)),
            pl.BlockSpec((1, SLAB), lambda i: (0, 0)),
            pl.BlockSpec((2 * SLAB, D_MODEL), lambda i: (0, 0)),
            pl.BlockSpec((tm, D_MODEL), lambda i: (i, 0)),
        ],
        out_specs=pl.BlockSpec((tm, D_MODEL), lambda i: (i, 0)),
        out_shape=jax.ShapeDtypeStruct((T, D_MODEL), F32),
        scratch_shapes=[
            pltpu.VMEM((tm + POOL_HALO, SLAB), F32),
            pltpu.VMEM((tm, SLAB), BF16),
            pltpu.VMEM((tm, SLAB), BF16),
            pltpu.VMEM((len(dils), SLAB // LANES, tm, LANES), F32),
            pltpu.VMEM((len(dils), tm, LANES), F32),
        ],
        compiler_params=_params("arbitrary"),
        name="even_out_prompt",
    )(za, za, *[o for o, _ in attn], *[l for _, l in attn], pool_w, pool_scale, w_out, x)


def _odd_mix(u_ref, vn_ref, go_ref, hd_ref, hdh_ref, ws_ref, bs_ref, cw_ref, wo_ref, x_ref,
             ext_ref, yc_ref, yd_ref, tiles_per_seq):
    tm = u_ref.shape[0]
    it = pl.program_id(0) % tiles_per_seq

    ti = lax.broadcasted_iota(jnp.int32, (CHUNK, CHUNK), 0)
    si = lax.broadcasted_iota(jnp.int32, (CHUNK, CHUNK), 1)
    for g in range(C_GROUPS):
        cols = slice(g * LANES, (g + 1) * LANES)
        wm = jnp.where(si <= ti, ws_ref[g], 0.0).astype(BF16)
        for c in range(tm // CHUNK):
            rows = slice(c * CHUNK, (c + 1) * CHUNK)
            sp = jnp.dot(wm, vn_ref[rows, cols].astype(BF16), preferred_element_type=F32)
            sp = sp + bs_ref[:, cols]
            yc_ref[rows, cols] = (u_ref[rows, cols] * sp).astype(BF16)

    hd = hd_ref[...]
    halo = hdh_ref[...]
    ext_ref[0:CONV_HALO, :] = jnp.where(it == 0, jnp.zeros_like(halo), halo)
    ext_ref[CONV_HALO:, :] = hd
    conv = cw_ref[CONV_W - 1:CONV_W, :] * hd
    for j in range(CONV_W - 1):
        off = CONV_HALO - (CONV_W - 1) + j
        conv = conv + cw_ref[j:j + 1, :] * ext_ref[off:off + tm, :]
    yd_ref[...] = (go_ref[...] * conv).astype(BF16)

    y = jnp.dot(yc_ref[...], wo_ref[0:SLAB, :], preferred_element_type=F32)
    y = y + jnp.dot(yd_ref[...], wo_ref[SLAB:, :], preferred_element_type=F32)
    return x_ref[...] + y


def _odd_out_ffn_kernel(u_ref, vn_ref, go_ref, hd_ref, hdh_ref, ws_ref, bs_ref, cw_ref, wo_ref,
                        x_ref, p_ref, gf_ref, w1_ref, w2_ref, gp_ref, wg_ref, wp_ref, gl_ref,
                        out_ref, ext_ref, yc_ref, yd_ref, *, tiles_per_seq, final_norm, tf):
    r = _odd_mix(u_ref, vn_ref, go_ref, hd_ref, hdh_ref, ws_ref, bs_ref, cw_ref, wo_ref, x_ref,
                 ext_ref, yc_ref, yd_ref, tiles_per_seq)
    out_ref[...] = _mlp_ple(r, p_ref, gf_ref, w1_ref, w2_ref, gp_ref, wg_ref, wp_ref, gl_ref,
                            final_norm, tf)


def _odd_out_ffn(layer, z, ws, bs_rows, conv_w, w_out, x, seq, p, g_ffn, w1, w2, g_ple, wg, wp,
                 g_last, final_norm, tm, tf):
    T = x.shape[0]
    tiles_per_seq = seq // tm
    hb = tm // CONV_HALO

    def slab(s):
        return pl.BlockSpec((None, tm, SLAB), lambda i: (s, i, 0))

    return pl.pallas_call(
        functools.partial(_odd_out_ffn_kernel, tiles_per_seq=tiles_per_seq,
                          final_norm=final_norm, tf=tf),
        grid=(T // tm,),
        in_specs=[
            slab(0), slab(1), slab(2), slab(3),
            pl.BlockSpec((None, CONV_HALO, SLAB), lambda i: (3, jnp.maximum(i * hb - 1, 0), 0)),
            _resident((C_GROUPS, CHUNK, CHUNK)),
            _resident((CHUNK, SLAB)),
            _resident((CONV_W, SLAB)),
            _resident((2 * SLAB, D_MODEL)),
            pl.BlockSpec((tm, D_MODEL), lambda i: (i, 0)),
            pl.BlockSpec((None, tm, PLE_DIM), lambda i: (layer, i, 0)),
            _layer_resident(layer, (1, D_MODEL)),
            _layer_resident(layer, (D_MODEL, D_FF)),
            _layer_resident(layer, (D_FF, D_MODEL)),
            _layer_resident(layer, (1, D_MODEL)),
            _layer_resident(layer, (D_MODEL, D_MODEL)),
            _layer_resident(layer, (PLE_DIM, D_MODEL)),
            _resident((1, D_MODEL)),
        ],
        out_specs=pl.BlockSpec((tm, D_MODEL), lambda i: (i, 0)),
        out_shape=jax.ShapeDtypeStruct((T, D_MODEL), F32),
        scratch_shapes=[
            pltpu.VMEM((tm + CONV_HALO, SLAB), F32),
            pltpu.VMEM((tm, SLAB), BF16),
            pltpu.VMEM((tm, SLAB), BF16),
        ],
        compiler_params=_params("arbitrary"),
        name="odd_out_ffn",
    )(z, z, z, z, z, ws, bs_rows, conv_w, w_out, x, p, g_ffn, w1, w2, g_ple, wg, wp, g_last)


def _mlp_ple(x, p_ref, gf_ref, w1_ref, w2_ref, gp_ref, wg_ref, wp_ref, gl_ref, final_norm, tf):
    hn = _rms(x, gf_ref[...]).astype(BF16)
    acc = None
    for c in range(D_FF // tf):
        h1 = jnp.dot(hn, w1_ref[:, c * tf:(c + 1) * tf], preferred_element_type=F32)
        h1 = jnp.square(jnp.maximum(h1, 0.0)).astype(BF16)
        part = jnp.dot(h1, w2_ref[c * tf:(c + 1) * tf, :], preferred_element_type=F32)
        acc = part if acc is None else acc + part
    r = x + acc
    hp = _rms(r, gp_ref[...]).astype(BF16)
    gate = jax.nn.sigmoid(jnp.dot(hp, wg_ref[...], preferred_element_type=F32))
    proj = jnp.dot(p_ref[...].astype(BF16), wp_ref[...], preferred_element_type=F32)
    r = r + gate * proj
    if final_norm:
        r = _rms(r, gl_ref[...])
    return r


def _ffn_ple_kernel(x_ref, p_ref, gf_ref, w1_ref, w2_ref, gp_ref, wg_ref, wp_ref, gl_ref,
                    out_ref, *, final_norm, tf):
    out_ref[...] = _mlp_ple(x_ref[...], p_ref, gf_ref, w1_ref, w2_ref, gp_ref, wg_ref, wp_ref,
                            gl_ref, final_norm, tf)


def _layer_resident(layer, shape):
    return pl.BlockSpec((None,) + shape, lambda *_: (layer,) + (0,) * len(shape),
                        pipeline_mode=pl.Buffered(1))


def _ffn_ple(layer, x, p, g_ffn, w1, w2, g_ple, wg, wp, g_last, final_norm, tm, tf):
    T = x.shape[0]
    return pl.pallas_call(
        functools.partial(_ffn_ple_kernel, final_norm=final_norm, tf=tf),
        grid=(T // tm,),
        in_specs=[
            pl.BlockSpec((tm, D_MODEL), lambda i: (i, 0)),
            pl.BlockSpec((None, tm, PLE_DIM), lambda i: (layer, i, 0)),
            _layer_resident(layer, (1, D_MODEL)),
            _layer_resident(layer, (D_MODEL, D_FF)),
            _layer_resident(layer, (D_FF, D_MODEL)),
            _layer_resident(layer, (1, D_MODEL)),
            _layer_resident(layer, (D_MODEL, D_MODEL)),
            _layer_resident(layer, (PLE_DIM, D_MODEL)),
            _resident((1, D_MODEL)),
        ],
        out_specs=pl.BlockSpec((tm, D_MODEL), lambda i: (i, 0)),
        out_shape=jax.ShapeDtypeStruct((T, D_MODEL), F32),
        compiler_params=_params("arbitrary"),
        name="ffn_ple",
    )(x, p, g_ffn, w1, w2, g_ple, wg, wp, g_last)


def _ffn_ple_hosting_kernel(x_ref, p_ref, gf_ref, w1_ref, w2_ref, gp_ref, wg_ref, wp_ref, gl_ref,
                            q0_ref, q1_ref, q2_ref, c0_ref, c1_ref, c2_ref, out_ref, pacc_ref,
                            pml_ref, hn_ref, acc_ref, *, final_norm, tf, splits):
    j = pl.program_id(1)
    share = D_FF // splits

    def mlp_share(k):
        acc = None
        for c in range(share // tf):
            cols = slice(k * share + c * tf, k * share + (c + 1) * tf)
            h1 = jnp.dot(hn_ref[...], w1_ref[:, cols], preferred_element_type=F32)
            h1 = jnp.square(jnp.maximum(h1, 0.0)).astype(BF16)
            part = jnp.dot(h1, w2_ref[cols, :], preferred_element_type=F32)
            acc = part if acc is None else acc + part
        return acc

    @pl.when(j == 0)
    def _():
        hn_ref[...] = _rms(x_ref[...], gf_ref[...]).astype(BF16)
        acc_ref[...] = mlp_share(0)

    for k in range(1, splits - 1):
        @pl.when(j == k)
        def _(k=k):
            acc_ref[...] += mlp_share(k)

    @pl.when(j == splits - 1)
    def _():
        r = x_ref[...] + (acc_ref[...] + mlp_share(splits - 1))
        hp = _rms(r, gp_ref[...]).astype(BF16)
        gate = jax.nn.sigmoid(jnp.dot(hp, wg_ref[...], preferred_element_type=F32))
        proj = jnp.dot(p_ref[...].astype(BF16), wp_ref[...], preferred_element_type=F32)
        r = r + gate * proj
        if final_norm:
            r = _rms(r, gl_ref[...])
        out_ref[...] = r

    row = (splits * pl.program_id(0) + j) % Q_ROWS
    q_rows = [[q_ref[t, pl.ds(row, 1), :] for t in range(q_ref.shape[0])]
              for q_ref in (q0_ref, q1_ref, q2_ref)]
    _partial_cached_attention(q_rows, (c0_ref, c1_ref, c2_ref), pacc_ref, pml_ref)


def _ffn_ple_hosting(layer, x, p, g_ffn, w1, w2, g_ple, wg, wp, g_last, final_norm, tf,
                     qkv, caches, splits):
    T = x.shape[0]
    _, n_tok, n_seq, _ = qkv[0].shape
    tm = splits * T // n_seq
    seq_of = lambda i, j: splits * i + j

    def cache_spec(c):
        return pl.BlockSpec((None, 2, SLAB, c.shape[-1]), lambda i, j: (seq_of(i, j), 0, 0, 0))

    return pl.pallas_call(
        functools.partial(_ffn_ple_hosting_kernel, final_norm=final_norm, tf=tf, splits=splits),
        grid=(T // tm, splits),
        in_specs=[
            pl.BlockSpec((tm, D_MODEL), lambda i, j: (i, 0)),
            pl.BlockSpec((None, tm, PLE_DIM), lambda i, j: (layer, i, 0)),
            _layer_resident(layer, (1, D_MODEL)),
            _layer_resident(layer, (D_MODEL, D_FF)),
            _layer_resident(layer, (D_FF, D_MODEL)),
            _layer_resident(layer, (1, D_MODEL)),
            _layer_resident(layer, (D_MODEL, D_MODEL)),
            _layer_resident(layer, (PLE_DIM, D_MODEL)),
            _resident((1, D_MODEL)),
        ] + [pl.BlockSpec((None, n_tok, Q_ROWS, SLAB),
                          lambda i, j: (0, 0, seq_of(i, j) // Q_ROWS, 0)) for _ in qkv]
        + [cache_spec(c) for c in caches],
        out_specs=[
            pl.BlockSpec((tm, D_MODEL), lambda i, j: (i, 0)),
            pl.BlockSpec((None, n_tok * HEADS, SLAB), lambda i, j: (seq_of(i, j), 0, 0)),
            pl.BlockSpec((None, n_tok * HEADS, LANES), lambda i, j: (seq_of(i, j), 0, 0)),
        ],
        out_shape=[
            jax.ShapeDtypeStruct((T, D_MODEL), F32),
            jax.ShapeDtypeStruct((n_seq, n_tok * HEADS, SLAB), F32),
            jax.ShapeDtypeStruct((n_seq, n_tok * HEADS, LANES), F32),
        ],
        scratch_shapes=[pltpu.VMEM((tm, D_MODEL), BF16), pltpu.VMEM((tm, D_MODEL), F32)],
        compiler_params=pltpu.CompilerParams(dimension_semantics=("arbitrary", "arbitrary"),
                                             vmem_limit_bytes=HOSTING_VMEM_LIMIT),
        name="ffn_ple_hosting",
    )(x, p, g_ffn, w1, w2, g_ple, wg, wp, g_last, *qkv, *caches)


def _own_lanes():
    sub = lax.broadcasted_iota(jnp.int32, (HEADS, SLAB), 0)
    lane_head = lax.broadcasted_iota(jnp.int32, (HEADS, SLAB), 1) // HEAD_DIM
    return sub == lane_head


def _block_diag_queries(q_rows):
    own = _own_lanes()
    scale = HEAD_DIM ** -0.5
    return jnp.concatenate(
        [jnp.where(own, jnp.broadcast_to(q * scale, (HEADS, SLAB)), 0.0) for q in q_rows], axis=0)


def _cached_scores(qbd, kt, dil):
    s = jnp.dot(qbd.astype(BF16), kt.astype(BF16), preferred_element_type=F32)
    pos = lax.broadcasted_iota(jnp.int32, s.shape, 1)
    row_tok = lax.broadcasted_iota(jnp.int32, (s.shape[0], 1), 0) // HEADS
    valid = (pos >= row_tok) if dil == 1 else (jnp.bitwise_and(pos, dil - 1) == row_tok)
    return jnp.where(valid, s, NEG)


def _weighted_cached_values(e, vt):
    return lax.dot_general(e.astype(BF16), vt.astype(BF16), (((1,), (1,)), ((), ())),
                           preferred_element_type=F32)


def _partial_cached_attention(q_rows, cache_refs, acc_ref, ml_ref):
    scores = []
    m = None
    for g, (c_ref, (_, dil)) in enumerate(zip(cache_refs, DIL_CFG)):
        s = _cached_scores(_block_diag_queries(q_rows[g]), c_ref[0], dil)
        row_max = jnp.max(s, axis=1, keepdims=True)
        m = row_max if m is None else jnp.maximum(m, row_max)
        scores.append(s)
    acc, l = None, None
    for s, c_ref in zip(scores, cache_refs):
        e = jnp.exp(s - m)
        part = _weighted_cached_values(e, c_ref[1])
        row_sum = jnp.sum(e, axis=1, keepdims=True)
        acc = part if acc is None else acc + part
        l = row_sum if l is None else l + row_sum
    acc_ref[...] = acc
    lane = lax.broadcasted_iota(jnp.int32, ml_ref.shape, 1)
    ml_ref[...] = jnp.where(lane == 0, m, jnp.where(lane == 1, l, 0.0))


def _attn_sample_kernel(g0_ref, g1_ref, g2_ref, pacc_ref, pml_ref, o_ref, *, n_seq):
    groups = (g0_ref, g1_ref, g2_ref)
    n_tok = g0_ref.shape[1]
    n_rows = n_tok * HEADS
    own = _own_lanes()
    row_tok = lax.broadcasted_iota(jnp.int32, (n_rows, 1), 0) // HEADS

    def one_sequence(n, carry):
        m_part = pml_ref[n, :, 0:1]
        m = m_part
        s_new = []
        for g, (g_ref, (_, dil)) in enumerate(zip(groups, DIL_CFG)):
            qbd = _block_diag_queries([g_ref[0, t, pl.ds(n, 1), :] for t in range(n_tok)])
            for tp in range(n_tok):
                sn = jnp.sum(qbd * g_ref[1, tp, pl.ds(n, 1), :], axis=1, keepdims=True)
                ok = (row_tok >= tp) if dil == 1 else (row_tok == tp)
                sn = jnp.where(ok, sn, NEG)
                m = jnp.maximum(m, sn)
                s_new.append((g, tp, sn))
        w_part = jnp.exp(m_part - m)
        acc = w_part * pacc_ref[n]
        den = w_part * pml_ref[n, :, 1:2]
        for g, tp, sn in s_new:
            e = jnp.exp(sn - m)
            den = den + e
            acc = acc + e * groups[g][2, tp, pl.ds(n, 1), :]
        res = acc * (1.0 / den)
        for t in range(n_tok):
            rows = res[t * HEADS:(t + 1) * HEADS, :]
            o_ref[n, t:t + 1, :] = jnp.sum(jnp.where(own, rows, 0.0), axis=0, keepdims=True)
        return carry

    lax.fori_loop(0, n_seq, one_sequence, 0, unroll=4)


def _attn_sample(qkv, pacc, pml, seqs_per_step=16):
    _, n_tok, n_seq, _ = qkv[0].shape
    nb = seqs_per_step

    def spec(*tail):
        return pl.BlockSpec((nb,) + tail, lambda i: (i,) + (0,) * len(tail))

    return pl.pallas_call(
        functools.partial(_attn_sample_kernel, n_seq=nb),
        grid=(n_seq // nb,),
        in_specs=[pl.BlockSpec((3, n_tok, nb, SLAB), lambda i: (0, 0, i, 0)) for _ in qkv]
        + [spec(n_tok * HEADS, SLAB), spec(n_tok * HEADS, LANES)],
        out_specs=spec(n_tok, SLAB),
        out_shape=jax.ShapeDtypeStruct((n_seq, n_tok, SLAB), F32),
        compiler_params=_params("arbitrary"),
        name="attn_sample",
    )(*qkv, pacc, pml)


def _even_out_sample_kernel(a_ref, ctx_ref, yb_ref, pw_ref, ps_ref, wo_ref, x_ref, out_ref,
                            *, n_seq, n_tok):
    def ext_row(e, cols):
        if e >= POOL_STATE:
            t = e - POOL_STATE
            return a_ref[t * n_seq:(t + 1) * n_seq, cols]
        return ctx_ref[e, :, cols]

    for t in range(n_tok):
        rows = slice(t * n_seq, (t + 1) * n_seq)
        ya = []
        for gi, w in enumerate(POOL_WINDOWS):
            cols = slice(gi * POOL_GROUP, (gi + 1) * POOL_GROUP)
            terms = [ext_row(POOL_STATE + t - k, cols) for k in range(1, w)]
            ya.append(_pool_mix(terms, a_ref[rows, cols], float(w), pw_ref, ps_ref, gi))
        y = jnp.zeros((n_seq, D_MODEL), F32)
        for gi in range(len(POOL_WINDOWS)):
            y = y + jnp.dot(ya[gi].astype(BF16), wo_ref[gi * POOL_GROUP:(gi + 1) * POOL_GROUP, :],
                            preferred_element_type=F32)
        yb = yb_ref[:, t * SLAB:(t + 1) * SLAB].astype(BF16)
        y = y + jnp.dot(yb, wo_ref[SLAB:, :], preferred_element_type=F32)
        out_ref[rows, :] = x_ref[rows, :] + y


def _even_out_sample(z, ctx, yb, pool_w, pool_scale, w_out, x, n_seq, n_tok):
    T = x.shape[0]
    full = lambda *shape: pl.BlockSpec(shape, lambda i: (0,) * len(shape))
    return pl.pallas_call(
        functools.partial(_even_out_sample_kernel, n_seq=n_seq, n_tok=n_tok),
        grid=(1,),
        in_specs=[
            full(T, SLAB),
            full(POOL_STATE, n_seq, SLAB),
            full(n_seq, n_tok * SLAB),
            full(len(POOL_WINDOWS), POOL_GROUP, POOL_GROUP),
            full(1, SLAB),
            full(2 * SLAB, D_MODEL),
            full(T, D_MODEL),
        ],
        out_specs=full(T, D_MODEL),
        out_shape=jax.ShapeDtypeStruct((T, D_MODEL), F32),
        compiler_params=_params("arbitrary"),
        name="even_out_sample",
    )(z, ctx, yb, pool_w, pool_scale, w_out, x)


def _odd_out_sample_kernel(z_ref, ctx_ref, coef_ref, bias_ref, cw_ref, wo_ref, x_ref,
                           out_ref, *, n_seq, n_tok, mix_terms):
    def ext_row(e):
        if e >= CONV_W - 1:
            t = e - (CONV_W - 1)
            return z_ref[3, t * n_seq:(t + 1) * n_seq, :]
        return ctx_ref[:, e * SLAB:(e + 1) * SLAB]

    for t in range(n_tok):
        rows = slice(t * n_seq, (t + 1) * n_seq)
        sp = jnp.zeros((n_seq, SLAB), F32) + bias_ref[t:t + 1, :]
        for s in mix_terms[t]:
            r = t * n_tok + s
            sp = sp + coef_ref[r:r + 1, :] * z_ref[1, s * n_seq:(s + 1) * n_seq, :]
        yc = z_ref[0, rows, :] * sp
        conv = jnp.zeros((n_seq, SLAB), F32)
        for j in range(CONV_W):
            conv = conv + cw_ref[j:j + 1, :] * ext_row(t + j)
        yd = z_ref[2, rows, :] * conv
        y = jnp.dot(yc.astype(BF16), wo_ref[0:SLAB, :], preferred_element_type=F32)
        y = y + jnp.dot(yd.astype(BF16), wo_ref[SLAB:, :], preferred_element_type=F32)
        out_ref[rows, :] = x_ref[rows, :] + y


def _odd_out_sample(z, ctx, coef, bias, conv_w, w_out, x, n_seq, n_tok, mix_terms):
    T = x.shape[0]
    full = lambda *shape: pl.BlockSpec(shape, lambda i: (0,) * len(shape))
    return pl.pallas_call(
        functools.partial(_odd_out_sample_kernel, n_seq=n_seq, n_tok=n_tok, mix_terms=mix_terms),
        grid=(1,),
        in_specs=[
            full(4, T, SLAB),
            full(n_seq, (CONV_W - 1) * SLAB),
            full(n_tok * n_tok, SLAB),
            full(n_tok, SLAB),
            full(CONV_W, SLAB),
            full(2 * SLAB, D_MODEL),
            full(T, D_MODEL),
        ],
        out_specs=full(T, D_MODEL),
        out_shape=jax.ShapeDtypeStruct((T, D_MODEL), F32),
        compiler_params=_params("arbitrary"),
        name="odd_out_sample",
    )(z, ctx, coef, bias, conv_w, w_out, x)


def _rope_tables(pos, split):
    half = HEAD_DIM // 2
    inv = jnp.power(jnp.float32(ROPE_THETA), -jnp.arange(half, dtype=F32) / half)
    ang = pos.astype(F32)[:, None] * inv[None, :]
    cos = jnp.cos(ang)
    sin = jnp.sin(ang)
    if split:
        return jnp.tile(cos, (1, LANES // half)), jnp.tile(sin, (1, LANES // half))
    cos_t = jnp.tile(jnp.concatenate([cos, cos], axis=-1), (1, LANES // HEAD_DIM))
    sin_t = jnp.tile(jnp.concatenate([-sin, sin], axis=-1), (1, LANES // HEAD_DIM))
    return cos_t, sin_t


def _split_qk_columns_kernel(w_ref, o_ref):
    half = HEAD_DIM // 2
    n_slabs = w_ref.shape[1] // SLAB
    for s in range(n_slabs):
        cols = slice(s * SLAB, (s + 1) * SLAB)
        x = w_ref[:, cols]
        if s >= 1 and (s - 1) % 3 < 2:
            xt = x.T
            xt = jnp.concatenate(
                [xt[h * HEAD_DIM + part * half:h * HEAD_DIM + (part + 1) * half, :]
                 for part in range(2) for h in range(HEADS)], axis=0)
            x = xt.T
        o_ref[:, cols] = x.astype(BF16)


def _split_qk_columns(w, tr=256):
    d, n = w.shape
    return pl.pallas_call(
        _split_qk_columns_kernel,
        grid=(d // tr,),
        in_specs=[pl.BlockSpec((tr, n), lambda i: (i, 0))],
        out_specs=pl.BlockSpec((tr, n), lambda i: (i, 0)),
        out_shape=jax.ShapeDtypeStruct((d, n), BF16),
        compiler_params=_params("arbitrary"),
        name="split_qk_columns",
    )(w)


def kernel(x_prompt, x_sample, cache_kv_w128, cache_kv_w512, cache_kv_w2048, state_pool, state_conv,
           p_prompt, p_sample, ev_w_in, ev_pool_w, ev_pool_scale, ev_w_out, od_w_in, od_ln_g, od_ln_b,
           od_ws, od_bs, od_conv_w, od_w_out, norm_mix, norm_ffn, norm_ple, ffn_w1, ffn_w2,
           ple_w_proj, ple_w_gate, norm_final):
    n_p, seq, _ = x_prompt.shape
    n_s, n_tok, _ = x_sample.shape
    depth = norm_mix.shape[0]
    tp = n_p * seq
    ts = n_s * n_tok

    bf = lambda w: w.astype(BF16)
    row = lambda v: v.reshape(1, -1)
    ev_w_in_b, ev_pool_w_b, ev_w_out_b = bf(ev_w_in), bf(ev_pool_w), bf(ev_w_out)
    od_w_in_b, od_w_out_b = bf(od_w_in), bf(od_w_out)
    w1_b, w2_b, wg_b, wp_b = bf(ffn_w1), bf(ffn_w2), bf(ple_w_gate), bf(ple_w_proj)

    cos_p, sin_p = _rope_tables(jnp.arange(seq), True)
    pos_s = [PAST_LEN + t for t in range(n_tok)]
    cos_s, sin_s = _rope_tables(jnp.repeat(jnp.asarray(pos_s, jnp.int32), n_s), False)

    mix_terms = tuple(
        tuple(s for s in range(n_tok)
              if pos_s[s] // CHUNK == pos_s[t] // CHUNK and pos_s[s] % CHUNK <= pos_s[t] % CHUNK)
        for t in range(n_tok))
    local = [p % CHUNK for p in pos_s]

    rp = x_prompt.reshape(tp, D_MODEL)
    rs = x_sample.transpose(1, 0, 2).reshape(ts, D_MODEL)
    pp = p_prompt.reshape(depth, tp, PLE_DIM)
    ps = p_sample.transpose(0, 2, 1, 3).reshape(depth, ts, PLE_DIM)

    tm_p, tm_mix, tm_ffn, tf = 512, 512, 512, 512
    tm_odd = 1024
    assert seq % max(tm_p, tm_mix, tm_ffn, tm_odd) == 0 and ts % Q_ROWS == 0
    assert all(win == dil * ATT_BLOCK and (seq // dil) % min(ATT_ROWS, seq // dil) == 0
               and min(win, seq) % (dil * LANES) == 0 for win, dil in DIL_CFG)
    assert tp * HOST_SPLITS == tm_ffn * n_s, "one sample sequence per hosting-MLP grid step"
    assert PAST_LEN >= max(max(POOL_WINDOWS), CONV_W), "sample windows are full"
    assert all(cache.shape[2] == win for cache, (win, _) in
               zip((cache_kv_w128, cache_kv_w512, cache_kv_w2048), DIL_CFG))
    kv_p = [[] for _ in DIL_CFG]
    kv_s = [[] for _ in DIL_CFG]
    pool_p, pool_s, conv_p, conv_s, cv_s = [], [], [], [], []

    for i in range(depth):
        g_mix = row(norm_mix[i])
        ffn_args = (norm_ffn[:, None, :], w1_b, w2_b, norm_ple[:, None, :], wg_b, wp_b,
                    row(norm_final), i == depth - 1)
        if i % 2 == 0:
            e = i // 2
            pscale = row(ev_pool_scale[e])
            dils = tuple(d for _, d in DIL_CFG)
            w_split = _split_qk_columns(ev_w_in[e])
            za, *groups = _proj_even(rp, g_mix, w_split, cos_p, sin_p, tm_p, n_p, dils, True)
            groups = [grp.reshape(3, n_p, dil, seq // dil, SLAB) for grp, dil in zip(groups, dils)]
            attn = [_attn_prompt(grp, dil) for grp, dil in zip(groups, dils)]
            rp = _even_out_prompt(za, attn, ev_pool_w_b[e], pscale, ev_w_out_b[e], rp, seq, tm_mix)
            for g, ((win, _), grp) in enumerate(zip(DIL_CFG, groups)):
                keep = min(win, seq)
                tail = _kv_tail(grp, keep).reshape(n_p, 2, HEADS, HEAD_DIM, keep)
                kv_p[g].append(tail.transpose(0, 4, 1, 2, 3))
            pool_p.append(za.reshape(n_p, seq, SLAB)[:, seq - POOL_STATE:])
            zas, *sgroups = _proj_even(rs, g_mix, ev_w_in_b[e], cos_s, sin_s, ts, 1, (1, 1, 1),
                                       False)
            sgroups = [g.reshape(3, n_tok, n_s, SLAB) for g in sgroups]
            native = lambda c: c.transpose(0, 2, 3, 4, 1).reshape(n_s, 2, SLAB, c.shape[1])
            caches = [native(c[e]) for c in (cache_kv_w128, cache_kv_w512, cache_kv_w2048)]
            rp, pacc, pml = _ffn_ple_hosting(i, rp, pp, *ffn_args, tf, sgroups, caches,
                                             HOST_SPLITS)
            yb = _attn_sample(sgroups, pacc, pml)
            rs = _even_out_sample(zas, state_pool[e].transpose(1, 0, 2),
                                  yb.reshape(n_s, n_tok * SLAB), ev_pool_w_b[e], pscale,
                                  ev_w_out_b[e], rs, n_s, n_tok)
            for g, grp in enumerate(sgroups):
                kv = grp[1:].transpose(2, 1, 0, 3)
                kv_s[g].append(kv.reshape(n_s, n_tok, 2, HEADS, HEAD_DIM))
            a_n = zas.reshape(n_tok, n_s, SLAB).transpose(1, 0, 2)
            pool_s.append(jnp.concatenate([state_pool[e], a_n], axis=1)[:, -POOL_STATE:])
        else:
            o = i // 2
            ln_g, ln_b = row(od_ln_g[o]), row(od_ln_b[o])
            bs_rows = jnp.repeat(od_bs[o].T, LANES, axis=1)
            z = _proj_odd(rp, g_mix, od_w_in_b[o], ln_g, ln_b, tm_odd)
            rp = _odd_out_ffn(i, z, od_ws[o], bs_rows, od_conv_w[o], od_w_out_b[o], rp, seq, pp,
                              *ffn_args, tm_ffn, tf)
            conv_p.append(z.reshape(4, n_p, seq, SLAB)[3, :, seq - (CONV_W - 1):])
            zs = _proj_odd(rs, g_mix, od_w_in_b[o], ln_g, ln_b, ts)
            coef = jnp.stack([jnp.repeat(od_ws[o][:, local[t], local[s]], LANES)
                              for t in range(n_tok) for s in range(n_tok)])
            bias = jnp.stack([bs_rows[local[t]] for t in range(n_tok)])
            rs = _odd_out_sample(zs, state_conv[o].reshape(n_s, (CONV_W - 1) * SLAB), coef, bias,
                                 od_conv_w[o], od_w_out_b[o], rs, n_s, n_tok, mix_terms)
            hd_n = zs[3].reshape(n_tok, n_s, SLAB).transpose(1, 0, 2)
            conv_s.append(jnp.concatenate([state_conv[o], hd_n], axis=1)[:, -(CONV_W - 1):])
            cv_s.append(zs[1].reshape(n_tok, n_s, SLAB).transpose(1, 0, 2))

        rs = _ffn_ple(i, rs, ps, *ffn_args, ts, tf)

    y_prompt = rp.reshape(n_p, seq, D_MODEL)
    y_sample = rs.reshape(n_tok, n_s, D_MODEL).transpose(1, 0, 2)
    st = lambda lst: jnp.stack(lst, axis=0)
    return (y_prompt, y_sample, st(kv_p[0]), st(kv_p[1]), st(kv_p[2]),
            st(kv_s[0]), st(kv_s[1]), st(kv_s[2]),
            st(pool_p), st(pool_s), st(conv_p), st(conv_s), st(cv_s))
```

```python
import functools
import math

import jax
import jax.numpy as jnp
from jax import lax
from jax.experimental import pallas as pl
from jax.experimental.pallas import tpu as pltpu

F32 = jnp.float32
BF16 = jnp.bfloat16

D_MODEL = 1024
D_FF = 4 * D_MODEL
PLE_DIM = 256
EPS = 1e-6
ROPE_THETA = 10000.0
PAST_LEN = 2048

SLAB = 512
POOL_WINDOWS = (2, 4, 8, 16)
POOL_GROUP = 128
POOL_STATE = 15
POOL_HALO = 16
DIL_CFG = ((128, 1), (512, 4), (2048, 16))
HEADS = 8
HEAD_DIM = 64
ATT_BLOCK = 128
ATT_ROWS = 512
CHUNK = 128
C_GROUPS = 4
CONV_W = 3
CONV_HALO = 8
LANES = 128
NEG = -1e30

VMEM_LIMIT = 52 * 1024 * 1024
HOSTING_VMEM_LIMIT = 60 * 1024 * 1024
HOST_SPLITS = 4
Q_ROWS = 8


def _params(*sem):
    return pltpu.CompilerParams(dimension_semantics=sem, vmem_limit_bytes=VMEM_LIMIT)


def _rms(x, g):
    ms = jnp.mean(x * x, axis=-1, keepdims=True)
    return x * lax.rsqrt(ms + EPS) * g


def _gelu(x):
    c = math.sqrt(2.0 / math.pi)
    return 0.5 * x * (1.0 + jnp.tanh(c * (x + 0.044715 * (x * x * x))))


def _proj_even_kernel(x_ref, g_ref, w_ref, cos_ref, sin_ref, za_ref, g0_ref, g1_ref, g2_ref,
                      zs_ref, *, dils, split):
    tm = x_ref.shape[0]
    nl = SLAB // LANES
    hn = _rms(x_ref[...], g_ref[...]).astype(BF16)
    za_ref[...] = jnp.dot(hn, w_ref[:, 0:SLAB], preferred_element_type=F32)

    def rotate(chunks):
        cos = cos_ref[...]
        sin = sin_ref[...]
        if split:
            h = nl // 2
            return ([chunks[i] * cos - chunks[i + h] * sin for i in range(h)]
                    + [chunks[i] * cos + chunks[i - h] * sin for i in range(h, nl)])
        lane = lax.broadcasted_iota(jnp.int32, cos.shape, 1)
        first_half = jnp.bitwise_and(lane, HEAD_DIM - 1) < (HEAD_DIM // 2)
        out = []
        for zc in chunks:
            partner = jnp.where(first_half,
                                pltpu.roll(zc, LANES - HEAD_DIM // 2, 1),
                                pltpu.roll(zc, HEAD_DIM // 2, 1))
            out.append(zc * cos + partner * sin)
        return out

    slot = 0
    for g, (out_ref, dil) in enumerate(zip((g0_ref, g1_ref, g2_ref), dils)):
        for c in range(3):
            col0 = (1 + 3 * g + c) * SLAB
            z = jnp.dot(hn, w_ref[:, col0:col0 + SLAB], preferred_element_type=F32)
            chunks = [z[:, i * LANES:(i + 1) * LANES] for i in range(nl)]
            if c < 2:
                chunks = rotate(chunks)
            for i, zc in enumerate(chunks):
                cols = slice(i * LANES, (i + 1) * LANES)
                if dil == 1:
                    out_ref[c, :, cols] = zc
                else:
                    zs_ref[slot] = zc
                    for r in range(dil):
                        out_ref[c, r, :, cols] = zs_ref[slot, pl.ds(r, tm // dil, stride=dil), :]
                    slot += 1


def _resident(shape):
    return pl.BlockSpec(shape, lambda *_: (0,) * len(shape), pipeline_mode=pl.Buffered(1))


def _proj_even(x, g, w, cos, sin, tm, n_seq, dils, split):
    T = x.shape[0]
    ntab = cos.shape[0] // tm
    tps = T // n_seq // tm

    def group_spec(dil):
        if dil == 1:
            return pl.BlockSpec((3, tm, SLAB), lambda i: (0, i, 0))
        return pl.BlockSpec((3, None, dil, tm // dil, SLAB), lambda i: (0, i // tps, 0, i % tps, 0))

    def group_shape(dil):
        if dil == 1:
            return jax.ShapeDtypeStruct((3, T, SLAB), F32)
        return jax.ShapeDtypeStruct((3, n_seq, dil, T // n_seq // dil, SLAB), F32)

    n_slots = max(1, 3 * (SLAB // LANES) * sum(d > 1 for d in dils))
    return pl.pallas_call(
        functools.partial(_proj_even_kernel, dils=dils, split=split),
        grid=(T // tm,),
        in_specs=[
            pl.BlockSpec((tm, D_MODEL), lambda i: (i, 0)),
            _resident((1, D_MODEL)),
            _resident(w.shape),
            pl.BlockSpec((tm, LANES), lambda i: (i % ntab, 0)),
            pl.BlockSpec((tm, LANES), lambda i: (i % ntab, 0)),
        ],
        out_specs=[pl.BlockSpec((tm, SLAB), lambda i: (i, 0))] + [group_spec(d) for d in dils],
        out_shape=[jax.ShapeDtypeStruct((T, SLAB), F32)] + [group_shape(d) for d in dils],
        scratch_shapes=[pltpu.VMEM((n_slots, tm, LANES), F32)],
        compiler_params=_params("arbitrary"),
        name="proj_even",
    )(x, g, w, cos, sin)


def _proj_odd_kernel(x_ref, g_ref, w_ref, lng_ref, lnb_ref, o_ref):
    hn = _rms(x_ref[...], g_ref[...]).astype(BF16)

    def slab(s):
        return jnp.dot(hn, w_ref[:, s * SLAB:(s + 1) * SLAB], preferred_element_type=F32)

    o_ref[0] = _gelu(slab(0))
    zv = slab(1)
    for c in range(C_GROUPS):
        sl = slice(c * LANES, (c + 1) * LANES)
        v = _gelu(zv[:, sl])
        mu = jnp.mean(v, axis=-1, keepdims=True)
        dv = v - mu
        var = jnp.mean(dv * dv, axis=-1, keepdims=True)
        o_ref[1, :, sl] = dv * lax.rsqrt(var + EPS) * lng_ref[:, sl] + lnb_ref[:, sl]
    o_ref[2] = slab(2)
    o_ref[3] = slab(3) * slab(4)


def _proj_odd(x, g, w, ln_g, ln_b, tm):
    T = x.shape[0]
    return pl.pallas_call(
        _proj_odd_kernel,
        grid=(T // tm,),
        in_specs=[
            pl.BlockSpec((tm, D_MODEL), lambda i: (i, 0)),
            _resident((1, D_MODEL)),
            _resident(w.shape),
            _resident((1, SLAB)),
            _resident((1, SLAB)),
        ],
        out_specs=pl.BlockSpec((4, tm, SLAB), lambda i: (0, i, 0)),
        out_shape=jax.ShapeDtypeStruct((4, T, SLAB), F32),
        compiler_params=_params("arbitrary"),
        name="proj_odd",
    )(x, g, w, ln_g, ln_b)


def _attn_prompt_kernel(q_ref, kp_ref, kc_ref, vp_ref, vc_ref, o_ref, lse_ref,
                        qst_ref, ks_ref, vs_ref, bias_ref, s_ref, e_ref, mx_ref):
    step = pl.program_id(2)
    att_rows = q_ref.shape[0]
    sub = att_rows // ATT_BLOCK
    half_rot = HEAD_DIM // 2
    heads_per_blk = LANES // half_rot
    n_blk = HEADS // heads_per_blk
    log2e = 1.4426950408889634
    ln2 = 0.6931471805599453
    scale = HEAD_DIM ** -0.5 * log2e

    qlane = lax.broadcasted_iota(jnp.int32, (att_rows, 2 * LANES), 1)
    head_in_blk = jnp.bitwise_and(qlane, LANES - 1) // half_rot
    for j in range(n_blk):
        lo = slice(j * LANES, (j + 1) * LANES)
        hi = slice((n_blk + j) * LANES, (n_blk + j + 1) * LANES)
        dst = slice(2 * j * LANES, (2 * j + 2) * LANES)
        q2 = (jnp.concatenate([q_ref[:, lo], q_ref[:, hi]], axis=1) * scale).astype(BF16)
        for hq in range(heads_per_blk):
            qm = jnp.where(head_in_blk == hq, q2, jnp.zeros_like(q2))
            for b in range(sub):
                r0 = ((j * sub + b) * heads_per_blk + hq) * ATT_BLOCK
                qst_ref[r0:r0 + ATT_BLOCK, :] = qm[b * ATT_BLOCK:(b + 1) * ATT_BLOCK, :]
        ks_ref[0:ATT_BLOCK, dst] = jnp.concatenate([kp_ref[:, lo], kp_ref[:, hi]],
                                                   axis=1).astype(BF16)
        ks_ref[ATT_BLOCK:, dst] = jnp.concatenate([kc_ref[:, lo], kc_ref[:, hi]],
                                                  axis=1).astype(BF16)
    for hp in range(HEADS // 2):
        src = slice(hp * LANES, (hp + 1) * LANES)
        vs_ref[0:ATT_BLOCK, 2 * hp * LANES:(2 * hp + 1) * LANES] = vp_ref[:, src].astype(BF16)
        vs_ref[ATT_BLOCK:, 2 * hp * LANES:(2 * hp + 1) * LANES] = vc_ref[:, src].astype(BF16)
        vs_ref[:, (2 * hp + 1) * LANES:(2 * hp + 2) * LANES] = jnp.ones(
            (att_rows + ATT_BLOCK, LANES), BF16)

    qi = lax.broadcasted_iota(jnp.int32, (ATT_BLOCK, 2 * ATT_BLOCK), 0)
    ki = lax.broadcasted_iota(jnp.int32, (ATT_BLOCK, 2 * ATT_BLOCK), 1)
    rel = qi + ATT_BLOCK - ki
    band = jnp.logical_and(rel >= 0, rel <= ATT_BLOCK)
    bias_ref[0] = jnp.where(band, 0.0, NEG)
    bias_ref[1] = jnp.where(jnp.logical_and(band, ki >= ATT_BLOCK), 0.0, NEG)
    lane = lax.broadcasted_iota(jnp.int32, (ATT_BLOCK, LANES), 1)
    low_head = lane < HEAD_DIM
    blk_rows = heads_per_blk * ATT_BLOCK

    first = (step == 0).astype(jnp.int32)
    hb = ATT_BLOCK // 2
    lane_hb = lax.broadcasted_iota(jnp.int32, (hb, LANES), 1)

    def scores(b):
        for j in range(n_blk):
            base = (j * sub + b) * blk_rows
            k2 = ks_ref[b * ATT_BLOCK:(b + 2) * ATT_BLOCK, 2 * j * LANES:(2 * j + 2) * LANES]
            s_ref[b, j * blk_rows:(j + 1) * blk_rows, :] = lax.dot_general(
                qst_ref[base:base + blk_rows, :], k2, (((1,), (1,)), ((), ())),
                preferred_element_type=F32)

    def softmax(b):
        which = first if b == 0 else 0
        for rh in range(2):
            m_tile = jnp.zeros((hb, LANES), F32)
            for h in range(HEADS):
                rows = slice(h * ATT_BLOCK + rh * hb, h * ATT_BLOCK + (rh + 1) * hb)
                s = s_ref[b, rows, :] + bias_ref[which, rh * hb:(rh + 1) * hb, :]
                m = jnp.max(s, axis=-1, keepdims=True)
                e_ref[b, rows, :] = jnp.exp2((s - m).astype(BF16))
                m_tile = jnp.where(lane_hb == h, m, m_tile)
            mx_ref[b, rh * hb:(rh + 1) * hb, :] = m_tile

    def weighted_values(b):
        log_l = jnp.zeros((ATT_BLOCK, LANES), F32)
        for hp in range(HEADS // 2):
            v_ones = vs_ref[b * ATT_BLOCK:(b + 2) * ATT_BLOCK, 2 * hp * LANES:(2 * hp + 2) * LANES]
            pv = jnp.dot(e_ref[b, 2 * hp * ATT_BLOCK:(2 * hp + 2) * ATT_BLOCK, :], v_ones,
                         preferred_element_type=F32)
            l_lo, l_hi = pv[0:ATT_BLOCK, LANES:], pv[ATT_BLOCK:, LANES:]
            o_ref[b * ATT_BLOCK:(b + 1) * ATT_BLOCK, hp * LANES:(hp + 1) * LANES] = jnp.where(
                low_head, pv[0:ATT_BLOCK, 0:LANES] * (1.0 / l_lo),
                pv[ATT_BLOCK:, 0:LANES] * (1.0 / l_hi)).astype(o_ref.dtype)
            log_l = jnp.where(lane == 2 * hp, jnp.log(l_lo),
                              jnp.where(lane == 2 * hp + 1, jnp.log(l_hi), log_l))
        lse_ref[b * ATT_BLOCK:(b + 1) * ATT_BLOCK, :] = mx_ref[b] * ln2 + log_l

    for t in range(sub + 2):
        if t < sub:
            scores(t)
        if 0 <= t - 1 < sub:
            softmax(t - 1)
        if 0 <= t - 2 < sub:
            weighted_values(t - 2)


def _attn_prompt(qkv, dil):
    _, n_seq, _, m_rows, _ = qkv.shape
    att_rows = min(ATT_ROWS, m_rows)
    steps = m_rows // att_rows
    sub = att_rows // ATT_BLOCK

    def cur(slab):
        return pl.BlockSpec((None, None, None, att_rows, SLAB), lambda n, r, b: (slab, n, r, b, 0))

    def prev(slab):
        return pl.BlockSpec((None, None, None, ATT_BLOCK, SLAB),
                            lambda n, r, b: (slab, n, r, jnp.maximum(b * sub - 1, 0), 0))

    return pl.pallas_call(
        _attn_prompt_kernel,
        grid=(n_seq, dil, steps),
        in_specs=[cur(0), prev(1), cur(1), prev(2), cur(2)],
        out_specs=[
            pl.BlockSpec((None, None, att_rows, SLAB), lambda n, r, b: (n, r, b, 0)),
            pl.BlockSpec((None, None, att_rows, LANES), lambda n, r, b: (n, r, b, 0)),
        ],
        out_shape=[
            jax.ShapeDtypeStruct((n_seq, dil, m_rows, SLAB), BF16),
            jax.ShapeDtypeStruct((n_seq, dil, m_rows, LANES), F32),
        ],
        scratch_shapes=[
            pltpu.VMEM((HEADS * att_rows, 2 * LANES), BF16),
            pltpu.VMEM((att_rows + ATT_BLOCK, SLAB), BF16),
            pltpu.VMEM((att_rows + ATT_BLOCK, 2 * SLAB), BF16),
            pltpu.VMEM((2, ATT_BLOCK, 2 * ATT_BLOCK), F32),
            pltpu.VMEM((sub, HEADS * ATT_BLOCK, 2 * ATT_BLOCK), F32),
            pltpu.VMEM((sub, HEADS * ATT_BLOCK, 2 * ATT_BLOCK), BF16),
            pltpu.VMEM((sub, ATT_BLOCK, LANES), F32),
        ],
        compiler_params=_params("arbitrary", "arbitrary", "arbitrary"),
        name=f"attn_prompt_d{dil}",
    )(qkv, qkv, qkv, qkv, qkv)


def _kv_tail_kernel(k_ref, v_ref, o_ref, tok_ref):
    dil, rows, _ = k_ref.shape
    keep = dil * rows
    nl = SLAB // LANES
    half_rot = HEAD_DIM // 2
    heads_per_chunk = LANES // half_rot
    for kv, src in enumerate((k_ref, v_ref)):
        for lc in range(nl):
            for r in range(dil):
                tok_ref[lc, pl.ds(r, rows, stride=dil), :] = src[r, :, lc * LANES:(lc + 1) * LANES]
            for pc in range(keep // LANES):
                pcols = slice(pc * LANES, (pc + 1) * LANES)
                t = tok_ref[lc, pcols, :].T
                if kv == 0:
                    half, hblk = divmod(lc, nl // 2)
                    for hq in range(heads_per_chunk):
                        r0 = (hblk * heads_per_chunk + hq) * HEAD_DIM + half * half_rot
                        o_ref[0, r0:r0 + half_rot, pcols] = t[hq * half_rot:(hq + 1) * half_rot, :]
                else:
                    o_ref[1, lc * LANES:(lc + 1) * LANES, pcols] = t


def _kv_tail(grp, keep):
    _, n_seq, dil, m_rows, _ = grp.shape
    rows = keep // dil
    last = m_rows // rows - 1

    def spec(slab):
        return pl.BlockSpec((None, None, dil, rows, SLAB), lambda n: (slab, n, 0, last, 0))

    return pl.pallas_call(
        _kv_tail_kernel,
        grid=(n_seq,),
        in_specs=[spec(1), spec(2)],
        out_specs=pl.BlockSpec((None, 2, SLAB, keep), lambda n: (n, 0, 0, 0)),
        out_shape=jax.ShapeDtypeStruct((n_seq, 2, SLAB, keep), F32),
        scratch_shapes=[pltpu.VMEM((SLAB // LANES, keep, LANES), F32)],
        compiler_params=_params("arbitrary"),
        name=f"kv_tail_d{dil}",
    )(grp, grp)


def _pool_mix(window_terms, a_cols, cnt, pw_ref, scale_ref, gi):
    acc = a_cols
    for term in window_terms:
        acc = acc + term
    pooled = acc / cnt - a_cols
    cols = slice(gi * POOL_GROUP, (gi + 1) * POOL_GROUP)
    mixed = jnp.dot(pooled.astype(BF16), pw_ref[gi], preferred_element_type=F32)
    return mixed * scale_ref[:, cols]


def _even_out_prompt_kernel(a_ref, halo_ref, o0_ref, o1_ref, o2_ref, l0_ref, l1_ref, l2_ref,
                            pw_ref, ps_ref, wo_ref, x_ref, out_ref, ext_ref, ya_ref, yb_ref,
                            oi_ref, li_ref, *, tiles_per_seq):
    tm = a_ref.shape[0]
    it = pl.program_id(0) % tiles_per_seq
    halo = halo_ref[...]
    ext_ref[0:POOL_HALO, :] = jnp.where(it == 0, jnp.zeros_like(halo), halo)
    ext_ref[POOL_HALO:, :] = a_ref[...]
    pos = it * tm + lax.broadcasted_iota(jnp.int32, (tm, 1), 0)
    for gi, w in enumerate(POOL_WINDOWS):
        cols = slice(gi * POOL_GROUP, (gi + 1) * POOL_GROUP)
        terms = [ext_ref[POOL_HALO - k:POOL_HALO - k + tm, cols] for k in range(1, w)]
        cnt = jnp.minimum(w, pos + 1).astype(F32)
        ya_ref[:, cols] = _pool_mix(terms, a_ref[:, cols], cnt, pw_ref, ps_ref, gi).astype(BF16)

    for g, (o_ref, l_ref) in enumerate(((o0_ref, l0_ref), (o1_ref, l1_ref), (o2_ref, l2_ref))):
        dil = o_ref.shape[0]
        for r in range(dil):
            rows = pl.ds(r, tm // dil, stride=dil)
            li_ref[g, rows, :] = l_ref[r]
            for lc in range(SLAB // LANES):
                oi_ref[g, lc, rows, :] = o_ref[r, :, lc * LANES:(lc + 1) * LANES].astype(F32)

    l0, l1, l2 = li_ref[0], li_ref[1], li_ref[2]
    mx = jnp.maximum(jnp.maximum(l0, l1), l2)
    e0, e1, e2 = jnp.exp(l0 - mx), jnp.exp(l1 - mx), jnp.exp(l2 - mx)
    inv = 1.0 / (e0 + e1 + e2)
    w0, w1 = e0 * inv, e1 * inv
    for h in range(HEADS):
        lc, lo = divmod(h * HEAD_DIM, LANES)
        sl = slice(lo, lo + HEAD_DIM)
        o2 = oi_ref[2, lc, :, sl]
        yb = (o2 + w0[:, h:h + 1] * (oi_ref[0, lc, :, sl] - o2)
              + w1[:, h:h + 1] * (oi_ref[1, lc, :, sl] - o2))
        yb_ref[:, h * HEAD_DIM:(h + 1) * HEAD_DIM] = yb.astype(BF16)

    y = jnp.dot(ya_ref[...], wo_ref[0:SLAB, :], preferred_element_type=F32)
    y = y + jnp.dot(yb_ref[...], wo_ref[SLAB:, :], preferred_element_type=F32)
    out_ref[...] = x_ref[...] + y


def _even_out_prompt(za, attn, pool_w, pool_scale, w_out, x, seq, tm):
    T = x.shape[0]
    tps = seq // tm
    hb = tm // POOL_HALO

    def residue_spec(dil, width):
        return pl.BlockSpec((None, dil, tm // dil, width), lambda i: (i // tps, 0, i % tps, 0))

    dils = [o.shape[1] for o, _ in attn]
    return pl.pallas_call(
        functools.partial(_even_out_prompt_kernel, tiles_per_seq=tps),
        grid=(T // tm,),
        in_specs=[
            pl.BlockSpec((tm, SLAB), lambda i: (i, 0)),
            pl.BlockSpec((POOL_HALO, SLAB), lambda i: (jnp.maximum(i * hb - 1, 0), 0)),
        ]
        + [residue_spec(d, SLAB) for d in dils] + [residue_spec(d, LANES) for d in dils]
        + [
            pl.BlockSpec((len(POOL_WINDOWS), POOL_GROUP, POOL_GROUP), lambda i: (0, 0, 0)),
            pl.BlockSpec((1, SLAB), lambda i: (0, 0)),
            pl.BlockSpec((2 * SLAB, D_MODEL), lambda i: (0, 0)),
            pl.BlockSpec((tm, D_MODEL), lambda i: (i, 0)),
        ],
        out_specs=pl.BlockSpec((tm, D_MODEL), lambda i: (i, 0)),
        out_shape=jax.ShapeDtypeStruct((T, D_MODEL), F32),
        scratch_shapes=[
            pltpu.VMEM((tm + POOL_HALO, SLAB), F32),
            pltpu.VMEM((tm, SLAB), BF16),
            pltpu.VMEM((tm, SLAB), BF16),
            pltpu.VMEM((len(dils), SLAB // LANES, tm, LANES), F32),
            pltpu.VMEM((len(dils), tm, LANES), F32),
        ],
        compiler_params=_params("arbitrary"),
        name="even_out_prompt",
    )(za, za, *[o for o, _ in attn], *[l for _, l in attn], pool_w, pool_scale, w_out, x)


def _odd_mix(u_ref, vn_ref, go_ref, hd_ref, hdh_ref, ws_ref, bs_ref, cw_ref, wo_ref, x_ref,
             ext_ref, yc_ref, yd_ref, tiles_per_seq):
    tm = u_ref.shape[0]
    it = pl.program_id(0) % tiles_per_seq

    ti = lax.broadcasted_iota(jnp.int32, (CHUNK, CHUNK), 0)
    si = lax.broadcasted_iota(jnp.int32, (CHUNK, CHUNK), 1)
    for g in range(C_GROUPS):
        cols = slice(g * LANES, (g + 1) * LANES)
        wm = jnp.where(si <= ti, ws_ref[g], 0.0).astype(BF16)
        for c in range(tm // CHUNK):
            rows = slice(c * CHUNK, (c + 1) * CHUNK)
            sp = jnp.dot(wm, vn_ref[rows, cols].astype(BF16), preferred_element_type=F32)
            sp = sp + bs_ref[:, cols]
            yc_ref[rows, cols] = (u_ref[rows, cols] * sp).astype(BF16)

    hd = hd_ref[...]
    halo = hdh_ref[...]
    ext_ref[0:CONV_HALO, :] = jnp.where(it == 0, jnp.zeros_like(halo), halo)
    ext_ref[CONV_HALO:, :] = hd
    conv = cw_ref[CONV_W - 1:CONV_W, :] * hd
    for j in range(CONV_W - 1):
        off = CONV_HALO - (CONV_W - 1) + j
        conv = conv + cw_ref[j:j + 1, :] * ext_ref[off:off + tm, :]
    yd_ref[...] = (go_ref[...] * conv).astype(BF16)

    y = jnp.dot(yc_ref[...], wo_ref[0:SLAB, :], preferred_element_type=F32)
    y = y + jnp.dot(yd_ref[...], wo_ref[SLAB:, :], preferred_element_type=F32)
    return x_ref[...] + y


def _odd_out_ffn_kernel(u_ref, vn_ref, go_ref, hd_ref, hdh_ref, ws_ref, bs_ref, cw_ref, wo_ref,
                        x_ref, p_ref, gf_ref, w1_ref, w2_ref, gp_ref, wg_ref, wp_ref, gl_ref,
                        out_ref, ext_ref, yc_ref, yd_ref, *, tiles_per_seq, final_norm, tf):
    r = _odd_mix(u_ref, vn_ref, go_ref, hd_ref, hdh_ref, ws_ref, bs_ref, cw_ref, wo_ref, x_ref,
                 ext_ref, yc_ref, yd_ref, tiles_per_seq)
    out_ref[...] = _mlp_ple(r, p_ref, gf_ref, w1_ref, w2_ref, gp_ref, wg_ref, wp_ref, gl_ref,
                            final_norm, tf)


def _odd_out_ffn(layer, z, ws, bs_rows, conv_w, w_out, x, seq, p, g_ffn, w1, w2, g_ple, wg, wp,
                 g_last, final_norm, tm, tf):
    T = x.shape[0]
    tiles_per_seq = seq // tm
    hb = tm // CONV_HALO

    def slab(s):
        return pl.BlockSpec((None, tm, SLAB), lambda i: (s, i, 0))

    return pl.pallas_call(
        functools.partial(_odd_out_ffn_kernel, tiles_per_seq=tiles_per_seq,
                          final_norm=final_norm, tf=tf),
        grid=(T // tm,),
        in_specs=[
            slab(0), slab(1), slab(2), slab(3),
            pl.BlockSpec((None, CONV_HALO, SLAB), lambda i: (3, jnp.maximum(i * hb - 1, 0), 0)),
            _resident((C_GROUPS, CHUNK, CHUNK)),
            _resident((CHUNK, SLAB)),
            _resident((CONV_W, SLAB)),
            _resident((2 * SLAB, D_MODEL)),
            pl.BlockSpec((tm, D_MODEL), lambda i: (i, 0)),
            pl.BlockSpec((None, tm, PLE_DIM), lambda i: (layer, i, 0)),
            _layer_resident(layer, (1, D_MODEL)),
            _layer_resident(layer, (D_MODEL, D_FF)),
            _layer_resident(layer, (D_FF, D_MODEL)),
            _layer_resident(layer, (1, D_MODEL)),
            _layer_resident(layer, (D_MODEL, D_MODEL)),
            _layer_resident(layer, (PLE_DIM, D_MODEL)),
            _resident((1, D_MODEL)),
        ],
        out_specs=pl.BlockSpec((tm, D_MODEL), lambda i: (i, 0)),
        out_shape=jax.ShapeDtypeStruct((T, D_MODEL), F32),
        scratch_shapes=[
            pltpu.VMEM((tm + CONV_HALO, SLAB), F32),
            pltpu.VMEM((tm, SLAB), BF16),
            pltpu.VMEM((tm, SLAB), BF16),
        ],
        compiler_params=_params("arbitrary"),
        name="odd_out_ffn",
    )(z, z, z, z, z, ws, bs_rows, conv_w, w_out, x, p, g_ffn, w1, w2, g_ple, wg, wp, g_last)


def _mlp_ple(x, p_ref, gf_ref, w1_ref, w2_ref, gp_ref, wg_ref, wp_ref, gl_ref, final_norm, tf):
    hn = _rms(x, gf_ref[...]).astype(BF16)
    acc = None
    for c in range(D_FF // tf):
        h1 = jnp.dot(hn, w1_ref[:, c * tf:(c + 1) * tf], preferred_element_type=F32)
        h1 = jnp.square(jnp.maximum(h1, 0.0)).astype(BF16)
        part = jnp.dot(h1, w2_ref[c * tf:(c + 1) * tf, :], preferred_element_type=F32)
        acc = part if acc is None else acc + part
    r = x + acc
    hp = _rms(r, gp_ref[...]).astype(BF16)
    gate = jax.nn.sigmoid(jnp.dot(hp, wg_ref[...], preferred_element_type=F32))
    proj = jnp.dot(p_ref[...].astype(BF16), wp_ref[...], preferred_element_type=F32)
    r = r + gate * proj
    if final_norm:
        r = _rms(r, gl_ref[...])
    return r


def _ffn_ple_kernel(x_ref, p_ref, gf_ref, w1_ref, w2_ref, gp_ref, wg_ref, wp_ref, gl_ref,
                    out_ref, *, final_norm, tf):
    out_ref[...] = _mlp_ple(x_ref[...], p_ref, gf_ref, w1_ref, w2_ref, gp_ref, wg_ref, wp_ref,
                            gl_ref, final_norm, tf)


def _layer_resident(layer, shape):
    return pl.BlockSpec((None,) + shape, lambda *_: (layer,) + (0,) * len(shape),
                        pipeline_mode=pl.Buffered(1))


def _ffn_ple(layer, x, p, g_ffn, w1, w2, g_ple, wg, wp, g_last, final_norm, tm, tf):
    T = x.shape[0]
    return pl.pallas_call(
        functools.partial(_ffn_ple_kernel, final_norm=final_norm, tf=tf),
        grid=(T // tm,),
        in_specs=[
            pl.BlockSpec((tm, D_MODEL), lambda i: (i, 0)),
            pl.BlockSpec((None, tm, PLE_DIM), lambda i: (layer, i, 0)),
            _layer_resident(layer, (1, D_MODEL)),
            _layer_resident(layer, (D_MODEL, D_FF)),
            _layer_resident(layer, (D_FF, D_MODEL)),
            _layer_resident(layer, (1, D_MODEL)),
            _layer_resident(layer, (D_MODEL, D_MODEL)),
            _layer_resident(layer, (PLE_DIM, D_MODEL)),
            _resident((1, D_MODEL)),
        ],
        out_specs=pl.BlockSpec((tm, D_MODEL), lambda i: (i, 0)),
        out_shape=jax.ShapeDtypeStruct((T, D_MODEL), F32),
        compiler_params=_params("arbitrary"),
        name="ffn_ple",
    )(x, p, g_ffn, w1, w2, g_ple, wg, wp, g_last)


def _ffn_ple_hosting_kernel(x_ref, p_ref, gf_ref, w1_ref, w2_ref, gp_ref, wg_ref, wp_ref, gl_ref,
                            q0_ref, q1_ref, q2_ref, c0_ref, c1_ref, c2_ref, out_ref, pacc_ref,
                            pml_ref, hn_ref, acc_ref, *, final_norm, tf, splits):
    j = pl.program_id(1)
    share = D_FF // splits

    def mlp_share(k):
        acc = None
        for c in range(share // tf):
            cols = slice(k * share + c * tf, k * share + (c + 1) * tf)
            h1 = jnp.dot(hn_ref[...], w1_ref[:, cols], preferred_element_type=F32)
            h1 = jnp.square(jnp.maximum(h1, 0.0)).astype(BF16)
            part = jnp.dot(h1, w2_ref[cols, :], preferred_element_type=F32)
            acc = part if acc is None else acc + part
        return acc

    @pl.when(j == 0)
    def _():
        hn_ref[...] = _rms(x_ref[...], gf_ref[...]).astype(BF16)
        acc_ref[...] = mlp_share(0)

    for k in range(1, splits - 1):
        @pl.when(j == k)
        def _(k=k):
            acc_ref[...] += mlp_share(k)

    @pl.when(j == splits - 1)
    def _():
        r = x_ref[...] + (acc_ref[...] + mlp_share(splits - 1))
        hp = _rms(r, gp_ref[...]).astype(BF16)
        gate = jax.nn.sigmoid(jnp.dot(hp, wg_ref[...], preferred_element_type=F32))
        proj = jnp.dot(p_ref[...].astype(BF16), wp_ref[...], preferred_element_type=F32)
        r = r + gate * proj
        if final_norm:
            r = _rms(r, gl_ref[...])
        out_ref[...] = r

    row = (splits * pl.program_id(0) + j) % Q_ROWS
    q_rows = [[q_ref[t, pl.ds(row, 1), :] for t in range(q_ref.shape[0])]
              for q_ref in (q0_ref, q1_ref, q2_ref)]
    _partial_cached_attention(q_rows, (c0_ref, c1_ref, c2_ref), pacc_ref, pml_ref)


def _ffn_ple_hosting(layer, x, p, g_ffn, w1, w2, g_ple, wg, wp, g_last, final_norm, tf,
                     qkv, caches, splits):
    T = x.shape[0]
    _, n_tok, n_seq, _ = qkv[0].shape
    tm = splits * T // n_seq
    seq_of = lambda i, j: splits * i + j

    def cache_spec(c):
        return pl.BlockSpec((None, 2, SLAB, c.shape[-1]), lambda i, j: (seq_of(i, j), 0, 0, 0))

    return pl.pallas_call(
        functools.partial(_ffn_ple_hosting_kernel, final_norm=final_norm, tf=tf, splits=splits),
        grid=(T // tm, splits),
        in_specs=[
            pl.BlockSpec((tm, D_MODEL), lambda i, j: (i, 0)),
            pl.BlockSpec((None, tm, PLE_DIM), lambda i, j: (layer, i, 0)),
            _layer_resident(layer, (1, D_MODEL)),
            _layer_resident(layer, (D_MODEL, D_FF)),
            _layer_resident(layer, (D_FF, D_MODEL)),
            _layer_resident(layer, (1, D_MODEL)),
            _layer_resident(layer, (D_MODEL, D_MODEL)),
            _layer_resident(layer, (PLE_DIM, D_MODEL)),
            _resident((1, D_MODEL)),
        ] + [pl.BlockSpec((None, n_tok, Q_ROWS, SLAB),
                          lambda i, j: (0, 0, seq_of(i, j) // Q_ROWS, 0)) for _ in qkv]
        + [cache_spec(c) for c in caches],
        out_specs=[
            pl.BlockSpec((tm, D_MODEL), lambda i, j: (i, 0)),
            pl.BlockSpec((None, n_tok * HEADS, SLAB), lambda i, j: (seq_of(i, j), 0, 0)),
            pl.BlockSpec((None, n_tok * HEADS, LANES), lambda i, j: (seq_of(i, j), 0, 0)),
        ],
        out_shape=[
            jax.ShapeDtypeStruct((T, D_MODEL), F32),
            jax.ShapeDtypeStruct((n_seq, n_tok * HEADS, SLAB), F32),
            jax.ShapeDtypeStruct((n_seq, n_tok * HEADS, LANES), F32),
        ],
        scratch_shapes=[pltpu.VMEM((tm, D_MODEL), BF16), pltpu.VMEM((tm, D_MODEL), F32)],
        compiler_params=pltpu.CompilerParams(dimension_semantics=("arbitrary", "arbitrary"),
                                             vmem_limit_bytes=HOSTING_VMEM_LIMIT),
        name="ffn_ple_hosting",
    )(x, p, g_ffn, w1, w2, g_ple, wg, wp, g_last, *qkv, *caches)


def _own_lanes():
    sub = lax.broadcasted_iota(jnp.int32, (HEADS, SLAB), 0)
    lane_head = lax.broadcasted_iota(jnp.int32, (HEADS, SLAB), 1) // HEAD_DIM
    return sub == lane_head


def _block_diag_queries(q_rows):
    own = _own_lanes()
    scale = HEAD_DIM ** -0.5
    return jnp.concatenate(
        [jnp.where(own, jnp.broadcast_to(q * scale, (HEADS, SLAB)), 0.0) for q in q_rows], axis=0)


def _cached_scores(qbd, kt, dil):
    s = jnp.dot(qbd.astype(BF16), kt.astype(BF16), preferred_element_type=F32)
    pos = lax.broadcasted_iota(jnp.int32, s.shape, 1)
    row_tok = lax.broadcasted_iota(jnp.int32, (s.shape[0], 1), 0) // HEADS
    valid = (pos >= row_tok) if dil == 1 else (jnp.bitwise_and(pos, dil - 1) == row_tok)
    return jnp.where(valid, s, NEG)


def _weighted_cached_values(e, vt):
    return lax.dot_general(e.astype(BF16), vt.astype(BF16), (((1,), (1,)), ((), ())),
                           preferred_element_type=F32)


def _partial_cached_attention(q_rows, cache_refs, acc_ref, ml_ref):
    scores = []
    m = None
    for g, (c_ref, (_, dil)) in enumerate(zip(cache_refs, DIL_CFG)):
        s = _cached_scores(_block_diag_queries(q_rows[g]), c_ref[0], dil)
        row_max = jnp.max(s, axis=1, keepdims=True)
        m = row_max if m is None else jnp.maximum(m, row_max)
        scores.append(s)
    acc, l = None, None
    for s, c_ref in zip(scores, cache_refs):
        e = jnp.exp(s - m)
        part = _weighted_cached_values(e, c_ref[1])
        row_sum = jnp.sum(e, axis=1, keepdims=True)
        acc = part if acc is None else acc + part
        l = row_sum if l is None else l + row_sum
    acc_ref[...] = acc
    lane = lax.broadcasted_iota(jnp.int32, ml_ref.shape, 1)
    ml_ref[...] = jnp.where(lane == 0, m, jnp.where(lane == 1, l, 0.0))


def _attn_sample_kernel(g0_ref, g1_ref, g2_ref, pacc_ref, pml_ref, o_ref, *, n_seq):
    groups = (g0_ref, g1_ref, g2_ref)
    n_tok = g0_ref.shape[1]
    n_rows = n_tok * HEADS
    own = _own_lanes()
    row_tok = lax.broadcasted_iota(jnp.int32, (n_rows, 1), 0) // HEADS

    def one_sequence(n, carry):
        m_part = pml_ref[n, :, 0:1]
        m = m_part
        s_new = []
        for g, (g_ref, (_, dil)) in enumerate(zip(groups, DIL_CFG)):
            qbd = _block_diag_queries([g_ref[0, t, pl.ds(n, 1), :] for t in range(n_tok)])
            for tp in range(n_tok):
                sn = jnp.sum(qbd * g_ref[1, tp, pl.ds(n, 1), :], axis=1, keepdims=True)
                ok = (row_tok >= tp) if dil == 1 else (row_tok == tp)
                sn = jnp.where(ok, sn, NEG)
                m = jnp.maximum(m, sn)
                s_new.append((g, tp, sn))
        w_part = jnp.exp(m_part - m)
        acc = w_part * pacc_ref[n]
        den = w_part * pml_ref[n, :, 1:2]
        for g, tp, sn in s_new:
            e = jnp.exp(sn - m)
            den = den + e
            acc = acc + e * groups[g][2, tp, pl.ds(n, 1), :]
        res = acc * (1.0 / den)
        for t in range(n_tok):
            rows = res[t * HEADS:(t + 1) * HEADS, :]
            o_ref[n, t:t + 1, :] = jnp.sum(jnp.where(own, rows, 0.0), axis=0, keepdims=True)
        return carry

    lax.fori_loop(0, n_seq, one_sequence, 0, unroll=4)


def _attn_sample(qkv, pacc, pml, seqs_per_step=16):
    _, n_tok, n_seq, _ = qkv[0].shape
    nb = seqs_per_step

    def spec(*tail):
        return pl.BlockSpec((nb,) + tail, lambda i: (i,) + (0,) * len(tail))

    return pl.pallas_call(
        functools.partial(_attn_sample_kernel, n_seq=nb),
        grid=(n_seq // nb,),
        in_specs=[pl.BlockSpec((3, n_tok, nb, SLAB), lambda i: (0, 0, i, 0)) for _ in qkv]
        + [spec(n_tok * HEADS, SLAB), spec(n_tok * HEADS, LANES)],
        out_specs=spec(n_tok, SLAB),
        out_shape=jax.ShapeDtypeStruct((n_seq, n_tok, SLAB), F32),
        compiler_params=_params("arbitrary"),
        name="attn_sample",
    )(*qkv, pacc, pml)


def _even_out_sample_kernel(a_ref, ctx_ref, yb_ref, pw_ref, ps_ref, wo_ref, x_ref, out_ref,
                            *, n_seq, n_tok):
    def ext_row(e, cols):
        if e >= POOL_STATE:
            t = e - POOL_STATE
            return a_ref[t * n_seq:(t + 1) * n_seq, cols]
        return ctx_ref[e, :, cols]

    for t in range(n_tok):
        rows = slice(t * n_seq, (t + 1) * n_seq)
        ya = []
        for gi, w in enumerate(POOL_WINDOWS):
            cols = slice(gi * POOL_GROUP, (gi + 1) * POOL_GROUP)
            terms = [ext_row(POOL_STATE + t - k, cols) for k in range(1, w)]
            ya.append(_pool_mix(terms, a_ref[rows, cols], float(w), pw_ref, ps_ref, gi))
        y = jnp.zeros((n_seq, D_MODEL), F32)
        for gi in range(len(POOL_WINDOWS)):
            y = y + jnp.dot(ya[gi].astype(BF16), wo_ref[gi * POOL_GROUP:(gi + 1) * POOL_GROUP, :],
                            preferred_element_type=F32)
        yb = yb_ref[:, t * SLAB:(t + 1) * SLAB].astype(BF16)
        y = y + jnp.dot(yb, wo_ref[SLAB:, :], preferred_element_type=F32)
        out_ref[rows, :] = x_ref[rows, :] + y


def _even_out_sample(z, ctx, yb, pool_w, pool_scale, w_out, x, n_seq, n_tok):
    T = x.shape[0]
    full = lambda *shape: pl.BlockSpec(shape, lambda i: (0,) * len(shape))
    return pl.pallas_call(
        functools.partial(_even_out_sample_kernel, n_seq=n_seq, n_tok=n_tok),
        grid=(1,),
        in_specs=[
            full(T, SLAB),
            full(POOL_STATE, n_seq, SLAB),
            full(n_seq, n_tok * SLAB),
            full(len(POOL_WINDOWS), POOL_GROUP, POOL_GROUP),
            full(1, SLAB),
            full(2 * SLAB, D_MODEL),
            full(T, D_MODEL),
        ],
        out_specs=full(T, D_MODEL),
        out_shape=jax.ShapeDtypeStruct((T, D_MODEL), F32),
        compiler_params=_params("arbitrary"),
        name="even_out_sample",
    )(z, ctx, yb, pool_w, pool_scale, w_out, x)


def _odd_out_sample_kernel(z_ref, ctx_ref, coef_ref, bias_ref, cw_ref, wo_ref, x_ref,
                           out_ref, *, n_seq, n_tok, mix_terms):
    def ext_row(e):
        if e >= CONV_W - 1:
            t = e - (CONV_W - 1)
            return z_ref[3, t * n_seq:(t + 1) * n_seq, :]
        return ctx_ref[:, e * SLAB:(e + 1) * SLAB]

    for t in range(n_tok):
        rows = slice(t * n_seq, (t + 1) * n_seq)
        sp = jnp.zeros((n_seq, SLAB), F32) + bias_ref[t:t + 1, :]
        for s in mix_terms[t]:
            r = t * n_tok + s
            sp = sp + coef_ref[r:r + 1, :] * z_ref[1, s * n_seq:(s + 1) * n_seq, :]
        yc = z_ref[0, rows, :] * sp
        conv = jnp.zeros((n_seq, SLAB), F32)
        for j in range(CONV_W):
            conv = conv + cw_ref[j:j + 1, :] * ext_row(t + j)
        yd = z_ref[2, rows, :] * conv
        y = jnp.dot(yc.astype(BF16), wo_ref[0:SLAB, :], preferred_element_type=F32)
        y = y + jnp.dot(yd.astype(BF16), wo_ref[SLAB:, :], preferred_element_type=F32)
        out_ref[rows, :] = x_ref[rows, :] + y


def _odd_out_sample(z, ctx, coef, bias, conv_w, w_out, x, n_seq, n_tok, mix_terms):
    T = x.shape[0]
    full = lambda *shape: pl.BlockSpec(shape, lambda i: (0,) * len(shape))
    return pl.pallas_call(
        functools.partial(_odd_out_sample_kernel, n_seq=n_seq, n_tok=n_tok, mix_terms=mix_terms),
        grid=(1,),
        in_specs=[
            full(4, T, SLAB),
            full(n_seq, (CONV_W - 1) * SLAB),
            full(n_tok * n_tok, SLAB),
            full(n_tok, SLAB),
            full(CONV_W, SLAB),
            full(2 * SLAB, D_MODEL),
            full(T, D_MODEL),
        ],
        out_specs=full(T, D_MODEL),
        out_shape=jax.ShapeDtypeStruct((T, D_MODEL), F32),
        compiler_params=_params("arbitrary"),
        name="odd_out_sample",
    )(z, ctx, coef, bias, conv_w, w_out, x)


def _rope_tables(pos, split):
    half = HEAD_DIM // 2
    inv = jnp.power(jnp.float32(ROPE_THETA), -jnp.arange(half, dtype=F32) / half)
    ang = pos.astype(F32)[:, None] * inv[None, :]
    cos = jnp.cos(ang)
    sin = jnp.sin(ang)
    if split:
        return jnp.tile(cos, (1, LANES // half)), jnp.tile(sin, (1, LANES // half))
    cos_t = jnp.tile(jnp.concatenate([cos, cos], axis=-1), (1, LANES // HEAD_DIM))
    sin_t = jnp.tile(jnp.concatenate([-sin, sin], axis=-1), (1, LANES // HEAD_DIM))
    return cos_t, sin_t


def _split_qk_columns_kernel(w_ref, o_ref):
    half = HEAD_DIM // 2
    n_slabs = w_ref.shape[1] // SLAB
    for s in range(n_slabs):
        cols = slice(s * SLAB, (s + 1) * SLAB)
        x = w_ref[:, cols]
        if s >= 1 and (s - 1) % 3 < 2:
            xt = x.T
            xt = jnp.concatenate(
                [xt[h * HEAD_DIM + part * half:h * HEAD_DIM + (part + 1) * half, :]
                 for part in range(2) for h in range(HEADS)], axis=0)
            x = xt.T
        o_ref[:, cols] = x.astype(BF16)


def _split_qk_columns(w, tr=256):
    d, n = w.shape
    return pl.pallas_call(
        _split_qk_columns_kernel,
        grid=(d // tr,),
        in_specs=[pl.BlockSpec((tr, n), lambda i: (i, 0))],
        out_specs=pl.BlockSpec((tr, n), lambda i: (i, 0)),
        out_shape=jax.ShapeDtypeStruct((d, n), BF16),
        compiler_params=_params("arbitrary"),
        name="split_qk_columns",
    )(w)


def kernel(x_prompt, x_sample, cache_kv_w128, cache_kv_w512, cache_kv_w2048, state_pool, state_conv,
           p_prompt, p_sample, ev_w_in, ev_pool_w, ev_pool_scale, ev_w_out, od_w_in, od_ln_g, od_ln_b,
           od_ws, od_bs, od_conv_w, od_w_out, norm_mix, norm_ffn, norm_ple, ffn_w1, ffn_w2,
           ple_w_proj, ple_w_gate, norm_final):
    n_p, seq, _ = x_prompt.shape
    n_s, n_tok, _ = x_sample.shape
    depth = norm_mix.shape[0]
    tp = n_p * seq
    ts = n_s * n_tok

    bf = lambda w: w.astype(BF16)
    row = lambda v: v.reshape(1, -1)
    ev_w_in_b, ev_pool_w_b, ev_w_out_b = bf(ev_w_in), bf(ev_pool_w), bf(ev_w_out)
    od_w_in_b, od_w_out_b = bf(od_w_in), bf(od_w_out)
    w1_b, w2_b, wg_b, wp_b = bf(ffn_w1), bf(ffn_w2), bf(ple_w_gate), bf(ple_w_proj)

    cos_p, sin_p = _rope_tables(jnp.arange(seq), True)
    pos_s = [PAST_LEN + t for t in range(n_tok)]
    cos_s, sin_s = _rope_tables(jnp.repeat(jnp.asarray(pos_s, jnp.int32), n_s), False)

    mix_terms = tuple(
        tuple(s for s in range(n_tok)
              if pos_s[s] // CHUNK == pos_s[t] // CHUNK and pos_s[s] % CHUNK <= pos_s[t] % CHUNK)
        for t in range(n_tok))
    local = [p % CHUNK for p in pos_s]

    rp = x_prompt.reshape(tp, D_MODEL)
    rs = x_sample.transpose(1, 0, 2).reshape(ts, D_MODEL)
    pp = p_prompt.reshape(depth, tp, PLE_DIM)
    ps = p_sample.transpose(0, 2, 1, 3).reshape(depth, ts, PLE_DIM)

    tm_p, tm_mix, tm_ffn, tf = 512, 1024, 512, 512
    tm_odd = 1024
    assert seq % max(tm_p, tm_mix, tm_ffn, tm_odd) == 0 and ts % Q_ROWS == 0
    assert all(win == dil * ATT_BLOCK and (seq // dil) % min(ATT_ROWS, seq // dil) == 0
               and min(win, seq) % (dil * LANES) == 0 for win, dil in DIL_CFG)
    assert tp * HOST_SPLITS == tm_ffn * n_s, "one sample sequence per hosting-MLP grid step"
    assert PAST_LEN >= max(max(POOL_WINDOWS), CONV_W), "sample windows are full"
    assert all(cache.shape[2] == win for cache, (win, _) in
               zip((cache_kv_w128, cache_kv_w512, cache_kv_w2048), DIL_CFG))
    kv_p = [[] for _ in DIL_CFG]
    kv_s = [[] for _ in DIL_CFG]
    pool_p, pool_s, conv_p, conv_s, cv_s = [], [], [], [], []

    for i in range(depth):
        g_mix = row(norm_mix[i])
        ffn_args = (norm_ffn[:, None, :], w1_b, w2_b, norm_ple[:, None, :], wg_b, wp_b,
                    row(norm_final), i == depth - 1)
        if i % 2 == 0:
            e = i // 2
            pscale = row(ev_pool_scale[e])
            dils = tuple(d for _, d in DIL_CFG)
            w_split = _split_qk_columns(ev_w_in[e])
            za, *groups = _proj_even(rp, g_mix, w_split, cos_p, sin_p, tm_p, n_p, dils, True)
            groups = [grp.reshape(3, n_p, dil, seq // dil, SLAB) for grp, dil in zip(groups, dils)]
            attn = [_attn_prompt(grp, dil) for grp, dil in zip(groups, dils)]
            rp = _even_out_prompt(za, attn, ev_pool_w_b[e], pscale, ev_w_out_b[e], rp, seq, tm_mix)
            for g, ((win, _), grp) in enumerate(zip(DIL_CFG, groups)):
                keep = min(win, seq)
                tail = _kv_tail(grp, keep).reshape(n_p, 2, HEADS, HEAD_DIM, keep)
                kv_p[g].append(tail.transpose(0, 4, 1, 2, 3))
            pool_p.append(za.reshape(n_p, seq, SLAB)[:, seq - POOL_STATE:])
            zas, *sgroups = _proj_even(rs, g_mix, ev_w_in_b[e], cos_s, sin_s, ts, 1, (1, 1, 1),
                                       False)
            sgroups = [g.reshape(3, n_tok, n_s, SLAB) for g in sgroups]
            native = lambda c: c.transpose(0, 2, 3, 4, 1).reshape(n_s, 2, SLAB, c.shape[1])
            caches = [native(c[e]) for c in (cache_kv_w128, cache_kv_w512, cache_kv_w2048)]
            rp, pacc, pml = _ffn_ple_hosting(i, rp, pp, *ffn_args, tf, sgroups, caches,
                                             HOST_SPLITS)
            yb = _attn_sample(sgroups, pacc, pml)
            rs = _even_out_sample(zas, state_pool[e].transpose(1, 0, 2),
                                  yb.reshape(n_s, n_tok * SLAB), ev_pool_w_b[e], pscale,
                                  ev_w_out_b[e], rs, n_s, n_tok)
            for g, grp in enumerate(sgroups):
                kv = grp[1:].transpose(2, 1, 0, 3)
                kv_s[g].append(kv.reshape(n_s, n_tok, 2, HEADS, HEAD_DIM))
            a_n = zas.reshape(n_tok, n_s, SLAB).transpose(1, 0, 2)
            pool_s.append(jnp.concatenate([state_pool[e], a_n], axis=1)[:, -POOL_STATE:])
        else:
            o = i // 2
            ln_g, ln_b = row(od_ln_g[o]), row(od_ln_b[o])
            bs_rows = jnp.repeat(od_bs[o].T, LANES, axis=1)
            z = _proj_odd(rp, g_mix, od_w_in_b[o], ln_g, ln_b, tm_odd)
            rp = _odd_out_ffn(i, z, od_ws[o], bs_rows, od_conv_w[o], od_w_out_b[o], rp, seq, pp,
                              *ffn_args, tm_ffn, tf)
            conv_p.append(z.reshape(4, n_p, seq, SLAB)[3, :, seq - (CONV_W - 1):])
            zs = _proj_odd(rs, g_mix, od_w_in_b[o], ln_g, ln_b, ts)
            coef = jnp.stack([jnp.repeat(od_ws[o][:, local[t], local[s]], LANES)
                              for t in range(n_tok) for s in range(n_tok)])
            bias = jnp.stack([bs_rows[local[t]] for t in range(n_tok)])
            rs = _odd_out_sample(zs, state_conv[o].reshape(n_s, (CONV_W - 1) * SLAB), coef, bias,
                                 od_conv_w[o], od_w_out_b[o], rs, n_s, n_tok, mix_terms)
            hd_n = zs[3].reshape(n_tok, n_s, SLAB).transpose(1, 0, 2)
            conv_s.append(jnp.concatenate([state_conv[o], hd_n], axis=1)[:, -(CONV_W - 1):])
            cv_s.append(zs[1].reshape(n_tok, n_s, SLAB).transpose(1, 0, 2))

        rs = _ffn_ple(i, rs, ps, *ffn_args, ts, tf)

    y_prompt = rp.reshape(n_p, seq, D_MODEL)
    y_sample = rs.reshape(n_tok, n_s, D_MODEL).transpose(1, 0, 2)
    st = lambda lst: jnp.stack(lst, axis=0)
    return (y_prompt, y_sample, st(kv_p[0]), st(kv_p[1]), st(kv_p[2]),
            st(kv_s[0]), st(kv_s[1]), st(kv_s[2]),
            st(pool_p), st(pool_s), st(conv_p), st(conv_s), st(cv_s))
```
